```python
import jax
import jax.numpy as jnp
from jax import lax
import numpy as np


D_MODEL = 2048
BATCH = 8
SEQ = 8192
DEPTH = 4

GRID_W = 64
CTX_LEN = 256

HEAD_DIM = 128
N_HEADS = D_MODEL // HEAD_DIM
N_KV_HEADS = N_HEADS // 4
GQA_GROUP = N_HEADS // N_KV_HEADS
ATTN_DIM = N_HEADS * HEAD_DIM
KV_DIM = N_KV_HEADS * HEAD_DIM
Q_BLOCK = 128
ATTN_SCALE = HEAD_DIM ** -0.5
ROPE_THETA = 10000.0
AXIS_DIM = HEAD_DIM // 2
ODD_IN = 2 * ATTN_DIM + 2 * KV_DIM

WIDTH_A = D_MODEL
WIDTH_B = D_MODEL
CONV_A_WIDTH = 31
CONV_B_WIDTH = 3
EVEN_IN = 3 * WIDTH_A + 4 * WIDTH_B
EVEN_SPLITS = [WIDTH_A, 2 * WIDTH_A, 3 * WIDTH_A, 3 * WIDTH_A + WIDTH_B,
               3 * WIDTH_A + 2 * WIDTH_B, 3 * WIDTH_A + 3 * WIDTH_B]

N_EVEN = (DEPTH + 1) // 2
N_ODD = DEPTH // 2
DEEPNORM_ALPHA = (2.0 * DEPTH) ** 0.25
DEEPNORM_BETA = (8.0 * DEPTH) ** -0.25
LN_EPS = 1e-5
RMS_EPS = 1e-6

kernel_name = 'hybrid_conv_gqa_prefix_dit_block'


def layer_norm(x, g, b):
    xf = x.astype(jnp.float32)
    mu = jnp.mean(xf, axis=-1, keepdims=True)
    var = jnp.mean(jnp.square(xf - mu), axis=-1, keepdims=True)
    return ((xf - mu) * lax.rsqrt(var + LN_EPS)).astype(x.dtype) * g + b


def rms_norm(x, g):
    xf = x.astype(jnp.float32)
    inv = lax.rsqrt(jnp.mean(jnp.square(xf), axis=-1, keepdims=True) + RMS_EPS)
    return (xf * inv).astype(x.dtype) * g


def adaln(cond, w, b):
    m = jax.nn.silu(cond) @ w + b
    return jnp.split(m, 3, axis=-1)


def depthwise_conv(u, w):
    pad = w.shape[0] // 2
    return lax.conv_general_dilated(u, w[:, None, :].astype(u.dtype), (1,), [(pad, pad)],
                                    dimension_numbers=('NWC', 'WIO', 'NWC'),
                                    feature_group_count=u.shape[-1])


def rope_tables(n_tokens, dtype):
    rows_n = n_tokens // GRID_W
    row = jnp.repeat(jnp.arange(rows_n, dtype=jnp.float32), GRID_W)
    col = jnp.tile(jnp.arange(GRID_W, dtype=jnp.float32), rows_n)
    inv_freq = ROPE_THETA ** (-jnp.arange(0, AXIS_DIM, 2, dtype=jnp.float32) / AXIS_DIM)
    ang_r = row[:, None] * inv_freq[None, :]
    ang_c = col[:, None] * inv_freq[None, :]
    return (jnp.cos(ang_r).astype(dtype), jnp.sin(ang_r).astype(dtype),
            jnp.cos(ang_c).astype(dtype), jnp.sin(ang_c).astype(dtype))


def rotate(xh, cos, sin):
    x1, x2 = jnp.split(xh, 2, axis=-1)
    cos = cos[None, :, None, :]
    sin = sin[None, :, None, :]
    return jnp.concatenate([x1 * cos - x2 * sin, x1 * sin + x2 * cos], axis=-1)


def apply_rope_2d(x, tabs):
    cos_r, sin_r, cos_c, sin_c = tabs
    x_row, x_col = jnp.split(x, 2, axis=-1)
    return jnp.concatenate([rotate(x_row, cos_r, sin_r), rotate(x_col, cos_c, sin_c)], axis=-1)


def gqa(q, k, v):
    b, nq = q.shape[:2]
    qg = q.reshape(b, nq, N_KV_HEADS, GQA_GROUP, HEAD_DIM)
    s = jnp.einsum('bqkgd,bnkd->bkgqn', qg, k, preferred_element_type=jnp.float32) * ATTN_SCALE
    p = jax.nn.softmax(s, axis=-1).astype(v.dtype)
    o = jnp.einsum('bkgqn,bnkd->bqkgd', p, v)
    return o.reshape(b, nq, ATTN_DIM)


def q_gate_proj(h, w_qg, qg):
    q, g = jnp.split(h @ w_qg, 2, axis=-1)
    q = rms_norm(q.reshape(h.shape[0], h.shape[1], N_HEADS, HEAD_DIM), qg)
    return q, g


def kv_proj(h, w_kv, kg):
    k, v = jnp.split(h @ w_kv, 2, axis=-1)
    k = rms_norm(k.reshape(h.shape[0], h.shape[1], N_KV_HEADS, HEAD_DIM), kg)
    v = v.reshape(h.shape[0], h.shape[1], N_KV_HEADS, HEAD_DIM)
    return k, v


def attn_mixer(h_lat, h_ctx, w_in, qg, kg, w_out, tabs, ctx_out):
    w_qg = w_in[:, :2 * ATTN_DIM]
    w_kv = w_in[:, 2 * ATTN_DIM:]
    b, s = h_lat.shape[:2]
    k_c, v_c = kv_proj(h_ctx, w_kv, kg)
    q_l, g_l = q_gate_proj(h_lat, w_qg, qg)
    k_l, v_l = kv_proj(h_lat, w_kv, kg)
    q_l = apply_rope_2d(q_l, tabs)
    k_l = apply_rope_2d(k_l, tabs)
    k_all = jnp.concatenate([k_c, k_l], axis=1)
    v_all = jnp.concatenate([v_c, v_l], axis=1)
    n_blocks = s // Q_BLOCK
    q_blocks = q_l.reshape(b, n_blocks, Q_BLOCK, N_HEADS, HEAD_DIM).swapaxes(0, 1)
    o = lax.map(lambda qb: gqa(qb, k_all, v_all), q_blocks)
    o = o.swapaxes(0, 1).reshape(b, s, ATTN_DIM)
    y_lat = (o * jax.nn.silu(g_l)) @ w_out
    if not ctx_out:
        return y_lat, None
    q_c, g_c = q_gate_proj(h_ctx, w_qg, qg)
    o_c = gqa(q_c, k_c, v_c)
    y_ctx = (o_c * jax.nn.silu(g_c)) @ w_out
    return y_lat, y_ctx


def conv_mixer(h, w_in, ca_w, ca_b, na_g, na_b, cb_w, w_out):
    a_val, a_glu, a_gate, b_x, b_b, b_c, b_gate = jnp.split(h @ w_in, EVEN_SPLITS, axis=-1)
    u = a_val * jax.nn.sigmoid(a_glu)
    u = depthwise_conv(u, ca_w) + ca_b
    u = jax.nn.silu(layer_norm(u, na_g, na_b))
    a_out = u * jax.nn.silu(a_gate)
    v = depthwise_conv(b_c * b_x, cb_w)
    b_out = b_b * v * jax.nn.silu(b_gate)
    return jnp.concatenate([a_out, b_out], axis=-1) @ w_out


def _fwd_setup_inputs(seed: int = 0) -> dict:
    key = jax.random.key(seed)
    ks = jax.random.split(key, 19)

    def nrm(k, shape, s):
        return jax.random.normal(k, shape, jnp.float32) * s

    return {
        'x': nrm(ks[0], (BATCH, SEQ, D_MODEL), 1.0),
        'c': nrm(ks[1], (BATCH, D_MODEL), 1.0),
        'ctx': nrm(ks[2], (BATCH, CTX_LEN, D_MODEL), 1.0),
        'c_ctx': nrm(ks[3], (D_MODEL,), 1.0),
        'w_mod': nrm(ks[4], (DEPTH, D_MODEL, 3 * D_MODEL), D_MODEL ** -0.5),
        'b_mod': nrm(ks[5], (DEPTH, 3 * D_MODEL), 0.01),
        'post_ln_g': 1.0 + nrm(ks[6], (DEPTH, D_MODEL), 0.02),
        'post_ln_b': nrm(ks[7], (DEPTH, D_MODEL), 0.02),
        'w_in_e': nrm(ks[8], (N_EVEN, D_MODEL, EVEN_IN), D_MODEL ** -0.5),
        'conv_a_w': nrm(ks[9], (N_EVEN, CONV_A_WIDTH, WIDTH_A), CONV_A_WIDTH ** -0.5),
        'conv_a_b': nrm(ks[10], (N_EVEN, WIDTH_A), 0.02),
        'norm_a_g': 1.0 + nrm(ks[11], (N_EVEN, WIDTH_A), 0.02),
        'norm_a_b': nrm(ks[12], (N_EVEN, WIDTH_A), 0.02),
        'conv_b_w': nrm(ks[13], (N_EVEN, CONV_B_WIDTH, WIDTH_B), CONV_B_WIDTH ** -0.5),
        'w_out_e': nrm(ks[14], (N_EVEN, WIDTH_A + WIDTH_B, D_MODEL),
                       DEEPNORM_BETA * (WIDTH_A + WIDTH_B) ** -0.5),
        'w_in_o': nrm(ks[15], (N_ODD, D_MODEL, ODD_IN), D_MODEL ** -0.5),
        'q_norm_g': 1.0 + nrm(ks[16], (N_ODD, HEAD_DIM), 0.02),
        'k_norm_g': 1.0 + nrm(ks[17], (N_ODD, HEAD_DIM), 0.02),
        'w_out_o': nrm(ks[18], (N_ODD, ATTN_DIM, D_MODEL), DEEPNORM_BETA * ATTN_DIM ** -0.5),
    }


def _fwd_reference(x, c, ctx, c_ctx, w_mod, b_mod, post_ln_g, post_ln_b, w_in_e, conv_a_w, conv_a_b,
              norm_a_g, norm_a_b, conv_b_w, w_out_e, w_in_o, q_norm_g, k_norm_g, w_out_o):
    tabs = rope_tables(x.shape[1], x.dtype)
    for layer in range(DEPTH):
        last = layer == DEPTH - 1
        is_attn = layer % 2 == 1
        i = layer // 2
        need_ctx = (not last) or is_attn
        shift, scale, gate = adaln(c, w_mod[layer], b_mod[layer])
        h_lat = x * (1 + scale[:, None, :]) + shift[:, None, :]
        if need_ctx:
            shift_c, scale_c, gate_c = adaln(c_ctx, w_mod[layer], b_mod[layer])
            h_ctx = ctx * (1 + scale_c) + shift_c
        if is_attn:
            y_lat, y_ctx = attn_mixer(h_lat, h_ctx, w_in_o[i], q_norm_g[i], k_norm_g[i],
                                      w_out_o[i], tabs, not last)
        else:
            y_lat = conv_mixer(h_lat, w_in_e[i], conv_a_w[i], conv_a_b[i], norm_a_g[i],
                               norm_a_b[i], conv_b_w[i], w_out_e[i])
            if not last:
                y_ctx = conv_mixer(h_ctx, w_in_e[i], conv_a_w[i], conv_a_b[i], norm_a_g[i],
                                   norm_a_b[i], conv_b_w[i], w_out_e[i])
        x = layer_norm(DEEPNORM_ALPHA * x + gate[:, None, :] * y_lat,
                       post_ln_g[layer], post_ln_b[layer])
        if not last:
            ctx = layer_norm(DEEPNORM_ALPHA * ctx + gate_c * y_ctx,
                             post_ln_g[layer], post_ln_b[layer])
    return x


import jax as _jax
import jax.numpy as _jnp

TWIN_FORMAT = 'train_step'
FWD_PARAMS = ['x', 'c', 'ctx', 'c_ctx', 'w_mod', 'b_mod', 'post_ln_g', 'post_ln_b', 'w_in_e', 'conv_a_w', 'conv_a_b', 'norm_a_g', 'norm_a_b', 'conv_b_w', 'w_out_e', 'w_in_o', 'q_norm_g', 'k_norm_g', 'w_out_o']
TWIN_WEIGHTS = ['c_ctx', 'w_mod', 'b_mod', 'post_ln_g', 'post_ln_b', 'w_in_e', 'conv_a_w', 'conv_a_b', 'norm_a_g', 'norm_a_b', 'conv_b_w', 'w_out_e', 'w_in_o', 'q_norm_g', 'k_norm_g', 'w_out_o']
TWIN_DIFF_INPUT = 'x'
TWIN_INPUTS = ['x', 'c', 'ctx', 'c_ctx', 'w_mod', 'b_mod', 'post_ln_g', 'post_ln_b', 'w_in_e', 'conv_a_w', 'conv_a_b', 'norm_a_g', 'norm_a_b', 'conv_b_w', 'w_out_e', 'w_in_o', 'q_norm_g', 'k_norm_g', 'w_out_o', 'loss_target', 'm_c_ctx', 'm_w_mod', 'm_b_mod', 'm_post_ln_g', 'm_post_ln_b', 'm_w_in_e', 'm_conv_a_w', 'm_conv_a_b', 'm_norm_a_g', 'm_norm_a_b', 'm_conv_b_w', 'm_w_out_e', 'm_w_in_o', 'm_q_norm_g', 'm_k_norm_g', 'm_w_out_o', 'v_c_ctx', 'v_w_mod', 'v_b_mod', 'v_post_ln_g', 'v_post_ln_b', 'v_w_in_e', 'v_conv_a_w', 'v_conv_a_b', 'v_norm_a_g', 'v_norm_a_b', 'v_conv_b_w', 'v_w_out_e', 'v_w_in_o', 'v_q_norm_g', 'v_k_norm_g', 'v_w_out_o']
TWIN_OUTPUTS = ['loss', 'grad_x', 'grad_c_ctx', 'grad_w_mod', 'grad_b_mod', 'grad_post_ln_g', 'grad_post_ln_b', 'grad_w_in_e', 'grad_conv_a_w', 'grad_conv_a_b', 'grad_norm_a_g', 'grad_norm_a_b', 'grad_conv_b_w', 'grad_w_out_e', 'grad_w_in_o', 'grad_q_norm_g', 'grad_k_norm_g', 'grad_w_out_o', 'delta_c_ctx', 'delta_w_mod', 'delta_b_mod', 'delta_post_ln_g', 'delta_post_ln_b', 'delta_w_in_e', 'delta_conv_a_w', 'delta_conv_a_b', 'delta_norm_a_g', 'delta_norm_a_b', 'delta_conv_b_w', 'delta_w_out_e', 'delta_w_in_o', 'delta_q_norm_g', 'delta_k_norm_g', 'delta_w_out_o', 'new_m_c_ctx', 'new_m_w_mod', 'new_m_b_mod', 'new_m_post_ln_g', 'new_m_post_ln_b', 'new_m_w_in_e', 'new_m_conv_a_w', 'new_m_conv_a_b', 'new_m_norm_a_g', 'new_m_norm_a_b', 'new_m_conv_b_w', 'new_m_w_out_e', 'new_m_w_in_o', 'new_m_q_norm_g', 'new_m_k_norm_g', 'new_m_w_out_o', 'new_v_c_ctx', 'new_v_w_mod', 'new_v_b_mod', 'new_v_post_ln_g', 'new_v_post_ln_b', 'new_v_w_in_e', 'new_v_conv_a_w', 'new_v_conv_a_b', 'new_v_norm_a_g', 'new_v_norm_a_b', 'new_v_conv_b_w', 'new_v_w_out_e', 'new_v_w_in_o', 'new_v_q_norm_g', 'new_v_k_norm_g', 'new_v_w_out_o']
TWIN_LEAF_KINDS = {'loss': 'loss', 'grad_x': 'grad_x', 'grad_c_ctx': 'grad_w', 'grad_w_mod': 'grad_w', 'grad_b_mod': 'grad_w', 'grad_post_ln_g': 'grad_w', 'grad_post_ln_b': 'grad_w', 'grad_w_in_e': 'grad_w', 'grad_conv_a_w': 'grad_w', 'grad_conv_a_b': 'grad_w', 'grad_norm_a_g': 'grad_w', 'grad_norm_a_b': 'grad_w', 'grad_conv_b_w': 'grad_w', 'grad_w_out_e': 'grad_w', 'grad_w_in_o': 'grad_w', 'grad_q_norm_g': 'grad_w', 'grad_k_norm_g': 'grad_w', 'grad_w_out_o': 'grad_w', 'delta_c_ctx': 'delta_w', 'delta_w_mod': 'delta_w', 'delta_b_mod': 'delta_w', 'delta_post_ln_g': 'delta_w', 'delta_post_ln_b': 'delta_w', 'delta_w_in_e': 'delta_w', 'delta_conv_a_w': 'delta_w', 'delta_conv_a_b': 'delta_w', 'delta_norm_a_g': 'delta_w', 'delta_norm_a_b': 'delta_w', 'delta_conv_b_w': 'delta_w', 'delta_w_out_e': 'delta_w', 'delta_w_in_o': 'delta_w', 'delta_q_norm_g': 'delta_w', 'delta_k_norm_g': 'delta_w', 'delta_w_out_o': 'delta_w', 'new_m_c_ctx': 'new_m', 'new_m_w_mod': 'new_m', 'new_m_b_mod': 'new_m', 'new_m_post_ln_g': 'new_m', 'new_m_post_ln_b': 'new_m', 'new_m_w_in_e': 'new_m', 'new_m_conv_a_w': 'new_m', 'new_m_conv_a_b': 'new_m', 'new_m_norm_a_g': 'new_m', 'new_m_norm_a_b': 'new_m', 'new_m_conv_b_w': 'new_m', 'new_m_w_out_e': 'new_m', 'new_m_w_in_o': 'new_m', 'new_m_q_norm_g': 'new_m', 'new_m_k_norm_g': 'new_m', 'new_m_w_out_o': 'new_m', 'new_v_c_ctx': 'new_v', 'new_v_w_mod': 'new_v', 'new_v_b_mod': 'new_v', 'new_v_post_ln_g': 'new_v', 'new_v_post_ln_b': 'new_v', 'new_v_w_in_e': 'new_v', 'new_v_conv_a_w': 'new_v', 'new_v_conv_a_b': 'new_v', 'new_v_norm_a_g': 'new_v', 'new_v_norm_a_b': 'new_v', 'new_v_conv_b_w': 'new_v', 'new_v_w_out_e': 'new_v', 'new_v_w_in_o': 'new_v', 'new_v_q_norm_g': 'new_v', 'new_v_k_norm_g': 'new_v', 'new_v_w_out_o': 'new_v'}


def _forward(args):
    return _fwd_reference(*[args[k] for k in FWD_PARAMS])


def _output_shape():
    def fwd():
        inp = _fwd_setup_inputs(0)
        return _fwd_reference(*[inp[k] for k in FWD_PARAMS])
    out = _jax.eval_shape(fwd)
    return out.shape, out.dtype

N_MICROBATCH = 1
ADAM_LR = 0.001
ADAM_B1 = 0.9
ADAM_B2 = 0.999
ADAM_EPS = 1e-08
ADAM_WD = 0.01
ADAM_STEP = 10
PER_EXAMPLE_BATCH_AXIS = {'x': 0, 'c': 0, 'ctx': 0, 'loss_target': 0}
SHARED_INPUTS = []
_WEIGHT_DTYPES = {'c_ctx': _jnp.float32, 'w_mod': _jnp.float32, 'b_mod': _jnp.float32, 'post_ln_g': _jnp.float32, 'post_ln_b': _jnp.float32, 'w_in_e': _jnp.float32, 'conv_a_w': _jnp.float32, 'conv_a_b': _jnp.float32, 'norm_a_g': _jnp.float32, 'norm_a_b': _jnp.float32, 'conv_b_w': _jnp.float32, 'w_out_e': _jnp.float32, 'w_in_o': _jnp.float32, 'q_norm_g': _jnp.float32, 'k_norm_g': _jnp.float32, 'w_out_o': _jnp.float32}
MOMENT_SCALE = {'c_ctx': 7.218847e-03, 'w_mod': 1.897920e-02, 'b_mod': 3.244225e-02, 'post_ln_g': 1.605105e+01, 'post_ln_b': 7.084132e-01, 'w_in_e': 1.952993e-02, 'conv_a_w': 6.923938e-03, 'conv_a_b': 9.781988e-03, 'norm_a_g': 8.047848e-03, 'norm_a_b': 7.395518e-03, 'conv_b_w': 2.493628e-02, 'w_out_e': 6.183591e-02, 'w_in_o': 8.967526e-03, 'q_norm_g': 7.653842e-03, 'k_norm_g': 7.608008e-03, 'w_out_o': 2.441451e-02}


def _to_microbatches(a, axis):
    t = _jnp.moveaxis(a, axis, 0)
    t = t.reshape((N_MICROBATCH, t.shape[0] // N_MICROBATCH) + t.shape[1:])
    return _jnp.moveaxis(t, 1, axis + 1)


def setup_inputs(seed: int = 0) -> dict:
    inp = _fwd_setup_inputs(seed)
    key = _jax.random.fold_in(_jax.random.key(seed), 7919)
    shape, _ = _output_shape()
    out = dict(inp)
    out["loss_target"] = _jax.random.normal(_jax.random.fold_in(key, 0), shape, _jnp.float32)
    for i, name in enumerate(TWIN_WEIGHTS):
        w = inp[name].astype(_jnp.float32)
        if MOMENT_SCALE is None:
            s = _jnp.sqrt(_jnp.mean(_jnp.square(w)) + 1e-30)
        else:
            s = MOMENT_SCALE[name]
        km, kv = _jax.random.split(_jax.random.fold_in(key, i + 1))
        out[name] = w
        out["m_" + name] = s * _jax.random.normal(km, w.shape, _jnp.float32)
        out["v_" + name] = (s * s) * _jax.random.uniform(kv, w.shape, _jnp.float32, 0.5, 1.5)
    if N_MICROBATCH > 1:
        for name, axis in PER_EXAMPLE_BATCH_AXIS.items():
            out[name] = _to_microbatches(out[name], axis)
    return {'x': out['x'], 'c': out['c'], 'ctx': out['ctx'], 'c_ctx': out['c_ctx'], 'w_mod': out['w_mod'], 'b_mod': out['b_mod'], 'post_ln_g': out['post_ln_g'], 'post_ln_b': out['post_ln_b'], 'w_in_e': out['w_in_e'], 'conv_a_w': out['conv_a_w'], 'conv_a_b': out['conv_a_b'], 'norm_a_g': out['norm_a_g'], 'norm_a_b': out['norm_a_b'], 'conv_b_w': out['conv_b_w'], 'w_out_e': out['w_out_e'], 'w_in_o': out['w_in_o'], 'q_norm_g': out['q_norm_g'], 'k_norm_g': out['k_norm_g'], 'w_out_o': out['w_out_o'], 'loss_target': out['loss_target'], 'm_c_ctx': out['m_c_ctx'], 'm_w_mod': out['m_w_mod'], 'm_b_mod': out['m_b_mod'], 'm_post_ln_g': out['m_post_ln_g'], 'm_post_ln_b': out['m_post_ln_b'], 'm_w_in_e': out['m_w_in_e'], 'm_conv_a_w': out['m_conv_a_w'], 'm_conv_a_b': out['m_conv_a_b'], 'm_norm_a_g': out['m_norm_a_g'], 'm_norm_a_b': out['m_norm_a_b'], 'm_conv_b_w': out['m_conv_b_w'], 'm_w_out_e': out['m_w_out_e'], 'm_w_in_o': out['m_w_in_o'], 'm_q_norm_g': out['m_q_norm_g'], 'm_k_norm_g': out['m_k_norm_g'], 'm_w_out_o': out['m_w_out_o'], 'v_c_ctx': out['v_c_ctx'], 'v_w_mod': out['v_w_mod'], 'v_b_mod': out['v_b_mod'], 'v_post_ln_g': out['v_post_ln_g'], 'v_post_ln_b': out['v_post_ln_b'], 'v_w_in_e': out['v_w_in_e'], 'v_conv_a_w': out['v_conv_a_w'], 'v_conv_a_b': out['v_conv_a_b'], 'v_norm_a_g': out['v_norm_a_g'], 'v_norm_a_b': out['v_norm_a_b'], 'v_conv_b_w': out['v_conv_b_w'], 'v_w_out_e': out['v_w_out_e'], 'v_w_in_o': out['v_w_in_o'], 'v_q_norm_g': out['v_q_norm_g'], 'v_k_norm_g': out['v_k_norm_g'], 'v_w_out_o': out['v_w_out_o']}


def _loss(weights, diff, rest, loss_target):
    with _jax.named_scope("forward"):
        args = {**rest, TWIN_DIFF_INPUT: diff, **{k: w.astype(_WEIGHT_DTYPES[k]) for k, w in weights.items()}}
        y = _forward(args)
    with _jax.named_scope("loss_head"):
        err = _jnp.square(y.astype(_jnp.float32) - loss_target)
        return 0.5 * _jnp.sum(_jnp.mean(err, axis=-1)) if err.ndim else 0.5 * err


def _adamw(w, g, m, v):
    m = ADAM_B1 * m + (1.0 - ADAM_B1) * g
    v = ADAM_B2 * v + (1.0 - ADAM_B2) * _jnp.square(g)
    m_hat = m / (1.0 - ADAM_B1 ** ADAM_STEP)
    v_hat = v / (1.0 - ADAM_B2 ** ADAM_STEP)
    delta = -ADAM_LR * (m_hat / (_jnp.sqrt(v_hat) + ADAM_EPS) + ADAM_WD * w)
    return delta, m, v


def reference(x, c, ctx, c_ctx, w_mod, b_mod, post_ln_g, post_ln_b, w_in_e, conv_a_w, conv_a_b, norm_a_g, norm_a_b, conv_b_w, w_out_e, w_in_o, q_norm_g, k_norm_g, w_out_o, loss_target, m_c_ctx, m_w_mod, m_b_mod, m_post_ln_g, m_post_ln_b, m_w_in_e, m_conv_a_w, m_conv_a_b, m_norm_a_g, m_norm_a_b, m_conv_b_w, m_w_out_e, m_w_in_o, m_q_norm_g, m_k_norm_g, m_w_out_o, v_c_ctx, v_w_mod, v_b_mod, v_post_ln_g, v_post_ln_b, v_w_in_e, v_conv_a_w, v_conv_a_b, v_norm_a_g, v_norm_a_b, v_conv_b_w, v_w_out_e, v_w_in_o, v_q_norm_g, v_k_norm_g, v_w_out_o):
    given = dict(x=x, c=c, ctx=ctx, c_ctx=c_ctx, w_mod=w_mod, b_mod=b_mod, post_ln_g=post_ln_g, post_ln_b=post_ln_b, w_in_e=w_in_e, conv_a_w=conv_a_w, conv_a_b=conv_a_b, norm_a_g=norm_a_g, norm_a_b=norm_a_b, conv_b_w=conv_b_w, w_out_e=w_out_e, w_in_o=w_in_o, q_norm_g=q_norm_g, k_norm_g=k_norm_g, w_out_o=w_out_o, loss_target=loss_target, m_c_ctx=m_c_ctx, m_w_mod=m_w_mod, m_b_mod=m_b_mod, m_post_ln_g=m_post_ln_g, m_post_ln_b=m_post_ln_b, m_w_in_e=m_w_in_e, m_conv_a_w=m_conv_a_w, m_conv_a_b=m_conv_a_b, m_norm_a_g=m_norm_a_g, m_norm_a_b=m_norm_a_b, m_conv_b_w=m_conv_b_w, m_w_out_e=m_w_out_e, m_w_in_o=m_w_in_o, m_q_norm_g=m_q_norm_g, m_k_norm_g=m_k_norm_g, m_w_out_o=m_w_out_o, v_c_ctx=v_c_ctx, v_w_mod=v_w_mod, v_b_mod=v_b_mod, v_post_ln_g=v_post_ln_g, v_post_ln_b=v_post_ln_b, v_w_in_e=v_w_in_e, v_conv_a_w=v_conv_a_w, v_conv_a_b=v_conv_a_b, v_norm_a_g=v_norm_a_g, v_norm_a_b=v_norm_a_b, v_conv_b_w=v_conv_b_w, v_w_out_e=v_w_out_e, v_w_in_o=v_w_in_o, v_q_norm_g=v_q_norm_g, v_k_norm_g=v_k_norm_g, v_w_out_o=v_w_out_o)
    weights = {n: given[n] for n in TWIN_WEIGHTS}
    shared = {n: given[n] for n in SHARED_INPUTS}
    per_example = {n: given[n] for n in ['x', 'c', 'ctx']}
    grad_fn = _jax.value_and_grad(_loss, argnums=(0, 1))

    def one_microbatch(ex, loss_target):
        ex = dict(ex)
        diff = ex.pop(TWIN_DIFF_INPUT)
        return grad_fn(weights, diff, {**shared, **ex}, loss_target)

    if N_MICROBATCH == 1:
        loss, (grad_w, grad_x) = one_microbatch(per_example, given["loss_target"])
    else:
        def body(carry, xs):
            loss_sum, grad_sum = carry
            l_k, (gw_k, gx_k) = one_microbatch(xs[0], xs[1])
            with _jax.named_scope("update"):
                return (loss_sum + l_k, _jax.tree.map(_jnp.add, grad_sum, gw_k)), gx_k

        init = (_jnp.zeros((), _jnp.float32), _jax.tree.map(_jnp.zeros_like, weights))
        (loss, grad_w), grad_x = _jax.lax.scan(body, init, (per_example, given["loss_target"]))
    with _jax.named_scope("update"):
        delta_w, new_m, new_v = {}, {}, {}
        for n in TWIN_WEIGHTS:
            delta_w[n], new_m[n], new_v[n] = _adamw(weights[n], grad_w[n], given["m_" + n], given["v_" + n])
    return (loss, grad_x, *[grad_w[n] for n in TWIN_WEIGHTS], *[delta_w[n] for n in TWIN_WEIGHTS],
            *[new_m[n] for n in TWIN_WEIGHTS], *[new_v[n] for n in TWIN_WEIGHTS])
```

```python
import functools

import jax
import jax.numpy as jnp
from jax import lax
from jax.experimental import pallas as pl
from jax.experimental.pallas import tpu as pltpu

F32 = jnp.float32
BF16 = jnp.bfloat16

LANES = 128
SUBLANES = 8
HEAD_DIM = 128
GQA_GROUP = 4
GRID_W = 64
ROPE_THETA = 10000.0
LN_EPS = 1e-5
RMS_EPS = 1e-6
CONV_A_TAPS = 31
CONV_B_TAPS = 3
HALO = 16
CONV_ROWS = 128
ADAM_LR = 0.001
ADAM_B1 = 0.9
ADAM_B2 = 0.999
ADAM_EPS = 1e-08
ADAM_WD = 0.01
ADAM_STEP = 10
VMEM_LIMIT = 56 * 1024 * 1024
MESH_AXES = ("x", "y", "c")
MESH = pl.DeviceIdType.MESH
ANY = pl.BlockSpec(memory_space=pl.ANY)
VMEM_FULL = pl.BlockSpec(memory_space=pltpu.VMEM)


def _sds(shape, dtype):
    return jax.ShapeDtypeStruct(tuple(shape), dtype)


def _cp(*sem):
    return pltpu.CompilerParams(dimension_semantics=sem, vmem_limit_bytes=VMEM_LIMIT)


def _pick(n, cands):
    for c in cands:
        if n % c == 0:
            return c
    return n


def _sigmoid(x):
    return 1.0 / (1.0 + jnp.exp(-x))


def _silu(x):
    return x * _sigmoid(x)


def _dsilu(x):
    s = _sigmoid(x)
    return s * (1.0 + x * (1.0 - s))


def _row(tm, d):
    return pl.BlockSpec((tm, d), lambda j: (j, 0))


def _seg(d):
    return pl.BlockSpec((None, 1, d), lambda j: (jnp.minimum(j, 1), 0, 0))


def _vec(d):
    return pl.BlockSpec((1, d), lambda j: (0, 0))


def _colblk(tm, width, blk):
    return pl.BlockSpec((tm, width), lambda j: (j, blk))


def _seg_acc(d):
    return pl.BlockSpec((None, SUBLANES, d), lambda j: (jnp.minimum(j, 1), 0, 0))


def _ln_stats(z):
    mu = jnp.mean(z, axis=-1, keepdims=True)
    zc = z - mu
    var = jnp.mean(zc * zc, axis=-1, keepdims=True)
    rstd = lax.rsqrt(var + LN_EPS)
    return zc * rstd, rstd


def _ln_bwd(dxh, xhat, rstd):
    m1 = jnp.mean(dxh, axis=-1, keepdims=True)
    m2 = jnp.mean(dxh * xhat, axis=-1, keepdims=True)
    return rstd * (dxh - m1 - xhat * m2)


def _colsum(v):
    return jnp.sum(v, axis=0, keepdims=True)


def mod_fwd(xc, scale2, shift2, tm):
    t, d = xc.shape

    def body(x_ref, sc_ref, sh_ref, h_ref):
        h_ref[...] = (x_ref[...] * (1.0 + sc_ref[...]) + sh_ref[...]).astype(h_ref.dtype)

    return pl.pallas_call(
        body, name="mod_fwd", grid=(t // tm,),
        in_specs=[_row(tm, d), _seg(d), _seg(d)], out_specs=_row(tm, d),
        out_shape=_sds((t, d), BF16), compiler_params=_cp("parallel"))(xc, scale2, shift2)


def post_ln_fwd(xc, y, gate2, g, b, alpha, tm):
    t, d = xc.shape

    def body(x_ref, y_ref, gt_ref, g_ref, b_ref, o_ref):
        z = alpha * x_ref[...] + gt_ref[...] * y_ref[...]
        xhat, _ = _ln_stats(z)
        o_ref[...] = xhat * g_ref[...] + b_ref[...]

    return pl.pallas_call(
        body, name="post_ln_fwd", grid=(t // tm,),
        in_specs=[_row(tm, d), _row(tm, d), _seg(d), _vec(d), _vec(d)], out_specs=_row(tm, d),
        out_shape=_sds((t, d), F32), compiler_params=_cp("parallel"))(xc, y, gate2, g, b)


def post_ln_bwd(dout, xc, y, gate2, g, alpha, tm):
    t, d = xc.shape

    def body(do_ref, x_ref, y_ref, gt_ref, g_ref, dzx_ref, dy_ref, acc_ref):
        @pl.when(pl.program_id(0) <= 1)
        def _():
            acc_ref[...] = jnp.zeros_like(acc_ref)

        yv = y_ref[...]
        gate = gt_ref[...]
        xhat, rstd = _ln_stats(alpha * x_ref[...] + gate * yv)
        dout = do_ref[...]
        dz = _ln_bwd(dout * g_ref[...], xhat, rstd)
        dzx_ref[...] = alpha * dz
        dy_ref[...] = (gate * dz).astype(dy_ref.dtype)
        acc_ref[0:1, :] += _colsum(dz * yv)
        acc_ref[1:2, :] += _colsum(dout * xhat)
        acc_ref[2:3, :] += _colsum(dout)

    return pl.pallas_call(
        body, name="post_ln_bwd", grid=(t // tm,),
        in_specs=[_row(tm, d), _row(tm, d), _row(tm, d), _seg(d), _vec(d)],
        out_specs=[_row(tm, d), _row(tm, d), _seg_acc(d)],
        out_shape=[_sds((t, d), F32), _sds((t, d), BF16), _sds((2, SUBLANES, d), F32)],
        compiler_params=_cp("arbitrary"))(dout, xc, y, gate2, g)


def mod_bwd(dh, dzx, xc, scale2, tm):
    t, d = xc.shape

    def body(dh_ref, dzx_ref, x_ref, sc_ref, dx_ref, acc_ref):
        @pl.when(pl.program_id(0) <= 1)
        def _():
            acc_ref[...] = jnp.zeros_like(acc_ref)

        dhv = dh_ref[...].astype(F32)
        dx_ref[...] = dzx_ref[...] + dhv * (1.0 + sc_ref[...])
        acc_ref[0:1, :] += _colsum(dhv)
        acc_ref[1:2, :] += _colsum(dhv * x_ref[...])

    return pl.pallas_call(
        body, name="mod_bwd", grid=(t // tm,),
        in_specs=[_row(tm, d), _row(tm, d), _row(tm, d), _seg(d)],
        out_specs=[_row(tm, d), _seg_acc(d)],
        out_shape=[_sds((t, d), F32), _sds((2, SUBLANES, d), F32)],
        compiler_params=_cp("arbitrary"))(dh, dzx, xc, scale2)


def loss_head(xc, target, lc, tm):
    t, d = xc.shape

    def body(x_ref, t_ref, dx_ref, acc_ref):
        j = pl.program_id(0)

        @pl.when(j == 0)
        def _():
            acc_ref[...] = jnp.zeros_like(acc_ref)
            dx_ref[...] = jnp.zeros_like(dx_ref)

        @pl.when(j > 0)
        def _():
            err = x_ref[...] - t_ref[...]
            dx_ref[...] = err * (1.0 / d)
            col = _colsum(err * err)
            tot = col[:, 0:LANES]
            for k in range(1, d // LANES):
                tot = tot + col[:, k * LANES:(k + 1) * LANES]
            acc_ref[0:1, :] += tot

    nlc = lc // tm
    return pl.pallas_call(
        body, name="loss_head", grid=(t // tm,),
        in_specs=[_row(tm, d), pl.BlockSpec((tm, d), lambda j: (jnp.maximum(j - nlc, 0), 0))],
        out_specs=[_row(tm, d), pl.BlockSpec((SUBLANES, LANES), lambda j: (0, 0))],
        out_shape=[_sds((t, d), F32), _sds((SUBLANES, LANES), F32)],
        compiler_params=_cp("arbitrary"))(xc, target)


def mm_nn(a, w3, out_dtype):
    m, k = a.shape
    ns, _, nl = w3.shape
    tm = _pick(m, (768, 512, 256, 128))
    tn = _pick(nl, (512, 256, 128))
    npj = nl // tn

    def body(a_ref, w_ref, o_ref):
        o_ref[...] = jnp.dot(a_ref[...], w_ref[...], preferred_element_type=F32).astype(o_ref.dtype)

    return pl.pallas_call(
        body, name="mm_nn", grid=(m // tm, ns * npj),
        in_specs=[pl.BlockSpec((tm, k), lambda i, j: (i, 0)),
                  pl.BlockSpec((None, k, tn), lambda i, j: (j // npj, 0, j % npj))],
        out_specs=pl.BlockSpec((tm, tn), lambda i, j: (i, j)),
        out_shape=_sds((m, ns * nl), out_dtype), compiler_params=_cp("parallel", "parallel"))(a, w3)


def mm_nt(a, w3, out_dtype):
    m, _ = a.shape
    ns, k, nl = w3.shape
    tm = _pick(m, (768, 512, 256, 128))
    tk = _pick(k, (2048, 1024, 512, 256, 128))
    tn = _pick(nl, (512, 256, 128))
    npj = nl // tn
    nsteps = ns * npj

    def body(a_ref, w_ref, o_ref, acc_ref):
        n = pl.program_id(2)

        @pl.when(n == 0)
        def _():
            acc_ref[...] = jnp.zeros_like(acc_ref)

        acc_ref[...] += lax.dot_general(a_ref[...], w_ref[...], (((1,), (1,)), ((), ())),
                                        preferred_element_type=F32)

        @pl.when(n == nsteps - 1)
        def _():
            o_ref[...] = acc_ref[...].astype(o_ref.dtype)

    return pl.pallas_call(
        body, name="mm_nt", grid=(m // tm, k // tk, nsteps),
        in_specs=[pl.BlockSpec((tm, tn), lambda i, kk, n: (i, n)),
                  pl.BlockSpec((None, tk, tn), lambda i, kk, n: (n // npj, kk, n % npj))],
        out_specs=pl.BlockSpec((tm, tk), lambda i, kk, n: (i, kk)),
        out_shape=_sds((m, k), out_dtype), scratch_shapes=[pltpu.VMEM((tm, tk), F32)],
        compiler_params=_cp("parallel", "parallel", "arbitrary"))(a, w3)


def mm_tn(a, b, ns):
    m, k = a.shape
    nl = b.shape[1] // ns
    tm = _pick(m, (768, 512, 256, 128))
    tk = _pick(k, (1024, 512, 256, 128))
    tn = _pick(nl, (512, 256, 128))
    npj = nl // tn

    def body(a_ref, b_ref, o_ref):
        @pl.when(pl.program_id(2) == 0)
        def _():
            o_ref[...] = jnp.zeros_like(o_ref)

        o_ref[...] += lax.dot_general(a_ref[...], b_ref[...], (((0,), (0,)), ((), ())),
                                      preferred_element_type=F32)

    return pl.pallas_call(
        body, name="mm_tn", grid=(k // tk, ns * npj, m // tm),
        in_specs=[pl.BlockSpec((tm, tk), lambda i, j, r: (r, i)),
                  pl.BlockSpec((tm, tn), lambda i, j, r: (r, j))],
        out_specs=pl.BlockSpec((None, tk, tn), lambda i, j, r: (j // npj, i, j % npj)),
        out_shape=_sds((ns, k, nl), F32),
        compiler_params=_cp("parallel", "parallel", "arbitrary"))(a, b)


def _win_start(j, ncc):
    return pl.multiple_of(j * CONV_ROWS + jnp.where(j >= ncc, HALO, 0), SUBLANES)


def _tok_start(j):
    return pl.multiple_of(j * CONV_ROWS, CONV_ROWS)


def _shifted(xw, off):
    n = xw.shape[0]
    sh = (n - off) % n
    y = pltpu.roll(xw, sh, 0) if sh else xw
    return y[:CONV_ROWS]


def _conv_fwd(xw, w_ref, ntaps):
    pad = ntaps // 2
    acc = None
    for k in range(ntaps):
        term = w_ref[k:k + 1, :] * _shifted(xw, HALO + k - pad)
        acc = term if acc is None else acc + term
    return acc


def _conv_bwd_data(xw, w_ref, ntaps):
    pad = ntaps // 2
    acc = None
    for k in range(ntaps):
        term = w_ref[k:k + 1, :] * _shifted(xw, HALO - k + pad)
        acc = term if acc is None else acc + term
    return acc


def _conv_bwd_weight(dw_ref, d, xw, ntaps):
    pad = ntaps // 2
    for k in range(ntaps):
        dw_ref[k:k + 1, :] += _colsum(d * _shifted(xw, HALO + k - pad))


def _zero_halos(pad_ref, lc, t):
    z = jnp.zeros((HALO, LANES), F32)
    pad_ref[0:HALO, :] = z
    pad_ref[HALO + lc:2 * HALO + lc, :] = z
    pad_ref[2 * HALO + t:3 * HALO + t, :] = z


def _pad_dst(j, ncc):
    return pl.multiple_of(j * CONV_ROWS + HALO + jnp.where(j >= ncc, HALO, 0), SUBLANES)


def _chan(t, blk0):
    return pl.BlockSpec((t, LANES), lambda ct: (0, blk0 + ct))


def _tapw(rows):
    return pl.BlockSpec((rows, LANES), lambda ct: (0, ct))


def conv_a_fwd(p, w_pad, bias, lc, d):
    t = p.shape[0]
    nct, nch, ncc = d // LANES, t // CONV_ROWS, lc // CONV_ROWS

    def body(av_ref, ag_ref, w_ref, b_ref, u1_ref, pad_ref):
        _zero_halos(pad_ref, lc, t)

        def fill(j, carry):
            rows = pl.ds(_tok_start(j), CONV_ROWS)
            u0 = av_ref[rows, :].astype(F32) * _sigmoid(ag_ref[rows, :].astype(F32))
            pad_ref[pl.ds(_pad_dst(j, ncc), CONV_ROWS), :] = u0
            return carry

        lax.fori_loop(0, nch, fill, 0)

        def conv(j, carry):
            xw = pad_ref[pl.ds(_win_start(j, ncc), CONV_ROWS + 2 * HALO), :]
            u1_ref[pl.ds(_tok_start(j), CONV_ROWS), :] = _conv_fwd(xw, w_ref, CONV_A_TAPS) + b_ref[...]
            return carry

        lax.fori_loop(0, nch, conv, 0)

    return pl.pallas_call(
        body, name="conv_a_fwd", grid=(nct,),
        in_specs=[_chan(t, 0), _chan(t, nct), _tapw(32), _tapw(1)],
        out_specs=_chan(t, 0), out_shape=_sds((t, d), F32),
        scratch_shapes=[pltpu.VMEM((t + 3 * HALO, LANES), F32)],
        compiler_params=_cp("parallel"))(p, p, w_pad, bias)


def conv_b_fwd(p, w_pad, lc, d):
    t = p.shape[0]
    nct, nch, ncc = d // LANES, t // CONV_ROWS, lc // CONV_ROWS

    def body(bx_ref, bb_ref, bc_ref, bg_ref, w_ref, o_ref, pad_ref):
        _zero_halos(pad_ref, lc, t)

        def fill(j, carry):
            rows = pl.ds(_tok_start(j), CONV_ROWS)
            pad_ref[pl.ds(_pad_dst(j, ncc), CONV_ROWS), :] = (
                bc_ref[rows, :].astype(F32) * bx_ref[rows, :].astype(F32))
            return carry

        lax.fori_loop(0, nch, fill, 0)

        def conv(j, carry):
            rows = pl.ds(_tok_start(j), CONV_ROWS)
            xw = pad_ref[pl.ds(_win_start(j, ncc), CONV_ROWS + 2 * HALO), :]
            v = _conv_fwd(xw, w_ref, CONV_B_TAPS)
            o_ref[rows, :] = (bb_ref[rows, :].astype(F32) * v
                              * _silu(bg_ref[rows, :].astype(F32))).astype(o_ref.dtype)
            return carry

        lax.fori_loop(0, nch, conv, 0)

    return pl.pallas_call(
        body, name="conv_b_fwd", grid=(nct,),
        in_specs=[_chan(t, 3 * nct), _chan(t, 4 * nct), _chan(t, 5 * nct), _chan(t, 6 * nct), _tapw(8)],
        out_specs=_chan(t, 0), out_shape=_sds((t, d), BF16),
        scratch_shapes=[pltpu.VMEM((t + 3 * HALO, LANES), F32)],
        compiler_params=_cp("parallel"))(p, p, p, p, w_pad)


def ln_a_fwd(u1, p, g, b, tm):
    t, d = u1.shape

    def body(u_ref, ag_ref, g_ref, b_ref, o_ref):
        xhat, _ = _ln_stats(u_ref[...])
        u2 = xhat * g_ref[...] + b_ref[...]
        o_ref[...] = (_silu(u2) * _silu(ag_ref[...].astype(F32))).astype(o_ref.dtype)

    return pl.pallas_call(
        body, name="ln_a_fwd", grid=(t // tm,),
        in_specs=[_row(tm, d), _colblk(tm, d, 2), _vec(d), _vec(d)], out_specs=_row(tm, d),
        out_shape=_sds((t, d), BF16), compiler_params=_cp("parallel"))(u1, p, g, b)


def ln_a_bwd(dab, u1, p, g, b, tm):
    t, d = u1.shape

    def body(da_ref, u_ref, ag_ref, g_ref, b_ref, du_ref, dag_ref, acc_ref):
        @pl.when(pl.program_id(0) == 0)
        def _():
            acc_ref[...] = jnp.zeros_like(acc_ref)

        xhat, rstd = _ln_stats(u_ref[...])
        u2 = xhat * g_ref[...] + b_ref[...]
        ag = ag_ref[...].astype(F32)
        da = da_ref[...].astype(F32)
        dag_ref[...] = (da * _silu(u2) * _dsilu(ag)).astype(dag_ref.dtype)
        du2 = da * _silu(ag) * _dsilu(u2)
        du1 = _ln_bwd(du2 * g_ref[...], xhat, rstd)
        du_ref[...] = du1
        acc_ref[0:1, :] += _colsum(du2 * xhat)
        acc_ref[1:2, :] += _colsum(du2)
        acc_ref[2:3, :] += _colsum(du1)

    return pl.pallas_call(
        body, name="ln_a_bwd", grid=(t // tm,),
        in_specs=[_colblk(tm, d, 0), _row(tm, d), _colblk(tm, d, 2), _vec(d), _vec(d)],
        out_specs=[_row(tm, d), _row(tm, d), pl.BlockSpec((SUBLANES, d), lambda j: (0, 0))],
        out_shape=[_sds((t, d), F32), _sds((t, d), BF16), _sds((SUBLANES, d), F32)],
        compiler_params=_cp("arbitrary"))(dab, u1, p, g, b)


def conv_a_bwd(du1, p, w_pad, lc, d):
    t = p.shape[0]
    nct, nch, ncc = d // LANES, t // CONV_ROWS, lc // CONV_ROWS

    def body(du_ref, av_ref, ag_ref, w_ref, dav_ref, dag_ref, dw_ref, pad_u, pad_d):
        _zero_halos(pad_u, lc, t)
        _zero_halos(pad_d, lc, t)
        dw_ref[...] = jnp.zeros_like(dw_ref)

        def fill(j, carry):
            rows = pl.ds(_tok_start(j), CONV_ROWS)
            dst = pl.ds(_pad_dst(j, ncc), CONV_ROWS)
            pad_u[dst, :] = av_ref[rows, :].astype(F32) * _sigmoid(ag_ref[rows, :].astype(F32))
            pad_d[dst, :] = du_ref[rows, :]
            return carry

        lax.fori_loop(0, nch, fill, 0)

        def step(j, carry):
            rows = pl.ds(_tok_start(j), CONV_ROWS)
            win = pl.ds(_win_start(j, ncc), CONV_ROWS + 2 * HALO)
            du0 = _conv_bwd_data(pad_d[win, :], w_ref, CONV_A_TAPS)
            sig = _sigmoid(ag_ref[rows, :].astype(F32))
            dav_ref[rows, :] = (du0 * sig).astype(dav_ref.dtype)
            dag_ref[rows, :] = (du0 * av_ref[rows, :].astype(F32) * sig * (1.0 - sig)).astype(dag_ref.dtype)
            _conv_bwd_weight(dw_ref, du_ref[rows, :], pad_u[win, :], CONV_A_TAPS)
            return carry

        lax.fori_loop(0, nch, step, 0)

    return pl.pallas_call(
        body, name="conv_a_bwd", grid=(nct,),
        in_specs=[_chan(t, 0), _chan(t, 0), _chan(t, nct), _tapw(32)],
        out_specs=[_chan(t, 0), _chan(t, 0), _tapw(32)],
        out_shape=[_sds((t, d), BF16), _sds((t, d), BF16), _sds((32, d), F32)],
        scratch_shapes=[pltpu.VMEM((t + 3 * HALO, LANES), F32), pltpu.VMEM((t + 3 * HALO, LANES), F32)],
        compiler_params=_cp("parallel"))(du1, p, p, w_pad)


def conv_b_bwd(dab, p, w_pad, lc, d):
    t = p.shape[0]
    nct, nch, ncc = d // LANES, t // CONV_ROWS, lc // CONV_ROWS

    def body(db_ref, bx_ref, bb_ref, bc_ref, bg_ref, w_ref,
             dbx_ref, dbb_ref, dbc_ref, dbg_ref, dw_ref, pad_t, pad_d):
        _zero_halos(pad_t, lc, t)
        _zero_halos(pad_d, lc, t)
        dw_ref[...] = jnp.zeros_like(dw_ref)

        def fill(j, carry):
            rows = pl.ds(_tok_start(j), CONV_ROWS)
            pad_t[pl.ds(_pad_dst(j, ncc), CONV_ROWS), :] = (
                bc_ref[rows, :].astype(F32) * bx_ref[rows, :].astype(F32))
            return carry

        lax.fori_loop(0, nch, fill, 0)

        def first(j, carry):
            rows = pl.ds(_tok_start(j), CONV_ROWS)
            xw = pad_t[pl.ds(_win_start(j, ncc), CONV_ROWS + 2 * HALO), :]
            v = _conv_fwd(xw, w_ref, CONV_B_TAPS)
            bg = bg_ref[rows, :].astype(F32)
            bb = bb_ref[rows, :].astype(F32)
            db = db_ref[rows, :].astype(F32)
            sg = _silu(bg)
            dbb_ref[rows, :] = (db * v * sg).astype(dbb_ref.dtype)
            dbg_ref[rows, :] = (db * bb * v * _dsilu(bg)).astype(dbg_ref.dtype)
            dv = db * bb * sg
            pad_d[pl.ds(_pad_dst(j, ncc), CONV_ROWS), :] = dv
            _conv_bwd_weight(dw_ref, dv, xw, CONV_B_TAPS)
            return carry

        lax.fori_loop(0, nch, first, 0)

        def second(j, carry):
            rows = pl.ds(_tok_start(j), CONV_ROWS)
            dt = _conv_bwd_data(pad_d[pl.ds(_win_start(j, ncc), CONV_ROWS + 2 * HALO), :], w_ref, CONV_B_TAPS)
            dbc_ref[rows, :] = (dt * bx_ref[rows, :].astype(F32)).astype(dbc_ref.dtype)
            dbx_ref[rows, :] = (dt * bc_ref[rows, :].astype(F32)).astype(dbx_ref.dtype)
            return carry

        lax.fori_loop(0, nch, second, 0)

    return pl.pallas_call(
        body, name="conv_b_bwd", grid=(nct,),
        in_specs=[_chan(t, nct), _chan(t, 3 * nct), _chan(t, 4 * nct), _chan(t, 5 * nct), _chan(t, 6 * nct),
                  _tapw(8)],
        out_specs=[_chan(t, 0)] * 4 + [_tapw(8)],
        out_shape=[_sds((t, d), BF16)] * 4 + [_sds((8, d), F32)],
        scratch_shapes=[pltpu.VMEM((t + 3 * HALO, LANES), F32), pltpu.VMEM((t + 3 * HALO, LANES), F32)],
        compiler_params=_cp("parallel"))(dab, p, p, p, p, w_pad)


def _swap_halves(z, first_half):
    return jnp.where(first_half, pltpu.roll(z, 96, 1), pltpu.roll(z, 32, 1))


def _first_half_mask(rows):
    lane = lax.broadcasted_iota(jnp.int32, (rows, HEAD_DIM), 1)
    return (lane & 32) == 0


def qk_fwd(qgkv, qg, kg, cos_t, sin_t, ad, kvd, tm):
    t = qgkv.shape[0]

    def body(q_ref, k_ref, qg_ref, kg_ref, c_ref, s_ref, qo_ref, ko_ref):
        first = _first_half_mask(tm)
        cosv, sinv = c_ref[...], s_ref[...]

        def head(x, gain):
            inv = lax.rsqrt(jnp.mean(x * x, axis=-1, keepdims=True) + RMS_EPS)
            yv = x * inv * gain
            return yv * cosv + _swap_halves(yv, first) * sinv

        for h in range(ad // HEAD_DIM):
            sl = slice(h * HEAD_DIM, (h + 1) * HEAD_DIM)
            qo_ref[:, sl] = head(q_ref[:, sl].astype(F32), qg_ref[...]).astype(qo_ref.dtype)
        for h in range(kvd // HEAD_DIM):
            sl = slice(h * HEAD_DIM, (h + 1) * HEAD_DIM)
            ko_ref[:, sl] = head(k_ref[:, sl].astype(F32), kg_ref[...]).astype(ko_ref.dtype)

    return pl.pallas_call(
        body, name="qk_fwd", grid=(t // tm,),
        in_specs=[_colblk(tm, ad, 0), _colblk(tm, kvd, 2 * ad // kvd), _vec(HEAD_DIM), _vec(HEAD_DIM),
                  _row(tm, HEAD_DIM), _row(tm, HEAD_DIM)],
        out_specs=[_row(tm, ad), _row(tm, kvd)],
        out_shape=[_sds((t, ad), BF16), _sds((t, kvd), BF16)],
        compiler_params=_cp("parallel"))(qgkv, qgkv, qg, kg, cos_t, sin_t)


def qk_bwd(dqr, dkr, qgkv, qg, kg, cos_t, sin_t, ad, kvd, tm):
    t = qgkv.shape[0]

    def body(dq_ref, dk_ref, q_ref, k_ref, qg_ref, kg_ref, c_ref, s_ref, dqo_ref, dko_ref, acc_ref):
        @pl.when(pl.program_id(0) == 0)
        def _():
            acc_ref[...] = jnp.zeros_like(acc_ref)

        first = _first_half_mask(tm)
        cosv, sinv = c_ref[...], s_ref[...]

        def head(x, gain, dout):
            inv = lax.rsqrt(jnp.mean(x * x, axis=-1, keepdims=True) + RMS_EPS)
            xn = x * inv
            dy = dout * cosv + _swap_halves(dout * sinv, first)
            dxn = dy * gain
            dx = inv * (dxn - xn * jnp.mean(dxn * xn, axis=-1, keepdims=True))
            return dx, _colsum(dy * xn)

        dqg = jnp.zeros((1, HEAD_DIM), F32)
        for h in range(ad // HEAD_DIM):
            sl = slice(h * HEAD_DIM, (h + 1) * HEAD_DIM)
            dx, dg = head(q_ref[:, sl].astype(F32), qg_ref[...], dq_ref[:, sl])
            dqo_ref[:, sl] = dx.astype(dqo_ref.dtype)
            dqg = dqg + dg
        dkg = jnp.zeros((1, HEAD_DIM), F32)
        for h in range(kvd // HEAD_DIM):
            sl = slice(h * HEAD_DIM, (h + 1) * HEAD_DIM)
            dx, dg = head(k_ref[:, sl].astype(F32), kg_ref[...], dk_ref[:, sl])
            dko_ref[:, sl] = dx.astype(dko_ref.dtype)
            dkg = dkg + dg
        acc_ref[0:1, :] += dqg
        acc_ref[1:2, :] += dkg

    return pl.pallas_call(
        body, name="qk_bwd", grid=(t // tm,),
        in_specs=[_row(tm, ad), _row(tm, kvd), _colblk(tm, ad, 0), _colblk(tm, kvd, 2 * ad // kvd),
                  _vec(HEAD_DIM), _vec(HEAD_DIM), _row(tm, HEAD_DIM), _row(tm, HEAD_DIM)],
        out_specs=[_row(tm, ad), _row(tm, kvd), pl.BlockSpec((SUBLANES, HEAD_DIM), lambda j: (0, 0))],
        out_shape=[_sds((t, ad), BF16), _sds((t, kvd), BF16), _sds((SUBLANES, HEAD_DIM), F32)],
        compiler_params=_cp("arbitrary"))(dqr, dkr, qgkv, qgkv, qg, kg, cos_t, sin_t)


_NT = (((1,), (1,)), ((), ()))


def flash_fwd(qr, kr, qgkv, ad, kvd, tm, scale):
    t = qr.shape[0]
    nkv = kvd // HEAD_DIM
    gw = ad // nkv
    hpg = gw // HEAD_DIM
    nt = t // tm
    v_blk0 = (2 * ad + kvd) // HEAD_DIM
    gate_blk0 = ad // gw

    def body(q_ref, k_ref, v_ref, g_ref, o_ref, og_ref, lse_ref):
        qi = pl.program_id(1)
        nkc = jnp.where(qi == 0, 1, nt)
        lane = lax.broadcasted_iota(jnp.int32, (tm, LANES), 1)
        lse_blk = jnp.zeros((tm, LANES), F32)
        for h in range(hpg):
            sl = slice(h * HEAD_DIM, (h + 1) * HEAD_DIM)
            q = q_ref[:, sl]

            def step(c, carry):
                m, l, acc = carry
                rows = pl.ds(pl.multiple_of(c * tm, tm), tm)
                s = lax.dot_general(q, k_ref[rows, :], _NT, preferred_element_type=F32) * scale
                m_new = jnp.maximum(m, jnp.max(s, axis=-1, keepdims=True))
                a = jnp.exp(m - m_new)
                p = jnp.exp(s - m_new)
                l = a * l + jnp.sum(p, axis=-1, keepdims=True)
                acc = a * acc + jnp.dot(p.astype(BF16), v_ref[rows, :], preferred_element_type=F32)
                return m_new, l, acc

            init = (jnp.full((tm, 1), -1e30, F32), jnp.zeros((tm, 1), F32), jnp.zeros((tm, HEAD_DIM), F32))
            m, l, acc = lax.fori_loop(0, nkc, step, init)
            o = acc / l
            o_ref[:, sl] = o.astype(o_ref.dtype)
            og_ref[:, sl] = (o * _silu(g_ref[:, sl].astype(F32))).astype(og_ref.dtype)
            lse_blk = jnp.where(lane == h, m + jnp.log(l), lse_blk)
        lse_ref[...] = lse_blk

    return pl.pallas_call(
        body, name="flash_fwd", grid=(nkv, nt),
        in_specs=[pl.BlockSpec((tm, gw), lambda g, i: (i, g)),
                  pl.BlockSpec((t, HEAD_DIM), lambda g, i: (0, g)),
                  pl.BlockSpec((t, HEAD_DIM), lambda g, i: (0, v_blk0 + g)),
                  pl.BlockSpec((tm, gw), lambda g, i: (i, gate_blk0 + g))],
        out_specs=[pl.BlockSpec((tm, gw), lambda g, i: (i, g)),
                   pl.BlockSpec((tm, gw), lambda g, i: (i, g)),
                   pl.BlockSpec((None, tm, LANES), lambda g, i: (g, i, 0))],
        out_shape=[_sds((t, ad), BF16), _sds((t, ad), BF16), _sds((nkv, t, LANES), F32)],
        compiler_params=_cp("parallel", "parallel"))(qr, kr, qgkv, qgkv)


def gate_bwd(dog, o, qgkv, ad, kvd, tm):
    t = o.shape[0]
    nkv = kvd // HEAD_DIM
    hpg = ad // nkv // HEAD_DIM

    def body(dog_ref, o_ref, g_ref, do_ref, dg_ref, dl_ref):
        lane = lax.broadcasted_iota(jnp.int32, (tm, LANES), 1)
        for grp in range(nkv):
            blk = jnp.zeros((tm, LANES), F32)
            for hh in range(hpg):
                h = grp * hpg + hh
                sl = slice(h * HEAD_DIM, (h + 1) * HEAD_DIM)
                dv = dog_ref[:, sl].astype(F32)
                ov = o_ref[:, sl].astype(F32)
                gv = g_ref[:, sl].astype(F32)
                doh = dv * _silu(gv)
                do_ref[:, sl] = doh.astype(do_ref.dtype)
                dg_ref[:, sl] = (dv * ov * _dsilu(gv)).astype(dg_ref.dtype)
                blk = jnp.where(lane == hh, jnp.sum(doh * ov, axis=-1, keepdims=True), blk)
            dl_ref[grp] = blk

    return pl.pallas_call(
        body, name="gate_bwd", grid=(t // tm,),
        in_specs=[_row(tm, ad), _row(tm, ad), _colblk(tm, ad, 1)],
        out_specs=[_row(tm, ad), _row(tm, ad), pl.BlockSpec((nkv, tm, LANES), lambda j: (0, j, 0))],
        out_shape=[_sds((t, ad), BF16), _sds((t, ad), BF16), _sds((nkv, t, LANES), F32)],
        compiler_params=_cp("parallel"))(dog, o, qgkv)


def flash_bwd_dq(qr, do, kr, qgkv, lse, delta, ad, kvd, tm, scale):
    t = qr.shape[0]
    nkv = kvd // HEAD_DIM
    gw = ad // nkv
    hpg = gw // HEAD_DIM
    nt = t // tm
    v_blk0 = (2 * ad + kvd) // HEAD_DIM

    def body(q_ref, do_ref, k_ref, v_ref, lse_ref, dl_ref, dq_ref):
        qi = pl.program_id(1)
        nkc = jnp.where(qi == 0, 1, nt)
        for h in range(hpg):
            sl = slice(h * HEAD_DIM, (h + 1) * HEAD_DIM)
            q = q_ref[:, sl]
            doh = do_ref[:, sl]
            lse_h = lse_ref[:, h:h + 1]
            dl_h = dl_ref[:, h:h + 1]

            def step(c, dq):
                rows = pl.ds(pl.multiple_of(c * tm, tm), tm)
                kc = k_ref[rows, :]
                s = lax.dot_general(q, kc, _NT, preferred_element_type=F32) * scale
                p = jnp.exp(s - lse_h)
                dp = lax.dot_general(doh, v_ref[rows, :], _NT, preferred_element_type=F32)
                ds = p * (dp - dl_h) * scale
                return dq + jnp.dot(ds.astype(BF16), kc, preferred_element_type=F32)

            dq_ref[:, sl] = lax.fori_loop(0, nkc, step, jnp.zeros((tm, HEAD_DIM), F32))

    return pl.pallas_call(
        body, name="flash_bwd_dq", grid=(nkv, nt),
        in_specs=[pl.BlockSpec((tm, gw), lambda g, i: (i, g)),
                  pl.BlockSpec((tm, gw), lambda g, i: (i, g)),
                  pl.BlockSpec((t, HEAD_DIM), lambda g, i: (0, g)),
                  pl.BlockSpec((t, HEAD_DIM), lambda g, i: (0, v_blk0 + g)),
                  pl.BlockSpec((None, tm, LANES), lambda g, i: (g, i, 0)),
                  pl.BlockSpec((None, tm, LANES), lambda g, i: (g, i, 0))],
        out_specs=pl.BlockSpec((tm, gw), lambda g, i: (i, g)),
        out_shape=_sds((t, ad), F32),
        compiler_params=_cp("parallel", "parallel"))(qr, do, kr, qgkv, lse, delta)


def flash_bwd_dkv(qr, do, kr, qgkv, lse_t, delta_t, ad, kvd, tm, scale):
    t = qr.shape[0]
    nkv = kvd // HEAD_DIM
    gw = ad // nkv
    hpg = gw // HEAD_DIM
    nt = t // tm
    v_blk0 = (2 * ad + kvd) // HEAD_DIM

    def body(k_ref, v_ref, q_ref, do_ref, lse_ref, dl_ref, dk_ref, dv_ref):
        ki = pl.program_id(1)
        kc = k_ref[...]
        vc = v_ref[...]

        def qstep(qc, carry):
            dk, dv = carry
            rows = pl.ds(pl.multiple_of(qc * tm, tm), tm)
            for h in range(hpg):
                sl = slice(h * HEAD_DIM, (h + 1) * HEAD_DIM)
                q = q_ref[rows, sl]
                doh = do_ref[rows, sl]
                st = lax.dot_general(kc, q, _NT, preferred_element_type=F32) * scale
                pt = jnp.exp(st - lse_ref[qc, h:h + 1, :])
                dv = dv + jnp.dot(pt.astype(BF16), doh, preferred_element_type=F32)
                dpt = lax.dot_general(vc, doh, _NT, preferred_element_type=F32)
                dst = pt * (dpt - dl_ref[qc, h:h + 1, :]) * scale
                dk = dk + jnp.dot(dst.astype(BF16), q, preferred_element_type=F32)
            return dk, dv

        zero = jnp.zeros((tm, HEAD_DIM), F32)
        dk, dv = lax.fori_loop(jnp.where(ki == 0, 0, 1), nt, qstep, (zero, zero))
        dk_ref[...] = dk
        dv_ref[...] = dv.astype(dv_ref.dtype)

    return pl.pallas_call(
        body, name="flash_bwd_dkv", grid=(nkv, nt),
        in_specs=[pl.BlockSpec((tm, HEAD_DIM), lambda g, i: (i, g)),
                  pl.BlockSpec((tm, HEAD_DIM), lambda g, i: (i, v_blk0 + g)),
                  pl.BlockSpec((t, gw), lambda g, i: (0, g)),
                  pl.BlockSpec((t, gw), lambda g, i: (0, g)),
                  pl.BlockSpec((None, nt, hpg, tm), lambda g, i: (g, 0, 0, 0)),
                  pl.BlockSpec((None, nt, hpg, tm), lambda g, i: (g, 0, 0, 0))],
        out_specs=[pl.BlockSpec((tm, HEAD_DIM), lambda g, i: (i, g)),
                   pl.BlockSpec((tm, HEAD_DIM), lambda g, i: (i, g))],
        out_shape=[_sds((t, kvd), F32), _sds((t, kvd), BF16)],
        compiler_params=_cp("parallel", "parallel"))(kr, qgkv, qr, do, lse_t, delta_t)


def _rows_per_head(a, tm, hpg):
    nkv, t, _ = a.shape
    return a[:, :, :hpg].reshape(nkv, t // tm, tm, hpg).transpose(0, 1, 3, 2)


def adaln_fwd(c16, w_mod):
    nlay, d, nl = w_mod.shape
    tn = _pick(nl, (512, 256, 128))

    def body(c_ref, w_ref, o_ref):
        o_ref[...] = jnp.dot(_silu(c_ref[...]), w_ref[...], preferred_element_type=F32,
                             precision=lax.Precision.HIGHEST)

    return pl.pallas_call(
        body, name="adaln_fwd", grid=(nlay, nl // tn),
        in_specs=[pl.BlockSpec((16, d), lambda l, j: (0, 0)),
                  pl.BlockSpec((None, d, tn), lambda l, j: (l, 0, j))],
        out_specs=pl.BlockSpec((None, 16, tn), lambda l, j: (l, 0, j)),
        out_shape=_sds((nlay, 16, nl), F32), compiler_params=_cp("parallel", "parallel"))(c16, w_mod)


def adaln_bwd(c16t, dm, w_mod):
    nlay, d, nl = w_mod.shape
    tn = _pick(nl, (512, 256, 128))

    def body(c_ref, dm_ref, w_ref, dw_ref, dc_ref):
        @pl.when(pl.program_id(1) == 0)
        def _():
            dc_ref[...] = jnp.zeros_like(dc_ref)

        dmv = dm_ref[...]
        dw_ref[...] = jnp.dot(_silu(c_ref[...]), dmv, preferred_element_type=F32,
                              precision=lax.Precision.HIGHEST)
        dc_ref[...] += lax.dot_general(dmv, w_ref[...], _NT, preferred_element_type=F32,
                                       precision=lax.Precision.HIGHEST)

    return pl.pallas_call(
        body, name="adaln_bwd", grid=(nlay, nl // tn),
        in_specs=[pl.BlockSpec((d, 16), lambda l, j: (0, 0)),
                  pl.BlockSpec((None, 16, tn), lambda l, j: (l, 0, j)),
                  pl.BlockSpec((None, d, tn), lambda l, j: (l, 0, j))],
        out_specs=[pl.BlockSpec((None, d, tn), lambda l, j: (l, 0, j)),
                   pl.BlockSpec((None, 16, d), lambda l, j: (l, 0, 0))],
        out_shape=[_sds((nlay, d, nl), F32), _sds((nlay, 16, d), F32)],
        compiler_params=_cp("parallel", "arbitrary"))(c16t, dm, w_mod)


def sum_leading(a, name):
    n = a.shape[0]

    def body(a_ref, o_ref):
        acc = a_ref[0]
        for i in range(1, n):
            acc = acc + a_ref[i]
        o_ref[...] = acc

    return pl.pallas_call(body, name=name, in_specs=[VMEM_FULL], out_specs=VMEM_FULL,
                          out_shape=_sds(a.shape[1:], F32),
                          compiler_params=pltpu.CompilerParams(vmem_limit_bytes=VMEM_LIMIT))(a)


def c_ctx_grad(parts, c_ctx):
    n = parts.shape[0]

    def body(p_ref, c_ref, o_ref):
        acc = p_ref[0]
        for i in range(1, n):
            acc = acc + p_ref[i]
        o_ref[...] = acc * _dsilu(c_ref[...])

    return pl.pallas_call(body, name="c_ctx_grad", in_specs=[VMEM_FULL, VMEM_FULL], out_specs=VMEM_FULL,
                          out_shape=_sds(c_ctx.shape, F32))(parts, c_ctx)


def _as2d(a):
    return a.reshape(-1, a.shape[-1])


def _row_tile(r, c):
    for tr in (1024, 512, 256, 128, 64, 32, 16, 8):
        if r % tr == 0 and tr * c * 4 <= (1 << 20):
            return tr
    return r


def add_n(arrs, name):
    shape = arrs[0].shape
    flat = [_as2d(a) for a in arrs]
    r, c = flat[0].shape
    tr = _row_tile(r, c)

    def body(*refs):
        acc = refs[0][...]
        for ref in refs[1:-1]:
            acc = acc + ref[...]
        refs[-1][...] = acc

    out = pl.pallas_call(
        body, name=name, grid=(r // tr,),
        in_specs=[pl.BlockSpec((tr, c), lambda i: (i, 0))] * len(flat),
        out_specs=pl.BlockSpec((tr, c), lambda i: (i, 0)),
        out_shape=_sds((r, c), F32), compiler_params=_cp("parallel"))(*flat)
    return out.reshape(shape)


def adamw(w, g, m, v):
    shape = w.shape
    flat = [_as2d(a.reshape((1,) + shape) if len(shape) == 1 else a) for a in (w, g, m, v)]
    r, c = flat[0].shape
    tr = _row_tile(r, c)
    c1 = 1.0 - ADAM_B1 ** ADAM_STEP
    c2 = 1.0 - ADAM_B2 ** ADAM_STEP

    def body(w_ref, g_ref, m_ref, v_ref, d_ref, nm_ref, nv_ref):
        gv = g_ref[...]
        nm = ADAM_B1 * m_ref[...] + (1.0 - ADAM_B1) * gv
        nv = ADAM_B2 * v_ref[...] + (1.0 - ADAM_B2) * (gv * gv)
        d_ref[...] = -ADAM_LR * ((nm / c1) / (jnp.sqrt(nv / c2) + ADAM_EPS) + ADAM_WD * w_ref[...])
        nm_ref[...] = nm
        nv_ref[...] = nv

    spec = pl.BlockSpec((tr, c), lambda i: (i, 0))
    outs = pl.pallas_call(
        body, name="adamw", grid=(r // tr,), in_specs=[spec] * 4, out_specs=[spec] * 3,
        out_shape=[_sds((r, c), F32)] * 3, compiler_params=_cp("parallel"))(*flat)
    return tuple(o.reshape(shape) for o in outs)


def _place():
    return lax.axis_index("x"), lax.axis_index("y"), lax.axis_index("c")


def _remote(src, dst, ssem, rsem, dev):
    return pltpu.make_async_remote_copy(src_ref=src, dst_ref=dst, send_sem=ssem, recv_sem=rsem,
                                        device_id=dev, device_id_type=MESH)


def all_gather8(v, name):
    m_per, n = v.shape

    def body(x_ref, out_ref, send_sems, recv_sems, local_sem):
        x, y, c = _place()
        me, sibling = (x, y, c), (x, y, 1 - c)
        chips = [(1 - x, y), (x, 1 - y), (1 - x, 1 - y)]

        def rows(px, py, pc):
            return out_ref.at[pl.ds((4 * px + 2 * py + pc) * m_per, m_per), :]

        def copy(k, block, to, src=None):
            return _remote(rows(*block) if src is None else src, rows(*block),
                           send_sems.at[k], recv_sems.at[k], to)

        mine = pltpu.make_async_copy(x_ref, rows(*me), local_sem)
        mine.start()
        first = [copy(0, me, sibling, src=x_ref)]
        first += [copy(1 + j, me, (*chip, c), src=x_ref) for j, chip in enumerate(chips)]
        for cp in first:
            cp.start()
        passed = [copy(4 + j, (*chip, c), sibling) for j, chip in enumerate(chips)]
        for j, chip in enumerate(chips):
            copy(1 + j, (*chip, c), me).wait_recv()
            passed[j].start()
        copy(0, sibling, me).wait_recv()
        for j, chip in enumerate(chips):
            copy(4 + j, (*chip, 1 - c), me).wait_recv()
        for cp in first + passed:
            cp.wait_send()
        mine.wait()

    return pl.pallas_call(
        body, name=name, out_shape=_sds((8 * m_per, n), v.dtype),
        in_specs=[VMEM_FULL], out_specs=VMEM_FULL,
        scratch_shapes=[pltpu.SemaphoreType.DMA((7,)), pltpu.SemaphoreType.DMA((7,)), pltpu.SemaphoreType.DMA],
        compiler_params=pltpu.CompilerParams(vmem_limit_bytes=VMEM_LIMIT))(v)


def gather_weight(w, name):
    _, r, cdim = w.shape

    def body(w_ref, out_ref, send_sems, recv_sems, local_sems):
        x, y, c = _place()
        sibling = (x, y, 1 - c)
        chips = [(1 - x, y), (x, 1 - y), (1 - x, 1 - y)]
        mine = 2 * x + y
        local = [pltpu.make_async_copy(w_ref.at[i], out_ref.at[i, mine], local_sems.at[i]) for i in range(2)]
        for cp in local:
            cp.start()
        first = [_remote(w_ref.at[c], out_ref.at[c, mine], send_sems.at[k], recv_sems.at[k], (*chip, c))
                 for k, chip in enumerate(chips)]
        for cp in first:
            cp.start()
        passed = []
        for k, (px, py) in enumerate(chips):
            theirs = out_ref.at[c, 2 * px + py]
            _remote(w_ref.at[c], theirs, send_sems.at[k], recv_sems.at[k], sibling).wait_recv()
            cp = _remote(theirs, theirs, send_sems.at[3 + k], recv_sems.at[3 + k], sibling)
            cp.start()
            passed.append(cp)
        for k, (px, py) in enumerate(chips):
            other = out_ref.at[1 - c, 2 * px + py]
            _remote(other, other, send_sems.at[3 + k], recv_sems.at[3 + k], sibling).wait_recv()
        for cp in first + passed:
            cp.wait_send()
        for cp in local:
            cp.wait()

    return pl.pallas_call(
        body, name=name, out_shape=_sds((2, 4, r, cdim), w.dtype), in_specs=[ANY], out_specs=ANY,
        scratch_shapes=[pltpu.SemaphoreType.DMA((6,)), pltpu.SemaphoreType.DMA((6,)),
                        pltpu.SemaphoreType.DMA((2,))])(w)


def rs_pair_swap(g0, g1, name):
    def body(g0_ref, g1_ref, own_ref, st_ref, send_sem, recv_sem, local_sem):
        x, y, c = _place()
        sibling = (x, y, 1 - c)

        @pl.when(c == 0)
        def _():
            keep = pltpu.make_async_copy(g0_ref, own_ref, local_sem)
            keep.start()
            cp = _remote(g1_ref, st_ref, send_sem, recv_sem, sibling)
            cp.start()
            cp.wait()
            keep.wait()

        @pl.when(c == 1)
        def _():
            keep = pltpu.make_async_copy(g1_ref, own_ref, local_sem)
            keep.start()
            cp = _remote(g0_ref, st_ref, send_sem, recv_sem, sibling)
            cp.start()
            cp.wait()
            keep.wait()

    return pl.pallas_call(
        body, name=name, out_shape=[_sds(g0.shape, F32), _sds(g0.shape, F32)],
        in_specs=[ANY, ANY], out_specs=[ANY, ANY],
        scratch_shapes=[pltpu.SemaphoreType.DMA, pltpu.SemaphoreType.DMA, pltpu.SemaphoreType.DMA])(g0, g1)


def rs_chip_exchange(ps, name):
    _, r, cdim = ps.shape

    def body(ps_ref, mine_ref, st_ref, send_sems, recv_sems, local_sem):
        x, y, c = _place()
        chips = [(1 - x, y), (x, 1 - y), (1 - x, 1 - y)]
        keep = pltpu.make_async_copy(ps_ref.at[2 * x + y], mine_ref, local_sem)
        keep.start()
        cps = [_remote(ps_ref.at[2 * px + py], st_ref.at[k], send_sems.at[k], recv_sems.at[k], (px, py, c))
               for k, (px, py) in enumerate(chips)]
        for cp in cps:
            cp.start()
        for cp in cps:
            cp.wait()
        keep.wait()

    return pl.pallas_call(
        body, name=name, out_shape=[_sds((r, cdim), F32), _sds((3, r, cdim), F32)],
        in_specs=[ANY], out_specs=[ANY, ANY],
        scratch_shapes=[pltpu.SemaphoreType.DMA((3,)), pltpu.SemaphoreType.DMA((3,)), pltpu.SemaphoreType.DMA])(ps)


def rs_pair_share(red, name):
    def body(red_ref, out_ref, send_sem, recv_sem, local_sem):
        x, y, c = _place()
        sibling = (x, y, 1 - c)
        keep = pltpu.make_async_copy(red_ref, out_ref.at[c], local_sem)
        keep.start()
        cp = _remote(red_ref, out_ref.at[c], send_sem, recv_sem, sibling)
        cp.start()
        cp.wait_send()
        _remote(red_ref, out_ref.at[1 - c], send_sem, recv_sem, sibling).wait_recv()
        keep.wait()

    return pl.pallas_call(
        body, name=name, out_shape=_sds((2,) + red.shape, F32), in_specs=[ANY], out_specs=ANY,
        scratch_shapes=[pltpu.SemaphoreType.DMA, pltpu.SemaphoreType.DMA, pltpu.SemaphoreType.DMA])(red)


def reduce_scatter_grad(g0, g1, tag):
    own, theirs = rs_pair_swap(g0, g1, "rs_swap_" + tag)
    ps = add_n([own, theirs], "rs_pair_add_" + tag)
    mine, others = rs_chip_exchange(ps, "rs_exchange_" + tag)
    red = add_n([mine, others[0], others[1], others[2]], "rs_shard_add_" + tag)
    return rs_pair_share(red, "rs_share_" + tag)


def _rope_tables(lc, s):
    rows_n = s // GRID_W
    row = jnp.repeat(jnp.arange(rows_n, dtype=F32), GRID_W)
    col = jnp.tile(jnp.arange(GRID_W, dtype=F32), rows_n)
    axis_dim = HEAD_DIM // 2
    inv_freq = ROPE_THETA ** (-jnp.arange(0, axis_dim, 2, dtype=F32) / axis_dim)
    ang_r = row[:, None] * inv_freq[None, :]
    ang_c = col[:, None] * inv_freq[None, :]
    cr, sr, cc, sc = jnp.cos(ang_r), jnp.sin(ang_r), jnp.cos(ang_c), jnp.sin(ang_c)
    cos_l = jnp.concatenate([cr, cr, cc, cc], axis=1)
    sin_l = jnp.concatenate([-sr, sr, -sc, sc], axis=1)
    cos_t = jnp.concatenate([jnp.ones((lc, HEAD_DIM), F32), cos_l], axis=0)
    sin_t = jnp.concatenate([jnp.zeros((lc, HEAD_DIM), F32), sin_l], axis=0)
    return cos_t, sin_t


def _pad_rows(a, rows):
    return jnp.concatenate([a, jnp.zeros((rows - a.shape[0],) + a.shape[1:], a.dtype)], axis=0)


def _pad_cols(a, cols):
    return jnp.concatenate([a, jnp.zeros(a.shape[:-1] + (cols - a.shape[-1],), a.dtype)], axis=-1)


def kernel(x, c, ctx, c_ctx, w_mod, b_mod, post_ln_g, post_ln_b, w_in_e, conv_a_w, conv_a_b, norm_a_g, norm_a_b, conv_b_w, w_out_e, w_in_o, q_norm_g, k_norm_g, w_out_o, loss_target, m_c_ctx, m_w_mod, m_b_mod, m_post_ln_g, m_post_ln_b, m_w_in_e, m_conv_a_w, m_conv_a_b, m_norm_a_g, m_norm_a_b, m_conv_b_w, m_w_out_e, m_w_in_o, m_q_norm_g, m_k_norm_g, m_w_out_o, v_c_ctx, v_w_mod, v_b_mod, v_post_ln_g, v_post_ln_b, v_w_in_e, v_conv_a_w, v_conv_a_b, v_norm_a_g, v_norm_a_b, v_conv_b_w, v_w_out_e, v_w_in_o, v_q_norm_g, v_k_norm_g, v_w_out_o):
    s, d = x.shape[1], x.shape[2]
    lc = ctx.shape[1]
    t = lc + s
    tm = lc
    depth = w_mod.shape[0]
    n_even, n_odd = w_in_e.shape[0], w_in_o.shape[0]
    ad = w_out_o.shape[1] * 4
    kvd = (w_in_o.shape[2] * 4 - 2 * ad) // 2
    nkv = kvd // HEAD_DIM
    hpg = ad // nkv // HEAD_DIM
    nlm = w_mod.shape[2]
    alpha = (2.0 * depth) ** 0.25
    scale = HEAD_DIM ** -0.5
    assert n_even == 2 and n_odd == 2 and depth == 4 and hpg == GQA_GROUP
    assert lc % CONV_ROWS == 0 and s % tm == 0 and d % LANES == 0

    xi, yi, ci = _place()
    shard = 2 * xi + yi
    dev = 4 * xi + 2 * yi + ci

    wg_in_e = gather_weight(w_in_e.astype(BF16), "gather_w_in_e")
    wg_out_e = gather_weight(w_out_e.astype(BF16), "gather_w_out_e")
    wg_in_o = gather_weight(w_in_o.astype(BF16), "gather_w_in_o")
    wg_out_o = gather_weight(w_out_o.astype(BF16), "gather_w_out_o")

    c_all = all_gather8(_pad_rows(c, 8), "gather_c")
    c16 = _pad_rows(jnp.concatenate([c_all[0::8], c_ctx[None, :]], axis=0), 16)
    m_part = adaln_fwd(c16, w_mod)
    m_all = all_gather8(m_part.reshape(depth * 16, nlm), "gather_mod")
    m_all = m_all.reshape(8, depth, 16, nlm)[0::2]
    m_full = m_all.transpose(1, 2, 0, 3).reshape(depth, 16, 4 * nlm) + b_mod[:, None, :]
    m_lat = lax.dynamic_index_in_dim(m_full, dev, axis=1, keepdims=False)
    m_ctx = m_full[:, 8]

    def seg2(l, part):
        return jnp.stack([m_ctx[l, part * d:(part + 1) * d], m_lat[l, part * d:(part + 1) * d]])[:, None, :]

    cos_t, sin_t = _rope_tables(lc, s)

    small_gathered = all_gather8(
        _pad_rows(jnp.concatenate([conv_a_w.reshape(n_even * CONV_A_TAPS, -1),
                                   conv_b_w.reshape(n_even * CONV_B_TAPS, -1)], axis=0), 72), "gather_taps")
    taps = small_gathered.reshape(8, 72, -1)[0::2]
    taps = taps.transpose(1, 0, 2).reshape(72, d)
    caw = taps[:n_even * CONV_A_TAPS].reshape(n_even, CONV_A_TAPS, d)
    cbw = taps[n_even * CONV_A_TAPS:n_even * (CONV_A_TAPS + CONV_B_TAPS)].reshape(n_even, CONV_B_TAPS, d)
    caw_pad = jnp.concatenate([caw, jnp.zeros((n_even, 32 - CONV_A_TAPS, d), F32)], axis=1)
    cbw_pad = jnp.concatenate([cbw, jnp.zeros((n_even, 8 - CONV_B_TAPS, d), F32)], axis=1)

    xc = jnp.concatenate([ctx[0], x[0]], axis=0)
    saved = []
    for l in range(depth):
        i = l // 2
        shift2, scale2, gate2 = seg2(l, 0), seg2(l, 1), seg2(l, 2)
        h = mod_fwd(xc, scale2, shift2, tm)
        if l % 2 == 0:
            p = mm_nn(h, wg_in_e[i], BF16)
            u1 = conv_a_fwd(p, caw_pad[i], conv_a_b[i][None, :], lc, d)
            a_out = ln_a_fwd(u1, p, norm_a_g[i][None, :], norm_a_b[i][None, :], tm)
            b_out = conv_b_fwd(p, cbw_pad[i], lc, d)
            ab = jnp.concatenate([a_out, b_out], axis=1)
            y = mm_nn(ab, wg_out_e[i].reshape(1, 2 * d, d), F32)
            saved.append(dict(xc=xc, h=h, p=p, u1=u1, ab=ab, y=y))
        else:
            qgkv = mm_nn(h, wg_in_o[i], BF16)
            qr, kr = qk_fwd(qgkv, q_norm_g[i][None, :], k_norm_g[i][None, :], cos_t, sin_t, ad, kvd, tm)
            o, og, lse = flash_fwd(qr, kr, qgkv, ad, kvd, tm, scale)
            y = mm_nn(og, wg_out_o[i].reshape(1, ad, d), F32)
            saved.append(dict(xc=xc, h=h, qgkv=qgkv, qr=qr, kr=kr, o=o, og=og, lse=lse, y=y))
        xc = post_ln_fwd(xc, y, gate2, post_ln_g[l][None, :], post_ln_b[l][None, :], alpha, tm)

    dxc, loss_acc = loss_head(xc, loss_target[0], lc, tm)
    loss = lax.psum(0.5 / d * jnp.sum(loss_acc[0]), MESH_AXES)

    g_in_e, g_out_e, g_in_o, g_out_o = [None] * 2, [None] * 2, [None] * 2, [None] * 2
    d_mod_lat, d_mod_ctx = [None] * depth, [None] * depth
    d_pln_g, d_pln_b = [None] * depth, [None] * depth
    d_cab, d_nag, d_nab, d_caw, d_cbw = [None] * 2, [None] * 2, [None] * 2, [None] * 2, [None] * 2
    d_qg, d_kg = [None] * 2, [None] * 2
    for l in reversed(range(depth)):
        i = l // 2
        sv = saved[l]
        scale2, gate2 = seg2(l, 1), seg2(l, 2)
        dzx, dy, acc_ln = post_ln_bwd(dxc, sv["xc"], sv["y"], gate2, post_ln_g[l][None, :], alpha, tm)
        d_pln_g[l] = acc_ln[0, 1] + acc_ln[1, 1]
        d_pln_b[l] = acc_ln[0, 2] + acc_ln[1, 2]
        if l % 2 == 0:
            w_out3 = wg_out_e[i].reshape(1, 2 * d, d)
            dab = mm_nt(dy, w_out3, BF16)
            g_out_e[i] = mm_tn(sv["ab"], dy, 1).reshape(4, 2 * d // 4, d)
            du1, d_agate, acc_a = ln_a_bwd(dab, sv["u1"], sv["p"], norm_a_g[i][None, :], norm_a_b[i][None, :], tm)
            d_nag[i], d_nab[i], d_cab[i] = acc_a[0], acc_a[1], acc_a[2]
            d_aval, d_aglu, d_caw[i] = conv_a_bwd(du1, sv["p"], caw_pad[i], lc, d)
            d_bx, d_bb, d_bc, d_bg, d_cbw[i] = conv_b_bwd(dab, sv["p"], cbw_pad[i], lc, d)
            dp = jnp.concatenate([d_aval, d_aglu, d_agate, d_bx, d_bb, d_bc, d_bg], axis=1)
            dh = mm_nt(dp, wg_in_e[i], F32)
            g_in_e[i] = mm_tn(sv["h"], dp, 4)
        else:
            w_out3 = wg_out_o[i].reshape(1, ad, d)
            dog = mm_nt(dy, w_out3, BF16)
            g_out_o[i] = mm_tn(sv["og"], dy, 1).reshape(4, ad // 4, d)
            do, dgate, delta = gate_bwd(dog, sv["o"], sv["qgkv"], ad, kvd, tm)
            dqr = flash_bwd_dq(sv["qr"], do, sv["kr"], sv["qgkv"], sv["lse"], delta, ad, kvd, tm, scale)
            dkr, dv = flash_bwd_dkv(sv["qr"], do, sv["kr"], sv["qgkv"], _rows_per_head(sv["lse"], tm, hpg),
                                    _rows_per_head(delta, tm, hpg), ad, kvd, tm, scale)
            dq, dk, acc_qk = qk_bwd(dqr, dkr, sv["qgkv"], q_norm_g[i][None, :], k_norm_g[i][None, :],
                                    cos_t, sin_t, ad, kvd, tm)
            d_qg[i], d_kg[i] = acc_qk[0], acc_qk[1]
            dqgkv = jnp.concatenate([dq, dgate, dk, dv], axis=1)
            dh = mm_nt(dqgkv, wg_in_o[i], F32)
            g_in_o[i] = mm_tn(sv["h"], dqgkv, 4)
        dxc, acc_mod = mod_bwd(dh, dzx, sv["xc"], scale2, tm)
        d_mod_ctx[l] = jnp.stack([acc_mod[0, 0], acc_mod[0, 1], acc_ln[0, 0]])
        d_mod_lat[l] = jnp.stack([acc_mod[1, 0], acc_mod[1, 1], acc_ln[1, 0]])
    grad_x = dxc[lc:][None]

    pack = jnp.concatenate(
        [jnp.concatenate(d_mod_ctx, axis=0),
         jnp.stack(d_pln_g), jnp.stack(d_pln_b),
         jnp.stack(d_cab), jnp.stack(d_nag), jnp.stack(d_nab),
         jnp.concatenate(d_caw, axis=0),
         jnp.concatenate(d_cbw, axis=0),
         _pad_cols(jnp.stack(d_qg), d), _pad_cols(jnp.stack(d_kg), d),
         jnp.zeros((2, d), F32),
         jnp.concatenate(d_mod_lat, axis=0),
         jnp.zeros((4, d), F32)], axis=0)
    gathered = all_gather8(pack, "gather_small").reshape(8, 128, d)
    small = sum_leading(gathered, "sum_small")
    dm_ctx = small[0:12].reshape(depth, 1, 3 * d)
    dm_lat = gathered[:, 112:124].reshape(8, depth, 3 * d).transpose(1, 0, 2)
    dm = jnp.concatenate([dm_lat, dm_ctx, jnp.zeros((depth, 7, 3 * d), F32)], axis=1)
    g_b_mod = sum_leading(dm.transpose(1, 0, 2), "sum_b_mod")
    dm_shard = lax.dynamic_slice_in_dim(dm, shard * nlm, nlm, axis=2)
    g_w_mod, dc_part = adaln_bwd(c16.T, dm_shard, w_mod)
    dc_all = all_gather8(_pad_rows(dc_part[:, 8, :], 8), "gather_dc").reshape(8, 8, d)
    g_c_ctx = c_ctx_grad(dc_all[0::2, :depth].reshape(4 * depth, 1, d), c_ctx[None, :])[0]

    g_pln_g, g_pln_b = small[12:16], small[16:20]
    g_cab, g_nag, g_nab = small[20:22], small[22:24], small[24:26]
    dch = d // 4
    g_caw = lax.dynamic_slice_in_dim(small[26:90].reshape(2, 32, d)[:, :CONV_A_TAPS], shard * dch, dch, axis=2)
    g_cbw = lax.dynamic_slice_in_dim(small[90:106].reshape(2, 8, d)[:, :CONV_B_TAPS], shard * dch, dch, axis=2)
    g_qg, g_kg = small[106:108, :HEAD_DIM], small[108:110, :HEAD_DIM]

    g_w_in_e = reduce_scatter_grad(g_in_e[0], g_in_e[1], "in_e")
    g_w_out_e = reduce_scatter_grad(g_out_e[0], g_out_e[1], "out_e")
    g_w_in_o = reduce_scatter_grad(g_in_o[0], g_in_o[1], "in_o")
    g_w_out_o = reduce_scatter_grad(g_out_o[0], g_out_o[1], "out_o")

    grads = [g_c_ctx, g_w_mod, g_b_mod, g_pln_g, g_pln_b, g_w_in_e, g_caw, g_cab, g_nag, g_nab, g_cbw,
             g_w_out_e, g_w_in_o, g_qg, g_kg, g_w_out_o]
    weights = [c_ctx, w_mod, b_mod, post_ln_g, post_ln_b, w_in_e, conv_a_w, conv_a_b, norm_a_g, norm_a_b,
               conv_b_w, w_out_e, w_in_o, q_norm_g, k_norm_g, w_out_o]
    ms = [m_c_ctx, m_w_mod, m_b_mod, m_post_ln_g, m_post_ln_b, m_w_in_e, m_conv_a_w, m_conv_a_b, m_norm_a_g,
          m_norm_a_b, m_conv_b_w, m_w_out_e, m_w_in_o, m_q_norm_g, m_k_norm_g, m_w_out_o]
    vs = [v_c_ctx, v_w_mod, v_b_mod, v_post_ln_g, v_post_ln_b, v_w_in_e, v_conv_a_w, v_conv_a_b, v_norm_a_g,
          v_norm_a_b, v_conv_b_w, v_w_out_e, v_w_in_o, v_q_norm_g, v_k_norm_g, v_w_out_o]
    deltas, new_ms, new_vs = [], [], []
    for wv, gv, mv, vv in zip(weights, grads, ms, vs):
        dl, nm, nv = adamw(wv, gv, mv, vv)
        deltas.append(dl)
        new_ms.append(nm)
        new_vs.append(nv)
    return (loss, grad_x, *grads, *deltas, *new_ms, *new_vs)
```

```python
import functools

import jax
import jax.numpy as jnp
from jax import lax
from jax.experimental import pallas as pl
from jax.experimental.pallas import tpu as pltpu

F32 = jnp.float32
BF16 = jnp.bfloat16

LANES = 128
SUBLANES = 8
HEAD_DIM = 128
GQA_GROUP = 4
GRID_W = 64
ROPE_THETA = 10000.0
LN_EPS = 1e-5
RMS_EPS = 1e-6
CONV_A_TAPS = 31
CONV_B_TAPS = 3
HALO = 16
CONV_ROWS = 128
ADAM_LR = 0.001
ADAM_B1 = 0.9
ADAM_B2 = 0.999
ADAM_EPS = 1e-08
ADAM_WD = 0.01
ADAM_STEP = 10
VMEM_LIMIT = 56 * 1024 * 1024
D2D_PARTS = 4
LOG2_E = 1.4426950408889634
LN_2 = 0.6931471805599453
MESH_AXES = ("x", "y", "c")
MESH = pl.DeviceIdType.MESH
ANY = pl.BlockSpec(memory_space=pl.ANY)
VMEM_FULL = pl.BlockSpec(memory_space=pltpu.VMEM)


def _sds(shape, dtype):
    return jax.ShapeDtypeStruct(tuple(shape), dtype)


def _cp(*sem):
    return pltpu.CompilerParams(dimension_semantics=sem, vmem_limit_bytes=VMEM_LIMIT)


def _pick(n, cands):
    for c in cands:
        if n % c == 0:
            return c
    return n


def _sigmoid(x):
    return 1.0 / (1.0 + jnp.exp(-x))


def _silu(x):
    return x * _sigmoid(x)


def _dsilu(x):
    s = _sigmoid(x)
    return s * (1.0 + x * (1.0 - s))


def _row(tm, d):
    return pl.BlockSpec((tm, d), lambda j: (j, 0))


def _seg(d):
    return pl.BlockSpec((None, 1, d), lambda j: (jnp.minimum(j, 1), 0, 0))


def _vec(d):
    return pl.BlockSpec((1, d), lambda j: (0, 0))


def _colblk(tm, width, blk):
    return pl.BlockSpec((tm, width), lambda j: (j, blk))


def _seg_acc(d):
    return pl.BlockSpec((None, SUBLANES, d), lambda j: (jnp.minimum(j, 1), 0, 0))


def _ln_stats(z):
    mu = jnp.mean(z, axis=-1, keepdims=True)
    zc = z - mu
    var = jnp.mean(zc * zc, axis=-1, keepdims=True)
    rstd = lax.rsqrt(var + LN_EPS)
    return zc * rstd, rstd


def _ln_bwd(dxh, xhat, rstd):
    m1 = jnp.mean(dxh, axis=-1, keepdims=True)
    m2 = jnp.mean(dxh * xhat, axis=-1, keepdims=True)
    return rstd * (dxh - m1 - xhat * m2)


def _colsum(v):
    return jnp.sum(v, axis=0, keepdims=True)


def mod_fwd(xc, scale2, shift2, tm):
    t, d = xc.shape

    def body(x_ref, sc_ref, sh_ref, h_ref):
        h_ref[...] = (x_ref[...] * (1.0 + sc_ref[...]) + sh_ref[...]).astype(h_ref.dtype)

    return pl.pallas_call(
        body, name="mod_fwd", grid=(t // tm,),
        in_specs=[_row(tm, d), _seg(d), _seg(d)], out_specs=_row(tm, d),
        out_shape=_sds((t, d), BF16), compiler_params=_cp("parallel"))(xc, scale2, shift2)


def post_ln_fwd(xc, y, gate2, g, b, alpha, tm):
    t, d = xc.shape

    def body(x_ref, y_ref, gt_ref, g_ref, b_ref, o_ref):
        z = alpha * x_ref[...] + gt_ref[...] * y_ref[...]
        xhat, _ = _ln_stats(z)
        o_ref[...] = xhat * g_ref[...] + b_ref[...]

    return pl.pallas_call(
        body, name="post_ln_fwd", grid=(t // tm,),
        in_specs=[_row(tm, d), _row(tm, d), _seg(d), _vec(d), _vec(d)], out_specs=_row(tm, d),
        out_shape=_sds((t, d), F32), compiler_params=_cp("parallel"))(xc, y, gate2, g, b)


def post_ln_bwd(dout, xc, y, gate2, g, alpha, tm):
    t, d = xc.shape

    def body(do_ref, x_ref, y_ref, gt_ref, g_ref, dzx_ref, dy_ref, acc_ref):
        @pl.when(pl.program_id(0) <= 1)
        def _():
            acc_ref[...] = jnp.zeros_like(acc_ref)

        yv = y_ref[...]
        gate = gt_ref[...]
        xhat, rstd = _ln_stats(alpha * x_ref[...] + gate * yv)
        dout = do_ref[...]
        dz = _ln_bwd(dout * g_ref[...], xhat, rstd)
        dzx_ref[...] = alpha * dz
        dy_ref[...] = (gate * dz).astype(dy_ref.dtype)
        acc_ref[0:1, :] += _colsum(dz * yv)
        acc_ref[1:2, :] += _colsum(dout * xhat)
        acc_ref[2:3, :] += _colsum(dout)

    return pl.pallas_call(
        body, name="post_ln_bwd", grid=(t // tm,),
        in_specs=[_row(tm, d), _row(tm, d), _row(tm, d), _seg(d), _vec(d)],
        out_specs=[_row(tm, d), _row(tm, d), _seg_acc(d)],
        out_shape=[_sds((t, d), F32), _sds((t, d), BF16), _sds((2, SUBLANES, d), F32)],
        compiler_params=_cp("arbitrary"))(dout, xc, y, gate2, g)


def mod_bwd(dh, dzx, xc, scale2, tm):
    t, d = xc.shape

    def body(dh_ref, dzx_ref, x_ref, sc_ref, dx_ref, acc_ref):
        @pl.when(pl.program_id(0) <= 1)
        def _():
            acc_ref[...] = jnp.zeros_like(acc_ref)

        dhv = dh_ref[...].astype(F32)
        dx_ref[...] = dzx_ref[...] + dhv * (1.0 + sc_ref[...])
        acc_ref[0:1, :] += _colsum(dhv)
        acc_ref[1:2, :] += _colsum(dhv * x_ref[...])

    return pl.pallas_call(
        body, name="mod_bwd", grid=(t // tm,),
        in_specs=[_row(tm, d), _row(tm, d), _row(tm, d), _seg(d)],
        out_specs=[_row(tm, d), _seg_acc(d)],
        out_shape=[_sds((t, d), F32), _sds((2, SUBLANES, d), F32)],
        compiler_params=_cp("arbitrary"))(dh, dzx, xc, scale2)


def loss_head(xc, target, lc, tm):
    t, d = xc.shape

    def body(x_ref, t_ref, dx_ref, acc_ref):
        j = pl.program_id(0)

        @pl.when(j == 0)
        def _():
            acc_ref[...] = jnp.zeros_like(acc_ref)
            dx_ref[...] = jnp.zeros_like(dx_ref)

        @pl.when(j > 0)
        def _():
            err = x_ref[...] - t_ref[...]
            dx_ref[...] = err * (1.0 / d)
            col = _colsum(err * err)
            tot = col[:, 0:LANES]
            for k in range(1, d // LANES):
                tot = tot + col[:, k * LANES:(k + 1) * LANES]
            acc_ref[0:1, :] += tot

    nlc = lc // tm
    return pl.pallas_call(
        body, name="loss_head", grid=(t // tm,),
        in_specs=[_row(tm, d), pl.BlockSpec((tm, d), lambda j: (jnp.maximum(j - nlc, 0), 0))],
        out_specs=[_row(tm, d), pl.BlockSpec((SUBLANES, LANES), lambda j: (0, 0))],
        out_shape=[_sds((t, d), F32), _sds((SUBLANES, LANES), F32)],
        compiler_params=_cp("arbitrary"))(xc, target)


def mm_nn(a, w3, out_dtype):
    m, k = a.shape
    ns, _, nl = w3.shape
    tm = _pick(m, (768, 512, 256, 128))
    tn = _pick(nl, (512, 256, 128))
    npj = nl // tn

    def body(a_ref, w_ref, o_ref):
        o_ref[...] = jnp.dot(a_ref[...], w_ref[...], preferred_element_type=F32).astype(o_ref.dtype)

    return pl.pallas_call(
        body, name="mm_nn", grid=(m // tm, ns * npj),
        in_specs=[pl.BlockSpec((tm, k), lambda i, j: (i, 0)),
                  pl.BlockSpec((None, k, tn), lambda i, j: (j // npj, 0, j % npj))],
        out_specs=pl.BlockSpec((tm, tn), lambda i, j: (i, j)),
        out_shape=_sds((m, ns * nl), out_dtype), compiler_params=_cp("parallel", "parallel"))(a, w3)


def mm_nt(a, w3, out_dtype):
    m, _ = a.shape
    ns, k, nl = w3.shape
    tm = _pick(m, (768, 512, 256, 128))
    tk = _pick(k, (2048, 1024, 512, 256, 128))
    tn = _pick(nl, (512, 256, 128))
    npj = nl // tn
    nsteps = ns * npj

    def body(a_ref, w_ref, o_ref, acc_ref):
        n = pl.program_id(2)

        @pl.when(n == 0)
        def _():
            acc_ref[...] = jnp.zeros_like(acc_ref)

        acc_ref[...] += lax.dot_general(a_ref[...], w_ref[...], (((1,), (1,)), ((), ())),
                                        preferred_element_type=F32)

        @pl.when(n == nsteps - 1)
        def _():
            o_ref[...] = acc_ref[...].astype(o_ref.dtype)

    return pl.pallas_call(
        body, name="mm_nt", grid=(m // tm, k // tk, nsteps),
        in_specs=[pl.BlockSpec((tm, tn), lambda i, kk, n: (i, n)),
                  pl.BlockSpec((None, tk, tn), lambda i, kk, n: (n // npj, kk, n % npj))],
        out_specs=pl.BlockSpec((tm, tk), lambda i, kk, n: (i, kk)),
        out_shape=_sds((m, k), out_dtype), scratch_shapes=[pltpu.VMEM((tm, tk), F32)],
        compiler_params=_cp("parallel", "parallel", "arbitrary"))(a, w3)


def mm_tn(a, b, ns):
    m, k = a.shape
    nl = b.shape[1] // ns
    tm = _pick(m, (768, 512, 256, 128))
    tk = _pick(k, (1024, 512, 256, 128))
    tn = _pick(nl, (512, 256, 128))
    npj = nl // tn

    def body(a_ref, b_ref, o_ref):
        @pl.when(pl.program_id(2) == 0)
        def _():
            o_ref[...] = jnp.zeros_like(o_ref)

        o_ref[...] += lax.dot_general(a_ref[...], b_ref[...], (((0,), (0,)), ((), ())),
                                      preferred_element_type=F32)

    return pl.pallas_call(
        body, name="mm_tn", grid=(k // tk, ns * npj, m // tm),
        in_specs=[pl.BlockSpec((tm, tk), lambda i, j, r: (r, i)),
                  pl.BlockSpec((tm, tn), lambda i, j, r: (r, j))],
        out_specs=pl.BlockSpec((None, tk, tn), lambda i, j, r: (j // npj, i, j % npj)),
        out_shape=_sds((ns, k, nl), F32),
        compiler_params=_cp("parallel", "parallel", "arbitrary"))(a, b)


def _win_start(j, ncc):
    return pl.multiple_of(j * CONV_ROWS + jnp.where(j >= ncc, HALO, 0), SUBLANES)


def _tok_start(j):
    return pl.multiple_of(j * CONV_ROWS, CONV_ROWS)


def _shifted(xw, off):
    n = xw.shape[0]
    sh = (n - off) % n
    y = pltpu.roll(xw, sh, 0) if sh else xw
    return y[:CONV_ROWS]


def _conv_fwd(xw, w_ref, ntaps):
    pad = ntaps // 2
    acc = None
    for k in range(ntaps):
        term = w_ref[k:k + 1, :] * _shifted(xw, HALO + k - pad)
        acc = term if acc is None else acc + term
    return acc


def _conv_bwd_data(xw, w_ref, ntaps):
    pad = ntaps // 2
    acc = None
    for k in range(ntaps):
        term = w_ref[k:k + 1, :] * _shifted(xw, HALO - k + pad)
        acc = term if acc is None else acc + term
    return acc


def _conv_bwd_weight(dw_ref, d, xw, ntaps):
    pad = ntaps // 2
    for k in range(ntaps):
        dw_ref[k:k + 1, :] += _colsum(d * _shifted(xw, HALO + k - pad))


def _zero_halos(pad_ref, lc, t):
    z = jnp.zeros((HALO, LANES), F32)
    pad_ref[0:HALO, :] = z
    pad_ref[HALO + lc:2 * HALO + lc, :] = z
    pad_ref[2 * HALO + t:3 * HALO + t, :] = z


def _pad_dst(j, ncc):
    return pl.multiple_of(j * CONV_ROWS + HALO + jnp.where(j >= ncc, HALO, 0), SUBLANES)


def _chan(t, blk0):
    return pl.BlockSpec((t, LANES), lambda ct: (0, blk0 + ct))


def _tapw(rows):
    return pl.BlockSpec((rows, LANES), lambda ct: (0, ct))


def conv_a_fwd(p, w_pad, bias, lc, d):
    t = p.shape[0]
    nct, nch, ncc = d // LANES, t // CONV_ROWS, lc // CONV_ROWS

    def body(av_ref, ag_ref, w_ref, b_ref, u1_ref, pad_ref):
        _zero_halos(pad_ref, lc, t)

        def fill(j, carry):
            rows = pl.ds(_tok_start(j), CONV_ROWS)
            u0 = av_ref[rows, :].astype(F32) * _sigmoid(ag_ref[rows, :].astype(F32))
            pad_ref[pl.ds(_pad_dst(j, ncc), CONV_ROWS), :] = u0
            return carry

        lax.fori_loop(0, nch, fill, 0)

        def conv(j, carry):
            xw = pad_ref[pl.ds(_win_start(j, ncc), CONV_ROWS + 2 * HALO), :]
            u1_ref[pl.ds(_tok_start(j), CONV_ROWS), :] = _conv_fwd(xw, w_ref, CONV_A_TAPS) + b_ref[...]
            return carry

        lax.fori_loop(0, nch, conv, 0)

    return pl.pallas_call(
        body, name="conv_a_fwd", grid=(nct,),
        in_specs=[_chan(t, 0), _chan(t, nct), _tapw(32), _tapw(1)],
        out_specs=_chan(t, 0), out_shape=_sds((t, d), F32),
        scratch_shapes=[pltpu.VMEM((t + 3 * HALO, LANES), F32)],
        compiler_params=_cp("parallel"))(p, p, w_pad, bias)


def conv_b_fwd(p, w_pad, lc, d):
    t = p.shape[0]
    nct, nch, ncc = d // LANES, t // CONV_ROWS, lc // CONV_ROWS

    def body(bx_ref, bb_ref, bc_ref, bg_ref, w_ref, o_ref, pad_ref):
        _zero_halos(pad_ref, lc, t)

        def fill(j, carry):
            rows = pl.ds(_tok_start(j), CONV_ROWS)
            pad_ref[pl.ds(_pad_dst(j, ncc), CONV_ROWS), :] = (
                bc_ref[rows, :].astype(F32) * bx_ref[rows, :].astype(F32))
            return carry

        lax.fori_loop(0, nch, fill, 0)

        def conv(j, carry):
            rows = pl.ds(_tok_start(j), CONV_ROWS)
            xw = pad_ref[pl.ds(_win_start(j, ncc), CONV_ROWS + 2 * HALO), :]
            v = _conv_fwd(xw, w_ref, CONV_B_TAPS)
            o_ref[rows, :] = (bb_ref[rows, :].astype(F32) * v
                              * _silu(bg_ref[rows, :].astype(F32))).astype(o_ref.dtype)
            return carry

        lax.fori_loop(0, nch, conv, 0)

    return pl.pallas_call(
        body, name="conv_b_fwd", grid=(nct,),
        in_specs=[_chan(t, 3 * nct), _chan(t, 4 * nct), _chan(t, 5 * nct), _chan(t, 6 * nct), _tapw(8)],
        out_specs=_chan(t, 0), out_shape=_sds((t, d), BF16),
        scratch_shapes=[pltpu.VMEM((t + 3 * HALO, LANES), F32)],
        compiler_params=_cp("parallel"))(p, p, p, p, w_pad)


def ln_a_fwd(u1, p, g, b, tm):
    t, d = u1.shape

    def body(u_ref, ag_ref, g_ref, b_ref, o_ref):
        xhat, _ = _ln_stats(u_ref[...])
        u2 = xhat * g_ref[...] + b_ref[...]
        o_ref[...] = (_silu(u2) * _silu(ag_ref[...].astype(F32))).astype(o_ref.dtype)

    return pl.pallas_call(
        body, name="ln_a_fwd", grid=(t // tm,),
        in_specs=[_row(tm, d), _colblk(tm, d, 2), _vec(d), _vec(d)], out_specs=_row(tm, d),
        out_shape=_sds((t, d), BF16), compiler_params=_cp("parallel"))(u1, p, g, b)


def ln_a_bwd(dab, u1, p, g, b, tm):
    t, d = u1.shape

    def body(da_ref, u_ref, ag_ref, g_ref, b_ref, du_ref, dag_ref, acc_ref):
        @pl.when(pl.program_id(0) == 0)
        def _():
            acc_ref[...] = jnp.zeros_like(acc_ref)

        xhat, rstd = _ln_stats(u_ref[...])
        u2 = xhat * g_ref[...] + b_ref[...]
        ag = ag_ref[...].astype(F32)
        da = da_ref[...].astype(F32)
        dag_ref[...] = (da * _silu(u2) * _dsilu(ag)).astype(dag_ref.dtype)
        du2 = da * _silu(ag) * _dsilu(u2)
        du1 = _ln_bwd(du2 * g_ref[...], xhat, rstd)
        du_ref[...] = du1
        acc_ref[0:1, :] += _colsum(du2 * xhat)
        acc_ref[1:2, :] += _colsum(du2)
        acc_ref[2:3, :] += _colsum(du1)

    return pl.pallas_call(
        body, name="ln_a_bwd", grid=(t // tm,),
        in_specs=[_colblk(tm, d, 0), _row(tm, d), _colblk(tm, d, 2), _vec(d), _vec(d)],
        out_specs=[_row(tm, d), _row(tm, d), pl.BlockSpec((SUBLANES, d), lambda j: (0, 0))],
        out_shape=[_sds((t, d), F32), _sds((t, d), BF16), _sds((SUBLANES, d), F32)],
        compiler_params=_cp("arbitrary"))(dab, u1, p, g, b)


def conv_a_bwd(du1, p, w_pad, lc, d):
    t = p.shape[0]
    nct, nch, ncc = d // LANES, t // CONV_ROWS, lc // CONV_ROWS

    def body(du_ref, av_ref, ag_ref, w_ref, dav_ref, dag_ref, dw_ref, pad_u, pad_d):
        _zero_halos(pad_u, lc, t)
        _zero_halos(pad_d, lc, t)
        dw_ref[...] = jnp.zeros_like(dw_ref)

        def fill(j, carry):
            rows = pl.ds(_tok_start(j), CONV_ROWS)
            dst = pl.ds(_pad_dst(j, ncc), CONV_ROWS)
            pad_u[dst, :] = av_ref[rows, :].astype(F32) * _sigmoid(ag_ref[rows, :].astype(F32))
            pad_d[dst, :] = du_ref[rows, :]
            return carry

        lax.fori_loop(0, nch, fill, 0)

        def step(j, carry):
            rows = pl.ds(_tok_start(j), CONV_ROWS)
            win = pl.ds(_win_start(j, ncc), CONV_ROWS + 2 * HALO)
            du0 = _conv_bwd_data(pad_d[win, :], w_ref, CONV_A_TAPS)
            sig = _sigmoid(ag_ref[rows, :].astype(F32))
            dav_ref[rows, :] = (du0 * sig).astype(dav_ref.dtype)
            dag_ref[rows, :] = (du0 * av_ref[rows, :].astype(F32) * sig * (1.0 - sig)).astype(dag_ref.dtype)
            _conv_bwd_weight(dw_ref, du_ref[rows, :], pad_u[win, :], CONV_A_TAPS)
            return carry

        lax.fori_loop(0, nch, step, 0)

    return pl.pallas_call(
        body, name="conv_a_bwd", grid=(nct,),
        in_specs=[_chan(t, 0), _chan(t, 0), _chan(t, nct), _tapw(32)],
        out_specs=[_chan(t, 0), _chan(t, 0), _tapw(32)],
        out_shape=[_sds((t, d), BF16), _sds((t, d), BF16), _sds((32, d), F32)],
        scratch_shapes=[pltpu.VMEM((t + 3 * HALO, LANES), F32), pltpu.VMEM((t + 3 * HALO, LANES), F32)],
        compiler_params=_cp("parallel"))(du1, p, p, w_pad)


def conv_b_bwd(dab, p, w_pad, lc, d):
    t = p.shape[0]
    nct, nch, ncc = d // LANES, t // CONV_ROWS, lc // CONV_ROWS

    def body(db_ref, bx_ref, bb_ref, bc_ref, bg_ref, w_ref,
             dbx_ref, dbb_ref, dbc_ref, dbg_ref, dw_ref, pad_t, pad_d):
        _zero_halos(pad_t, lc, t)
        _zero_halos(pad_d, lc, t)
        dw_ref[...] = jnp.zeros_like(dw_ref)

        def fill(j, carry):
            rows = pl.ds(_tok_start(j), CONV_ROWS)
            pad_t[pl.ds(_pad_dst(j, ncc), CONV_ROWS), :] = (
                bc_ref[rows, :].astype(F32) * bx_ref[rows, :].astype(F32))
            return carry

        lax.fori_loop(0, nch, fill, 0)

        def first(j, carry):
            rows = pl.ds(_tok_start(j), CONV_ROWS)
            xw = pad_t[pl.ds(_win_start(j, ncc), CONV_ROWS + 2 * HALO), :]
            v = _conv_fwd(xw, w_ref, CONV_B_TAPS)
            bg = bg_ref[rows, :].astype(F32)
            bb = bb_ref[rows, :].astype(F32)
            db = db_ref[rows, :].astype(F32)
            sg = _silu(bg)
            dbb_ref[rows, :] = (db * v * sg).astype(dbb_ref.dtype)
            dbg_ref[rows, :] = (db * bb * v * _dsilu(bg)).astype(dbg_ref.dtype)
            dv = db * bb * sg
            pad_d[pl.ds(_pad_dst(j, ncc), CONV_ROWS), :] = dv
            _conv_bwd_weight(dw_ref, dv, xw, CONV_B_TAPS)
            return carry

        lax.fori_loop(0, nch, first, 0)

        def second(j, carry):
            rows = pl.ds(_tok_start(j), CONV_ROWS)
            dt = _conv_bwd_data(pad_d[pl.ds(_win_start(j, ncc), CONV_ROWS + 2 * HALO), :], w_ref, CONV_B_TAPS)
            dbc_ref[rows, :] = (dt * bx_ref[rows, :].astype(F32)).astype(dbc_ref.dtype)
            dbx_ref[rows, :] = (dt * bc_ref[rows, :].astype(F32)).astype(dbx_ref.dtype)
            return carry

        lax.fori_loop(0, nch, second, 0)

    return pl.pallas_call(
        body, name="conv_b_bwd", grid=(nct,),
        in_specs=[_chan(t, nct), _chan(t, 3 * nct), _chan(t, 4 * nct), _chan(t, 5 * nct), _chan(t, 6 * nct),
                  _tapw(8)],
        out_specs=[_chan(t, 0)] * 4 + [_tapw(8)],
        out_shape=[_sds((t, d), BF16)] * 4 + [_sds((8, d), F32)],
        scratch_shapes=[pltpu.VMEM((t + 3 * HALO, LANES), F32), pltpu.VMEM((t + 3 * HALO, LANES), F32)],
        compiler_params=_cp("parallel"))(dab, p, p, p, p, w_pad)


def _swap_halves(z, first_half):
    return jnp.where(first_half, pltpu.roll(z, 96, 1), pltpu.roll(z, 32, 1))


def _first_half_mask(rows):
    lane = lax.broadcasted_iota(jnp.int32, (rows, HEAD_DIM), 1)
    return (lane & 32) == 0


def qk_fwd(qgkv, qg, kg, cos_t, sin_t, ad, kvd, tm, qscale):
    t = qgkv.shape[0]

    def body(q_ref, k_ref, qg_ref, kg_ref, c_ref, s_ref, qo_ref, ko_ref):
        first = _first_half_mask(tm)
        cosv, sinv = c_ref[...], s_ref[...]

        def head(x, gain):
            inv = lax.rsqrt(jnp.mean(x * x, axis=-1, keepdims=True) + RMS_EPS)
            yv = x * inv * gain
            return yv * cosv + _swap_halves(yv, first) * sinv

        for h in range(ad // HEAD_DIM):
            sl = slice(h * HEAD_DIM, (h + 1) * HEAD_DIM)
            qo_ref[:, sl] = (head(q_ref[:, sl].astype(F32), qg_ref[...]) * qscale).astype(qo_ref.dtype)
        for h in range(kvd // HEAD_DIM):
            sl = slice(h * HEAD_DIM, (h + 1) * HEAD_DIM)
            ko_ref[:, sl] = head(k_ref[:, sl].astype(F32), kg_ref[...]).astype(ko_ref.dtype)

    return pl.pallas_call(
        body, name="qk_fwd", grid=(t // tm,),
        in_specs=[_colblk(tm, ad, 0), _colblk(tm, kvd, 2 * ad // kvd), _vec(HEAD_DIM), _vec(HEAD_DIM),
                  _row(tm, HEAD_DIM), _row(tm, HEAD_DIM)],
        out_specs=[_row(tm, ad), _row(tm, kvd)],
        out_shape=[_sds((t, ad), BF16), _sds((t, kvd), BF16)],
        compiler_params=_cp("parallel"))(qgkv, qgkv, qg, kg, cos_t, sin_t)


def qk_bwd(dqr, dkr, qgkv, qg, kg, cos_t, sin_t, ad, kvd, tm):
    t = qgkv.shape[0]

    def body(dq_ref, dk_ref, q_ref, k_ref, qg_ref, kg_ref, c_ref, s_ref, dqo_ref, dko_ref, acc_ref):
        @pl.when(pl.program_id(0) == 0)
        def _():
            acc_ref[...] = jnp.zeros_like(acc_ref)

        first = _first_half_mask(tm)
        cosv, sinv = c_ref[...], s_ref[...]

        def head(x, gain, dout):
            inv = lax.rsqrt(jnp.mean(x * x, axis=-1, keepdims=True) + RMS_EPS)
            xn = x * inv
            dy = dout * cosv + _swap_halves(dout * sinv, first)
            dxn = dy * gain
            dx = inv * (dxn - xn * jnp.mean(dxn * xn, axis=-1, keepdims=True))
            return dx, _colsum(dy * xn)

        dqg = jnp.zeros((1, HEAD_DIM), F32)
        for h in range(ad // HEAD_DIM):
            sl = slice(h * HEAD_DIM, (h + 1) * HEAD_DIM)
            dx, dg = head(q_ref[:, sl].astype(F32), qg_ref[...], dq_ref[:, sl])
            dqo_ref[:, sl] = dx.astype(dqo_ref.dtype)
            dqg = dqg + dg
        dkg = jnp.zeros((1, HEAD_DIM), F32)
        for h in range(kvd // HEAD_DIM):
            sl = slice(h * HEAD_DIM, (h + 1) * HEAD_DIM)
            dx, dg = head(k_ref[:, sl].astype(F32), kg_ref[...], dk_ref[:, sl])
            dko_ref[:, sl] = dx.astype(dko_ref.dtype)
            dkg = dkg + dg
        acc_ref[0:1, :] += dqg
        acc_ref[1:2, :] += dkg

    return pl.pallas_call(
        body, name="qk_bwd", grid=(t // tm,),
        in_specs=[_row(tm, ad), _row(tm, kvd), _colblk(tm, ad, 0), _colblk(tm, kvd, 2 * ad // kvd),
                  _vec(HEAD_DIM), _vec(HEAD_DIM), _row(tm, HEAD_DIM), _row(tm, HEAD_DIM)],
        out_specs=[_row(tm, ad), _row(tm, kvd), pl.BlockSpec((SUBLANES, HEAD_DIM), lambda j: (0, 0))],
        out_shape=[_sds((t, ad), BF16), _sds((t, kvd), BF16), _sds((SUBLANES, HEAD_DIM), F32)],
        compiler_params=_cp("arbitrary"))(dqr, dkr, qgkv, qgkv, qg, kg, cos_t, sin_t)


_NT = (((1,), (1,)), ((), ()))


def flash_fwd(qr, kr, qgkv, ad, kvd, tm):
    t = qr.shape[0]
    nkv = kvd // HEAD_DIM
    gw = ad // nkv
    hpg = gw // HEAD_DIM
    nt = t // tm
    v_blk0 = (2 * ad + kvd) // HEAD_DIM
    gate_blk0 = ad // gw

    def body(q_ref, k_ref, v_ref, g_ref, o_ref, og_ref, lse_ref):
        qi = pl.program_id(1)
        nkc = jnp.where(qi == 0, 1, nt)
        heads = [slice(h * HEAD_DIM, (h + 1) * HEAD_DIM) for h in range(hpg)]
        qs = [q_ref[:, sl] for sl in heads]

        def step(c, carry):
            rows = pl.ds(pl.multiple_of(c * tm, tm), tm)
            kc = k_ref[rows, :]
            vc = v_ref[rows, :]
            out = []
            for h in range(hpg):
                m, l, acc = carry[h]
                s = lax.dot_general(qs[h], kc, _NT, preferred_element_type=F32)
                m_new = jnp.maximum(m, jnp.max(s, axis=-1, keepdims=True))
                a = jnp.exp2(m - m_new)
                p = jnp.exp2(s - m_new)
                l = a * l + jnp.sum(p, axis=-1, keepdims=True)
                acc = a * acc + jnp.dot(p.astype(BF16), vc, preferred_element_type=F32)
                out.append((m_new, l, acc))
            return tuple(out)

        init = tuple((jnp.full((tm, 1), -1e30, F32), jnp.zeros((tm, 1), F32), jnp.zeros((tm, HEAD_DIM), F32))
                     for _ in range(hpg))
        res = lax.fori_loop(0, nkc, step, init)
        lane = lax.broadcasted_iota(jnp.int32, (tm, LANES), 1)
        lse_blk = jnp.zeros((tm, LANES), F32)
        for h, sl in enumerate(heads):
            m, l, acc = res[h]
            o = acc / l
            o_ref[:, sl] = o.astype(o_ref.dtype)
            og_ref[:, sl] = (o * _silu(g_ref[:, sl].astype(F32))).astype(og_ref.dtype)
            lse_blk = jnp.where(lane == h, m + jnp.log(l) * LOG2_E, lse_blk)
        lse_ref[...] = lse_blk

    return pl.pallas_call(
        body, name="flash_fwd", grid=(nkv, nt),
        in_specs=[pl.BlockSpec((tm, gw), lambda g, i: (i, g)),
                  pl.BlockSpec((t, HEAD_DIM), lambda g, i: (0, g)),
                  pl.BlockSpec((t, HEAD_DIM), lambda g, i: (0, v_blk0 + g)),
                  pl.BlockSpec((tm, gw), lambda g, i: (i, gate_blk0 + g))],
        out_specs=[pl.BlockSpec((tm, gw), lambda g, i: (i, g)),
                   pl.BlockSpec((tm, gw), lambda g, i: (i, g)),
                   pl.BlockSpec((None, tm, LANES), lambda g, i: (g, i, 0))],
        out_shape=[_sds((t, ad), BF16), _sds((t, ad), BF16), _sds((nkv, t, LANES), F32)],
        compiler_params=_cp("parallel", "parallel"))(qr, kr, qgkv, qgkv)


def gate_bwd(dog, o, qgkv, ad, kvd, tm):
    t = o.shape[0]
    nkv = kvd // HEAD_DIM
    hpg = ad // nkv // HEAD_DIM

    def body(dog_ref, o_ref, g_ref, do_ref, dg_ref, dl_ref):
        lane = lax.broadcasted_iota(jnp.int32, (tm, LANES), 1)
        for grp in range(nkv):
            blk = jnp.zeros((tm, LANES), F32)
            for hh in range(hpg):
                h = grp * hpg + hh
                sl = slice(h * HEAD_DIM, (h + 1) * HEAD_DIM)
                dv = dog_ref[:, sl].astype(F32)
                ov = o_ref[:, sl].astype(F32)
                gv = g_ref[:, sl].astype(F32)
                doh = dv * _silu(gv)
                do_ref[:, sl] = doh.astype(do_ref.dtype)
                dg_ref[:, sl] = (dv * ov * _dsilu(gv)).astype(dg_ref.dtype)
                blk = jnp.where(lane == hh, jnp.sum(doh * ov, axis=-1, keepdims=True), blk)
            dl_ref[grp] = blk

    return pl.pallas_call(
        body, name="gate_bwd", grid=(t // tm,),
        in_specs=[_row(tm, ad), _row(tm, ad), _colblk(tm, ad, 1)],
        out_specs=[_row(tm, ad), _row(tm, ad), pl.BlockSpec((nkv, tm, LANES), lambda j: (0, j, 0))],
        out_shape=[_sds((t, ad), BF16), _sds((t, ad), BF16), _sds((nkv, t, LANES), F32)],
        compiler_params=_cp("parallel"))(dog, o, qgkv)


def flash_bwd_dq(qr, do, kr, qgkv, lse, delta, ad, kvd, tm, scale):
    t = qr.shape[0]
    nkv = kvd // HEAD_DIM
    gw = ad // nkv
    hpg = gw // HEAD_DIM
    nt = t // tm
    v_blk0 = (2 * ad + kvd) // HEAD_DIM

    def body(q_ref, do_ref, k_ref, v_ref, lse_ref, dl_ref, dq_ref):
        qi = pl.program_id(1)
        nkc = jnp.where(qi == 0, 1, nt)
        heads = [slice(h * HEAD_DIM, (h + 1) * HEAD_DIM) for h in range(hpg)]
        qs = [q_ref[:, sl] for sl in heads]
        dos = [do_ref[:, sl] for sl in heads]
        lses = [lse_ref[:, h:h + 1] for h in range(hpg)]
        dls = [dl_ref[:, h:h + 1] for h in range(hpg)]

        def step(c, dqs):
            rows = pl.ds(pl.multiple_of(c * tm, tm), tm)
            kc = k_ref[rows, :]
            vc = v_ref[rows, :]
            out = []
            for h in range(hpg):
                s = lax.dot_general(qs[h], kc, _NT, preferred_element_type=F32)
                p = jnp.exp2(s - lses[h])
                dp = lax.dot_general(dos[h], vc, _NT, preferred_element_type=F32)
                ds = p * (dp - dls[h])
                out.append(dqs[h] + jnp.dot(ds.astype(BF16), kc, preferred_element_type=F32))
            return tuple(out)

        res = lax.fori_loop(0, nkc, step, tuple(jnp.zeros((tm, HEAD_DIM), F32) for _ in range(hpg)))
        for h, sl in enumerate(heads):
            dq_ref[:, sl] = res[h] * scale

    return pl.pallas_call(
        body, name="flash_bwd_dq", grid=(nkv, nt),
        in_specs=[pl.BlockSpec((tm, gw), lambda g, i: (i, g)),
                  pl.BlockSpec((tm, gw), lambda g, i: (i, g)),
                  pl.BlockSpec((t, HEAD_DIM), lambda g, i: (0, g)),
                  pl.BlockSpec((t, HEAD_DIM), lambda g, i: (0, v_blk0 + g)),
                  pl.BlockSpec((None, tm, LANES), lambda g, i: (g, i, 0)),
                  pl.BlockSpec((None, tm, LANES), lambda g, i: (g, i, 0))],
        out_specs=pl.BlockSpec((tm, gw), lambda g, i: (i, g)),
        out_shape=_sds((t, ad), F32),
        compiler_params=_cp("parallel", "parallel"))(qr, do, kr, qgkv, lse, delta)


def flash_bwd_dkv(qr, do, kr, qgkv, lse_t, delta_t, ad, kvd, tm):
    t = qr.shape[0]
    nkv = kvd // HEAD_DIM
    gw = ad // nkv
    hpg = gw // HEAD_DIM
    nt = t // tm
    v_blk0 = (2 * ad + kvd) // HEAD_DIM

    def body(k_ref, v_ref, q_ref, do_ref, lse_ref, dl_ref, dk_ref, dv_ref):
        ki = pl.program_id(1)
        kc = k_ref[...]
        vc = v_ref[...]

        def qstep(qc, carry):
            dk, dv = carry
            rows = pl.ds(pl.multiple_of(qc * tm, tm), tm)
            for h in range(hpg):
                sl = slice(h * HEAD_DIM, (h + 1) * HEAD_DIM)
                q = q_ref[rows, sl]
                doh = do_ref[rows, sl]
                st = lax.dot_general(kc, q, _NT, preferred_element_type=F32)
                pt = jnp.exp2(st - lse_ref[qc, h:h + 1, :])
                dv = dv + jnp.dot(pt.astype(BF16), doh, preferred_element_type=F32)
                dpt = lax.dot_general(vc, doh, _NT, preferred_element_type=F32)
                dst = pt * (dpt - dl_ref[qc, h:h + 1, :])
                dk = dk + jnp.dot(dst.astype(BF16), q, preferred_element_type=F32)
            return dk, dv

        zero = jnp.zeros((tm, HEAD_DIM), F32)
        dk, dv = lax.fori_loop(jnp.where(ki == 0, 0, 1), nt, qstep, (zero, zero))
        dk_ref[...] = dk * LN_2
        dv_ref[...] = dv.astype(dv_ref.dtype)

    return pl.pallas_call(
        body, name="flash_bwd_dkv", grid=(nkv, nt),
        in_specs=[pl.BlockSpec((tm, HEAD_DIM), lambda g, i: (i, g)),
                  pl.BlockSpec((tm, HEAD_DIM), lambda g, i: (i, v_blk0 + g)),
                  pl.BlockSpec((t, gw), lambda g, i: (0, g)),
                  pl.BlockSpec((t, gw), lambda g, i: (0, g)),
                  pl.BlockSpec((None, nt, hpg, tm), lambda g, i: (g, 0, 0, 0)),
                  pl.BlockSpec((None, nt, hpg, tm), lambda g, i: (g, 0, 0, 0))],
        out_specs=[pl.BlockSpec((tm, HEAD_DIM), lambda g, i: (i, g)),
                   pl.BlockSpec((tm, HEAD_DIM), lambda g, i: (i, g))],
        out_shape=[_sds((t, kvd), F32), _sds((t, kvd), BF16)],
        compiler_params=_cp("parallel", "parallel"))(kr, qgkv, qr, do, lse_t, delta_t)


def _rows_per_head(a, tm, hpg):
    nkv, t, _ = a.shape
    return a[:, :, :hpg].reshape(nkv, t // tm, tm, hpg).transpose(0, 1, 3, 2)


def adaln_fwd(c16, w_mod):
    nlay, d, nl = w_mod.shape
    tn = _pick(nl, (512, 256, 128))

    def body(c_ref, w_ref, o_ref):
        o_ref[...] = jnp.dot(_silu(c_ref[...]), w_ref[...], preferred_element_type=F32,
                             precision=lax.Precision.HIGHEST)

    return pl.pallas_call(
        body, name="adaln_fwd", grid=(nlay, nl // tn),
        in_specs=[pl.BlockSpec((16, d), lambda l, j: (0, 0)),
                  pl.BlockSpec((None, d, tn), lambda l, j: (l, 0, j))],
        out_specs=pl.BlockSpec((None, 16, tn), lambda l, j: (l, 0, j)),
        out_shape=_sds((nlay, 16, nl), F32), compiler_params=_cp("parallel", "parallel"))(c16, w_mod)


def adaln_bwd(c16t, dm, w_mod):
    nlay, d, nl = w_mod.shape
    tn = _pick(nl, (512, 256, 128))

    def body(c_ref, dm_ref, w_ref, dw_ref, dc_ref):
        @pl.when(pl.program_id(1) == 0)
        def _():
            dc_ref[...] = jnp.zeros_like(dc_ref)

        dmv = dm_ref[...]
        dw_ref[...] = jnp.dot(_silu(c_ref[...]), dmv, preferred_element_type=F32,
                              precision=lax.Precision.HIGHEST)
        dc_ref[...] += lax.dot_general(dmv, w_ref[...], _NT, preferred_element_type=F32,
                                       precision=lax.Precision.HIGHEST)

    return pl.pallas_call(
        body, name="adaln_bwd", grid=(nlay, nl // tn),
        in_specs=[pl.BlockSpec((d, 16), lambda l, j: (0, 0)),
                  pl.BlockSpec((None, 16, tn), lambda l, j: (l, 0, j)),
                  pl.BlockSpec((None, d, tn), lambda l, j: (l, 0, j))],
        out_specs=[pl.BlockSpec((None, d, tn), lambda l, j: (l, 0, j)),
                   pl.BlockSpec((None, 16, d), lambda l, j: (l, 0, 0))],
        out_shape=[_sds((nlay, d, nl), F32), _sds((nlay, 16, d), F32)],
        compiler_params=_cp("parallel", "arbitrary"))(c16t, dm, w_mod)


def sum_leading(a, name):
    n = a.shape[0]

    def body(a_ref, o_ref):
        acc = a_ref[0]
        for i in range(1, n):
            acc = acc + a_ref[i]
        o_ref[...] = acc

    return pl.pallas_call(body, name=name, in_specs=[VMEM_FULL], out_specs=VMEM_FULL,
                          out_shape=_sds(a.shape[1:], F32),
                          compiler_params=pltpu.CompilerParams(vmem_limit_bytes=VMEM_LIMIT))(a)


def c_ctx_grad(parts, c_ctx):
    n = parts.shape[0]

    def body(p_ref, c_ref, o_ref):
        acc = p_ref[0]
        for i in range(1, n):
            acc = acc + p_ref[i]
        o_ref[...] = acc * _dsilu(c_ref[...])

    return pl.pallas_call(body, name="c_ctx_grad", in_specs=[VMEM_FULL, VMEM_FULL], out_specs=VMEM_FULL,
                          out_shape=_sds(c_ctx.shape, F32))(parts, c_ctx)


def _as2d(a):
    return a.reshape(-1, a.shape[-1])


def _row_tile(r, c):
    for tr in (1024, 512, 256, 128, 64, 32, 16, 8):
        if r % tr == 0 and tr * c * 4 <= (1 << 20):
            return tr
    return r


def add_n(arrs, name):
    shape = arrs[0].shape
    flat = [_as2d(a) for a in arrs]
    r, c = flat[0].shape
    tr = _row_tile(r, c)

    def body(*refs):
        acc = refs[0][...]
        for ref in refs[1:-1]:
            acc = acc + ref[...]
        refs[-1][...] = acc

    out = pl.pallas_call(
        body, name=name, grid=(r // tr,),
        in_specs=[pl.BlockSpec((tr, c), lambda i: (i, 0))] * len(flat),
        out_specs=pl.BlockSpec((tr, c), lambda i: (i, 0)),
        out_shape=_sds((r, c), F32), compiler_params=_cp("parallel"))(*flat)
    return out.reshape(shape)


def adamw(w, g, m, v):
    shape = w.shape
    flat = [_as2d(a.reshape((1,) + shape) if len(shape) == 1 else a) for a in (w, g, m, v)]
    r, c = flat[0].shape
    tr = _row_tile(r, c)
    c1 = 1.0 - ADAM_B1 ** ADAM_STEP
    c2 = 1.0 - ADAM_B2 ** ADAM_STEP

    def body(w_ref, g_ref, m_ref, v_ref, d_ref, nm_ref, nv_ref):
        gv = g_ref[...]
        nm = ADAM_B1 * m_ref[...] + (1.0 - ADAM_B1) * gv
        nv = ADAM_B2 * v_ref[...] + (1.0 - ADAM_B2) * (gv * gv)
        d_ref[...] = -ADAM_LR * ((nm / c1) / (jnp.sqrt(nv / c2) + ADAM_EPS) + ADAM_WD * w_ref[...])
        nm_ref[...] = nm
        nv_ref[...] = nv

    spec = pl.BlockSpec((tr, c), lambda i: (i, 0))
    outs = pl.pallas_call(
        body, name="adamw", grid=(r // tr,), in_specs=[spec] * 4, out_specs=[spec] * 3,
        out_shape=[_sds((r, c), F32)] * 3, compiler_params=_cp("parallel"))(*flat)
    return tuple(o.reshape(shape) for o in outs)


def _place():
    return lax.axis_index("x"), lax.axis_index("y"), lax.axis_index("c")


def _remote(src, dst, ssem, rsem, dev):
    return pltpu.make_async_remote_copy(src_ref=src, dst_ref=dst, send_sem=ssem, recv_sem=rsem,
                                        device_id=dev, device_id_type=MESH)


def all_gather8(v, name):
    m_per, n = v.shape

    def body(x_ref, out_ref, send_sems, recv_sems, local_sem):
        x, y, c = _place()
        me, sibling = (x, y, c), (x, y, 1 - c)
        chips = [(1 - x, y), (x, 1 - y), (1 - x, 1 - y)]

        def rows(px, py, pc):
            return out_ref.at[pl.ds((4 * px + 2 * py + pc) * m_per, m_per), :]

        def copy(k, block, to, src=None):
            return _remote(rows(*block) if src is None else src, rows(*block),
                           send_sems.at[k], recv_sems.at[k], to)

        mine = pltpu.make_async_copy(x_ref, rows(*me), local_sem)
        mine.start()
        first = [copy(0, me, sibling, src=x_ref)]
        first += [copy(1 + j, me, (*chip, c), src=x_ref) for j, chip in enumerate(chips)]
        for cp in first:
            cp.start()
        passed = [copy(4 + j, (*chip, c), sibling) for j, chip in enumerate(chips)]
        for j, chip in enumerate(chips):
            copy(1 + j, (*chip, c), me).wait_recv()
            passed[j].start()
        copy(0, sibling, me).wait_recv()
        for j, chip in enumerate(chips):
            copy(4 + j, (*chip, 1 - c), me).wait_recv()
        for cp in first + passed:
            cp.wait_send()
        mine.wait()

    return pl.pallas_call(
        body, name=name, out_shape=_sds((8 * m_per, n), v.dtype),
        in_specs=[VMEM_FULL], out_specs=VMEM_FULL,
        scratch_shapes=[pltpu.SemaphoreType.DMA((7,)), pltpu.SemaphoreType.DMA((7,)), pltpu.SemaphoreType.DMA],
        compiler_params=pltpu.CompilerParams(vmem_limit_bytes=VMEM_LIMIT))(v)


def _slabs(ref, n):
    rows = ref.shape[0] // n
    return [ref.at[pl.ds(i * rows, rows)] for i in range(n)]


def gather_weight(w, name):
    _, r, cdim = w.shape
    n = D2D_PARTS

    def body(w_ref, out_ref, send_sems, recv_sems, local_sems):
        x, y, c = _place()
        sibling = (x, y, 1 - c)
        chips = [(1 - x, y), (x, 1 - y), (1 - x, 1 - y)]
        mine = 2 * x + y
        local = [pltpu.make_async_copy(w_ref.at[i], out_ref.at[i, mine], local_sems.at[i]) for i in range(2)]
        for cp in local:
            cp.start()
        src = _slabs(w_ref.at[c], n)
        first = []
        for k, chip in enumerate(chips):
            dst = _slabs(out_ref.at[c, mine], n)
            for j in range(n):
                cp = _remote(src[j], dst[j], send_sems.at[k * n + j], recv_sems.at[k * n + j], (*chip, c))
                cp.start()
                first.append(cp)
        passed = []
        for j in range(n):
            for k, (px, py) in enumerate(chips):
                theirs = _slabs(out_ref.at[c, 2 * px + py], n)[j]
                _remote(src[j], theirs, send_sems.at[k * n + j], recv_sems.at[k * n + j], sibling).wait_recv()
                cp = _remote(theirs, theirs, send_sems.at[(3 + k) * n + j], recv_sems.at[(3 + k) * n + j], sibling)
                cp.start()
                passed.append(cp)
        for j in range(n):
            for k, (px, py) in enumerate(chips):
                other = _slabs(out_ref.at[1 - c, 2 * px + py], n)[j]
                _remote(other, other, send_sems.at[(3 + k) * n + j], recv_sems.at[(3 + k) * n + j],
                        sibling).wait_recv()
        for cp in first + passed:
            cp.wait_send()
        for cp in local:
            cp.wait()

    return pl.pallas_call(
        body, name=name, out_shape=_sds((2, 4, r, cdim), w.dtype), in_specs=[ANY], out_specs=ANY,
        scratch_shapes=[pltpu.SemaphoreType.DMA((6 * n,)), pltpu.SemaphoreType.DMA((6 * n,)),
                        pltpu.SemaphoreType.DMA((2,))])(w)


def rs_pair_swap(g0, g1, name):
    n = D2D_PARTS

    def body(g0_ref, g1_ref, own_ref, st_ref, send_sems, recv_sems, local_sem):
        x, y, c = _place()
        sibling = (x, y, 1 - c)

        def swap(keep_ref, give_ref):
            keep = pltpu.make_async_copy(keep_ref, own_ref, local_sem)
            keep.start()
            cps = []
            for s4 in range(4):
                src, dst = _slabs(give_ref.at[s4], n), _slabs(st_ref.at[s4], n)
                for j in range(n):
                    cps.append(_remote(src[j], dst[j], send_sems.at[s4 * n + j], recv_sems.at[s4 * n + j], sibling))
            for cp in cps:
                cp.start()
            for cp in cps:
                cp.wait()
            keep.wait()

        @pl.when(c == 0)
        def _():
            swap(g0_ref, g1_ref)

        @pl.when(c == 1)
        def _():
            swap(g1_ref, g0_ref)

    return pl.pallas_call(
        body, name=name, out_shape=[_sds(g0.shape, F32), _sds(g0.shape, F32)],
        in_specs=[ANY, ANY], out_specs=[ANY, ANY],
        scratch_shapes=[pltpu.SemaphoreType.DMA((4 * n,)), pltpu.SemaphoreType.DMA((4 * n,)),
                        pltpu.SemaphoreType.DMA])(g0, g1)


def rs_chip_exchange(ps, name):
    _, r, cdim = ps.shape

    def body(ps_ref, mine_ref, st_ref, send_sems, recv_sems, local_sem):
        x, y, c = _place()
        chips = [(1 - x, y), (x, 1 - y), (1 - x, 1 - y)]
        keep = pltpu.make_async_copy(ps_ref.at[2 * x + y], mine_ref, local_sem)
        keep.start()
        cps = [_remote(ps_ref.at[2 * px + py], st_ref.at[k], send_sems.at[k], recv_sems.at[k], (px, py, c))
               for k, (px, py) in enumerate(chips)]
        for cp in cps:
            cp.start()
        for cp in cps:
            cp.wait()
        keep.wait()

    return pl.pallas_call(
        body, name=name, out_shape=[_sds((r, cdim), F32), _sds((3, r, cdim), F32)],
        in_specs=[ANY], out_specs=[ANY, ANY],
        scratch_shapes=[pltpu.SemaphoreType.DMA((3,)), pltpu.SemaphoreType.DMA((3,)), pltpu.SemaphoreType.DMA])(ps)


def rs_pair_share(red, name):
    n = 2 * D2D_PARTS

    def body(red_ref, out_ref, send_sems, recv_sems, local_sem):
        x, y, c = _place()
        sibling = (x, y, 1 - c)
        keep = pltpu.make_async_copy(red_ref, out_ref.at[c], local_sem)
        keep.start()
        src, dst, got = _slabs(red_ref, n), _slabs(out_ref.at[c], n), _slabs(out_ref.at[1 - c], n)
        cps = [_remote(src[j], dst[j], send_sems.at[j], recv_sems.at[j], sibling) for j in range(n)]
        for cp in cps:
            cp.start()
        for cp in cps:
            cp.wait_send()
        for j in range(n):
            _remote(src[j], got[j], send_sems.at[j], recv_sems.at[j], sibling).wait_recv()
        keep.wait()

    return pl.pallas_call(
        body, name=name, out_shape=_sds((2,) + red.shape, F32), in_specs=[ANY], out_specs=ANY,
        scratch_shapes=[pltpu.SemaphoreType.DMA((n,)), pltpu.SemaphoreType.DMA((n,)), pltpu.SemaphoreType.DMA])(red)


def reduce_scatter_grad(g0, g1, tag):
    own, theirs = rs_pair_swap(g0, g1, "rs_swap_" + tag)
    ps = add_n([own, theirs], "rs_pair_add_" + tag)
    mine, others = rs_chip_exchange(ps, "rs_exchange_" + tag)
    red = add_n([mine, others[0], others[1], others[2]], "rs_shard_add_" + tag)
    return rs_pair_share(red, "rs_share_" + tag)


def _rope_tables(lc, s):
    rows_n = s // GRID_W
    row = jnp.repeat(jnp.arange(rows_n, dtype=F32), GRID_W)
    col = jnp.tile(jnp.arange(GRID_W, dtype=F32), rows_n)
    axis_dim = HEAD_DIM // 2
    inv_freq = ROPE_THETA ** (-jnp.arange(0, axis_dim, 2, dtype=F32) / axis_dim)
    ang_r = row[:, None] * inv_freq[None, :]
    ang_c = col[:, None] * inv_freq[None, :]
    cr, sr, cc, sc = jnp.cos(ang_r), jnp.sin(ang_r), jnp.cos(ang_c), jnp.sin(ang_c)
    cos_l = jnp.concatenate([cr, cr, cc, cc], axis=1)
    sin_l = jnp.concatenate([-sr, sr, -sc, sc], axis=1)
    cos_t = jnp.concatenate([jnp.ones((lc, HEAD_DIM), F32), cos_l], axis=0)
    sin_t = jnp.concatenate([jnp.zeros((lc, HEAD_DIM), F32), sin_l], axis=0)
    return cos_t, sin_t


def _pad_rows(a, rows):
    return jnp.concatenate([a, jnp.zeros((rows - a.shape[0],) + a.shape[1:], a.dtype)], axis=0)


def _pad_cols(a, cols):
    return jnp.concatenate([a, jnp.zeros(a.shape[:-1] + (cols - a.shape[-1],), a.dtype)], axis=-1)


def kernel(x, c, ctx, c_ctx, w_mod, b_mod, post_ln_g, post_ln_b, w_in_e, conv_a_w, conv_a_b, norm_a_g, norm_a_b, conv_b_w, w_out_e, w_in_o, q_norm_g, k_norm_g, w_out_o, loss_target, m_c_ctx, m_w_mod, m_b_mod, m_post_ln_g, m_post_ln_b, m_w_in_e, m_conv_a_w, m_conv_a_b, m_norm_a_g, m_norm_a_b, m_conv_b_w, m_w_out_e, m_w_in_o, m_q_norm_g, m_k_norm_g, m_w_out_o, v_c_ctx, v_w_mod, v_b_mod, v_post_ln_g, v_post_ln_b, v_w_in_e, v_conv_a_w, v_conv_a_b, v_norm_a_g, v_norm_a_b, v_conv_b_w, v_w_out_e, v_w_in_o, v_q_norm_g, v_k_norm_g, v_w_out_o):
    s, d = x.shape[1], x.shape[2]
    lc = ctx.shape[1]
    t = lc + s
    tm = lc
    depth = w_mod.shape[0]
    n_even, n_odd = w_in_e.shape[0], w_in_o.shape[0]
    ad = w_out_o.shape[1] * 4
    kvd = (w_in_o.shape[2] * 4 - 2 * ad) // 2
    nkv = kvd // HEAD_DIM
    hpg = ad // nkv // HEAD_DIM
    nlm = w_mod.shape[2]
    alpha = (2.0 * depth) ** 0.25
    scale = HEAD_DIM ** -0.5
    assert n_even == 2 and n_odd == 2 and depth == 4 and hpg == GQA_GROUP
    assert lc % CONV_ROWS == 0 and s % tm == 0 and d % LANES == 0

    xi, yi, ci = _place()
    shard = 2 * xi + yi
    dev = 4 * xi + 2 * yi + ci

    wg_in_e = gather_weight(w_in_e.astype(BF16), "gather_w_in_e")
    wg_out_e = gather_weight(w_out_e.astype(BF16), "gather_w_out_e")
    wg_in_o = gather_weight(w_in_o.astype(BF16), "gather_w_in_o")
    wg_out_o = gather_weight(w_out_o.astype(BF16), "gather_w_out_o")

    c_all = all_gather8(_pad_rows(c, 8), "gather_c")
    c16 = _pad_rows(jnp.concatenate([c_all[0::8], c_ctx[None, :]], axis=0), 16)
    m_part = adaln_fwd(c16, w_mod)
    m_all = all_gather8(m_part.reshape(depth * 16, nlm), "gather_mod")
    m_all = m_all.reshape(8, depth, 16, nlm)[0::2]
    m_full = m_all.transpose(1, 2, 0, 3).reshape(depth, 16, 4 * nlm) + b_mod[:, None, :]
    m_lat = lax.dynamic_index_in_dim(m_full, dev, axis=1, keepdims=False)
    m_ctx = m_full[:, 8]

    def seg2(l, part):
        return jnp.stack([m_ctx[l, part * d:(part + 1) * d], m_lat[l, part * d:(part + 1) * d]])[:, None, :]

    cos_t, sin_t = _rope_tables(lc, s)

    small_gathered = all_gather8(
        _pad_rows(jnp.concatenate([conv_a_w.reshape(n_even * CONV_A_TAPS, -1),
                                   conv_b_w.reshape(n_even * CONV_B_TAPS, -1)], axis=0), 72), "gather_taps")
    taps = small_gathered.reshape(8, 72, -1)[0::2]
    taps = taps.transpose(1, 0, 2).reshape(72, d)
    caw = taps[:n_even * CONV_A_TAPS].reshape(n_even, CONV_A_TAPS, d)
    cbw = taps[n_even * CONV_A_TAPS:n_even * (CONV_A_TAPS + CONV_B_TAPS)].reshape(n_even, CONV_B_TAPS, d)
    caw_pad = jnp.concatenate([caw, jnp.zeros((n_even, 32 - CONV_A_TAPS, d), F32)], axis=1)
    cbw_pad = jnp.concatenate([cbw, jnp.zeros((n_even, 8 - CONV_B_TAPS, d), F32)], axis=1)

    xc = jnp.concatenate([ctx[0], x[0]], axis=0)
    saved = []
    for l in range(depth):
        i = l // 2
        shift2, scale2, gate2 = seg2(l, 0), seg2(l, 1), seg2(l, 2)
        h = mod_fwd(xc, scale2, shift2, tm)
        if l % 2 == 0:
            p = mm_nn(h, wg_in_e[i], BF16)
            u1 = conv_a_fwd(p, caw_pad[i], conv_a_b[i][None, :], lc, d)
            a_out = ln_a_fwd(u1, p, norm_a_g[i][None, :], norm_a_b[i][None, :], tm)
            b_out = conv_b_fwd(p, cbw_pad[i], lc, d)
            ab = jnp.concatenate([a_out, b_out], axis=1)
            y = mm_nn(ab, wg_out_e[i].reshape(1, 2 * d, d), F32)
            saved.append(dict(xc=xc, h=h, p=p, u1=u1, ab=ab, y=y))
        else:
            qgkv = mm_nn(h, wg_in_o[i], BF16)
            qr, kr = qk_fwd(qgkv, q_norm_g[i][None, :], k_norm_g[i][None, :], cos_t, sin_t, ad, kvd, tm,
                            scale * LOG2_E)
            o, og, lse = flash_fwd(qr, kr, qgkv, ad, kvd, tm)
            y = mm_nn(og, wg_out_o[i].reshape(1, ad, d), F32)
            saved.append(dict(xc=xc, h=h, qgkv=qgkv, qr=qr, kr=kr, o=o, og=og, lse=lse, y=y))
        xc = post_ln_fwd(xc, y, gate2, post_ln_g[l][None, :], post_ln_b[l][None, :], alpha, tm)

    dxc, loss_acc = loss_head(xc, loss_target[0], lc, tm)
    loss = lax.psum(0.5 / d * jnp.sum(loss_acc[0]), MESH_AXES)

    g_in_e, g_out_e, g_in_o, g_out_o = [None] * 2, [None] * 2, [None] * 2, [None] * 2
    d_mod_lat, d_mod_ctx = [None] * depth, [None] * depth
    d_pln_g, d_pln_b = [None] * depth, [None] * depth
    d_cab, d_nag, d_nab, d_caw, d_cbw = [None] * 2, [None] * 2, [None] * 2, [None] * 2, [None] * 2
    d_qg, d_kg = [None] * 2, [None] * 2
    for l in reversed(range(depth)):
        i = l // 2
        sv = saved[l]
        scale2, gate2 = seg2(l, 1), seg2(l, 2)
        dzx, dy, acc_ln = post_ln_bwd(dxc, sv["xc"], sv["y"], gate2, post_ln_g[l][None, :], alpha, tm)
        d_pln_g[l] = acc_ln[0, 1] + acc_ln[1, 1]
        d_pln_b[l] = acc_ln[0, 2] + acc_ln[1, 2]
        if l % 2 == 0:
            w_out3 = wg_out_e[i].reshape(1, 2 * d, d)
            dab = mm_nt(dy, w_out3, BF16)
            g_out_e[i] = mm_tn(sv["ab"], dy, 1).reshape(4, 2 * d // 4, d)
            du1, d_agate, acc_a = ln_a_bwd(dab, sv["u1"], sv["p"], norm_a_g[i][None, :], norm_a_b[i][None, :], tm)
            d_nag[i], d_nab[i], d_cab[i] = acc_a[0], acc_a[1], acc_a[2]
            d_aval, d_aglu, d_caw[i] = conv_a_bwd(du1, sv["p"], caw_pad[i], lc, d)
            d_bx, d_bb, d_bc, d_bg, d_cbw[i] = conv_b_bwd(dab, sv["p"], cbw_pad[i], lc, d)
            dp = jnp.concatenate([d_aval, d_aglu, d_agate, d_bx, d_bb, d_bc, d_bg], axis=1)
            dh = mm_nt(dp, wg_in_e[i], F32)
            g_in_e[i] = mm_tn(sv["h"], dp, 4)
        else:
            w_out3 = wg_out_o[i].reshape(1, ad, d)
            dog = mm_nt(dy, w_out3, BF16)
            g_out_o[i] = mm_tn(sv["og"], dy, 1).reshape(4, ad // 4, d)
            do, dgate, delta = gate_bwd(dog, sv["o"], sv["qgkv"], ad, kvd, tm)
            dqr = flash_bwd_dq(sv["qr"], do, sv["kr"], sv["qgkv"], sv["lse"], delta, ad, kvd, tm, scale)
            dkr, dv = flash_bwd_dkv(sv["qr"], do, sv["kr"], sv["qgkv"], _rows_per_head(sv["lse"], tm, hpg),
                                    _rows_per_head(delta, tm, hpg), ad, kvd, tm)
            dq, dk, acc_qk = qk_bwd(dqr, dkr, sv["qgkv"], q_norm_g[i][None, :], k_norm_g[i][None, :],
                                    cos_t, sin_t, ad, kvd, tm)
            d_qg[i], d_kg[i] = acc_qk[0], acc_qk[1]
            dqgkv = jnp.concatenate([dq, dgate, dk, dv], axis=1)
            dh = mm_nt(dqgkv, wg_in_o[i], F32)
            g_in_o[i] = mm_tn(sv["h"], dqgkv, 4)
        dxc, acc_mod = mod_bwd(dh, dzx, sv["xc"], scale2, tm)
        d_mod_ctx[l] = jnp.stack([acc_mod[0, 0], acc_mod[0, 1], acc_ln[0, 0]])
        d_mod_lat[l] = jnp.stack([acc_mod[1, 0], acc_mod[1, 1], acc_ln[1, 0]])
    grad_x = dxc[lc:][None]

    pack = jnp.concatenate(
        [jnp.concatenate(d_mod_ctx, axis=0),
         jnp.stack(d_pln_g), jnp.stack(d_pln_b),
         jnp.stack(d_cab), jnp.stack(d_nag), jnp.stack(d_nab),
         jnp.concatenate(d_caw, axis=0),
         jnp.concatenate(d_cbw, axis=0),
         _pad_cols(jnp.stack(d_qg), d), _pad_cols(jnp.stack(d_kg), d),
         jnp.zeros((2, d), F32),
         jnp.concatenate(d_mod_lat, axis=0),
         jnp.zeros((4, d), F32)], axis=0)
    gathered = all_gather8(pack, "gather_small").reshape(8, 128, d)
    small = sum_leading(gathered, "sum_small")
    dm_ctx = small[0:12].reshape(depth, 1, 3 * d)
    dm_lat = gathered[:, 112:124].reshape(8, depth, 3 * d).transpose(1, 0, 2)
    dm = jnp.concatenate([dm_lat, dm_ctx, jnp.zeros((depth, 7, 3 * d), F32)], axis=1)
    g_b_mod = sum_leading(dm.transpose(1, 0, 2), "sum_b_mod")
    dm_shard = lax.dynamic_slice_in_dim(dm, shard * nlm, nlm, axis=2)
    g_w_mod, dc_part = adaln_bwd(c16.T, dm_shard, w_mod)
    dc_all = all_gather8(_pad_rows(dc_part[:, 8, :], 8), "gather_dc").reshape(8, 8, d)
    g_c_ctx = c_ctx_grad(dc_all[0::2, :depth].reshape(4 * depth, 1, d), c_ctx[None, :])[0]

    g_pln_g, g_pln_b = small[12:16], small[16:20]
    g_cab, g_nag, g_nab = small[20:22], small[22:24], small[24:26]
    dch = d // 4
    g_caw = lax.dynamic_slice_in_dim(small[26:90].reshape(2, 32, d)[:, :CONV_A_TAPS], shard * dch, dch, axis=2)
    g_cbw = lax.dynamic_slice_in_dim(small[90:106].reshape(2, 8, d)[:, :CONV_B_TAPS], shard * dch, dch, axis=2)
    g_qg, g_kg = small[106:108, :HEAD_DIM], small[108:110, :HEAD_DIM]

    g_w_in_e = reduce_scatter_grad(g_in_e[0], g_in_e[1], "in_e")
    g_w_out_e = reduce_scatter_grad(g_out_e[0], g_out_e[1], "out_e")
    g_w_in_o = reduce_scatter_grad(g_in_o[0], g_in_o[1], "in_o")
    g_w_out_o = reduce_scatter_grad(g_out_o[0], g_out_o[1], "out_o")

    grads = [g_c_ctx, g_w_mod, g_b_mod, g_pln_g, g_pln_b, g_w_in_e, g_caw, g_cab, g_nag, g_nab, g_cbw,
             g_w_out_e, g_w_in_o, g_qg, g_kg, g_w_out_o]
    weights = [c_ctx, w_mod, b_mod, post_ln_g, post_ln_b, w_in_e, conv_a_w, conv_a_b, norm_a_g, norm_a_b,
               conv_b_w, w_out_e, w_in_o, q_norm_g, k_norm_g, w_out_o]
    ms = [m_c_ctx, m_w_mod, m_b_mod, m_post_ln_g, m_post_ln_b, m_w_in_e, m_conv_a_w, m_conv_a_b, m_norm_a_g,
          m_norm_a_b, m_conv_b_w, m_w_out_e, m_w_in_o, m_q_norm_g, m_k_norm_g, m_w_out_o]
    vs = [v_c_ctx, v_w_mod, v_b_mod, v_post_ln_g, v_post_ln_b, v_w_in_e, v_conv_a_w, v_conv_a_b, v_norm_a_g,
          v_norm_a_b, v_conv_b_w, v_w_out_e, v_w_in_o, v_q_norm_g, v_k_norm_g, v_w_out_o]
    deltas, new_ms, new_vs = [], [], []
    for wv, gv, mv, vv in zip(weights, grads, ms, vs):
        dl, nm, nv = adamw(wv, gv, mv, vv)
        deltas.append(dl)
        new_ms.append(nm)
        new_vs.append(nv)
    return (loss, grad_x, *grads, *deltas, *new_ms, *new_vs)
```

```python
import functools

import jax
import jax.numpy as jnp
from jax import lax
from jax.experimental import pallas as pl
from jax.experimental.pallas import tpu as pltpu

F32 = jnp.float32
BF16 = jnp.bfloat16

LANES = 128
SUBLANES = 8
HEAD_DIM = 128
GQA_GROUP = 4
GRID_W = 64
ROPE_THETA = 10000.0
LN_EPS = 1e-5
RMS_EPS = 1e-6
CONV_A_TAPS = 31
CONV_B_TAPS = 3
HALO = 16
CONV_ROWS = 128
ADAM_LR = 0.001
ADAM_B1 = 0.9
ADAM_B2 = 0.999
ADAM_EPS = 1e-08
ADAM_WD = 0.01
ADAM_STEP = 10
VMEM_LIMIT = 56 * 1024 * 1024
D2D_PARTS = 4
LOG2_E = 1.4426950408889634
LN_2 = 0.6931471805599453
MESH_AXES = ("x", "y", "c")
MESH = pl.DeviceIdType.MESH
ANY = pl.BlockSpec(memory_space=pl.ANY)
VMEM_FULL = pl.BlockSpec(memory_space=pltpu.VMEM)


def _sds(shape, dtype):
    return jax.ShapeDtypeStruct(tuple(shape), dtype)


def _cp(*sem):
    return pltpu.CompilerParams(dimension_semantics=sem, vmem_limit_bytes=VMEM_LIMIT)


def _pick(n, cands):
    for c in cands:
        if n % c == 0:
            return c
    return n


def _sigmoid(x):
    return 1.0 / (1.0 + jnp.exp(-x))


def _silu(x):
    return x * _sigmoid(x)


def _dsilu(x):
    s = _sigmoid(x)
    return s * (1.0 + x * (1.0 - s))


def _row(tm, d):
    return pl.BlockSpec((tm, d), lambda j: (j, 0))


def _seg(d):
    return pl.BlockSpec((None, 1, d), lambda j: (jnp.minimum(j, 1), 0, 0))


def _vec(d):
    return pl.BlockSpec((1, d), lambda j: (0, 0))


def _colblk(tm, width, blk):
    return pl.BlockSpec((tm, width), lambda j: (j, blk))


def _seg_acc(d):
    return pl.BlockSpec((None, SUBLANES, d), lambda j: (jnp.minimum(j, 1), 0, 0))


def _ln_stats(z):
    mu = jnp.mean(z, axis=-1, keepdims=True)
    zc = z - mu
    var = jnp.mean(zc * zc, axis=-1, keepdims=True)
    rstd = lax.rsqrt(var + LN_EPS)
    return zc * rstd, rstd


def _ln_bwd(dxh, xhat, rstd):
    m1 = jnp.mean(dxh, axis=-1, keepdims=True)
    m2 = jnp.mean(dxh * xhat, axis=-1, keepdims=True)
    return rstd * (dxh - m1 - xhat * m2)


def _colsum(v):
    return jnp.sum(v, axis=0, keepdims=True)


def mod_fwd(xc, scale2, shift2, tm):
    t, d = xc.shape

    def body(x_ref, sc_ref, sh_ref, h_ref):
        h_ref[...] = (x_ref[...] * (1.0 + sc_ref[...]) + sh_ref[...]).astype(h_ref.dtype)

    return pl.pallas_call(
        body, name="mod_fwd", grid=(t // tm,),
        in_specs=[_row(tm, d), _seg(d), _seg(d)], out_specs=_row(tm, d),
        out_shape=_sds((t, d), BF16), compiler_params=_cp("parallel"))(xc, scale2, shift2)


def post_ln_fwd(xc, y, gate2, g, b, alpha, tm):
    t, d = xc.shape

    def body(x_ref, y_ref, gt_ref, g_ref, b_ref, o_ref):
        z = alpha * x_ref[...] + gt_ref[...] * y_ref[...]
        xhat, _ = _ln_stats(z)
        o_ref[...] = xhat * g_ref[...] + b_ref[...]

    return pl.pallas_call(
        body, name="post_ln_fwd", grid=(t // tm,),
        in_specs=[_row(tm, d), _row(tm, d), _seg(d), _vec(d), _vec(d)], out_specs=_row(tm, d),
        out_shape=_sds((t, d), F32), compiler_params=_cp("parallel"))(xc, y, gate2, g, b)


def post_ln_bwd(dout, xc, y, gate2, g, alpha, tm):
    t, d = xc.shape

    def body(do_ref, x_ref, y_ref, gt_ref, g_ref, dzx_ref, dy_ref, acc_ref):
        @pl.when(pl.program_id(0) <= 1)
        def _():
            acc_ref[...] = jnp.zeros_like(acc_ref)

        yv = y_ref[...]
        gate = gt_ref[...]
        xhat, rstd = _ln_stats(alpha * x_ref[...] + gate * yv)
        dout = do_ref[...]
        dz = _ln_bwd(dout * g_ref[...], xhat, rstd)
        dzx_ref[...] = alpha * dz
        dy_ref[...] = (gate * dz).astype(dy_ref.dtype)
        acc_ref[0:1, :] += _colsum(dz * yv)
        acc_ref[1:2, :] += _colsum(dout * xhat)
        acc_ref[2:3, :] += _colsum(dout)

    return pl.pallas_call(
        body, name="post_ln_bwd", grid=(t // tm,),
        in_specs=[_row(tm, d), _row(tm, d), _row(tm, d), _seg(d), _vec(d)],
        out_specs=[_row(tm, d), _row(tm, d), _seg_acc(d)],
        out_shape=[_sds((t, d), F32), _sds((t, d), BF16), _sds((2, SUBLANES, d), F32)],
        compiler_params=_cp("arbitrary"))(dout, xc, y, gate2, g)


def mod_bwd(dh, dzx, xc, scale2, tm):
    t, d = xc.shape

    def body(dh_ref, dzx_ref, x_ref, sc_ref, dx_ref, acc_ref):
        @pl.when(pl.program_id(0) <= 1)
        def _():
            acc_ref[...] = jnp.zeros_like(acc_ref)

        dhv = dh_ref[...].astype(F32)
        dx_ref[...] = dzx_ref[...] + dhv * (1.0 + sc_ref[...])
        acc_ref[0:1, :] += _colsum(dhv)
        acc_ref[1:2, :] += _colsum(dhv * x_ref[...])

    return pl.pallas_call(
        body, name="mod_bwd", grid=(t // tm,),
        in_specs=[_row(tm, d), _row(tm, d), _row(tm, d), _seg(d)],
        out_specs=[_row(tm, d), _seg_acc(d)],
        out_shape=[_sds((t, d), F32), _sds((2, SUBLANES, d), F32)],
        compiler_params=_cp("arbitrary"))(dh, dzx, xc, scale2)


def loss_head(xc, target, lc, tm):
    t, d = xc.shape

    def body(x_ref, t_ref, dx_ref, acc_ref):
        j = pl.program_id(0)

        @pl.when(j == 0)
        def _():
            acc_ref[...] = jnp.zeros_like(acc_ref)
            dx_ref[...] = jnp.zeros_like(dx_ref)

        @pl.when(j > 0)
        def _():
            err = x_ref[...] - t_ref[...]
            dx_ref[...] = err * (1.0 / d)
            col = _colsum(err * err)
            tot = col[:, 0:LANES]
            for k in range(1, d // LANES):
                tot = tot + col[:, k * LANES:(k + 1) * LANES]
            acc_ref[0:1, :] += tot

    nlc = lc // tm
    return pl.pallas_call(
        body, name="loss_head", grid=(t // tm,),
        in_specs=[_row(tm, d), pl.BlockSpec((tm, d), lambda j: (jnp.maximum(j - nlc, 0), 0))],
        out_specs=[_row(tm, d), pl.BlockSpec((SUBLANES, LANES), lambda j: (0, 0))],
        out_shape=[_sds((t, d), F32), _sds((SUBLANES, LANES), F32)],
        compiler_params=_cp("arbitrary"))(xc, target)


def mm_nn(a, w3, out_dtype):
    m, k = a.shape
    ns, _, nl = w3.shape
    tm = _pick(m, (768, 512, 256, 128))
    tn = _pick(nl, (512, 256, 128))
    npj = nl // tn

    def body(a_ref, w_ref, o_ref):
        o_ref[...] = jnp.dot(a_ref[...], w_ref[...], preferred_element_type=F32).astype(o_ref.dtype)

    return pl.pallas_call(
        body, name="mm_nn", grid=(m // tm, ns * npj),
        in_specs=[pl.BlockSpec((tm, k), lambda i, j: (i, 0)),
                  pl.BlockSpec((None, k, tn), lambda i, j: (j // npj, 0, j % npj))],
        out_specs=pl.BlockSpec((tm, tn), lambda i, j: (i, j)),
        out_shape=_sds((m, ns * nl), out_dtype), compiler_params=_cp("parallel", "parallel"))(a, w3)


def mm_nt(a, w3, out_dtype):
    m, _ = a.shape
    ns, k, nl = w3.shape
    tm = _pick(m, (768, 512, 256, 128))
    tk = _pick(k, (2048, 1024, 512, 256, 128))
    tn = _pick(nl, (512, 256, 128))
    npj = nl // tn
    nsteps = ns * npj

    def body(a_ref, w_ref, o_ref, acc_ref):
        n = pl.program_id(2)

        @pl.when(n == 0)
        def _():
            acc_ref[...] = jnp.zeros_like(acc_ref)

        acc_ref[...] += lax.dot_general(a_ref[...], w_ref[...], (((1,), (1,)), ((), ())),
                                        preferred_element_type=F32)

        @pl.when(n == nsteps - 1)
        def _():
            o_ref[...] = acc_ref[...].astype(o_ref.dtype)

    return pl.pallas_call(
        body, name="mm_nt", grid=(m // tm, k // tk, nsteps),
        in_specs=[pl.BlockSpec((tm, tn), lambda i, kk, n: (i, n)),
                  pl.BlockSpec((None, tk, tn), lambda i, kk, n: (n // npj, kk, n % npj))],
        out_specs=pl.BlockSpec((tm, tk), lambda i, kk, n: (i, kk)),
        out_shape=_sds((m, k), out_dtype), scratch_shapes=[pltpu.VMEM((tm, tk), F32)],
        compiler_params=_cp("parallel", "parallel", "arbitrary"))(a, w3)


def mm_tn(a, b, ns):
    m, k = a.shape
    nl = b.shape[1] // ns
    tm = _pick(m, (768, 512, 256, 128))
    tk = _pick(k, (1024, 512, 256, 128))
    tn = _pick(nl, (512, 256, 128))
    npj = nl // tn

    def body(a_ref, b_ref, o_ref):
        @pl.when(pl.program_id(2) == 0)
        def _():
            o_ref[...] = jnp.zeros_like(o_ref)

        o_ref[...] += lax.dot_general(a_ref[...], b_ref[...], (((0,), (0,)), ((), ())),
                                      preferred_element_type=F32)

    return pl.pallas_call(
        body, name="mm_tn", grid=(k // tk, ns * npj, m // tm),
        in_specs=[pl.BlockSpec((tm, tk), lambda i, j, r: (r, i)),
                  pl.BlockSpec((tm, tn), lambda i, j, r: (r, j))],
        out_specs=pl.BlockSpec((None, tk, tn), lambda i, j, r: (j // npj, i, j % npj)),
        out_shape=_sds((ns, k, nl), F32),
        compiler_params=_cp("parallel", "parallel", "arbitrary"))(a, b)


def _win_start(j, ncc):
    return pl.multiple_of(j * CONV_ROWS + jnp.where(j >= ncc, HALO, 0), SUBLANES)


def _tok_start(j):
    return pl.multiple_of(j * CONV_ROWS, CONV_ROWS)


def _shifted(xw, off):
    n = xw.shape[0]
    sh = (n - off) % n
    y = pltpu.roll(xw, sh, 0) if sh else xw
    return y[:CONV_ROWS]


def _conv_fwd(xw, w_ref, ntaps):
    pad = ntaps // 2
    acc = None
    for k in range(ntaps):
        term = w_ref[k:k + 1, :] * _shifted(xw, HALO + k - pad)
        acc = term if acc is None else acc + term
    return acc


def _conv_bwd_data(xw, w_ref, ntaps):
    pad = ntaps // 2
    acc = None
    for k in range(ntaps):
        term = w_ref[k:k + 1, :] * _shifted(xw, HALO - k + pad)
        acc = term if acc is None else acc + term
    return acc


def _conv_bwd_weight(dw_ref, d, xw, ntaps):
    pad = ntaps // 2
    for k in range(ntaps):
        dw_ref[k:k + 1, :] += _colsum(d * _shifted(xw, HALO + k - pad))


def _zero_halos(pad_ref, lc, t):
    z = jnp.zeros((HALO, LANES), F32)
    pad_ref[0:HALO, :] = z
    pad_ref[HALO + lc:2 * HALO + lc, :] = z
    pad_ref[2 * HALO + t:3 * HALO + t, :] = z


def _pad_dst(j, ncc):
    return pl.multiple_of(j * CONV_ROWS + HALO + jnp.where(j >= ncc, HALO, 0), SUBLANES)


def _chan(t, blk0):
    return pl.BlockSpec((t, LANES), lambda ct: (0, blk0 + ct))


def _tapw(rows):
    return pl.BlockSpec((rows, LANES), lambda ct: (0, ct))


def conv_a_fwd(p, w_pad, bias, lc, d):
    t = p.shape[0]
    nct, nch, ncc = d // LANES, t // CONV_ROWS, lc // CONV_ROWS

    def body(av_ref, ag_ref, w_ref, b_ref, u1_ref, pad_ref):
        _zero_halos(pad_ref, lc, t)

        def fill(j, carry):
            rows = pl.ds(_tok_start(j), CONV_ROWS)
            u0 = av_ref[rows, :].astype(F32) * _sigmoid(ag_ref[rows, :].astype(F32))
            pad_ref[pl.ds(_pad_dst(j, ncc), CONV_ROWS), :] = u0
            return carry

        lax.fori_loop(0, nch, fill, 0)

        def conv(j, carry):
            xw = pad_ref[pl.ds(_win_start(j, ncc), CONV_ROWS + 2 * HALO), :]
            u1_ref[pl.ds(_tok_start(j), CONV_ROWS), :] = _conv_fwd(xw, w_ref, CONV_A_TAPS) + b_ref[...]
            return carry

        lax.fori_loop(0, nch, conv, 0)

    return pl.pallas_call(
        body, name="conv_a_fwd", grid=(nct,),
        in_specs=[_chan(t, 0), _chan(t, nct), _tapw(32), _tapw(1)],
        out_specs=_chan(t, 0), out_shape=_sds((t, d), F32),
        scratch_shapes=[pltpu.VMEM((t + 3 * HALO, LANES), F32)],
        compiler_params=_cp("parallel"))(p, p, w_pad, bias)


def conv_b_fwd(p, w_pad, lc, d):
    t = p.shape[0]
    nct, nch, ncc = d // LANES, t // CONV_ROWS, lc // CONV_ROWS

    def body(bx_ref, bb_ref, bc_ref, bg_ref, w_ref, o_ref, pad_ref):
        _zero_halos(pad_ref, lc, t)

        def fill(j, carry):
            rows = pl.ds(_tok_start(j), CONV_ROWS)
            pad_ref[pl.ds(_pad_dst(j, ncc), CONV_ROWS), :] = (
                bc_ref[rows, :].astype(F32) * bx_ref[rows, :].astype(F32))
            return carry

        lax.fori_loop(0, nch, fill, 0)

        def conv(j, carry):
            rows = pl.ds(_tok_start(j), CONV_ROWS)
            xw = pad_ref[pl.ds(_win_start(j, ncc), CONV_ROWS + 2 * HALO), :]
            v = _conv_fwd(xw, w_ref, CONV_B_TAPS)
            o_ref[rows, :] = (bb_ref[rows, :].astype(F32) * v
                              * _silu(bg_ref[rows, :].astype(F32))).astype(o_ref.dtype)
            return carry

        lax.fori_loop(0, nch, conv, 0)

    return pl.pallas_call(
        body, name="conv_b_fwd", grid=(nct,),
        in_specs=[_chan(t, 3 * nct), _chan(t, 4 * nct), _chan(t, 5 * nct), _chan(t, 6 * nct), _tapw(8)],
        out_specs=_chan(t, 0), out_shape=_sds((t, d), BF16),
        scratch_shapes=[pltpu.VMEM((t + 3 * HALO, LANES), F32)],
        compiler_params=_cp("parallel"))(p, p, p, p, w_pad)


def ln_a_fwd(u1, p, g, b, tm):
    t, d = u1.shape

    def body(u_ref, ag_ref, g_ref, b_ref, o_ref):
        xhat, _ = _ln_stats(u_ref[...])
        u2 = xhat * g_ref[...] + b_ref[...]
        o_ref[...] = (_silu(u2) * _silu(ag_ref[...].astype(F32))).astype(o_ref.dtype)

    return pl.pallas_call(
        body, name="ln_a_fwd", grid=(t // tm,),
        in_specs=[_row(tm, d), _colblk(tm, d, 2), _vec(d), _vec(d)], out_specs=_row(tm, d),
        out_shape=_sds((t, d), BF16), compiler_params=_cp("parallel"))(u1, p, g, b)


def ln_a_bwd(dab, u1, p, g, b, tm):
    t, d = u1.shape

    def body(da_ref, u_ref, ag_ref, g_ref, b_ref, du_ref, dag_ref, acc_ref):
        @pl.when(pl.program_id(0) == 0)
        def _():
            acc_ref[...] = jnp.zeros_like(acc_ref)

        xhat, rstd = _ln_stats(u_ref[...])
        u2 = xhat * g_ref[...] + b_ref[...]
        ag = ag_ref[...].astype(F32)
        da = da_ref[...].astype(F32)
        dag_ref[...] = (da * _silu(u2) * _dsilu(ag)).astype(dag_ref.dtype)
        du2 = da * _silu(ag) * _dsilu(u2)
        du1 = _ln_bwd(du2 * g_ref[...], xhat, rstd)
        du_ref[...] = du1
        acc_ref[0:1, :] += _colsum(du2 * xhat)
        acc_ref[1:2, :] += _colsum(du2)
        acc_ref[2:3, :] += _colsum(du1)

    return pl.pallas_call(
        body, name="ln_a_bwd", grid=(t // tm,),
        in_specs=[_colblk(tm, d, 0), _row(tm, d), _colblk(tm, d, 2), _vec(d), _vec(d)],
        out_specs=[_row(tm, d), _row(tm, d), pl.BlockSpec((SUBLANES, d), lambda j: (0, 0))],
        out_shape=[_sds((t, d), F32), _sds((t, d), BF16), _sds((SUBLANES, d), F32)],
        compiler_params=_cp("arbitrary"))(dab, u1, p, g, b)


def conv_a_bwd(du1, p, w_pad, lc, d):
    t = p.shape[0]
    nct, nch, ncc = d // LANES, t // CONV_ROWS, lc // CONV_ROWS

    def body(du_ref, av_ref, ag_ref, w_ref, dav_ref, dag_ref, dw_ref, pad_u, pad_d):
        _zero_halos(pad_u, lc, t)
        _zero_halos(pad_d, lc, t)
        dw_ref[...] = jnp.zeros_like(dw_ref)

        def fill(j, carry):
            rows = pl.ds(_tok_start(j), CONV_ROWS)
            dst = pl.ds(_pad_dst(j, ncc), CONV_ROWS)
            pad_u[dst, :] = av_ref[rows, :].astype(F32) * _sigmoid(ag_ref[rows, :].astype(F32))
            pad_d[dst, :] = du_ref[rows, :]
            return carry

        lax.fori_loop(0, nch, fill, 0)

        def step(j, carry):
            rows = pl.ds(_tok_start(j), CONV_ROWS)
            win = pl.ds(_win_start(j, ncc), CONV_ROWS + 2 * HALO)
            du0 = _conv_bwd_data(pad_d[win, :], w_ref, CONV_A_TAPS)
            sig = _sigmoid(ag_ref[rows, :].astype(F32))
            dav_ref[rows, :] = (du0 * sig).astype(dav_ref.dtype)
            dag_ref[rows, :] = (du0 * av_ref[rows, :].astype(F32) * sig * (1.0 - sig)).astype(dag_ref.dtype)
            _conv_bwd_weight(dw_ref, du_ref[rows, :], pad_u[win, :], CONV_A_TAPS)
            return carry

        lax.fori_loop(0, nch, step, 0)

    return pl.pallas_call(
        body, name="conv_a_bwd", grid=(nct,),
        in_specs=[_chan(t, 0), _chan(t, 0), _chan(t, nct), _tapw(32)],
        out_specs=[_chan(t, 0), _chan(t, 0), _tapw(32)],
        out_shape=[_sds((t, d), BF16), _sds((t, d), BF16), _sds((32, d), F32)],
        scratch_shapes=[pltpu.VMEM((t + 3 * HALO, LANES), F32), pltpu.VMEM((t + 3 * HALO, LANES), F32)],
        compiler_params=_cp("parallel"))(du1, p, p, w_pad)


def conv_b_bwd(dab, p, w_pad, lc, d):
    t = p.shape[0]
    nct, nch, ncc = d // LANES, t // CONV_ROWS, lc // CONV_ROWS

    def body(db_ref, bx_ref, bb_ref, bc_ref, bg_ref, w_ref,
             dbx_ref, dbb_ref, dbc_ref, dbg_ref, dw_ref, pad_t, pad_d):
        _zero_halos(pad_t, lc, t)
        _zero_halos(pad_d, lc, t)
        dw_ref[...] = jnp.zeros_like(dw_ref)

        def fill(j, carry):
            rows = pl.ds(_tok_start(j), CONV_ROWS)
            pad_t[pl.ds(_pad_dst(j, ncc), CONV_ROWS), :] = (
                bc_ref[rows, :].astype(F32) * bx_ref[rows, :].astype(F32))
            return carry

        lax.fori_loop(0, nch, fill, 0)

        def first(j, carry):
            rows = pl.ds(_tok_start(j), CONV_ROWS)
            xw = pad_t[pl.ds(_win_start(j, ncc), CONV_ROWS + 2 * HALO), :]
            v = _conv_fwd(xw, w_ref, CONV_B_TAPS)
            bg = bg_ref[rows, :].astype(F32)
            bb = bb_ref[rows, :].astype(F32)
            db = db_ref[rows, :].astype(F32)
            sg = _silu(bg)
            dbb_ref[rows, :] = (db * v * sg).astype(dbb_ref.dtype)
            dbg_ref[rows, :] = (db * bb * v * _dsilu(bg)).astype(dbg_ref.dtype)
            dv = db * bb * sg
            pad_d[pl.ds(_pad_dst(j, ncc), CONV_ROWS), :] = dv
            _conv_bwd_weight(dw_ref, dv, xw, CONV_B_TAPS)
            return carry

        lax.fori_loop(0, nch, first, 0)

        def second(j, carry):
            rows = pl.ds(_tok_start(j), CONV_ROWS)
            dt = _conv_bwd_data(pad_d[pl.ds(_win_start(j, ncc), CONV_ROWS + 2 * HALO), :], w_ref, CONV_B_TAPS)
            dbc_ref[rows, :] = (dt * bx_ref[rows, :].astype(F32)).astype(dbc_ref.dtype)
            dbx_ref[rows, :] = (dt * bc_ref[rows, :].astype(F32)).astype(dbx_ref.dtype)
            return carry

        lax.fori_loop(0, nch, second, 0)

    return pl.pallas_call(
        body, name="conv_b_bwd", grid=(nct,),
        in_specs=[_chan(t, nct), _chan(t, 3 * nct), _chan(t, 4 * nct), _chan(t, 5 * nct), _chan(t, 6 * nct),
                  _tapw(8)],
        out_specs=[_chan(t, 0)] * 4 + [_tapw(8)],
        out_shape=[_sds((t, d), BF16)] * 4 + [_sds((8, d), F32)],
        scratch_shapes=[pltpu.VMEM((t + 3 * HALO, LANES), F32), pltpu.VMEM((t + 3 * HALO, LANES), F32)],
        compiler_params=_cp("parallel"))(dab, p, p, p, p, w_pad)


def _swap_halves(z, first_half):
    return jnp.where(first_half, pltpu.roll(z, 96, 1), pltpu.roll(z, 32, 1))


def _first_half_mask(rows):
    lane = lax.broadcasted_iota(jnp.int32, (rows, HEAD_DIM), 1)
    return (lane & 32) == 0


def qk_fwd(qgkv, qg, kg, cos_t, sin_t, ad, kvd, tm, qscale):
    t = qgkv.shape[0]

    def body(q_ref, k_ref, qg_ref, kg_ref, c_ref, s_ref, qo_ref, ko_ref):
        first = _first_half_mask(tm)
        cosv, sinv = c_ref[...], s_ref[...]

        def head(x, gain):
            inv = lax.rsqrt(jnp.mean(x * x, axis=-1, keepdims=True) + RMS_EPS)
            yv = x * inv * gain
            return yv * cosv + _swap_halves(yv, first) * sinv

        for h in range(ad // HEAD_DIM):
            sl = slice(h * HEAD_DIM, (h + 1) * HEAD_DIM)
            qo_ref[:, sl] = (head(q_ref[:, sl].astype(F32), qg_ref[...]) * qscale).astype(qo_ref.dtype)
        for h in range(kvd // HEAD_DIM):
            sl = slice(h * HEAD_DIM, (h + 1) * HEAD_DIM)
            ko_ref[:, sl] = head(k_ref[:, sl].astype(F32), kg_ref[...]).astype(ko_ref.dtype)

    return pl.pallas_call(
        body, name="qk_fwd", grid=(t // tm,),
        in_specs=[_colblk(tm, ad, 0), _colblk(tm, kvd, 2 * ad // kvd), _vec(HEAD_DIM), _vec(HEAD_DIM),
                  _row(tm, HEAD_DIM), _row(tm, HEAD_DIM)],
        out_specs=[_row(tm, ad), _row(tm, kvd)],
        out_shape=[_sds((t, ad), BF16), _sds((t, kvd), BF16)],
        compiler_params=_cp("parallel"))(qgkv, qgkv, qg, kg, cos_t, sin_t)


def qk_bwd(dqr, dkr, qgkv, qg, kg, cos_t, sin_t, ad, kvd, tm):
    t = qgkv.shape[0]

    def body(dq_ref, dk_ref, q_ref, k_ref, qg_ref, kg_ref, c_ref, s_ref, dqo_ref, dko_ref, acc_ref):
        @pl.when(pl.program_id(0) == 0)
        def _():
            acc_ref[...] = jnp.zeros_like(acc_ref)

        first = _first_half_mask(tm)
        cosv, sinv = c_ref[...], s_ref[...]

        def head(x, gain, dout):
            inv = lax.rsqrt(jnp.mean(x * x, axis=-1, keepdims=True) + RMS_EPS)
            xn = x * inv
            dy = dout * cosv + _swap_halves(dout * sinv, first)
            dxn = dy * gain
            dx = inv * (dxn - xn * jnp.mean(dxn * xn, axis=-1, keepdims=True))
            return dx, _colsum(dy * xn)

        dqg = jnp.zeros((1, HEAD_DIM), F32)
        for h in range(ad // HEAD_DIM):
            sl = slice(h * HEAD_DIM, (h + 1) * HEAD_DIM)
            dx, dg = head(q_ref[:, sl].astype(F32), qg_ref[...], dq_ref[:, sl])
            dqo_ref[:, sl] = dx.astype(dqo_ref.dtype)
            dqg = dqg + dg
        dkg = jnp.zeros((1, HEAD_DIM), F32)
        for h in range(kvd // HEAD_DIM):
            sl = slice(h * HEAD_DIM, (h + 1) * HEAD_DIM)
            dx, dg = head(k_ref[:, sl].astype(F32), kg_ref[...], dk_ref[:, sl])
            dko_ref[:, sl] = dx.astype(dko_ref.dtype)
            dkg = dkg + dg
        acc_ref[0:1, :] += dqg
        acc_ref[1:2, :] += dkg

    return pl.pallas_call(
        body, name="qk_bwd", grid=(t // tm,),
        in_specs=[_row(tm, ad), _row(tm, kvd), _colblk(tm, ad, 0), _colblk(tm, kvd, 2 * ad // kvd),
                  _vec(HEAD_DIM), _vec(HEAD_DIM), _row(tm, HEAD_DIM), _row(tm, HEAD_DIM)],
        out_specs=[_row(tm, ad), _row(tm, kvd), pl.BlockSpec((SUBLANES, HEAD_DIM), lambda j: (0, 0))],
        out_shape=[_sds((t, ad), BF16), _sds((t, kvd), BF16), _sds((SUBLANES, HEAD_DIM), F32)],
        compiler_params=_cp("arbitrary"))(dqr, dkr, qgkv, qgkv, qg, kg, cos_t, sin_t)


_NT = (((1,), (1,)), ((), ()))


def _chunks_t(a, nkv, tm):
    t = a.shape[0]
    return a.reshape(t // tm, tm, nkv, HEAD_DIM).transpose(2, 0, 3, 1)


def flash_fwd(qr, kr, qgkv, ad, kvd, tm):
    t = qr.shape[0]
    nkv = kvd // HEAD_DIM
    gw = ad // nkv
    hpg = gw // HEAD_DIM
    nt = t // tm
    v_blk0 = (2 * ad + kvd) // HEAD_DIM
    gate_blk0 = ad // gw

    def body(q_ref, k_ref, v_ref, g_ref, o_ref, og_ref, lse_ref):
        qi = pl.program_id(1)
        nkc = jnp.where(qi == 0, 1, nt)
        heads = [slice(h * HEAD_DIM, (h + 1) * HEAD_DIM) for h in range(hpg)]
        qs = [q_ref[:, sl] for sl in heads]

        def step(c, carry):
            rows = pl.ds(pl.multiple_of(c * tm, tm), tm)
            kc = k_ref[rows, :]
            vc = v_ref[rows, :]
            ss = [lax.dot_general(q, kc, _NT, preferred_element_type=F32) for q in qs]
            ps, stats = [], []
            for h in range(hpg):
                m, l, _ = carry[h]
                m_new = jnp.maximum(m, jnp.max(ss[h], axis=-1, keepdims=True))
                a = jnp.exp2(m - m_new)
                p = jnp.exp2(ss[h] - m_new)
                stats.append((m_new, a * l + jnp.sum(p, axis=-1, keepdims=True), a))
                ps.append(p.astype(BF16))
            return tuple((stats[h][0], stats[h][1],
                          stats[h][2] * carry[h][2] + jnp.dot(ps[h], vc, preferred_element_type=F32))
                         for h in range(hpg))

        init = tuple((jnp.full((tm, 1), -1e30, F32), jnp.zeros((tm, 1), F32), jnp.zeros((tm, HEAD_DIM), F32))
                     for _ in range(hpg))
        res = lax.fori_loop(0, nkc, step, init)
        lane = lax.broadcasted_iota(jnp.int32, (tm, LANES), 1)
        lse_blk = jnp.zeros((tm, LANES), F32)
        for h, sl in enumerate(heads):
            m, l, acc = res[h]
            o = acc / l
            o_ref[:, sl] = o.astype(o_ref.dtype)
            og_ref[:, sl] = (o * _silu(g_ref[:, sl].astype(F32))).astype(og_ref.dtype)
            lse_blk = jnp.where(lane == h, m + jnp.log(l) * LOG2_E, lse_blk)
        lse_ref[...] = lse_blk

    return pl.pallas_call(
        body, name="flash_fwd", grid=(nkv, nt),
        in_specs=[pl.BlockSpec((tm, gw), lambda g, i: (i, g)),
                  pl.BlockSpec((t, HEAD_DIM), lambda g, i: (0, g)),
                  pl.BlockSpec((t, HEAD_DIM), lambda g, i: (0, v_blk0 + g)),
                  pl.BlockSpec((tm, gw), lambda g, i: (i, gate_blk0 + g))],
        out_specs=[pl.BlockSpec((tm, gw), lambda g, i: (i, g)),
                   pl.BlockSpec((tm, gw), lambda g, i: (i, g)),
                   pl.BlockSpec((None, tm, LANES), lambda g, i: (g, i, 0))],
        out_shape=[_sds((t, ad), BF16), _sds((t, ad), BF16), _sds((nkv, t, LANES), F32)],
        compiler_params=_cp("parallel", "parallel"))(qr, kr, qgkv, qgkv)


def gate_bwd(dog, o, qgkv, ad, kvd, tm):
    t = o.shape[0]
    nkv = kvd // HEAD_DIM
    hpg = ad // nkv // HEAD_DIM

    def body(dog_ref, o_ref, g_ref, do_ref, dg_ref, dl_ref):
        lane = lax.broadcasted_iota(jnp.int32, (tm, LANES), 1)
        for grp in range(nkv):
            blk = jnp.zeros((tm, LANES), F32)
            for hh in range(hpg):
                h = grp * hpg + hh
                sl = slice(h * HEAD_DIM, (h + 1) * HEAD_DIM)
                dv = dog_ref[:, sl].astype(F32)
                ov = o_ref[:, sl].astype(F32)
                gv = g_ref[:, sl].astype(F32)
                doh = dv * _silu(gv)
                do_ref[:, sl] = doh.astype(do_ref.dtype)
                dg_ref[:, sl] = (dv * ov * _dsilu(gv)).astype(dg_ref.dtype)
                blk = jnp.where(lane == hh, jnp.sum(doh * ov, axis=-1, keepdims=True), blk)
            dl_ref[grp] = blk

    return pl.pallas_call(
        body, name="gate_bwd", grid=(t // tm,),
        in_specs=[_row(tm, ad), _row(tm, ad), _colblk(tm, ad, 1)],
        out_specs=[_row(tm, ad), _row(tm, ad), pl.BlockSpec((nkv, tm, LANES), lambda j: (0, j, 0))],
        out_shape=[_sds((t, ad), BF16), _sds((t, ad), BF16), _sds((nkv, t, LANES), F32)],
        compiler_params=_cp("parallel"))(dog, o, qgkv)


def flash_bwd(qr, do, kr, kt, qgkv, lse_t, delta_t, ad, kvd, tm, scale):
    t = qr.shape[0]
    nkv = kvd // HEAD_DIM
    gw = ad // nkv
    hpg = gw // HEAD_DIM
    nt = t // tm
    v_blk0 = (2 * ad + kvd) // HEAD_DIM

    def body(q_ref, do_ref, k_ref, v_ref, kt_ref, lse_ref, dl_ref, dq_ref, dk_ref, dv_ref):
        qi = pl.program_id(1)

        @pl.when(qi == 0)
        def _():
            dk_ref[...] = jnp.zeros_like(dk_ref)
            dv_ref[...] = jnp.zeros_like(dv_ref)

        nkc = jnp.where(qi == 0, 1, nt)
        heads = [slice(h * HEAD_DIM, (h + 1) * HEAD_DIM) for h in range(hpg)]
        qs = [q_ref[:, sl] for sl in heads]
        dos = [do_ref[:, sl] for sl in heads]
        lses = [lse_ref[h:h + 1, :] for h in range(hpg)]
        dls = [dl_ref[h:h + 1, :] for h in range(hpg)]

        def step(c, dqts):
            rows = pl.ds(pl.multiple_of(c * tm, tm), tm)
            kc = k_ref[rows, :]
            vc = v_ref[rows, :]
            ktc = kt_ref[c]
            dk = jnp.zeros((tm, HEAD_DIM), F32)
            dv = jnp.zeros((tm, HEAD_DIM), F32)
            out = []
            for h in range(hpg):
                st = lax.dot_general(kc, qs[h], _NT, preferred_element_type=F32)
                pt = jnp.exp2(st - lses[h])
                dv = dv + jnp.dot(pt.astype(BF16), dos[h], preferred_element_type=F32)
                dpt = lax.dot_general(vc, dos[h], _NT, preferred_element_type=F32)
                dst = (pt * (dpt - dls[h])).astype(BF16)
                dk = dk + jnp.dot(dst, qs[h], preferred_element_type=F32)
                out.append(dqts[h] + jnp.dot(ktc, dst, preferred_element_type=F32))
            dk_ref[rows, :] += dk
            dv_ref[rows, :] += dv
            return tuple(out)

        res = lax.fori_loop(0, nkc, step, tuple(jnp.zeros((HEAD_DIM, tm), F32) for _ in range(hpg)))
        for h, sl in enumerate(heads):
            dq_ref[:, sl] = res[h].T * scale

        @pl.when(qi == nt - 1)
        def _():
            dk_ref[...] = dk_ref[...] * LN_2

    return pl.pallas_call(
        body, name="flash_bwd", grid=(nkv, nt),
        in_specs=[pl.BlockSpec((tm, gw), lambda g, i: (i, g)),
                  pl.BlockSpec((tm, gw), lambda g, i: (i, g)),
                  pl.BlockSpec((t, HEAD_DIM), lambda g, i: (0, g)),
                  pl.BlockSpec((t, HEAD_DIM), lambda g, i: (0, v_blk0 + g)),
                  pl.BlockSpec((None, nt, HEAD_DIM, tm), lambda g, i: (g, 0, 0, 0)),
                  pl.BlockSpec((None, None, hpg, tm), lambda g, i: (g, i, 0, 0)),
                  pl.BlockSpec((None, None, hpg, tm), lambda g, i: (g, i, 0, 0))],
        out_specs=[pl.BlockSpec((tm, gw), lambda g, i: (i, g)),
                   pl.BlockSpec((t, HEAD_DIM), lambda g, i: (0, g)),
                   pl.BlockSpec((t, HEAD_DIM), lambda g, i: (0, g))],
        out_shape=[_sds((t, ad), F32), _sds((t, kvd), F32), _sds((t, kvd), F32)],
        compiler_params=_cp("parallel", "arbitrary"))(qr, do, kr, qgkv, kt, lse_t, delta_t)


def _rows_per_head(a, tm, hpg):
    nkv, t, _ = a.shape
    return a[:, :, :hpg].reshape(nkv, t // tm, tm, hpg).transpose(0, 1, 3, 2)


def adaln_fwd(c16, w_mod):
    nlay, d, nl = w_mod.shape
    tn = _pick(nl, (512, 256, 128))

    def body(c_ref, w_ref, o_ref):
        o_ref[...] = jnp.dot(_silu(c_ref[...]), w_ref[...], preferred_element_type=F32,
                             precision=lax.Precision.HIGHEST)

    return pl.pallas_call(
        body, name="adaln_fwd", grid=(nlay, nl // tn),
        in_specs=[pl.BlockSpec((16, d), lambda l, j: (0, 0)),
                  pl.BlockSpec((None, d, tn), lambda l, j: (l, 0, j))],
        out_specs=pl.BlockSpec((None, 16, tn), lambda l, j: (l, 0, j)),
        out_shape=_sds((nlay, 16, nl), F32), compiler_params=_cp("parallel", "parallel"))(c16, w_mod)


def adaln_bwd(c16t, dm, w_mod):
    nlay, d, nl = w_mod.shape
    tn = _pick(nl, (512, 256, 128))

    def body(c_ref, dm_ref, w_ref, dw_ref, dc_ref):
        @pl.when(pl.program_id(1) == 0)
        def _():
            dc_ref[...] = jnp.zeros_like(dc_ref)

        dmv = dm_ref[...]
        dw_ref[...] = jnp.dot(_silu(c_ref[...]), dmv, preferred_element_type=F32,
                              precision=lax.Precision.HIGHEST)
        dc_ref[...] += lax.dot_general(dmv, w_ref[...], _NT, preferred_element_type=F32,
                                       precision=lax.Precision.HIGHEST)

    return pl.pallas_call(
        body, name="adaln_bwd", grid=(nlay, nl // tn),
        in_specs=[pl.BlockSpec((d, 16), lambda l, j: (0, 0)),
                  pl.BlockSpec((None, 16, tn), lambda l, j: (l, 0, j)),
                  pl.BlockSpec((None, d, tn), lambda l, j: (l, 0, j))],
        out_specs=[pl.BlockSpec((None, d, tn), lambda l, j: (l, 0, j)),
                   pl.BlockSpec((None, 16, d), lambda l, j: (l, 0, 0))],
        out_shape=[_sds((nlay, d, nl), F32), _sds((nlay, 16, d), F32)],
        compiler_params=_cp("parallel", "arbitrary"))(c16t, dm, w_mod)


def sum_leading(a, name):
    n = a.shape[0]

    def body(a_ref, o_ref):
        acc = a_ref[0]
        for i in range(1, n):
            acc = acc + a_ref[i]
        o_ref[...] = acc

    return pl.pallas_call(body, name=name, in_specs=[VMEM_FULL], out_specs=VMEM_FULL,
                          out_shape=_sds(a.shape[1:], F32),
                          compiler_params=pltpu.CompilerParams(vmem_limit_bytes=VMEM_LIMIT))(a)


def c_ctx_grad(parts, c_ctx):
    n = parts.shape[0]

    def body(p_ref, c_ref, o_ref):
        acc = p_ref[0]
        for i in range(1, n):
            acc = acc + p_ref[i]
        o_ref[...] = acc * _dsilu(c_ref[...])

    return pl.pallas_call(body, name="c_ctx_grad", in_specs=[VMEM_FULL, VMEM_FULL], out_specs=VMEM_FULL,
                          out_shape=_sds(c_ctx.shape, F32))(parts, c_ctx)


def _as2d(a):
    return a.reshape(-1, a.shape[-1])


def _row_tile(r, c):
    for tr in (1024, 512, 256, 128, 64, 32, 16, 8):
        if r % tr == 0 and tr * c * 4 <= (1 << 20):
            return tr
    return r


def add_n(arrs, name):
    shape = arrs[0].shape
    flat = [_as2d(a) for a in arrs]
    r, c = flat[0].shape
    tr = _row_tile(r, c)

    def body(*refs):
        acc = refs[0][...]
        for ref in refs[1:-1]:
            acc = acc + ref[...]
        refs[-1][...] = acc

    out = pl.pallas_call(
        body, name=name, grid=(r // tr,),
        in_specs=[pl.BlockSpec((tr, c), lambda i: (i, 0))] * len(flat),
        out_specs=pl.BlockSpec((tr, c), lambda i: (i, 0)),
        out_shape=_sds((r, c), F32), compiler_params=_cp("parallel"))(*flat)
    return out.reshape(shape)


def adamw(w, g, m, v):
    shape = w.shape
    flat = [_as2d(a.reshape((1,) + shape) if len(shape) == 1 else a) for a in (w, g, m, v)]
    r, c = flat[0].shape
    tr = _row_tile(r, c)
    c1 = 1.0 - ADAM_B1 ** ADAM_STEP
    c2 = 1.0 - ADAM_B2 ** ADAM_STEP

    def body(w_ref, g_ref, m_ref, v_ref, d_ref, nm_ref, nv_ref):
        gv = g_ref[...]
        nm = ADAM_B1 * m_ref[...] + (1.0 - ADAM_B1) * gv
        nv = ADAM_B2 * v_ref[...] + (1.0 - ADAM_B2) * (gv * gv)
        d_ref[...] = -ADAM_LR * ((nm / c1) / (jnp.sqrt(nv / c2) + ADAM_EPS) + ADAM_WD * w_ref[...])
        nm_ref[...] = nm
        nv_ref[...] = nv

    spec = pl.BlockSpec((tr, c), lambda i: (i, 0))
    outs = pl.pallas_call(
        body, name="adamw", grid=(r // tr,), in_specs=[spec] * 4, out_specs=[spec] * 3,
        out_shape=[_sds((r, c), F32)] * 3, compiler_params=_cp("parallel"))(*flat)
    return tuple(o.reshape(shape) for o in outs)


def _place():
    return lax.axis_index("x"), lax.axis_index("y"), lax.axis_index("c")


def _remote(src, dst, ssem, rsem, dev):
    return pltpu.make_async_remote_copy(src_ref=src, dst_ref=dst, send_sem=ssem, recv_sem=rsem,
                                        device_id=dev, device_id_type=MESH)


def all_gather8(v, name):
    m_per, n = v.shape

    def body(x_ref, out_ref, send_sems, recv_sems, local_sem):
        x, y, c = _place()
        me, sibling = (x, y, c), (x, y, 1 - c)
        chips = [(1 - x, y), (x, 1 - y), (1 - x, 1 - y)]

        def rows(px, py, pc):
            return out_ref.at[pl.ds((4 * px + 2 * py + pc) * m_per, m_per), :]

        def copy(k, block, to, src=None):
            return _remote(rows(*block) if src is None else src, rows(*block),
                           send_sems.at[k], recv_sems.at[k], to)

        mine = pltpu.make_async_copy(x_ref, rows(*me), local_sem)
        mine.start()
        first = [copy(0, me, sibling, src=x_ref)]
        first += [copy(1 + j, me, (*chip, c), src=x_ref) for j, chip in enumerate(chips)]
        for cp in first:
            cp.start()
        passed = [copy(4 + j, (*chip, c), sibling) for j, chip in enumerate(chips)]
        for j, chip in enumerate(chips):
            copy(1 + j, (*chip, c), me).wait_recv()
            passed[j].start()
        copy(0, sibling, me).wait_recv()
        for j, chip in enumerate(chips):
            copy(4 + j, (*chip, 1 - c), me).wait_recv()
        for cp in first + passed:
            cp.wait_send()
        mine.wait()

    return pl.pallas_call(
        body, name=name, out_shape=_sds((8 * m_per, n), v.dtype),
        in_specs=[VMEM_FULL], out_specs=VMEM_FULL,
        scratch_shapes=[pltpu.SemaphoreType.DMA((7,)), pltpu.SemaphoreType.DMA((7,)), pltpu.SemaphoreType.DMA],
        compiler_params=pltpu.CompilerParams(vmem_limit_bytes=VMEM_LIMIT))(v)


def _slabs(ref, n):
    rows = ref.shape[0] // n
    return [ref.at[pl.ds(i * rows, rows)] for i in range(n)]


def gather_weight(w, name):
    _, r, cdim = w.shape
    n = D2D_PARTS

    def body(w_ref, out_ref, send_sems, recv_sems, local_sems):
        x, y, c = _place()
        sibling = (x, y, 1 - c)
        chips = [(1 - x, y), (x, 1 - y), (1 - x, 1 - y)]
        mine = 2 * x + y
        local = [pltpu.make_async_copy(w_ref.at[i], out_ref.at[i, mine], local_sems.at[i]) for i in range(2)]
        for cp in local:
            cp.start()
        src = _slabs(w_ref.at[c], n)
        first = []
        for k, chip in enumerate(chips):
            dst = _slabs(out_ref.at[c, mine], n)
            for j in range(n):
                cp = _remote(src[j], dst[j], send_sems.at[k * n + j], recv_sems.at[k * n + j], (*chip, c))
                cp.start()
                first.append(cp)
        passed = []
        for j in range(n):
            for k, (px, py) in enumerate(chips):
                theirs = _slabs(out_ref.at[c, 2 * px + py], n)[j]
                _remote(src[j], theirs, send_sems.at[k * n + j], recv_sems.at[k * n + j], sibling).wait_recv()
                cp = _remote(theirs, theirs, send_sems.at[(3 + k) * n + j], recv_sems.at[(3 + k) * n + j], sibling)
                cp.start()
                passed.append(cp)
        for j in range(n):
            for k, (px, py) in enumerate(chips):
                other = _slabs(out_ref.at[1 - c, 2 * px + py], n)[j]
                _remote(other, other, send_sems.at[(3 + k) * n + j], recv_sems.at[(3 + k) * n + j],
                        sibling).wait_recv()
        for cp in first + passed:
            cp.wait_send()
        for cp in local:
            cp.wait()

    return pl.pallas_call(
        body, name=name, out_shape=_sds((2, 4, r, cdim), w.dtype), in_specs=[ANY], out_specs=ANY,
        scratch_shapes=[pltpu.SemaphoreType.DMA((6 * n,)), pltpu.SemaphoreType.DMA((6 * n,)),
                        pltpu.SemaphoreType.DMA((2,))])(w)


def rs_pair_swap(g0, g1, name):
    n = D2D_PARTS

    def body(g0_ref, g1_ref, own_ref, st_ref, send_sems, recv_sems, local_sem):
        x, y, c = _place()
        sibling = (x, y, 1 - c)

        def swap(keep_ref, give_ref):
            keep = pltpu.make_async_copy(keep_ref, own_ref, local_sem)
            keep.start()
            cps = []
            for s4 in range(4):
                src, dst = _slabs(give_ref.at[s4], n), _slabs(st_ref.at[s4], n)
                for j in range(n):
                    cps.append(_remote(src[j], dst[j], send_sems.at[s4 * n + j], recv_sems.at[s4 * n + j], sibling))
            for cp in cps:
                cp.start()
            for cp in cps:
                cp.wait()
            keep.wait()

        @pl.when(c == 0)
        def _():
            swap(g0_ref, g1_ref)

        @pl.when(c == 1)
        def _():
            swap(g1_ref, g0_ref)

    return pl.pallas_call(
        body, name=name, out_shape=[_sds(g0.shape, F32), _sds(g0.shape, F32)],
        in_specs=[ANY, ANY], out_specs=[ANY, ANY],
        scratch_shapes=[pltpu.SemaphoreType.DMA((4 * n,)), pltpu.SemaphoreType.DMA((4 * n,)),
                        pltpu.SemaphoreType.DMA])(g0, g1)


def rs_chip_exchange(ps, name):
    _, r, cdim = ps.shape

    def body(ps_ref, mine_ref, st_ref, send_sems, recv_sems, local_sem):
        x, y, c = _place()
        chips = [(1 - x, y), (x, 1 - y), (1 - x, 1 - y)]
        keep = pltpu.make_async_copy(ps_ref.at[2 * x + y], mine_ref, local_sem)
        keep.start()
        cps = [_remote(ps_ref.at[2 * px + py], st_ref.at[k], send_sems.at[k], recv_sems.at[k], (px, py, c))
               for k, (px, py) in enumerate(chips)]
        for cp in cps:
            cp.start()
        for cp in cps:
            cp.wait()
        keep.wait()

    return pl.pallas_call(
        body, name=name, out_shape=[_sds((r, cdim), F32), _sds((3, r, cdim), F32)],
        in_specs=[ANY], out_specs=[ANY, ANY],
        scratch_shapes=[pltpu.SemaphoreType.DMA((3,)), pltpu.SemaphoreType.DMA((3,)), pltpu.SemaphoreType.DMA])(ps)


def rs_pair_share(red, name):
    n = 2 * D2D_PARTS

    def body(red_ref, out_ref, send_sems, recv_sems, local_sem):
        x, y, c = _place()
        sibling = (x, y, 1 - c)
        keep = pltpu.make_async_copy(red_ref, out_ref.at[c], local_sem)
        keep.start()
        src, dst, got = _slabs(red_ref, n), _slabs(out_ref.at[c], n), _slabs(out_ref.at[1 - c], n)
        cps = [_remote(src[j], dst[j], send_sems.at[j], recv_sems.at[j], sibling) for j in range(n)]
        for cp in cps:
            cp.start()
        for cp in cps:
            cp.wait_send()
        for j in range(n):
            _remote(src[j], got[j], send_sems.at[j], recv_sems.at[j], sibling).wait_recv()
        keep.wait()

    return pl.pallas_call(
        body, name=name, out_shape=_sds((2,) + red.shape, F32), in_specs=[ANY], out_specs=ANY,
        scratch_shapes=[pltpu.SemaphoreType.DMA((n,)), pltpu.SemaphoreType.DMA((n,)), pltpu.SemaphoreType.DMA])(red)


def reduce_scatter_grad(g0, g1, tag):
    own, theirs = rs_pair_swap(g0, g1, "rs_swap_" + tag)
    ps = add_n([own, theirs], "rs_pair_add_" + tag)
    mine, others = rs_chip_exchange(ps, "rs_exchange_" + tag)
    red = add_n([mine, others[0], others[1], others[2]], "rs_shard_add_" + tag)
    return rs_pair_share(red, "rs_share_" + tag)


def _rope_tables(lc, s):
    rows_n = s // GRID_W
    row = jnp.repeat(jnp.arange(rows_n, dtype=F32), GRID_W)
    col = jnp.tile(jnp.arange(GRID_W, dtype=F32), rows_n)
    axis_dim = HEAD_DIM // 2
    inv_freq = ROPE_THETA ** (-jnp.arange(0, axis_dim, 2, dtype=F32) / axis_dim)
    ang_r = row[:, None] * inv_freq[None, :]
    ang_c = col[:, None] * inv_freq[None, :]
    cr, sr, cc, sc = jnp.cos(ang_r), jnp.sin(ang_r), jnp.cos(ang_c), jnp.sin(ang_c)
    cos_l = jnp.concatenate([cr, cr, cc, cc], axis=1)
    sin_l = jnp.concatenate([-sr, sr, -sc, sc], axis=1)
    cos_t = jnp.concatenate([jnp.ones((lc, HEAD_DIM), F32), cos_l], axis=0)
    sin_t = jnp.concatenate([jnp.zeros((lc, HEAD_DIM), F32), sin_l], axis=0)
    return cos_t, sin_t


def _pad_rows(a, rows):
    return jnp.concatenate([a, jnp.zeros((rows - a.shape[0],) + a.shape[1:], a.dtype)], axis=0)


def _pad_cols(a, cols):
    return jnp.concatenate([a, jnp.zeros(a.shape[:-1] + (cols - a.shape[-1],), a.dtype)], axis=-1)


def kernel(x, c, ctx, c_ctx, w_mod, b_mod, post_ln_g, post_ln_b, w_in_e, conv_a_w, conv_a_b, norm_a_g, norm_a_b, conv_b_w, w_out_e, w_in_o, q_norm_g, k_norm_g, w_out_o, loss_target, m_c_ctx, m_w_mod, m_b_mod, m_post_ln_g, m_post_ln_b, m_w_in_e, m_conv_a_w, m_conv_a_b, m_norm_a_g, m_norm_a_b, m_conv_b_w, m_w_out_e, m_w_in_o, m_q_norm_g, m_k_norm_g, m_w_out_o, v_c_ctx, v_w_mod, v_b_mod, v_post_ln_g, v_post_ln_b, v_w_in_e, v_conv_a_w, v_conv_a_b, v_norm_a_g, v_norm_a_b, v_conv_b_w, v_w_out_e, v_w_in_o, v_q_norm_g, v_k_norm_g, v_w_out_o):
    s, d = x.shape[1], x.shape[2]
    lc = ctx.shape[1]
    t = lc + s
    tm = lc
    depth = w_mod.shape[0]
    n_even, n_odd = w_in_e.shape[0], w_in_o.shape[0]
    ad = w_out_o.shape[1] * 4
    kvd = (w_in_o.shape[2] * 4 - 2 * ad) // 2
    nkv = kvd // HEAD_DIM
    hpg = ad // nkv // HEAD_DIM
    nlm = w_mod.shape[2]
    alpha = (2.0 * depth) ** 0.25
    scale = HEAD_DIM ** -0.5
    assert n_even == 2 and n_odd == 2 and depth == 4 and hpg == GQA_GROUP
    assert lc % CONV_ROWS == 0 and s % tm == 0 and d % LANES == 0

    xi, yi, ci = _place()
    shard = 2 * xi + yi
    dev = 4 * xi + 2 * yi + ci

    wg_in_e = gather_weight(w_in_e.astype(BF16), "gather_w_in_e")
    wg_out_e = gather_weight(w_out_e.astype(BF16), "gather_w_out_e")
    wg_in_o = gather_weight(w_in_o.astype(BF16), "gather_w_in_o")
    wg_out_o = gather_weight(w_out_o.astype(BF16), "gather_w_out_o")

    c_all = all_gather8(_pad_rows(c, 8), "gather_c")
    c16 = _pad_rows(jnp.concatenate([c_all[0::8], c_ctx[None, :]], axis=0), 16)
    m_part = adaln_fwd(c16, w_mod)
    m_all = all_gather8(m_part.reshape(depth * 16, nlm), "gather_mod")
    m_all = m_all.reshape(8, depth, 16, nlm)[0::2]
    m_full = m_all.transpose(1, 2, 0, 3).reshape(depth, 16, 4 * nlm) + b_mod[:, None, :]
    m_lat = lax.dynamic_index_in_dim(m_full, dev, axis=1, keepdims=False)
    m_ctx = m_full[:, 8]

    def seg2(l, part):
        return jnp.stack([m_ctx[l, part * d:(part + 1) * d], m_lat[l, part * d:(part + 1) * d]])[:, None, :]

    cos_t, sin_t = _rope_tables(lc, s)

    small_gathered = all_gather8(
        _pad_rows(jnp.concatenate([conv_a_w.reshape(n_even * CONV_A_TAPS, -1),
                                   conv_b_w.reshape(n_even * CONV_B_TAPS, -1)], axis=0), 72), "gather_taps")
    taps = small_gathered.reshape(8, 72, -1)[0::2]
    taps = taps.transpose(1, 0, 2).reshape(72, d)
    caw = taps[:n_even * CONV_A_TAPS].reshape(n_even, CONV_A_TAPS, d)
    cbw = taps[n_even * CONV_A_TAPS:n_even * (CONV_A_TAPS + CONV_B_TAPS)].reshape(n_even, CONV_B_TAPS, d)
    caw_pad = jnp.concatenate([caw, jnp.zeros((n_even, 32 - CONV_A_TAPS, d), F32)], axis=1)
    cbw_pad = jnp.concatenate([cbw, jnp.zeros((n_even, 8 - CONV_B_TAPS, d), F32)], axis=1)

    xc = jnp.concatenate([ctx[0], x[0]], axis=0)
    saved = []
    for l in range(depth):
        i = l // 2
        shift2, scale2, gate2 = seg2(l, 0), seg2(l, 1), seg2(l, 2)
        h = mod_fwd(xc, scale2, shift2, tm)
        if l % 2 == 0:
            p = mm_nn(h, wg_in_e[i], BF16)
            u1 = conv_a_fwd(p, caw_pad[i], conv_a_b[i][None, :], lc, d)
            a_out = ln_a_fwd(u1, p, norm_a_g[i][None, :], norm_a_b[i][None, :], tm)
            b_out = conv_b_fwd(p, cbw_pad[i], lc, d)
            ab = jnp.concatenate([a_out, b_out], axis=1)
            y = mm_nn(ab, wg_out_e[i].reshape(1, 2 * d, d), F32)
            saved.append(dict(xc=xc, h=h, p=p, u1=u1, ab=ab, y=y))
        else:
            qgkv = mm_nn(h, wg_in_o[i], BF16)
            qr, kr = qk_fwd(qgkv, q_norm_g[i][None, :], k_norm_g[i][None, :], cos_t, sin_t, ad, kvd, tm,
                            scale * LOG2_E)
            o, og, lse = flash_fwd(qr, kr, qgkv, ad, kvd, tm)
            y = mm_nn(og, wg_out_o[i].reshape(1, ad, d), F32)
            saved.append(dict(xc=xc, h=h, qgkv=qgkv, qr=qr, kr=kr, o=o, og=og, lse=lse, y=y))
        xc = post_ln_fwd(xc, y, gate2, post_ln_g[l][None, :], post_ln_b[l][None, :], alpha, tm)

    dxc, loss_acc = loss_head(xc, loss_target[0], lc, tm)
    loss = lax.psum(0.5 / d * jnp.sum(loss_acc[0]), MESH_AXES)

    g_in_e, g_out_e, g_in_o, g_out_o = [None] * 2, [None] * 2, [None] * 2, [None] * 2
    d_mod_lat, d_mod_ctx = [None] * depth, [None] * depth
    d_pln_g, d_pln_b = [None] * depth, [None] * depth
    d_cab, d_nag, d_nab, d_caw, d_cbw = [None] * 2, [None] * 2, [None] * 2, [None] * 2, [None] * 2
    d_qg, d_kg = [None] * 2, [None] * 2
    for l in reversed(range(depth)):
        i = l // 2
        sv = saved[l]
        scale2, gate2 = seg2(l, 1), seg2(l, 2)
        dzx, dy, acc_ln = post_ln_bwd(dxc, sv["xc"], sv["y"], gate2, post_ln_g[l][None, :], alpha, tm)
        d_pln_g[l] = acc_ln[0, 1] + acc_ln[1, 1]
        d_pln_b[l] = acc_ln[0, 2] + acc_ln[1, 2]
        if l % 2 == 0:
            w_out3 = wg_out_e[i].reshape(1, 2 * d, d)
            dab = mm_nt(dy, w_out3, BF16)
            g_out_e[i] = mm_tn(sv["ab"], dy, 1).reshape(4, 2 * d // 4, d)
            du1, d_agate, acc_a = ln_a_bwd(dab, sv["u1"], sv["p"], norm_a_g[i][None, :], norm_a_b[i][None, :], tm)
            d_nag[i], d_nab[i], d_cab[i] = acc_a[0], acc_a[1], acc_a[2]
            d_aval, d_aglu, d_caw[i] = conv_a_bwd(du1, sv["p"], caw_pad[i], lc, d)
            d_bx, d_bb, d_bc, d_bg, d_cbw[i] = conv_b_bwd(dab, sv["p"], cbw_pad[i], lc, d)
            dp = jnp.concatenate([d_aval, d_aglu, d_agate, d_bx, d_bb, d_bc, d_bg], axis=1)
            dh = mm_nt(dp, wg_in_e[i], F32)
            g_in_e[i] = mm_tn(sv["h"], dp, 4)
        else:
            w_out3 = wg_out_o[i].reshape(1, ad, d)
            dog = mm_nt(dy, w_out3, BF16)
            g_out_o[i] = mm_tn(sv["og"], dy, 1).reshape(4, ad // 4, d)
            do, dgate, delta = gate_bwd(dog, sv["o"], sv["qgkv"], ad, kvd, tm)
            dqr, dkr, dv = flash_bwd(sv["qr"], do, sv["kr"], _chunks_t(sv["kr"], nkv, tm), sv["qgkv"],
                                     _rows_per_head(sv["lse"], tm, hpg), _rows_per_head(delta, tm, hpg),
                                     ad, kvd, tm, scale)
            dq, dk, acc_qk = qk_bwd(dqr, dkr, sv["qgkv"], q_norm_g[i][None, :], k_norm_g[i][None, :],
                                    cos_t, sin_t, ad, kvd, tm)
            d_qg[i], d_kg[i] = acc_qk[0], acc_qk[1]
            dqgkv = jnp.concatenate([dq, dgate, dk, dv.astype(BF16)], axis=1)
            dh = mm_nt(dqgkv, wg_in_o[i], F32)
            g_in_o[i] = mm_tn(sv["h"], dqgkv, 4)
        dxc, acc_mod = mod_bwd(dh, dzx, sv["xc"], scale2, tm)
        d_mod_ctx[l] = jnp.stack([acc_mod[0, 0], acc_mod[0, 1], acc_ln[0, 0]])
        d_mod_lat[l] = jnp.stack([acc_mod[1, 0], acc_mod[1, 1], acc_ln[1, 0]])
    grad_x = dxc[lc:][None]

    pack = jnp.concatenate(
        [jnp.concatenate(d_mod_ctx, axis=0),
         jnp.stack(d_pln_g), jnp.stack(d_pln_b),
         jnp.stack(d_cab), jnp.stack(d_nag), jnp.stack(d_nab),
         jnp.concatenate(d_caw, axis=0),
         jnp.concatenate(d_cbw, axis=0),
         _pad_cols(jnp.stack(d_qg), d), _pad_cols(jnp.stack(d_kg), d),
         jnp.zeros((2, d), F32),
         jnp.concatenate(d_mod_lat, axis=0),
         jnp.zeros((4, d), F32)], axis=0)
    gathered = all_gather8(pack, "gather_small").reshape(8, 128, d)
    small = sum_leading(gathered, "sum_small")
    dm_ctx = small[0:12].reshape(depth, 1, 3 * d)
    dm_lat = gathered[:, 112:124].reshape(8, depth, 3 * d).transpose(1, 0, 2)
    dm = jnp.concatenate([dm_lat, dm_ctx, jnp.zeros((depth, 7, 3 * d), F32)], axis=1)
    g_b_mod = sum_leading(dm.transpose(1, 0, 2), "sum_b_mod")
    dm_shard = lax.dynamic_slice_in_dim(dm, shard * nlm, nlm, axis=2)
    g_w_mod, dc_part = adaln_bwd(c16.T, dm_shard, w_mod)
    dc_all = all_gather8(_pad_rows(dc_part[:, 8, :], 8), "gather_dc").reshape(8, 8, d)
    g_c_ctx = c_ctx_grad(dc_all[0::2, :depth].reshape(4 * depth, 1, d), c_ctx[None, :])[0]

    g_pln_g, g_pln_b = small[12:16], small[16:20]
    g_cab, g_nag, g_nab = small[20:22], small[22:24], small[24:26]
    dch = d // 4
    g_caw = lax.dynamic_slice_in_dim(small[26:90].reshape(2, 32, d)[:, :CONV_A_TAPS], shard * dch, dch, axis=2)
    g_cbw = lax.dynamic_slice_in_dim(small[90:106].reshape(2, 8, d)[:, :CONV_B_TAPS], shard * dch, dch, axis=2)
    g_qg, g_kg = small[106:108, :HEAD_DIM], small[108:110, :HEAD_DIM]

    g_w_in_e = reduce_scatter_grad(g_in_e[0], g_in_e[1], "in_e")
    g_w_out_e = reduce_scatter_grad(g_out_e[0], g_out_e[1], "out_e")
    g_w_in_o = reduce_scatter_grad(g_in_o[0], g_in_o[1], "in_o")
    g_w_out_o = reduce_scatter_grad(g_out_o[0], g_out_o[1], "out_o")

    grads = [g_c_ctx, g_w_mod, g_b_mod, g_pln_g, g_pln_b, g_w_in_e, g_caw, g_cab, g_nag, g_nab, g_cbw,
             g_w_out_e, g_w_in_o, g_qg, g_kg, g_w_out_o]
    weights = [c_ctx, w_mod, b_mod, post_ln_g, post_ln_b, w_in_e, conv_a_w, conv_a_b, norm_a_g, norm_a_b,
               conv_b_w, w_out_e, w_in_o, q_norm_g, k_norm_g, w_out_o]
    ms = [m_c_ctx, m_w_mod, m_b_mod, m_post_ln_g, m_post_ln_b, m_w_in_e, m_conv_a_w, m_conv_a_b, m_norm_a_g,
          m_norm_a_b, m_conv_b_w, m_w_out_e, m_w_in_o, m_q_norm_g, m_k_norm_g, m_w_out_o]
    vs = [v_c_ctx, v_w_mod, v_b_mod, v_post_ln_g, v_post_ln_b, v_w_in_e, v_conv_a_w, v_conv_a_b, v_norm_a_g,
          v_norm_a_b, v_conv_b_w, v_w_out_e, v_w_in_o, v_q_norm_g, v_k_norm_g, v_w_out_o]
    deltas, new_ms, new_vs = [], [], []
    for wv, gv, mv, vv in zip(weights, grads, ms, vs):
        dl, nm, nv = adamw(wv, gv, mv, vv)
        deltas.append(dl)
        new_ms.append(nm)
        new_vs.append(nv)
    return (loss, grad_x, *grads, *deltas, *new_ms, *new_vs)
```

```python
import functools

import jax
import jax.numpy as jnp
from jax import lax
from jax.experimental import pallas as pl
from jax.experimental.pallas import tpu as pltpu

F32 = jnp.float32
BF16 = jnp.bfloat16

LANES = 128
SUBLANES = 8
HEAD_DIM = 128
GQA_GROUP = 4
GRID_W = 64
ROPE_THETA = 10000.0
LN_EPS = 1e-5
RMS_EPS = 1e-6
CONV_A_TAPS = 31
CONV_B_TAPS = 3
HALO = 16
CONV_ROWS = 128
ADAM_LR = 0.001
ADAM_B1 = 0.9
ADAM_B2 = 0.999
ADAM_EPS = 1e-08
ADAM_WD = 0.01
ADAM_STEP = 10
VMEM_LIMIT = 56 * 1024 * 1024
D2D_PARTS = 4
LOG2_E = 1.4426950408889634
LN_2 = 0.6931471805599453
MESH_AXES = ("x", "y", "c")
MESH = pl.DeviceIdType.MESH
ANY = pl.BlockSpec(memory_space=pl.ANY)
VMEM_FULL = pl.BlockSpec(memory_space=pltpu.VMEM)


def _sds(shape, dtype):
    return jax.ShapeDtypeStruct(tuple(shape), dtype)


def _cp(*sem):
    return pltpu.CompilerParams(dimension_semantics=sem, vmem_limit_bytes=VMEM_LIMIT)


def _pick(n, cands):
    for c in cands:
        if n % c == 0:
            return c
    return n


def _sigmoid(x):
    return 1.0 / (1.0 + jnp.exp(-x))


def _silu(x):
    return x * _sigmoid(x)


def _dsilu(x):
    s = _sigmoid(x)
    return s * (1.0 + x * (1.0 - s))


def _row(tm, d):
    return pl.BlockSpec((tm, d), lambda j: (j, 0))


def _seg(d):
    return pl.BlockSpec((None, 1, d), lambda j: (jnp.minimum(j, 1), 0, 0))


def _vec(d):
    return pl.BlockSpec((1, d), lambda j: (0, 0))


def _colblk(tm, width, blk):
    return pl.BlockSpec((tm, width), lambda j: (j, blk))


def _seg_acc(d):
    return pl.BlockSpec((None, SUBLANES, d), lambda j: (jnp.minimum(j, 1), 0, 0))


def _ln_stats(z):
    mu = jnp.mean(z, axis=-1, keepdims=True)
    zc = z - mu
    var = jnp.mean(zc * zc, axis=-1, keepdims=True)
    rstd = lax.rsqrt(var + LN_EPS)
    return zc * rstd, rstd


def _ln_bwd(dxh, xhat, rstd):
    m1 = jnp.mean(dxh, axis=-1, keepdims=True)
    m2 = jnp.mean(dxh * xhat, axis=-1, keepdims=True)
    return rstd * (dxh - m1 - xhat * m2)


def _colsum(v):
    return jnp.sum(v, axis=0, keepdims=True)


def mod_fwd(xc, scale2, shift2, tm):
    t, d = xc.shape

    def body(x_ref, sc_ref, sh_ref, h_ref):
        h_ref[...] = (x_ref[...] * (1.0 + sc_ref[...]) + sh_ref[...]).astype(h_ref.dtype)

    return pl.pallas_call(
        body, name="mod_fwd", grid=(t // tm,),
        in_specs=[_row(tm, d), _seg(d), _seg(d)], out_specs=_row(tm, d),
        out_shape=_sds((t, d), BF16), compiler_params=_cp("parallel"))(xc, scale2, shift2)


def post_ln_fwd(xc, y, gate2, g, b, alpha, tm):
    t, d = xc.shape

    def body(x_ref, y_ref, gt_ref, g_ref, b_ref, o_ref):
        z = alpha * x_ref[...] + gt_ref[...] * y_ref[...]
        xhat, _ = _ln_stats(z)
        o_ref[...] = xhat * g_ref[...] + b_ref[...]

    return pl.pallas_call(
        body, name="post_ln_fwd", grid=(t // tm,),
        in_specs=[_row(tm, d), _row(tm, d), _seg(d), _vec(d), _vec(d)], out_specs=_row(tm, d),
        out_shape=_sds((t, d), F32), compiler_params=_cp("parallel"))(xc, y, gate2, g, b)


def post_ln_bwd(dout, xc, y, gate2, g, alpha, tm):
    t, d = xc.shape

    def body(do_ref, x_ref, y_ref, gt_ref, g_ref, dzx_ref, dy_ref, acc_ref):
        @pl.when(pl.program_id(0) <= 1)
        def _():
            acc_ref[...] = jnp.zeros_like(acc_ref)

        yv = y_ref[...]
        gate = gt_ref[...]
        xhat, rstd = _ln_stats(alpha * x_ref[...] + gate * yv)
        dout = do_ref[...]
        dz = _ln_bwd(dout * g_ref[...], xhat, rstd)
        dzx_ref[...] = alpha * dz
        dy_ref[...] = (gate * dz).astype(dy_ref.dtype)
        acc_ref[0:1, :] += _colsum(dz * yv)
        acc_ref[1:2, :] += _colsum(dout * xhat)
        acc_ref[2:3, :] += _colsum(dout)

    return pl.pallas_call(
        body, name="post_ln_bwd", grid=(t // tm,),
        in_specs=[_row(tm, d), _row(tm, d), _row(tm, d), _seg(d), _vec(d)],
        out_specs=[_row(tm, d), _row(tm, d), _seg_acc(d)],
        out_shape=[_sds((t, d), F32), _sds((t, d), BF16), _sds((2, SUBLANES, d), F32)],
        compiler_params=_cp("arbitrary"))(dout, xc, y, gate2, g)


def mod_bwd(dh, dzx, xc, scale2, tm):
    t, d = xc.shape

    def body(dh_ref, dzx_ref, x_ref, sc_ref, dx_ref, acc_ref):
        @pl.when(pl.program_id(0) <= 1)
        def _():
            acc_ref[...] = jnp.zeros_like(acc_ref)

        dhv = dh_ref[...].astype(F32)
        dx_ref[...] = dzx_ref[...] + dhv * (1.0 + sc_ref[...])
        acc_ref[0:1, :] += _colsum(dhv)
        acc_ref[1:2, :] += _colsum(dhv * x_ref[...])

    return pl.pallas_call(
        body, name="mod_bwd", grid=(t // tm,),
        in_specs=[_row(tm, d), _row(tm, d), _row(tm, d), _seg(d)],
        out_specs=[_row(tm, d), _seg_acc(d)],
        out_shape=[_sds((t, d), F32), _sds((2, SUBLANES, d), F32)],
        compiler_params=_cp("arbitrary"))(dh, dzx, xc, scale2)


def loss_head(xc, target, lc, tm):
    t, d = xc.shape

    def body(x_ref, t_ref, dx_ref, acc_ref):
        j = pl.program_id(0)

        @pl.when(j == 0)
        def _():
            acc_ref[...] = jnp.zeros_like(acc_ref)
            dx_ref[...] = jnp.zeros_like(dx_ref)

        @pl.when(j > 0)
        def _():
            err = x_ref[...] - t_ref[...]
            dx_ref[...] = err * (1.0 / d)
            col = _colsum(err * err)
            tot = col[:, 0:LANES]
            for k in range(1, d // LANES):
                tot = tot + col[:, k * LANES:(k + 1) * LANES]
            acc_ref[0:1, :] += tot

    nlc = lc // tm
    return pl.pallas_call(
        body, name="loss_head", grid=(t // tm,),
        in_specs=[_row(tm, d), pl.BlockSpec((tm, d), lambda j: (jnp.maximum(j - nlc, 0), 0))],
        out_specs=[_row(tm, d), pl.BlockSpec((SUBLANES, LANES), lambda j: (0, 0))],
        out_shape=[_sds((t, d), F32), _sds((SUBLANES, LANES), F32)],
        compiler_params=_cp("arbitrary"))(xc, target)


def mm_nn(a, w3, out_dtype):
    m, k = a.shape
    ns, _, nl = w3.shape
    tm = _pick(m, (768, 512, 256, 128))
    tn = _pick(nl, (512, 256, 128))
    npj = nl // tn

    def body(a_ref, w_ref, o_ref):
        o_ref[...] = jnp.dot(a_ref[...], w_ref[...], preferred_element_type=F32).astype(o_ref.dtype)

    return pl.pallas_call(
        body, name="mm_nn", grid=(m // tm, ns * npj),
        in_specs=[pl.BlockSpec((tm, k), lambda i, j: (i, 0)),
                  pl.BlockSpec((None, k, tn), lambda i, j: (j // npj, 0, j % npj))],
        out_specs=pl.BlockSpec((tm, tn), lambda i, j: (i, j)),
        out_shape=_sds((m, ns * nl), out_dtype), compiler_params=_cp("parallel", "parallel"))(a, w3)


def mm_nt(a, w3, out_dtype):
    m, _ = a.shape
    ns, k, nl = w3.shape
    tm = _pick(m, (768, 512, 256, 128))
    tk = _pick(k, (2048, 1024, 512, 256, 128))
    tn = _pick(nl, (512, 256, 128))
    npj = nl // tn
    nsteps = ns * npj

    def body(a_ref, w_ref, o_ref, acc_ref):
        n = pl.program_id(2)

        @pl.when(n == 0)
        def _():
            acc_ref[...] = jnp.zeros_like(acc_ref)

        acc_ref[...] += lax.dot_general(a_ref[...], w_ref[...], (((1,), (1,)), ((), ())),
                                        preferred_element_type=F32)

        @pl.when(n == nsteps - 1)
        def _():
            o_ref[...] = acc_ref[...].astype(o_ref.dtype)

    return pl.pallas_call(
        body, name="mm_nt", grid=(m // tm, k // tk, nsteps),
        in_specs=[pl.BlockSpec((tm, tn), lambda i, kk, n: (i, n)),
                  pl.BlockSpec((None, tk, tn), lambda i, kk, n: (n // npj, kk, n % npj))],
        out_specs=pl.BlockSpec((tm, tk), lambda i, kk, n: (i, kk)),
        out_shape=_sds((m, k), out_dtype), scratch_shapes=[pltpu.VMEM((tm, tk), F32)],
        compiler_params=_cp("parallel", "parallel", "arbitrary"))(a, w3)


def mm_tn(a, b, ns):
    m, k = a.shape
    nl = b.shape[1] // ns
    tm = _pick(m, (768, 512, 256, 128))
    tk = _pick(k, (1024, 512, 256, 128))
    tn = _pick(nl, (512, 256, 128))
    npj = nl // tn

    def body(a_ref, b_ref, o_ref):
        @pl.when(pl.program_id(2) == 0)
        def _():
            o_ref[...] = jnp.zeros_like(o_ref)

        o_ref[...] += lax.dot_general(a_ref[...], b_ref[...], (((0,), (0,)), ((), ())),
                                      preferred_element_type=F32)

    return pl.pallas_call(
        body, name="mm_tn", grid=(k // tk, ns * npj, m // tm),
        in_specs=[pl.BlockSpec((tm, tk), lambda i, j, r: (r, i)),
                  pl.BlockSpec((tm, tn), lambda i, j, r: (r, j))],
        out_specs=pl.BlockSpec((None, tk, tn), lambda i, j, r: (j // npj, i, j % npj)),
        out_shape=_sds((ns, k, nl), F32),
        compiler_params=_cp("parallel", "parallel", "arbitrary"))(a, b)


def _win_start(j, ncc):
    return pl.multiple_of(j * CONV_ROWS + jnp.where(j >= ncc, HALO, 0), SUBLANES)


def _tok_start(j):
    return pl.multiple_of(j * CONV_ROWS, CONV_ROWS)


def _shifted(xw, off):
    n = xw.shape[0]
    sh = (n - off) % n
    y = pltpu.roll(xw, sh, 0) if sh else xw
    return y[:CONV_ROWS]


def _conv_fwd(xw, w_ref, ntaps):
    pad = ntaps // 2
    acc = None
    for k in range(ntaps):
        term = w_ref[k:k + 1, :] * _shifted(xw, HALO + k - pad)
        acc = term if acc is None else acc + term
    return acc


def _conv_bwd_data(xw, w_ref, ntaps):
    pad = ntaps // 2
    acc = None
    for k in range(ntaps):
        term = w_ref[k:k + 1, :] * _shifted(xw, HALO - k + pad)
        acc = term if acc is None else acc + term
    return acc


def _conv_bwd_weight(dw_ref, d, xw, ntaps):
    pad = ntaps // 2
    for k in range(ntaps):
        dw_ref[k:k + 1, :] += _colsum(d * _shifted(xw, HALO + k - pad))


def _zero_halos(pad_ref, lc, t):
    z = jnp.zeros((HALO, LANES), F32)
    pad_ref[0:HALO, :] = z
    pad_ref[HALO + lc:2 * HALO + lc, :] = z
    pad_ref[2 * HALO + t:3 * HALO + t, :] = z


def _pad_dst(j, ncc):
    return pl.multiple_of(j * CONV_ROWS + HALO + jnp.where(j >= ncc, HALO, 0), SUBLANES)


def _chan(t, blk0):
    return pl.BlockSpec((t, LANES), lambda ct: (0, blk0 + ct))


def _tapw(rows):
    return pl.BlockSpec((rows, LANES), lambda ct: (0, ct))


def conv_a_fwd(p, w_pad, bias, lc, d):
    t = p.shape[0]
    nct, nch, ncc = d // LANES, t // CONV_ROWS, lc // CONV_ROWS

    def body(av_ref, ag_ref, w_ref, b_ref, u1_ref, pad_ref):
        _zero_halos(pad_ref, lc, t)

        def fill(j, carry):
            rows = pl.ds(_tok_start(j), CONV_ROWS)
            u0 = av_ref[rows, :].astype(F32) * _sigmoid(ag_ref[rows, :].astype(F32))
            pad_ref[pl.ds(_pad_dst(j, ncc), CONV_ROWS), :] = u0
            return carry

        lax.fori_loop(0, nch, fill, 0)

        def conv(j, carry):
            xw = pad_ref[pl.ds(_win_start(j, ncc), CONV_ROWS + 2 * HALO), :]
            u1_ref[pl.ds(_tok_start(j), CONV_ROWS), :] = _conv_fwd(xw, w_ref, CONV_A_TAPS) + b_ref[...]
            return carry

        lax.fori_loop(0, nch, conv, 0)

    return pl.pallas_call(
        body, name="conv_a_fwd", grid=(nct,),
        in_specs=[_chan(t, 0), _chan(t, nct), _tapw(32), _tapw(1)],
        out_specs=_chan(t, 0), out_shape=_sds((t, d), F32),
        scratch_shapes=[pltpu.VMEM((t + 3 * HALO, LANES), F32)],
        compiler_params=_cp("parallel"))(p, p, w_pad, bias)


def conv_b_fwd(p, w_pad, lc, d):
    t = p.shape[0]
    nct, nch, ncc = d // LANES, t // CONV_ROWS, lc // CONV_ROWS

    def body(bx_ref, bb_ref, bc_ref, bg_ref, w_ref, o_ref, pad_ref):
        _zero_halos(pad_ref, lc, t)

        def fill(j, carry):
            rows = pl.ds(_tok_start(j), CONV_ROWS)
            pad_ref[pl.ds(_pad_dst(j, ncc), CONV_ROWS), :] = (
                bc_ref[rows, :].astype(F32) * bx_ref[rows, :].astype(F32))
            return carry

        lax.fori_loop(0, nch, fill, 0)

        def conv(j, carry):
            rows = pl.ds(_tok_start(j), CONV_ROWS)
            xw = pad_ref[pl.ds(_win_start(j, ncc), CONV_ROWS + 2 * HALO), :]
            v = _conv_fwd(xw, w_ref, CONV_B_TAPS)
            o_ref[rows, :] = (bb_ref[rows, :].astype(F32) * v
                              * _silu(bg_ref[rows, :].astype(F32))).astype(o_ref.dtype)
            return carry

        lax.fori_loop(0, nch, conv, 0)

    return pl.pallas_call(
        body, name="conv_b_fwd", grid=(nct,),
        in_specs=[_chan(t, 3 * nct), _chan(t, 4 * nct), _chan(t, 5 * nct), _chan(t, 6 * nct), _tapw(8)],
        out_specs=_chan(t, 0), out_shape=_sds((t, d), BF16),
        scratch_shapes=[pltpu.VMEM((t + 3 * HALO, LANES), F32)],
        compiler_params=_cp("parallel"))(p, p, p, p, w_pad)


def ln_a_fwd(u1, p, g, b, tm):
    t, d = u1.shape

    def body(u_ref, ag_ref, g_ref, b_ref, o_ref):
        xhat, _ = _ln_stats(u_ref[...])
        u2 = xhat * g_ref[...] + b_ref[...]
        o_ref[...] = (_silu(u2) * _silu(ag_ref[...].astype(F32))).astype(o_ref.dtype)

    return pl.pallas_call(
        body, name="ln_a_fwd", grid=(t // tm,),
        in_specs=[_row(tm, d), _colblk(tm, d, 2), _vec(d), _vec(d)], out_specs=_row(tm, d),
        out_shape=_sds((t, d), BF16), compiler_params=_cp("parallel"))(u1, p, g, b)


def ln_a_bwd(dab, u1, p, g, b, tm):
    t, d = u1.shape

    def body(da_ref, u_ref, ag_ref, g_ref, b_ref, du_ref, dag_ref, acc_ref):
        @pl.when(pl.program_id(0) == 0)
        def _():
            acc_ref[...] = jnp.zeros_like(acc_ref)

        xhat, rstd = _ln_stats(u_ref[...])
        u2 = xhat * g_ref[...] + b_ref[...]
        ag = ag_ref[...].astype(F32)
        da = da_ref[...].astype(F32)
        dag_ref[...] = (da * _silu(u2) * _dsilu(ag)).astype(dag_ref.dtype)
        du2 = da * _silu(ag) * _dsilu(u2)
        du1 = _ln_bwd(du2 * g_ref[...], xhat, rstd)
        du_ref[...] = du1
        acc_ref[0:1, :] += _colsum(du2 * xhat)
        acc_ref[1:2, :] += _colsum(du2)
        acc_ref[2:3, :] += _colsum(du1)

    return pl.pallas_call(
        body, name="ln_a_bwd", grid=(t // tm,),
        in_specs=[_colblk(tm, d, 0), _row(tm, d), _colblk(tm, d, 2), _vec(d), _vec(d)],
        out_specs=[_row(tm, d), _row(tm, d), pl.BlockSpec((SUBLANES, d), lambda j: (0, 0))],
        out_shape=[_sds((t, d), F32), _sds((t, d), BF16), _sds((SUBLANES, d), F32)],
        compiler_params=_cp("arbitrary"))(dab, u1, p, g, b)


def conv_a_bwd(du1, p, w_pad, lc, d):
    t = p.shape[0]
    nct, nch, ncc = d // LANES, t // CONV_ROWS, lc // CONV_ROWS

    def body(du_ref, av_ref, ag_ref, w_ref, dav_ref, dag_ref, dw_ref, pad_u, pad_d):
        _zero_halos(pad_u, lc, t)
        _zero_halos(pad_d, lc, t)
        dw_ref[...] = jnp.zeros_like(dw_ref)

        def fill(j, carry):
            rows = pl.ds(_tok_start(j), CONV_ROWS)
            dst = pl.ds(_pad_dst(j, ncc), CONV_ROWS)
            pad_u[dst, :] = av_ref[rows, :].astype(F32) * _sigmoid(ag_ref[rows, :].astype(F32))
            pad_d[dst, :] = du_ref[rows, :]
            return carry

        lax.fori_loop(0, nch, fill, 0)

        def step(j, carry):
            rows = pl.ds(_tok_start(j), CONV_ROWS)
            win = pl.ds(_win_start(j, ncc), CONV_ROWS + 2 * HALO)
            du0 = _conv_bwd_data(pad_d[win, :], w_ref, CONV_A_TAPS)
            sig = _sigmoid(ag_ref[rows, :].astype(F32))
            dav_ref[rows, :] = (du0 * sig).astype(dav_ref.dtype)
            dag_ref[rows, :] = (du0 * av_ref[rows, :].astype(F32) * sig * (1.0 - sig)).astype(dag_ref.dtype)
            _conv_bwd_weight(dw_ref, du_ref[rows, :], pad_u[win, :], CONV_A_TAPS)
            return carry

        lax.fori_loop(0, nch, step, 0)

    return pl.pallas_call(
        body, name="conv_a_bwd", grid=(nct,),
        in_specs=[_chan(t, 0), _chan(t, 0), _chan(t, nct), _tapw(32)],
        out_specs=[_chan(t, 0), _chan(t, 0), _tapw(32)],
        out_shape=[_sds((t, d), BF16), _sds((t, d), BF16), _sds((32, d), F32)],
        scratch_shapes=[pltpu.VMEM((t + 3 * HALO, LANES), F32), pltpu.VMEM((t + 3 * HALO, LANES), F32)],
        compiler_params=_cp("parallel"))(du1, p, p, w_pad)


def conv_b_bwd(dab, p, w_pad, lc, d):
    t = p.shape[0]
    nct, nch, ncc = d // LANES, t // CONV_ROWS, lc // CONV_ROWS

    def body(db_ref, bx_ref, bb_ref, bc_ref, bg_ref, w_ref,
             dbx_ref, dbb_ref, dbc_ref, dbg_ref, dw_ref, pad_t, pad_d):
        _zero_halos(pad_t, lc, t)
        _zero_halos(pad_d, lc, t)
        dw_ref[...] = jnp.zeros_like(dw_ref)

        def fill(j, carry):
            rows = pl.ds(_tok_start(j), CONV_ROWS)
            pad_t[pl.ds(_pad_dst(j, ncc), CONV_ROWS), :] = (
                bc_ref[rows, :].astype(F32) * bx_ref[rows, :].astype(F32))
            return carry

        lax.fori_loop(0, nch, fill, 0)

        def first(j, carry):
            rows = pl.ds(_tok_start(j), CONV_ROWS)
            xw = pad_t[pl.ds(_win_start(j, ncc), CONV_ROWS + 2 * HALO), :]
            v = _conv_fwd(xw, w_ref, CONV_B_TAPS)
            bg = bg_ref[rows, :].astype(F32)
            bb = bb_ref[rows, :].astype(F32)
            db = db_ref[rows, :].astype(F32)
            sg = _silu(bg)
            dbb_ref[rows, :] = (db * v * sg).astype(dbb_ref.dtype)
            dbg_ref[rows, :] = (db * bb * v * _dsilu(bg)).astype(dbg_ref.dtype)
            dv = db * bb * sg
            pad_d[pl.ds(_pad_dst(j, ncc), CONV_ROWS), :] = dv
            _conv_bwd_weight(dw_ref, dv, xw, CONV_B_TAPS)
            return carry

        lax.fori_loop(0, nch, first, 0)

        def second(j, carry):
            rows = pl.ds(_tok_start(j), CONV_ROWS)
            dt = _conv_bwd_data(pad_d[pl.ds(_win_start(j, ncc), CONV_ROWS + 2 * HALO), :], w_ref, CONV_B_TAPS)
            dbc_ref[rows, :] = (dt * bx_ref[rows, :].astype(F32)).astype(dbc_ref.dtype)
            dbx_ref[rows, :] = (dt * bc_ref[rows, :].astype(F32)).astype(dbx_ref.dtype)
            return carry

        lax.fori_loop(0, nch, second, 0)

    return pl.pallas_call(
        body, name="conv_b_bwd", grid=(nct,),
        in_specs=[_chan(t, nct), _chan(t, 3 * nct), _chan(t, 4 * nct), _chan(t, 5 * nct), _chan(t, 6 * nct),
                  _tapw(8)],
        out_specs=[_chan(t, 0)] * 4 + [_tapw(8)],
        out_shape=[_sds((t, d), BF16)] * 4 + [_sds((8, d), F32)],
        scratch_shapes=[pltpu.VMEM((t + 3 * HALO, LANES), F32), pltpu.VMEM((t + 3 * HALO, LANES), F32)],
        compiler_params=_cp("parallel"))(dab, p, p, p, p, w_pad)


def _swap_halves(z, first_half):
    return jnp.where(first_half, pltpu.roll(z, 96, 1), pltpu.roll(z, 32, 1))


def _first_half_mask(rows):
    lane = lax.broadcasted_iota(jnp.int32, (rows, HEAD_DIM), 1)
    return (lane & 32) == 0


def qk_fwd(qgkv, qg, kg, cos_t, sin_t, ad, kvd, tm, qscale):
    t = qgkv.shape[0]

    def body(q_ref, k_ref, qg_ref, kg_ref, c_ref, s_ref, qo_ref, ko_ref):
        first = _first_half_mask(tm)
        cosv, sinv = c_ref[...], s_ref[...]

        def head(x, gain):
            inv = lax.rsqrt(jnp.mean(x * x, axis=-1, keepdims=True) + RMS_EPS)
            yv = x * inv * gain
            return yv * cosv + _swap_halves(yv, first) * sinv

        for h in range(ad // HEAD_DIM):
            sl = slice(h * HEAD_DIM, (h + 1) * HEAD_DIM)
            qo_ref[:, sl] = (head(q_ref[:, sl].astype(F32), qg_ref[...]) * qscale).astype(qo_ref.dtype)
        for h in range(kvd // HEAD_DIM):
            sl = slice(h * HEAD_DIM, (h + 1) * HEAD_DIM)
            ko_ref[:, sl] = head(k_ref[:, sl].astype(F32), kg_ref[...]).astype(ko_ref.dtype)

    return pl.pallas_call(
        body, name="qk_fwd", grid=(t // tm,),
        in_specs=[_colblk(tm, ad, 0), _colblk(tm, kvd, 2 * ad // kvd), _vec(HEAD_DIM), _vec(HEAD_DIM),
                  _row(tm, HEAD_DIM), _row(tm, HEAD_DIM)],
        out_specs=[_row(tm, ad), _row(tm, kvd)],
        out_shape=[_sds((t, ad), BF16), _sds((t, kvd), BF16)],
        compiler_params=_cp("parallel"))(qgkv, qgkv, qg, kg, cos_t, sin_t)


def qk_bwd(dqr, dkr, qgkv, qg, kg, cos_t, sin_t, ad, kvd, tm):
    t = qgkv.shape[0]

    def body(dq_ref, dk_ref, q_ref, k_ref, qg_ref, kg_ref, c_ref, s_ref, dqo_ref, dko_ref, acc_ref):
        @pl.when(pl.program_id(0) == 0)
        def _():
            acc_ref[...] = jnp.zeros_like(acc_ref)

        first = _first_half_mask(tm)
        cosv, sinv = c_ref[...], s_ref[...]

        def head(x, gain, dout):
            inv = lax.rsqrt(jnp.mean(x * x, axis=-1, keepdims=True) + RMS_EPS)
            xn = x * inv
            dy = dout * cosv + _swap_halves(dout * sinv, first)
            dxn = dy * gain
            dx = inv * (dxn - xn * jnp.mean(dxn * xn, axis=-1, keepdims=True))
            return dx, _colsum(dy * xn)

        dqg = jnp.zeros((1, HEAD_DIM), F32)
        for h in range(ad // HEAD_DIM):
            sl = slice(h * HEAD_DIM, (h + 1) * HEAD_DIM)
            dx, dg = head(q_ref[:, sl].astype(F32), qg_ref[...], dq_ref[:, sl])
            dqo_ref[:, sl] = dx.astype(dqo_ref.dtype)
            dqg = dqg + dg
        dkg = jnp.zeros((1, HEAD_DIM), F32)
        for h in range(kvd // HEAD_DIM):
            sl = slice(h * HEAD_DIM, (h + 1) * HEAD_DIM)
            dx, dg = head(k_ref[:, sl].astype(F32), kg_ref[...], dk_ref[:, sl])
            dko_ref[:, sl] = dx.astype(dko_ref.dtype)
            dkg = dkg + dg
        acc_ref[0:1, :] += dqg
        acc_ref[1:2, :] += dkg

    return pl.pallas_call(
        body, name="qk_bwd", grid=(t // tm,),
        in_specs=[_row(tm, ad), _row(tm, kvd), _colblk(tm, ad, 0), _colblk(tm, kvd, 2 * ad // kvd),
                  _vec(HEAD_DIM), _vec(HEAD_DIM), _row(tm, HEAD_DIM), _row(tm, HEAD_DIM)],
        out_specs=[_row(tm, ad), _row(tm, kvd), pl.BlockSpec((SUBLANES, HEAD_DIM), lambda j: (0, 0))],
        out_shape=[_sds((t, ad), BF16), _sds((t, kvd), BF16), _sds((SUBLANES, HEAD_DIM), F32)],
        compiler_params=_cp("arbitrary"))(dqr, dkr, qgkv, qgkv, qg, kg, cos_t, sin_t)


_NT = (((1,), (1,)), ((), ()))


def _chunks_t(a, nkv, tm):
    t = a.shape[0]
    return a.reshape(t // tm, tm, nkv, HEAD_DIM).transpose(2, 0, 3, 1)


def flash_fwd(qr, kr, qgkv, ad, kvd, tm):
    t = qr.shape[0]
    nkv = kvd // HEAD_DIM
    gw = ad // nkv
    hpg = gw // HEAD_DIM
    nt = t // tm
    v_blk0 = (2 * ad + kvd) // HEAD_DIM
    gate_blk0 = ad // gw

    def body(q_ref, k_ref, v_ref, g_ref, o_ref, og_ref, lse_ref):
        qi = pl.program_id(1)
        nkc = jnp.where(qi == 0, 1, nt)
        heads = [slice(h * HEAD_DIM, (h + 1) * HEAD_DIM) for h in range(hpg)]
        qs = [q_ref[:, sl] for sl in heads]

        def step(c, carry):
            rows = pl.ds(pl.multiple_of(c * tm, tm), tm)
            kc = k_ref[rows, :]
            vc = v_ref[rows, :]
            ss = [lax.dot_general(q, kc, _NT, preferred_element_type=F32) for q in qs]
            ps, stats = [], []
            for h in range(hpg):
                m, l, _ = carry[h]
                m_new = jnp.maximum(m, jnp.max(ss[h], axis=-1, keepdims=True))
                a = jnp.exp2(m - m_new)
                p = jnp.exp2(ss[h] - m_new)
                stats.append((m_new, a * l + jnp.sum(p, axis=-1, keepdims=True), a))
                ps.append(p.astype(BF16))
            return tuple((stats[h][0], stats[h][1],
                          stats[h][2] * carry[h][2] + jnp.dot(ps[h], vc, preferred_element_type=F32))
                         for h in range(hpg))

        init = tuple((jnp.full((tm, 1), -1e30, F32), jnp.zeros((tm, 1), F32), jnp.zeros((tm, HEAD_DIM), F32))
                     for _ in range(hpg))
        res = lax.fori_loop(0, nkc, step, init)
        lane = lax.broadcasted_iota(jnp.int32, (tm, LANES), 1)
        lse_blk = jnp.zeros((tm, LANES), F32)
        for h, sl in enumerate(heads):
            m, l, acc = res[h]
            o = acc / l
            o_ref[:, sl] = o.astype(o_ref.dtype)
            og_ref[:, sl] = (o * _silu(g_ref[:, sl].astype(F32))).astype(og_ref.dtype)
            lse_blk = jnp.where(lane == h, m + jnp.log(l) * LOG2_E, lse_blk)
        lse_ref[...] = lse_blk

    return pl.pallas_call(
        body, name="flash_fwd", grid=(nkv, nt),
        in_specs=[pl.BlockSpec((tm, gw), lambda g, i: (i, g)),
                  pl.BlockSpec((t, HEAD_DIM), lambda g, i: (0, g)),
                  pl.BlockSpec((t, HEAD_DIM), lambda g, i: (0, v_blk0 + g)),
                  pl.BlockSpec((tm, gw), lambda g, i: (i, gate_blk0 + g))],
        out_specs=[pl.BlockSpec((tm, gw), lambda g, i: (i, g)),
                   pl.BlockSpec((tm, gw), lambda g, i: (i, g)),
                   pl.BlockSpec((None, tm, LANES), lambda g, i: (g, i, 0))],
        out_shape=[_sds((t, ad), BF16), _sds((t, ad), BF16), _sds((nkv, t, LANES), F32)],
        compiler_params=_cp("parallel", "parallel"))(qr, kr, qgkv, qgkv)


def gate_bwd(dog, o, qgkv, ad, kvd, tm):
    t = o.shape[0]
    nkv = kvd // HEAD_DIM
    hpg = ad // nkv // HEAD_DIM

    def body(dog_ref, o_ref, g_ref, do_ref, dg_ref, dl_ref):
        lane = lax.broadcasted_iota(jnp.int32, (tm, LANES), 1)
        for grp in range(nkv):
            blk = jnp.zeros((tm, LANES), F32)
            for hh in range(hpg):
                h = grp * hpg + hh
                sl = slice(h * HEAD_DIM, (h + 1) * HEAD_DIM)
                dv = dog_ref[:, sl].astype(F32)
                ov = o_ref[:, sl].astype(F32)
                gv = g_ref[:, sl].astype(F32)
                doh = dv * _silu(gv)
                do_ref[:, sl] = doh.astype(do_ref.dtype)
                dg_ref[:, sl] = (dv * ov * _dsilu(gv)).astype(dg_ref.dtype)
                blk = jnp.where(lane == hh, jnp.sum(doh * ov, axis=-1, keepdims=True), blk)
            dl_ref[grp] = blk

    return pl.pallas_call(
        body, name="gate_bwd", grid=(t // tm,),
        in_specs=[_row(tm, ad), _row(tm, ad), _colblk(tm, ad, 1)],
        out_specs=[_row(tm, ad), _row(tm, ad), pl.BlockSpec((nkv, tm, LANES), lambda j: (0, j, 0))],
        out_shape=[_sds((t, ad), BF16), _sds((t, ad), BF16), _sds((nkv, t, LANES), F32)],
        compiler_params=_cp("parallel"))(dog, o, qgkv)


def flash_bwd(qr, do, kr, kt, qgkv, lse_t, delta_t, ad, kvd, tm, scale):
    t = qr.shape[0]
    nkv = kvd // HEAD_DIM
    gw = ad // nkv
    hpg = gw // HEAD_DIM
    nt = t // tm
    v_blk0 = (2 * ad + kvd) // HEAD_DIM

    def body(q_ref, do_ref, k_ref, v_ref, kt_ref, lse_ref, dl_ref, dq_ref, dk_ref, dv_ref):
        qi = pl.program_id(1)

        @pl.when(qi == 0)
        def _():
            dk_ref[...] = jnp.zeros_like(dk_ref)
            dv_ref[...] = jnp.zeros_like(dv_ref)

        nkc = jnp.where(qi == 0, 1, nt)
        heads = [slice(h * HEAD_DIM, (h + 1) * HEAD_DIM) for h in range(hpg)]
        qs = [q_ref[:, sl] for sl in heads]
        dos = [do_ref[:, sl] for sl in heads]
        lses = [lse_ref[h:h + 1, :] for h in range(hpg)]
        dls = [dl_ref[h:h + 1, :] for h in range(hpg)]

        def step(c, dqts):
            rows = pl.ds(pl.multiple_of(c * tm, tm), tm)
            kc = k_ref[rows, :]
            vc = v_ref[rows, :]
            ktc = kt_ref[c]
            dk = jnp.zeros((tm, HEAD_DIM), F32)
            dv = jnp.zeros((tm, HEAD_DIM), F32)
            out = []
            for h in range(hpg):
                st = lax.dot_general(kc, qs[h], _NT, preferred_element_type=F32)
                pt = jnp.exp2(st - lses[h])
                dv = dv + jnp.dot(pt.astype(BF16), dos[h], preferred_element_type=F32)
                dpt = lax.dot_general(vc, dos[h], _NT, preferred_element_type=F32)
                dst = (pt * (dpt - dls[h])).astype(BF16)
                dk = dk + jnp.dot(dst, qs[h], preferred_element_type=F32)
                out.append(dqts[h] + jnp.dot(ktc, dst, preferred_element_type=F32))
            dk_ref[rows, :] += dk
            dv_ref[rows, :] += dv
            return tuple(out)

        res = lax.fori_loop(0, nkc, step, tuple(jnp.zeros((HEAD_DIM, tm), F32) for _ in range(hpg)))
        for h, sl in enumerate(heads):
            dq_ref[:, sl] = res[h].T * scale

        @pl.when(qi == nt - 1)
        def _():
            dk_ref[...] = dk_ref[...] * LN_2

    return pl.pallas_call(
        body, name="flash_bwd", grid=(nkv, nt),
        in_specs=[pl.BlockSpec((tm, gw), lambda g, i: (i, g)),
                  pl.BlockSpec((tm, gw), lambda g, i: (i, g)),
                  pl.BlockSpec((t, HEAD_DIM), lambda g, i: (0, g)),
                  pl.BlockSpec((t, HEAD_DIM), lambda g, i: (0, v_blk0 + g)),
                  pl.BlockSpec((None, nt, HEAD_DIM, tm), lambda g, i: (g, 0, 0, 0)),
                  pl.BlockSpec((None, None, hpg, tm), lambda g, i: (g, i, 0, 0)),
                  pl.BlockSpec((None, None, hpg, tm), lambda g, i: (g, i, 0, 0))],
        out_specs=[pl.BlockSpec((tm, gw), lambda g, i: (i, g)),
                   pl.BlockSpec((t, HEAD_DIM), lambda g, i: (0, g)),
                   pl.BlockSpec((t, HEAD_DIM), lambda g, i: (0, g))],
        out_shape=[_sds((t, ad), F32), _sds((t, kvd), F32), _sds((t, kvd), F32)],
        compiler_params=_cp("parallel", "arbitrary"))(qr, do, kr, qgkv, kt, lse_t, delta_t)


def _rows_per_head(a, tm, hpg):
    nkv, t, _ = a.shape
    return a[:, :, :hpg].reshape(nkv, t // tm, tm, hpg).transpose(0, 1, 3, 2)


def adaln_fwd(c16, w_mod):
    nlay, d, nl = w_mod.shape
    tn = _pick(nl, (512, 256, 128))

    def body(c_ref, w_ref, o_ref):
        o_ref[...] = jnp.dot(_silu(c_ref[...]), w_ref[...], preferred_element_type=F32,
                             precision=lax.Precision.HIGHEST)

    return pl.pallas_call(
        body, name="adaln_fwd", grid=(nlay, nl // tn),
        in_specs=[pl.BlockSpec((16, d), lambda l, j: (0, 0)),
                  pl.BlockSpec((None, d, tn), lambda l, j: (l, 0, j))],
        out_specs=pl.BlockSpec((None, 16, tn), lambda l, j: (l, 0, j)),
        out_shape=_sds((nlay, 16, nl), F32), compiler_params=_cp("parallel", "parallel"))(c16, w_mod)


def adaln_bwd(c16t, dm, w_mod):
    nlay, d, nl = w_mod.shape
    tn = _pick(nl, (512, 256, 128))

    def body(c_ref, dm_ref, w_ref, dw_ref, dc_ref):
        @pl.when(pl.program_id(1) == 0)
        def _():
            dc_ref[...] = jnp.zeros_like(dc_ref)

        dmv = dm_ref[...]
        dw_ref[...] = jnp.dot(_silu(c_ref[...]), dmv, preferred_element_type=F32,
                              precision=lax.Precision.HIGHEST)
        dc_ref[...] += lax.dot_general(dmv, w_ref[...], _NT, preferred_element_type=F32,
                                       precision=lax.Precision.HIGHEST)

    return pl.pallas_call(
        body, name="adaln_bwd", grid=(nlay, nl // tn),
        in_specs=[pl.BlockSpec((d, 16), lambda l, j: (0, 0)),
                  pl.BlockSpec((None, 16, tn), lambda l, j: (l, 0, j)),
                  pl.BlockSpec((None, d, tn), lambda l, j: (l, 0, j))],
        out_specs=[pl.BlockSpec((None, d, tn), lambda l, j: (l, 0, j)),
                   pl.BlockSpec((None, 16, d), lambda l, j: (l, 0, 0))],
        out_shape=[_sds((nlay, d, nl), F32), _sds((nlay, 16, d), F32)],
        compiler_params=_cp("parallel", "arbitrary"))(c16t, dm, w_mod)


def sum_leading(a, name):
    n = a.shape[0]

    def body(a_ref, o_ref):
        acc = a_ref[0]
        for i in range(1, n):
            acc = acc + a_ref[i]
        o_ref[...] = acc

    return pl.pallas_call(body, name=name, in_specs=[VMEM_FULL], out_specs=VMEM_FULL,
                          out_shape=_sds(a.shape[1:], F32),
                          compiler_params=pltpu.CompilerParams(vmem_limit_bytes=VMEM_LIMIT))(a)


def c_ctx_grad(parts, c_ctx):
    n = parts.shape[0]

    def body(p_ref, c_ref, o_ref):
        acc = p_ref[0]
        for i in range(1, n):
            acc = acc + p_ref[i]
        o_ref[...] = acc * _dsilu(c_ref[...])

    return pl.pallas_call(body, name="c_ctx_grad", in_specs=[VMEM_FULL, VMEM_FULL], out_specs=VMEM_FULL,
                          out_shape=_sds(c_ctx.shape, F32))(parts, c_ctx)


def _as2d(a):
    return a.reshape(-1, a.shape[-1])


def _row_tile(r, c):
    for tr in (1024, 512, 256, 128, 64, 32, 16, 8):
        if r % tr == 0 and tr * c * 4 <= (1 << 20):
            return tr
    return r


def add_n(arrs, name, out_dtypes=(F32,)):
    shape = arrs[0].shape
    flat = [_as2d(a) for a in arrs]
    r, c = flat[0].shape
    tr = _row_tile(r, c)
    n_in = len(flat)

    def body(*refs):
        acc = refs[0][...].astype(F32)
        for ref in refs[1:n_in]:
            acc = acc + ref[...].astype(F32)
        for ref in refs[n_in:]:
            ref[...] = acc.astype(ref.dtype)

    spec = pl.BlockSpec((tr, c), lambda i: (i, 0))
    outs = pl.pallas_call(
        body, name=name, grid=(r // tr,), in_specs=[spec] * n_in, out_specs=[spec] * len(out_dtypes),
        out_shape=[_sds((r, c), dt) for dt in out_dtypes], compiler_params=_cp("parallel"))(*flat)
    return [o.reshape(shape) for o in outs]


def adamw(w, g, m, v):
    shape = w.shape
    flat = [_as2d(a.reshape((1,) + shape) if len(shape) == 1 else a) for a in (w, g, m, v)]
    r, c = flat[0].shape
    tr = _row_tile(r, c)
    c1 = 1.0 - ADAM_B1 ** ADAM_STEP
    c2 = 1.0 - ADAM_B2 ** ADAM_STEP

    def body(w_ref, g_ref, m_ref, v_ref, d_ref, nm_ref, nv_ref):
        gv = g_ref[...]
        nm = ADAM_B1 * m_ref[...] + (1.0 - ADAM_B1) * gv
        nv = ADAM_B2 * v_ref[...] + (1.0 - ADAM_B2) * (gv * gv)
        d_ref[...] = -ADAM_LR * ((nm / c1) / (jnp.sqrt(nv / c2) + ADAM_EPS) + ADAM_WD * w_ref[...])
        nm_ref[...] = nm
        nv_ref[...] = nv

    spec = pl.BlockSpec((tr, c), lambda i: (i, 0))
    outs = pl.pallas_call(
        body, name="adamw", grid=(r // tr,), in_specs=[spec] * 4, out_specs=[spec] * 3,
        out_shape=[_sds((r, c), F32)] * 3, compiler_params=_cp("parallel"))(*flat)
    return tuple(o.reshape(shape) for o in outs)


def _place():
    return lax.axis_index("x"), lax.axis_index("y"), lax.axis_index("c")


def _remote(src, dst, ssem, rsem, dev):
    return pltpu.make_async_remote_copy(src_ref=src, dst_ref=dst, send_sem=ssem, recv_sem=rsem,
                                        device_id=dev, device_id_type=MESH)


def all_gather8(v, name):
    m_per, n = v.shape

    def body(x_ref, out_ref, send_sems, recv_sems, local_sem):
        x, y, c = _place()
        me, sibling = (x, y, c), (x, y, 1 - c)
        chips = [(1 - x, y), (x, 1 - y), (1 - x, 1 - y)]

        def rows(px, py, pc):
            return out_ref.at[pl.ds((4 * px + 2 * py + pc) * m_per, m_per), :]

        def copy(k, block, to, src=None):
            return _remote(rows(*block) if src is None else src, rows(*block),
                           send_sems.at[k], recv_sems.at[k], to)

        mine = pltpu.make_async_copy(x_ref, rows(*me), local_sem)
        mine.start()
        first = [copy(0, me, sibling, src=x_ref)]
        first += [copy(1 + j, me, (*chip, c), src=x_ref) for j, chip in enumerate(chips)]
        for cp in first:
            cp.start()
        passed = [copy(4 + j, (*chip, c), sibling) for j, chip in enumerate(chips)]
        for j, chip in enumerate(chips):
            copy(1 + j, (*chip, c), me).wait_recv()
            passed[j].start()
        copy(0, sibling, me).wait_recv()
        for j, chip in enumerate(chips):
            copy(4 + j, (*chip, 1 - c), me).wait_recv()
        for cp in first + passed:
            cp.wait_send()
        mine.wait()

    return pl.pallas_call(
        body, name=name, out_shape=_sds((8 * m_per, n), v.dtype),
        in_specs=[VMEM_FULL], out_specs=VMEM_FULL,
        scratch_shapes=[pltpu.SemaphoreType.DMA((7,)), pltpu.SemaphoreType.DMA((7,)), pltpu.SemaphoreType.DMA],
        compiler_params=pltpu.CompilerParams(vmem_limit_bytes=VMEM_LIMIT))(v)


def _slabs(ref, n):
    rows = ref.shape[0] // n
    return [ref.at[pl.ds(i * rows, rows)] for i in range(n)]


def gather_weight(w, name):
    _, r, cdim = w.shape
    n = D2D_PARTS

    def body(w_ref, out_ref, send_sems, recv_sems, local_sems):
        x, y, c = _place()
        sibling = (x, y, 1 - c)
        chips = [(1 - x, y), (x, 1 - y), (1 - x, 1 - y)]
        mine = 2 * x + y
        local = [pltpu.make_async_copy(w_ref.at[i], out_ref.at[i, mine], local_sems.at[i]) for i in range(2)]
        for cp in local:
            cp.start()
        src = _slabs(w_ref.at[c], n)
        first = []
        for k, chip in enumerate(chips):
            dst = _slabs(out_ref.at[c, mine], n)
            for j in range(n):
                cp = _remote(src[j], dst[j], send_sems.at[k * n + j], recv_sems.at[k * n + j], (*chip, c))
                cp.start()
                first.append(cp)
        passed = []
        for j in range(n):
            for k, (px, py) in enumerate(chips):
                theirs = _slabs(out_ref.at[c, 2 * px + py], n)[j]
                _remote(src[j], theirs, send_sems.at[k * n + j], recv_sems.at[k * n + j], sibling).wait_recv()
                cp = _remote(theirs, theirs, send_sems.at[(3 + k) * n + j], recv_sems.at[(3 + k) * n + j], sibling)
                cp.start()
                passed.append(cp)
        for j in range(n):
            for k, (px, py) in enumerate(chips):
                other = _slabs(out_ref.at[1 - c, 2 * px + py], n)[j]
                _remote(other, other, send_sems.at[(3 + k) * n + j], recv_sems.at[(3 + k) * n + j],
                        sibling).wait_recv()
        for cp in first + passed:
            cp.wait_send()
        for cp in local:
            cp.wait()

    return pl.pallas_call(
        body, name=name, out_shape=_sds((2, 4, r, cdim), w.dtype), in_specs=[ANY], out_specs=ANY,
        scratch_shapes=[pltpu.SemaphoreType.DMA((6 * n,)), pltpu.SemaphoreType.DMA((6 * n,)),
                        pltpu.SemaphoreType.DMA((2,))])(w)


def rs_chip_exchange(gb0, gb1, g0, g1, name):
    _, r, cdim = g0.shape

    def body(gb0_ref, gb1_ref, g0_ref, g1_ref, mine_ref, st_ref, send_sems, recv_sems, local_sems):
        x, y, c = _place()
        chips = [(1 - x, y), (x, 1 - y), (1 - x, 1 - y)]
        keep = [pltpu.make_async_copy(g_ref.at[2 * x + y], mine_ref.at[i], local_sems.at[i])
                for i, g_ref in enumerate((g0_ref, g1_ref))]
        for cp in keep:
            cp.start()
        cps = []
        for k, (px, py) in enumerate(chips):
            for i, gb_ref in enumerate((gb0_ref, gb1_ref)):
                cps.append(_remote(gb_ref.at[2 * px + py], st_ref.at[k, i], send_sems.at[2 * k + i],
                                   recv_sems.at[2 * k + i], (px, py, c)))
        for cp in cps:
            cp.start()
        for cp in cps:
            cp.wait()
        for cp in keep:
            cp.wait()

    return pl.pallas_call(
        body, name=name, out_shape=[_sds((2, r, cdim), F32), _sds((3, 2, r, cdim), BF16)],
        in_specs=[ANY] * 4, out_specs=[ANY, ANY],
        scratch_shapes=[pltpu.SemaphoreType.DMA((6,)), pltpu.SemaphoreType.DMA((6,)),
                        pltpu.SemaphoreType.DMA((2,))])(gb0, gb1, g0, g1)


def rs_pair_swap(part, part_b, name):
    _, r, cdim = part.shape
    n = 2 * D2D_PARTS

    def body(p_ref, pb_ref, own_ref, got_ref, send_sems, recv_sems, local_sem):
        x, y, c = _place()
        sibling = (x, y, 1 - c)
        keep = pltpu.make_async_copy(p_ref.at[c], own_ref, local_sem)
        keep.start()
        src, dst = _slabs(pb_ref.at[1 - c], n), _slabs(got_ref, n)
        cps = [_remote(src[j], dst[j], send_sems.at[j], recv_sems.at[j], sibling) for j in range(n)]
        for cp in cps:
            cp.start()
        for cp in cps:
            cp.wait()
        keep.wait()

    return pl.pallas_call(
        body, name=name, out_shape=[_sds((r, cdim), F32), _sds((r, cdim), BF16)],
        in_specs=[ANY, ANY], out_specs=[ANY, ANY],
        scratch_shapes=[pltpu.SemaphoreType.DMA((n,)), pltpu.SemaphoreType.DMA((n,)),
                        pltpu.SemaphoreType.DMA])(part, part_b)


def rs_pair_share(red, name):
    n = 2 * D2D_PARTS

    def body(red_ref, out_ref, send_sems, recv_sems, local_sem):
        x, y, c = _place()
        sibling = (x, y, 1 - c)
        keep = pltpu.make_async_copy(red_ref, out_ref.at[c], local_sem)
        keep.start()
        src, dst, got = _slabs(red_ref, n), _slabs(out_ref.at[c], n), _slabs(out_ref.at[1 - c], n)
        cps = [_remote(src[j], dst[j], send_sems.at[j], recv_sems.at[j], sibling) for j in range(n)]
        for cp in cps:
            cp.start()
        for cp in cps:
            cp.wait_send()
        for j in range(n):
            _remote(src[j], got[j], send_sems.at[j], recv_sems.at[j], sibling).wait_recv()
        keep.wait()

    return pl.pallas_call(
        body, name=name, out_shape=_sds((2,) + red.shape, red.dtype), in_specs=[ANY], out_specs=ANY,
        scratch_shapes=[pltpu.SemaphoreType.DMA((n,)), pltpu.SemaphoreType.DMA((n,)), pltpu.SemaphoreType.DMA])(red)


def reduce_scatter_grad(g0, g1, tag):
    mine, theirs = rs_chip_exchange(g0.astype(BF16), g1.astype(BF16), g0, g1, "rs_exchange_" + tag)
    part, part_b = add_n([mine, theirs[0], theirs[1], theirs[2]], "rs_chip_add_" + tag, (F32, BF16))
    own, got = rs_pair_swap(part, part_b, "rs_swap_" + tag)
    red, = add_n([own, got], "rs_pair_add_" + tag, (BF16,))
    return rs_pair_share(red, "rs_share_" + tag).astype(F32)


def _rope_tables(lc, s):
    rows_n = s // GRID_W
    row = jnp.repeat(jnp.arange(rows_n, dtype=F32), GRID_W)
    col = jnp.tile(jnp.arange(GRID_W, dtype=F32), rows_n)
    axis_dim = HEAD_DIM // 2
    inv_freq = ROPE_THETA ** (-jnp.arange(0, axis_dim, 2, dtype=F32) / axis_dim)
    ang_r = row[:, None] * inv_freq[None, :]
    ang_c = col[:, None] * inv_freq[None, :]
    cr, sr, cc, sc = jnp.cos(ang_r), jnp.sin(ang_r), jnp.cos(ang_c), jnp.sin(ang_c)
    cos_l = jnp.concatenate([cr, cr, cc, cc], axis=1)
    sin_l = jnp.concatenate([-sr, sr, -sc, sc], axis=1)
    cos_t = jnp.concatenate([jnp.ones((lc, HEAD_DIM), F32), cos_l], axis=0)
    sin_t = jnp.concatenate([jnp.zeros((lc, HEAD_DIM), F32), sin_l], axis=0)
    return cos_t, sin_t


def _pad_rows(a, rows):
    return jnp.concatenate([a, jnp.zeros((rows - a.shape[0],) + a.shape[1:], a.dtype)], axis=0)


def _pad_cols(a, cols):
    return jnp.concatenate([a, jnp.zeros(a.shape[:-1] + (cols - a.shape[-1],), a.dtype)], axis=-1)


def kernel(x, c, ctx, c_ctx, w_mod, b_mod, post_ln_g, post_ln_b, w_in_e, conv_a_w, conv_a_b, norm_a_g, norm_a_b, conv_b_w, w_out_e, w_in_o, q_norm_g, k_norm_g, w_out_o, loss_target, m_c_ctx, m_w_mod, m_b_mod, m_post_ln_g, m_post_ln_b, m_w_in_e, m_conv_a_w, m_conv_a_b, m_norm_a_g, m_norm_a_b, m_conv_b_w, m_w_out_e, m_w_in_o, m_q_norm_g, m_k_norm_g, m_w_out_o, v_c_ctx, v_w_mod, v_b_mod, v_post_ln_g, v_post_ln_b, v_w_in_e, v_conv_a_w, v_conv_a_b, v_norm_a_g, v_norm_a_b, v_conv_b_w, v_w_out_e, v_w_in_o, v_q_norm_g, v_k_norm_g, v_w_out_o):
    s, d = x.shape[1], x.shape[2]
    lc = ctx.shape[1]
    t = lc + s
    tm = lc
    depth = w_mod.shape[0]
    n_even, n_odd = w_in_e.shape[0], w_in_o.shape[0]
    ad = w_out_o.shape[1] * 4
    kvd = (w_in_o.shape[2] * 4 - 2 * ad) // 2
    nkv = kvd // HEAD_DIM
    hpg = ad // nkv // HEAD_DIM
    nlm = w_mod.shape[2]
    alpha = (2.0 * depth) ** 0.25
    scale = HEAD_DIM ** -0.5
    assert n_even == 2 and n_odd == 2 and depth == 4 and hpg == GQA_GROUP
    assert lc % CONV_ROWS == 0 and s % tm == 0 and d % LANES == 0

    xi, yi, ci = _place()
    shard = 2 * xi + yi
    dev = 4 * xi + 2 * yi + ci

    wg_in_e = gather_weight(w_in_e.astype(BF16), "gather_w_in_e")
    wg_out_e = gather_weight(w_out_e.astype(BF16), "gather_w_out_e")
    wg_in_o = gather_weight(w_in_o.astype(BF16), "gather_w_in_o")
    wg_out_o = gather_weight(w_out_o.astype(BF16), "gather_w_out_o")

    c_all = all_gather8(_pad_rows(c, 8), "gather_c")
    c16 = _pad_rows(jnp.concatenate([c_all[0::8], c_ctx[None, :]], axis=0), 16)
    m_part = adaln_fwd(c16, w_mod)
    m_all = all_gather8(m_part.reshape(depth * 16, nlm), "gather_mod")
    m_all = m_all.reshape(8, depth, 16, nlm)[0::2]
    m_full = m_all.transpose(1, 2, 0, 3).reshape(depth, 16, 4 * nlm) + b_mod[:, None, :]
    m_lat = lax.dynamic_index_in_dim(m_full, dev, axis=1, keepdims=False)
    m_ctx = m_full[:, 8]

    def seg2(l, part):
        return jnp.stack([m_ctx[l, part * d:(part + 1) * d], m_lat[l, part * d:(part + 1) * d]])[:, None, :]

    cos_t, sin_t = _rope_tables(lc, s)

    small_gathered = all_gather8(
        _pad_rows(jnp.concatenate([conv_a_w.reshape(n_even * CONV_A_TAPS, -1),
                                   conv_b_w.reshape(n_even * CONV_B_TAPS, -1)], axis=0), 72), "gather_taps")
    taps = small_gathered.reshape(8, 72, -1)[0::2]
    taps = taps.transpose(1, 0, 2).reshape(72, d)
    caw = taps[:n_even * CONV_A_TAPS].reshape(n_even, CONV_A_TAPS, d)
    cbw = taps[n_even * CONV_A_TAPS:n_even * (CONV_A_TAPS + CONV_B_TAPS)].reshape(n_even, CONV_B_TAPS, d)
    caw_pad = jnp.concatenate([caw, jnp.zeros((n_even, 32 - CONV_A_TAPS, d), F32)], axis=1)
    cbw_pad = jnp.concatenate([cbw, jnp.zeros((n_even, 8 - CONV_B_TAPS, d), F32)], axis=1)

    xc = jnp.concatenate([ctx[0], x[0]], axis=0)
    saved = []
    for l in range(depth):
        i = l // 2
        shift2, scale2, gate2 = seg2(l, 0), seg2(l, 1), seg2(l, 2)
        h = mod_fwd(xc, scale2, shift2, tm)
        if l % 2 == 0:
            p = mm_nn(h, wg_in_e[i], BF16)
            u1 = conv_a_fwd(p, caw_pad[i], conv_a_b[i][None, :], lc, d)
            a_out = ln_a_fwd(u1, p, norm_a_g[i][None, :], norm_a_b[i][None, :], tm)
            b_out = conv_b_fwd(p, cbw_pad[i], lc, d)
            ab = jnp.concatenate([a_out, b_out], axis=1)
            y = mm_nn(ab, wg_out_e[i].reshape(1, 2 * d, d), F32)
            saved.append(dict(xc=xc, h=h, p=p, u1=u1, ab=ab, y=y))
        else:
            qgkv = mm_nn(h, wg_in_o[i], BF16)
            qr, kr = qk_fwd(qgkv, q_norm_g[i][None, :], k_norm_g[i][None, :], cos_t, sin_t, ad, kvd, tm,
                            scale * LOG2_E)
            o, og, lse = flash_fwd(qr, kr, qgkv, ad, kvd, tm)
            y = mm_nn(og, wg_out_o[i].reshape(1, ad, d), F32)
            saved.append(dict(xc=xc, h=h, qgkv=qgkv, qr=qr, kr=kr, o=o, og=og, lse=lse, y=y))
        xc = post_ln_fwd(xc, y, gate2, post_ln_g[l][None, :], post_ln_b[l][None, :], alpha, tm)

    dxc, loss_acc = loss_head(xc, loss_target[0], lc, tm)
    loss = lax.psum(0.5 / d * jnp.sum(loss_acc[0]), MESH_AXES)

    g_in_e, g_out_e, g_in_o, g_out_o = [None] * 2, [None] * 2, [None] * 2, [None] * 2
    d_mod_lat, d_mod_ctx = [None] * depth, [None] * depth
    d_pln_g, d_pln_b = [None] * depth, [None] * depth
    d_cab, d_nag, d_nab, d_caw, d_cbw = [None] * 2, [None] * 2, [None] * 2, [None] * 2, [None] * 2
    d_qg, d_kg = [None] * 2, [None] * 2
    for l in reversed(range(depth)):
        i = l // 2
        sv = saved[l]
        scale2, gate2 = seg2(l, 1), seg2(l, 2)
        dzx, dy, acc_ln = post_ln_bwd(dxc, sv["xc"], sv["y"], gate2, post_ln_g[l][None, :], alpha, tm)
        d_pln_g[l] = acc_ln[0, 1] + acc_ln[1, 1]
        d_pln_b[l] = acc_ln[0, 2] + acc_ln[1, 2]
        if l % 2 == 0:
            w_out3 = wg_out_e[i].reshape(1, 2 * d, d)
            dab = mm_nt(dy, w_out3, BF16)
            g_out_e[i] = mm_tn(sv["ab"], dy, 1).reshape(4, 2 * d // 4, d)
            du1, d_agate, acc_a = ln_a_bwd(dab, sv["u1"], sv["p"], norm_a_g[i][None, :], norm_a_b[i][None, :], tm)
            d_nag[i], d_nab[i], d_cab[i] = acc_a[0], acc_a[1], acc_a[2]
            d_aval, d_aglu, d_caw[i] = conv_a_bwd(du1, sv["p"], caw_pad[i], lc, d)
            d_bx, d_bb, d_bc, d_bg, d_cbw[i] = conv_b_bwd(dab, sv["p"], cbw_pad[i], lc, d)
            dp = jnp.concatenate([d_aval, d_aglu, d_agate, d_bx, d_bb, d_bc, d_bg], axis=1)
            dh = mm_nt(dp, wg_in_e[i], F32)
            g_in_e[i] = mm_tn(sv["h"], dp, 4)
        else:
            w_out3 = wg_out_o[i].reshape(1, ad, d)
            dog = mm_nt(dy, w_out3, BF16)
            g_out_o[i] = mm_tn(sv["og"], dy, 1).reshape(4, ad // 4, d)
            do, dgate, delta = gate_bwd(dog, sv["o"], sv["qgkv"], ad, kvd, tm)
            dqr, dkr, dv = flash_bwd(sv["qr"], do, sv["kr"], _chunks_t(sv["kr"], nkv, tm), sv["qgkv"],
                                     _rows_per_head(sv["lse"], tm, hpg), _rows_per_head(delta, tm, hpg),
                                     ad, kvd, tm, scale)
            dq, dk, acc_qk = qk_bwd(dqr, dkr, sv["qgkv"], q_norm_g[i][None, :], k_norm_g[i][None, :],
                                    cos_t, sin_t, ad, kvd, tm)
            d_qg[i], d_kg[i] = acc_qk[0], acc_qk[1]
            dqgkv = jnp.concatenate([dq, dgate, dk, dv.astype(BF16)], axis=1)
            dh = mm_nt(dqgkv, wg_in_o[i], F32)
            g_in_o[i] = mm_tn(sv["h"], dqgkv, 4)
        dxc, acc_mod = mod_bwd(dh, dzx, sv["xc"], scale2, tm)
        d_mod_ctx[l] = jnp.stack([acc_mod[0, 0], acc_mod[0, 1], acc_ln[0, 0]])
        d_mod_lat[l] = jnp.stack([acc_mod[1, 0], acc_mod[1, 1], acc_ln[1, 0]])
    grad_x = dxc[lc:][None]

    pack = jnp.concatenate(
        [jnp.concatenate(d_mod_ctx, axis=0),
         jnp.stack(d_pln_g), jnp.stack(d_pln_b),
         jnp.stack(d_cab), jnp.stack(d_nag), jnp.stack(d_nab),
         jnp.concatenate(d_caw, axis=0),
         jnp.concatenate(d_cbw, axis=0),
         _pad_cols(jnp.stack(d_qg), d), _pad_cols(jnp.stack(d_kg), d),
         jnp.zeros((2, d), F32),
         jnp.concatenate(d_mod_lat, axis=0),
         jnp.zeros((4, d), F32)], axis=0)
    gathered = all_gather8(pack, "gather_small").reshape(8, 128, d)
    small = sum_leading(gathered, "sum_small")
    dm_ctx = small[0:12].reshape(depth, 1, 3 * d)
    dm_lat = gathered[:, 112:124].reshape(8, depth, 3 * d).transpose(1, 0, 2)
    dm = jnp.concatenate([dm_lat, dm_ctx, jnp.zeros((depth, 7, 3 * d), F32)], axis=1)
    g_b_mod = sum_leading(dm.transpose(1, 0, 2), "sum_b_mod")
    dm_shard = lax.dynamic_slice_in_dim(dm, shard * nlm, nlm, axis=2)
    g_w_mod, dc_part = adaln_bwd(c16.T, dm_shard, w_mod)
    dc_all = all_gather8(_pad_rows(dc_part[:, 8, :], 8), "gather_dc").reshape(8, 8, d)
    g_c_ctx = c_ctx_grad(dc_all[0::2, :depth].reshape(4 * depth, 1, d), c_ctx[None, :])[0]

    g_pln_g, g_pln_b = small[12:16], small[16:20]
    g_cab, g_nag, g_nab = small[20:22], small[22:24], small[24:26]
    dch = d // 4
    g_caw = lax.dynamic_slice_in_dim(small[26:90].reshape(2, 32, d)[:, :CONV_A_TAPS], shard * dch, dch, axis=2)
    g_cbw = lax.dynamic_slice_in_dim(small[90:106].reshape(2, 8, d)[:, :CONV_B_TAPS], shard * dch, dch, axis=2)
    g_qg, g_kg = small[106:108, :HEAD_DIM], small[108:110, :HEAD_DIM]

    g_w_in_e = reduce_scatter_grad(g_in_e[0], g_in_e[1], "in_e")
    g_w_out_e = reduce_scatter_grad(g_out_e[0], g_out_e[1], "out_e")
    g_w_in_o = reduce_scatter_grad(g_in_o[0], g_in_o[1], "in_o")
    g_w_out_o = reduce_scatter_grad(g_out_o[0], g_out_o[1], "out_o")

    grads = [g_c_ctx, g_w_mod, g_b_mod, g_pln_g, g_pln_b, g_w_in_e, g_caw, g_cab, g_nag, g_nab, g_cbw,
             g_w_out_e, g_w_in_o, g_qg, g_kg, g_w_out_o]
    weights = [c_ctx, w_mod, b_mod, post_ln_g, post_ln_b, w_in_e, conv_a_w, conv_a_b, norm_a_g, norm_a_b,
               conv_b_w, w_out_e, w_in_o, q_norm_g, k_norm_g, w_out_o]
    ms = [m_c_ctx, m_w_mod, m_b_mod, m_post_ln_g, m_post_ln_b, m_w_in_e, m_conv_a_w, m_conv_a_b, m_norm_a_g,
          m_norm_a_b, m_conv_b_w, m_w_out_e, m_w_in_o, m_q_norm_g, m_k_norm_g, m_w_out_o]
    vs = [v_c_ctx, v_w_mod, v_b_mod, v_post_ln_g, v_post_ln_b, v_w_in_e, v_conv_a_w, v_conv_a_b, v_norm_a_g,
          v_norm_a_b, v_conv_b_w, v_w_out_e, v_w_in_o, v_q_norm_g, v_k_norm_g, v_w_out_o]
    deltas, new_ms, new_vs = [], [], []
    for wv, gv, mv, vv in zip(weights, grads, ms, vs):
        dl, nm, nv = adamw(wv, gv, mv, vv)
        deltas.append(dl)
        new_ms.append(nm)
        new_vs.append(nv)
    return (loss, grad_x, *grads, *deltas, *new_ms, *new_vs)
```

```python
import functools

import jax
import jax.numpy as jnp
from jax import lax
from jax.experimental import pallas as pl
from jax.experimental.pallas import tpu as pltpu

F32 = jnp.float32
BF16 = jnp.bfloat16

LANES = 128
SUBLANES = 8
HEAD_DIM = 128
GQA_GROUP = 4
GRID_W = 64
ROPE_THETA = 10000.0
LN_EPS = 1e-5
RMS_EPS = 1e-6
CONV_A_TAPS = 31
CONV_B_TAPS = 3
HALO = 16
CONV_ROWS = 128
ADAM_LR = 0.001
ADAM_B1 = 0.9
ADAM_B2 = 0.999
ADAM_EPS = 1e-08
ADAM_WD = 0.01
ADAM_STEP = 10
VMEM_LIMIT = 56 * 1024 * 1024
D2D_PARTS = 4
LOG2_E = 1.4426950408889634
LN_2 = 0.6931471805599453
MESH_AXES = ("x", "y", "c")
MESH = pl.DeviceIdType.MESH
ANY = pl.BlockSpec(memory_space=pl.ANY)
VMEM_FULL = pl.BlockSpec(memory_space=pltpu.VMEM)


def _sds(shape, dtype):
    return jax.ShapeDtypeStruct(tuple(shape), dtype)


def _cp(*sem):
    return pltpu.CompilerParams(dimension_semantics=sem, vmem_limit_bytes=VMEM_LIMIT)


def _pick(n, cands):
    for c in cands:
        if n % c == 0:
            return c
    return n


def _sigmoid(x):
    return 1.0 / (1.0 + jnp.exp(-x))


def _silu(x):
    return x * _sigmoid(x)


def _dsilu(x):
    s = _sigmoid(x)
    return s * (1.0 + x * (1.0 - s))


def _row(tm, d):
    return pl.BlockSpec((tm, d), lambda j: (j, 0))


def _seg(d):
    return pl.BlockSpec((None, 1, d), lambda j: (jnp.minimum(j, 1), 0, 0))


def _vec(d):
    return pl.BlockSpec((1, d), lambda j: (0, 0))


def _colblk(tm, width, blk):
    return pl.BlockSpec((tm, width), lambda j: (j, blk))


def _seg_acc(d):
    return pl.BlockSpec((None, SUBLANES, d), lambda j: (jnp.minimum(j, 1), 0, 0))


def _ln_stats(z):
    mu = jnp.mean(z, axis=-1, keepdims=True)
    zc = z - mu
    var = jnp.mean(zc * zc, axis=-1, keepdims=True)
    rstd = lax.rsqrt(var + LN_EPS)
    return zc * rstd, rstd


def _ln_bwd(dxh, xhat, rstd):
    m1 = jnp.mean(dxh, axis=-1, keepdims=True)
    m2 = jnp.mean(dxh * xhat, axis=-1, keepdims=True)
    return rstd * (dxh - m1 - xhat * m2)


def _colsum(v):
    return jnp.sum(v, axis=0, keepdims=True)


def mod_fwd(xc, scale2, shift2, tm):
    t, d = xc.shape

    def body(x_ref, sc_ref, sh_ref, h_ref):
        h_ref[...] = (x_ref[...] * (1.0 + sc_ref[...]) + sh_ref[...]).astype(h_ref.dtype)

    return pl.pallas_call(
        body, name="mod_fwd", grid=(t // tm,),
        in_specs=[_row(tm, d), _seg(d), _seg(d)], out_specs=_row(tm, d),
        out_shape=_sds((t, d), BF16), compiler_params=_cp("parallel"))(xc, scale2, shift2)


def post_ln_fwd(xc, y, gate2, g, b, alpha, tm):
    t, d = xc.shape

    def body(x_ref, y_ref, gt_ref, g_ref, b_ref, o_ref):
        z = alpha * x_ref[...] + gt_ref[...] * y_ref[...]
        xhat, _ = _ln_stats(z)
        o_ref[...] = xhat * g_ref[...] + b_ref[...]

    return pl.pallas_call(
        body, name="post_ln_fwd", grid=(t // tm,),
        in_specs=[_row(tm, d), _row(tm, d), _seg(d), _vec(d), _vec(d)], out_specs=_row(tm, d),
        out_shape=_sds((t, d), F32), compiler_params=_cp("parallel"))(xc, y, gate2, g, b)


def post_ln_bwd(dout, xc, y, gate2, g, alpha, tm):
    t, d = xc.shape

    def body(do_ref, x_ref, y_ref, gt_ref, g_ref, dzx_ref, dy_ref, acc_ref):
        @pl.when(pl.program_id(0) <= 1)
        def _():
            acc_ref[...] = jnp.zeros_like(acc_ref)

        yv = y_ref[...]
        gate = gt_ref[...]
        xhat, rstd = _ln_stats(alpha * x_ref[...] + gate * yv)
        dout = do_ref[...]
        dz = _ln_bwd(dout * g_ref[...], xhat, rstd)
        dzx_ref[...] = alpha * dz
        dy_ref[...] = (gate * dz).astype(dy_ref.dtype)
        acc_ref[0:1, :] += _colsum(dz * yv)
        acc_ref[1:2, :] += _colsum(dout * xhat)
        acc_ref[2:3, :] += _colsum(dout)

    return pl.pallas_call(
        body, name="post_ln_bwd", grid=(t // tm,),
        in_specs=[_row(tm, d), _row(tm, d), _row(tm, d), _seg(d), _vec(d)],
        out_specs=[_row(tm, d), _row(tm, d), _seg_acc(d)],
        out_shape=[_sds((t, d), F32), _sds((t, d), BF16), _sds((2, SUBLANES, d), F32)],
        compiler_params=_cp("arbitrary"))(dout, xc, y, gate2, g)


def mod_bwd(dh, dzx, xc, scale2, tm):
    t, d = xc.shape

    def body(dh_ref, dzx_ref, x_ref, sc_ref, dx_ref, acc_ref):
        @pl.when(pl.program_id(0) <= 1)
        def _():
            acc_ref[...] = jnp.zeros_like(acc_ref)

        dhv = dh_ref[...].astype(F32)
        dx_ref[...] = dzx_ref[...] + dhv * (1.0 + sc_ref[...])
        acc_ref[0:1, :] += _colsum(dhv)
        acc_ref[1:2, :] += _colsum(dhv * x_ref[...])

    return pl.pallas_call(
        body, name="mod_bwd", grid=(t // tm,),
        in_specs=[_row(tm, d), _row(tm, d), _row(tm, d), _seg(d)],
        out_specs=[_row(tm, d), _seg_acc(d)],
        out_shape=[_sds((t, d), F32), _sds((2, SUBLANES, d), F32)],
        compiler_params=_cp("arbitrary"))(dh, dzx, xc, scale2)


def loss_head(xc, target, lc, tm):
    t, d = xc.shape

    def body(x_ref, t_ref, dx_ref, acc_ref):
        j = pl.program_id(0)

        @pl.when(j == 0)
        def _():
            acc_ref[...] = jnp.zeros_like(acc_ref)
            dx_ref[...] = jnp.zeros_like(dx_ref)

        @pl.when(j > 0)
        def _():
            err = x_ref[...] - t_ref[...]
            dx_ref[...] = err * (1.0 / d)
            col = _colsum(err * err)
            tot = col[:, 0:LANES]
            for k in range(1, d // LANES):
                tot = tot + col[:, k * LANES:(k + 1) * LANES]
            acc_ref[0:1, :] += tot

    nlc = lc // tm
    return pl.pallas_call(
        body, name="loss_head", grid=(t // tm,),
        in_specs=[_row(tm, d), pl.BlockSpec((tm, d), lambda j: (jnp.maximum(j - nlc, 0), 0))],
        out_specs=[_row(tm, d), pl.BlockSpec((SUBLANES, LANES), lambda j: (0, 0))],
        out_shape=[_sds((t, d), F32), _sds((SUBLANES, LANES), F32)],
        compiler_params=_cp("arbitrary"))(xc, target)


def mm_nn(a, w3, out_dtype):
    m, k = a.shape
    ns, _, nl = w3.shape
    tm = _pick(m, (768, 512, 256, 128))
    tn = _pick(nl, (512, 256, 128))
    npj = nl // tn

    def body(a_ref, w_ref, o_ref):
        o_ref[...] = jnp.dot(a_ref[...], w_ref[...], preferred_element_type=F32).astype(o_ref.dtype)

    return pl.pallas_call(
        body, name="mm_nn", grid=(m // tm, ns * npj),
        in_specs=[pl.BlockSpec((tm, k), lambda i, j: (i, 0)),
                  pl.BlockSpec((None, k, tn), lambda i, j: (j // npj, 0, j % npj))],
        out_specs=pl.BlockSpec((tm, tn), lambda i, j: (i, j)),
        out_shape=_sds((m, ns * nl), out_dtype), compiler_params=_cp("parallel", "parallel"))(a, w3)


def mm_nt(a, w3, out_dtype):
    m, _ = a.shape
    ns, k, nl = w3.shape
    tm = _pick(m, (768, 512, 256, 128))
    tk = _pick(k, (2048, 1024, 512, 256, 128))
    tn = _pick(nl, (512, 256, 128))
    npj = nl // tn
    nsteps = ns * npj

    def body(a_ref, w_ref, o_ref, acc_ref):
        n = pl.program_id(2)

        @pl.when(n == 0)
        def _():
            acc_ref[...] = jnp.zeros_like(acc_ref)

        acc_ref[...] += lax.dot_general(a_ref[...], w_ref[...], (((1,), (1,)), ((), ())),
                                        preferred_element_type=F32)

        @pl.when(n == nsteps - 1)
        def _():
            o_ref[...] = acc_ref[...].astype(o_ref.dtype)

    return pl.pallas_call(
        body, name="mm_nt", grid=(m // tm, k // tk, nsteps),
        in_specs=[pl.BlockSpec((tm, tn), lambda i, kk, n: (i, n)),
                  pl.BlockSpec((None, tk, tn), lambda i, kk, n: (n // npj, kk, n % npj))],
        out_specs=pl.BlockSpec((tm, tk), lambda i, kk, n: (i, kk)),
        out_shape=_sds((m, k), out_dtype), scratch_shapes=[pltpu.VMEM((tm, tk), F32)],
        compiler_params=_cp("parallel", "parallel", "arbitrary"))(a, w3)


def mm_tn(a, b, ns):
    m, k = a.shape
    nl = b.shape[1] // ns
    tm = _pick(m, (768, 512, 256, 128))
    tk = _pick(k, (1024, 512, 256, 128))
    tn = _pick(nl, (512, 256, 128))
    npj = nl // tn

    def body(a_ref, b_ref, o_ref):
        @pl.when(pl.program_id(2) == 0)
        def _():
            o_ref[...] = jnp.zeros_like(o_ref)

        o_ref[...] += lax.dot_general(a_ref[...], b_ref[...], (((0,), (0,)), ((), ())),
                                      preferred_element_type=F32)

    return pl.pallas_call(
        body, name="mm_tn", grid=(k // tk, ns * npj, m // tm),
        in_specs=[pl.BlockSpec((tm, tk), lambda i, j, r: (r, i)),
                  pl.BlockSpec((tm, tn), lambda i, j, r: (r, j))],
        out_specs=pl.BlockSpec((None, tk, tn), lambda i, j, r: (j // npj, i, j % npj)),
        out_shape=_sds((ns, k, nl), F32),
        compiler_params=_cp("parallel", "parallel", "arbitrary"))(a, b)


def _win_start(j, ncc):
    return pl.multiple_of(j * CONV_ROWS + jnp.where(j >= ncc, HALO, 0), SUBLANES)


def _tok_start(j):
    return pl.multiple_of(j * CONV_ROWS, CONV_ROWS)


def _shifted(xw, off):
    n = xw.shape[0]
    sh = (n - off) % n
    y = pltpu.roll(xw, sh, 0) if sh else xw
    return y[:CONV_ROWS]


def _conv_fwd(xw, w_ref, ntaps):
    pad = ntaps // 2
    acc = None
    for k in range(ntaps):
        term = w_ref[k:k + 1, :] * _shifted(xw, HALO + k - pad)
        acc = term if acc is None else acc + term
    return acc


def _conv_bwd_data(xw, w_ref, ntaps):
    pad = ntaps // 2
    acc = None
    for k in range(ntaps):
        term = w_ref[k:k + 1, :] * _shifted(xw, HALO - k + pad)
        acc = term if acc is None else acc + term
    return acc


def _conv_bwd_weight(dw_ref, d, xw, ntaps):
    pad = ntaps // 2
    for k in range(ntaps):
        dw_ref[k:k + 1, :] += _colsum(d * _shifted(xw, HALO + k - pad))


def _zero_halos(pad_ref, lc, t):
    z = jnp.zeros((HALO, LANES), F32)
    pad_ref[0:HALO, :] = z
    pad_ref[HALO + lc:2 * HALO + lc, :] = z
    pad_ref[2 * HALO + t:3 * HALO + t, :] = z


def _pad_dst(j, ncc):
    return pl.multiple_of(j * CONV_ROWS + HALO + jnp.where(j >= ncc, HALO, 0), SUBLANES)


def _chan(t, blk0):
    return pl.BlockSpec((t, LANES), lambda ct: (0, blk0 + ct))


def _tapw(rows):
    return pl.BlockSpec((rows, LANES), lambda ct: (0, ct))


def conv_a_fwd(p, w_pad, bias, lc, d):
    t = p.shape[0]
    nct, nch, ncc = d // LANES, t // CONV_ROWS, lc // CONV_ROWS

    def body(av_ref, ag_ref, w_ref, b_ref, u1_ref, pad_ref):
        _zero_halos(pad_ref, lc, t)

        def fill(j, carry):
            rows = pl.ds(_tok_start(j), CONV_ROWS)
            u0 = av_ref[rows, :].astype(F32) * _sigmoid(ag_ref[rows, :].astype(F32))
            pad_ref[pl.ds(_pad_dst(j, ncc), CONV_ROWS), :] = u0
            return carry

        lax.fori_loop(0, nch, fill, 0)

        def conv(j, carry):
            xw = pad_ref[pl.ds(_win_start(j, ncc), CONV_ROWS + 2 * HALO), :]
            u1_ref[pl.ds(_tok_start(j), CONV_ROWS), :] = _conv_fwd(xw, w_ref, CONV_A_TAPS) + b_ref[...]
            return carry

        lax.fori_loop(0, nch, conv, 0)

    return pl.pallas_call(
        body, name="conv_a_fwd", grid=(nct,),
        in_specs=[_chan(t, 0), _chan(t, nct), _tapw(32), _tapw(1)],
        out_specs=_chan(t, 0), out_shape=_sds((t, d), F32),
        scratch_shapes=[pltpu.VMEM((t + 3 * HALO, LANES), F32)],
        compiler_params=_cp("parallel"))(p, p, w_pad, bias)


def conv_b_fwd(p, w_pad, lc, d):
    t = p.shape[0]
    nct, nch, ncc = d // LANES, t // CONV_ROWS, lc // CONV_ROWS

    def body(bx_ref, bb_ref, bc_ref, bg_ref, w_ref, o_ref, pad_ref):
        _zero_halos(pad_ref, lc, t)

        def fill(j, carry):
            rows = pl.ds(_tok_start(j), CONV_ROWS)
            pad_ref[pl.ds(_pad_dst(j, ncc), CONV_ROWS), :] = (
                bc_ref[rows, :].astype(F32) * bx_ref[rows, :].astype(F32))
            return carry

        lax.fori_loop(0, nch, fill, 0)

        def conv(j, carry):
            rows = pl.ds(_tok_start(j), CONV_ROWS)
            xw = pad_ref[pl.ds(_win_start(j, ncc), CONV_ROWS + 2 * HALO), :]
            v = _conv_fwd(xw, w_ref, CONV_B_TAPS)
            o_ref[rows, :] = (bb_ref[rows, :].astype(F32) * v
                              * _silu(bg_ref[rows, :].astype(F32))).astype(o_ref.dtype)
            return carry

        lax.fori_loop(0, nch, conv, 0)

    return pl.pallas_call(
        body, name="conv_b_fwd", grid=(nct,),
        in_specs=[_chan(t, 3 * nct), _chan(t, 4 * nct), _chan(t, 5 * nct), _chan(t, 6 * nct), _tapw(8)],
        out_specs=_chan(t, 0), out_shape=_sds((t, d), BF16),
        scratch_shapes=[pltpu.VMEM((t + 3 * HALO, LANES), F32)],
        compiler_params=_cp("parallel"))(p, p, p, p, w_pad)


def ln_a_fwd(u1, p, g, b, tm):
    t, d = u1.shape

    def body(u_ref, ag_ref, g_ref, b_ref, o_ref):
        xhat, _ = _ln_stats(u_ref[...])
        u2 = xhat * g_ref[...] + b_ref[...]
        o_ref[...] = (_silu(u2) * _silu(ag_ref[...].astype(F32))).astype(o_ref.dtype)

    return pl.pallas_call(
        body, name="ln_a_fwd", grid=(t // tm,),
        in_specs=[_row(tm, d), _colblk(tm, d, 2), _vec(d), _vec(d)], out_specs=_row(tm, d),
        out_shape=_sds((t, d), BF16), compiler_params=_cp("parallel"))(u1, p, g, b)


def ln_a_bwd(dab, u1, p, g, b, tm):
    t, d = u1.shape

    def body(da_ref, u_ref, ag_ref, g_ref, b_ref, du_ref, dag_ref, acc_ref):
        @pl.when(pl.program_id(0) == 0)
        def _():
            acc_ref[...] = jnp.zeros_like(acc_ref)

        xhat, rstd = _ln_stats(u_ref[...])
        u2 = xhat * g_ref[...] + b_ref[...]
        ag = ag_ref[...].astype(F32)
        da = da_ref[...].astype(F32)
        dag_ref[...] = (da * _silu(u2) * _dsilu(ag)).astype(dag_ref.dtype)
        du2 = da * _silu(ag) * _dsilu(u2)
        du1 = _ln_bwd(du2 * g_ref[...], xhat, rstd)
        du_ref[...] = du1
        acc_ref[0:1, :] += _colsum(du2 * xhat)
        acc_ref[1:2, :] += _colsum(du2)
        acc_ref[2:3, :] += _colsum(du1)

    return pl.pallas_call(
        body, name="ln_a_bwd", grid=(t // tm,),
        in_specs=[_colblk(tm, d, 0), _row(tm, d), _colblk(tm, d, 2), _vec(d), _vec(d)],
        out_specs=[_row(tm, d), _row(tm, d), pl.BlockSpec((SUBLANES, d), lambda j: (0, 0))],
        out_shape=[_sds((t, d), F32), _sds((t, d), BF16), _sds((SUBLANES, d), F32)],
        compiler_params=_cp("arbitrary"))(dab, u1, p, g, b)


def conv_a_bwd(du1, p, w_pad, lc, d):
    t = p.shape[0]
    nct, nch, ncc = d // LANES, t // CONV_ROWS, lc // CONV_ROWS

    def body(du_ref, av_ref, ag_ref, w_ref, dav_ref, dag_ref, dw_ref, pad_u, pad_d):
        _zero_halos(pad_u, lc, t)
        _zero_halos(pad_d, lc, t)
        dw_ref[...] = jnp.zeros_like(dw_ref)

        def fill(j, carry):
            rows = pl.ds(_tok_start(j), CONV_ROWS)
            dst = pl.ds(_pad_dst(j, ncc), CONV_ROWS)
            pad_u[dst, :] = av_ref[rows, :].astype(F32) * _sigmoid(ag_ref[rows, :].astype(F32))
            pad_d[dst, :] = du_ref[rows, :]
            return carry

        lax.fori_loop(0, nch, fill, 0)

        def step(j, carry):
            rows = pl.ds(_tok_start(j), CONV_ROWS)
            win = pl.ds(_win_start(j, ncc), CONV_ROWS + 2 * HALO)
            du0 = _conv_bwd_data(pad_d[win, :], w_ref, CONV_A_TAPS)
            sig = _sigmoid(ag_ref[rows, :].astype(F32))
            dav_ref[rows, :] = (du0 * sig).astype(dav_ref.dtype)
            dag_ref[rows, :] = (du0 * av_ref[rows, :].astype(F32) * sig * (1.0 - sig)).astype(dag_ref.dtype)
            _conv_bwd_weight(dw_ref, du_ref[rows, :], pad_u[win, :], CONV_A_TAPS)
            return carry

        lax.fori_loop(0, nch, step, 0)

    return pl.pallas_call(
        body, name="conv_a_bwd", grid=(nct,),
        in_specs=[_chan(t, 0), _chan(t, 0), _chan(t, nct), _tapw(32)],
        out_specs=[_chan(t, 0), _chan(t, 0), _tapw(32)],
        out_shape=[_sds((t, d), BF16), _sds((t, d), BF16), _sds((32, d), F32)],
        scratch_shapes=[pltpu.VMEM((t + 3 * HALO, LANES), F32), pltpu.VMEM((t + 3 * HALO, LANES), F32)],
        compiler_params=_cp("parallel"))(du1, p, p, w_pad)


def conv_b_bwd(dab, p, w_pad, lc, d):
    t = p.shape[0]
    nct, nch, ncc = d // LANES, t // CONV_ROWS, lc // CONV_ROWS

    def body(db_ref, bx_ref, bb_ref, bc_ref, bg_ref, w_ref,
             dbx_ref, dbb_ref, dbc_ref, dbg_ref, dw_ref, pad_t, pad_d):
        _zero_halos(pad_t, lc, t)
        _zero_halos(pad_d, lc, t)
        dw_ref[...] = jnp.zeros_like(dw_ref)

        def fill(j, carry):
            rows = pl.ds(_tok_start(j), CONV_ROWS)
            pad_t[pl.ds(_pad_dst(j, ncc), CONV_ROWS), :] = (
                bc_ref[rows, :].astype(F32) * bx_ref[rows, :].astype(F32))
            return carry

        lax.fori_loop(0, nch, fill, 0)

        def first(j, carry):
            rows = pl.ds(_tok_start(j), CONV_ROWS)
            xw = pad_t[pl.ds(_win_start(j, ncc), CONV_ROWS + 2 * HALO), :]
            v = _conv_fwd(xw, w_ref, CONV_B_TAPS)
            bg = bg_ref[rows, :].astype(F32)
            bb = bb_ref[rows, :].astype(F32)
            db = db_ref[rows, :].astype(F32)
            sg = _silu(bg)
            dbb_ref[rows, :] = (db * v * sg).astype(dbb_ref.dtype)
            dbg_ref[rows, :] = (db * bb * v * _dsilu(bg)).astype(dbg_ref.dtype)
            dv = db * bb * sg
            pad_d[pl.ds(_pad_dst(j, ncc), CONV_ROWS), :] = dv
            _conv_bwd_weight(dw_ref, dv, xw, CONV_B_TAPS)
            return carry

        lax.fori_loop(0, nch, first, 0)

        def second(j, carry):
            rows = pl.ds(_tok_start(j), CONV_ROWS)
            dt = _conv_bwd_data(pad_d[pl.ds(_win_start(j, ncc), CONV_ROWS + 2 * HALO), :], w_ref, CONV_B_TAPS)
            dbc_ref[rows, :] = (dt * bx_ref[rows, :].astype(F32)).astype(dbc_ref.dtype)
            dbx_ref[rows, :] = (dt * bc_ref[rows, :].astype(F32)).astype(dbx_ref.dtype)
            return carry

        lax.fori_loop(0, nch, second, 0)

    return pl.pallas_call(
        body, name="conv_b_bwd", grid=(nct,),
        in_specs=[_chan(t, nct), _chan(t, 3 * nct), _chan(t, 4 * nct), _chan(t, 5 * nct), _chan(t, 6 * nct),
                  _tapw(8)],
        out_specs=[_chan(t, 0)] * 4 + [_tapw(8)],
        out_shape=[_sds((t, d), BF16)] * 4 + [_sds((8, d), F32)],
        scratch_shapes=[pltpu.VMEM((t + 3 * HALO, LANES), F32), pltpu.VMEM((t + 3 * HALO, LANES), F32)],
        compiler_params=_cp("parallel"))(dab, p, p, p, p, w_pad)


def _swap_halves(z, first_half):
    return jnp.where(first_half, pltpu.roll(z, 96, 1), pltpu.roll(z, 32, 1))


def _first_half_mask(rows):
    lane = lax.broadcasted_iota(jnp.int32, (rows, HEAD_DIM), 1)
    return (lane & 32) == 0


def qk_fwd(qgkv, qg, kg, cos_t, sin_t, ad, kvd, tm, qscale):
    t = qgkv.shape[0]

    def body(q_ref, k_ref, qg_ref, kg_ref, c_ref, s_ref, qo_ref, ko_ref):
        first = _first_half_mask(tm)
        cosv, sinv = c_ref[...], s_ref[...]

        def head(x, gain):
            inv = lax.rsqrt(jnp.mean(x * x, axis=-1, keepdims=True) + RMS_EPS)
            yv = x * inv * gain
            return yv * cosv + _swap_halves(yv, first) * sinv

        for h in range(ad // HEAD_DIM):
            sl = slice(h * HEAD_DIM, (h + 1) * HEAD_DIM)
            qo_ref[:, sl] = (head(q_ref[:, sl].astype(F32), qg_ref[...]) * qscale).astype(qo_ref.dtype)
        for h in range(kvd // HEAD_DIM):
            sl = slice(h * HEAD_DIM, (h + 1) * HEAD_DIM)
            ko_ref[:, sl] = head(k_ref[:, sl].astype(F32), kg_ref[...]).astype(ko_ref.dtype)

    return pl.pallas_call(
        body, name="qk_fwd", grid=(t // tm,),
        in_specs=[_colblk(tm, ad, 0), _colblk(tm, kvd, 2 * ad // kvd), _vec(HEAD_DIM), _vec(HEAD_DIM),
                  _row(tm, HEAD_DIM), _row(tm, HEAD_DIM)],
        out_specs=[_row(tm, ad), _row(tm, kvd)],
        out_shape=[_sds((t, ad), BF16), _sds((t, kvd), BF16)],
        compiler_params=_cp("parallel"))(qgkv, qgkv, qg, kg, cos_t, sin_t)


def qk_bwd(dqr, dkr, qgkv, qg, kg, cos_t, sin_t, ad, kvd, tm):
    t = qgkv.shape[0]

    def body(dq_ref, dk_ref, q_ref, k_ref, qg_ref, kg_ref, c_ref, s_ref, dqo_ref, dko_ref, acc_ref):
        @pl.when(pl.program_id(0) == 0)
        def _():
            acc_ref[...] = jnp.zeros_like(acc_ref)

        first = _first_half_mask(tm)
        cosv, sinv = c_ref[...], s_ref[...]

        def head(x, gain, dout):
            inv = lax.rsqrt(jnp.mean(x * x, axis=-1, keepdims=True) + RMS_EPS)
            xn = x * inv
            dy = dout * cosv + _swap_halves(dout * sinv, first)
            dxn = dy * gain
            dx = inv * (dxn - xn * jnp.mean(dxn * xn, axis=-1, keepdims=True))
            return dx, _colsum(dy * xn)

        dqg = jnp.zeros((1, HEAD_DIM), F32)
        for h in range(ad // HEAD_DIM):
            sl = slice(h * HEAD_DIM, (h + 1) * HEAD_DIM)
            dx, dg = head(q_ref[:, sl].astype(F32), qg_ref[...], dq_ref[:, sl])
            dqo_ref[:, sl] = dx.astype(dqo_ref.dtype)
            dqg = dqg + dg
        dkg = jnp.zeros((1, HEAD_DIM), F32)
        for h in range(kvd // HEAD_DIM):
            sl = slice(h * HEAD_DIM, (h + 1) * HEAD_DIM)
            dx, dg = head(k_ref[:, sl].astype(F32), kg_ref[...], dk_ref[:, sl])
            dko_ref[:, sl] = dx.astype(dko_ref.dtype)
            dkg = dkg + dg
        acc_ref[0:1, :] += dqg
        acc_ref[1:2, :] += dkg

    return pl.pallas_call(
        body, name="qk_bwd", grid=(t // tm,),
        in_specs=[_row(tm, ad), _row(tm, kvd), _colblk(tm, ad, 0), _colblk(tm, kvd, 2 * ad // kvd),
                  _vec(HEAD_DIM), _vec(HEAD_DIM), _row(tm, HEAD_DIM), _row(tm, HEAD_DIM)],
        out_specs=[_row(tm, ad), _row(tm, kvd), pl.BlockSpec((SUBLANES, HEAD_DIM), lambda j: (0, 0))],
        out_shape=[_sds((t, ad), BF16), _sds((t, kvd), BF16), _sds((SUBLANES, HEAD_DIM), F32)],
        compiler_params=_cp("arbitrary"))(dqr, dkr, qgkv, qgkv, qg, kg, cos_t, sin_t)


_NT = (((1,), (1,)), ((), ()))


def _chunks_t(a, nkv, tm):
    t = a.shape[0]
    return a.reshape(t // tm, tm, nkv, HEAD_DIM).transpose(2, 0, 3, 1)


def _tree_rows(x, op):
    slabs = [x[i:i + SUBLANES] for i in range(0, x.shape[0], SUBLANES)]
    while len(slabs) > 1:
        slabs = [op(slabs[i], slabs[i + 1]) for i in range(0, len(slabs), 2)]
    return slabs[0]


def flash_fwd(qr, kr, vt, qgkv, ad, kvd, tm):
    t = qr.shape[0]
    nkv = kvd // HEAD_DIM
    gw = ad // nkv
    hpg = gw // HEAD_DIM
    nt = t // tm
    gate_blk0 = ad // gw

    def body(q_ref, k_ref, vt_ref, g_ref, o_ref, og_ref, lse_ref):
        qi = pl.program_id(1)
        nkc = jnp.where(qi == 0, 1, nt)
        heads = [slice(h * HEAD_DIM, (h + 1) * HEAD_DIM) for h in range(hpg)]
        qs = [q_ref[:, sl] for sl in heads]

        def step(c, carry):
            kc = k_ref[pl.ds(pl.multiple_of(c * tm, tm), tm), :]
            vtc = vt_ref[c]
            sts = [lax.dot_general(kc, q, _NT, preferred_element_type=F32) for q in qs]
            out = []
            for h in range(hpg):
                m, l, acc = carry[h]
                m_new = jnp.maximum(m, jnp.max(_tree_rows(sts[h], jnp.maximum), axis=0, keepdims=True))
                a = jnp.exp2(m - m_new)
                pt = jnp.exp2(sts[h] - m_new)
                l = a * l + jnp.sum(_tree_rows(pt, jnp.add), axis=0, keepdims=True)
                acc = a * acc + jnp.dot(vtc, pt.astype(BF16), preferred_element_type=F32)
                out.append((m_new, l, acc))
            return tuple(out)

        init = tuple((jnp.full((1, tm), -1e30, F32), jnp.zeros((1, tm), F32), jnp.zeros((HEAD_DIM, tm), F32))
                     for _ in range(hpg))
        res = lax.fori_loop(0, nkc, step, init)
        for h, sl in enumerate(heads):
            m, l, acc = res[h]
            o = (acc / l).T
            o_ref[:, sl] = o.astype(o_ref.dtype)
            og_ref[:, sl] = (o * _silu(g_ref[:, sl].astype(F32))).astype(og_ref.dtype)
            lse_ref[h:h + 1, :] = m + jnp.log(l) * LOG2_E

    return pl.pallas_call(
        body, name="flash_fwd", grid=(nkv, nt),
        in_specs=[pl.BlockSpec((tm, gw), lambda g, i: (i, g)),
                  pl.BlockSpec((t, HEAD_DIM), lambda g, i: (0, g)),
                  pl.BlockSpec((None, nt, HEAD_DIM, tm), lambda g, i: (g, 0, 0, 0)),
                  pl.BlockSpec((tm, gw), lambda g, i: (i, gate_blk0 + g))],
        out_specs=[pl.BlockSpec((tm, gw), lambda g, i: (i, g)),
                   pl.BlockSpec((tm, gw), lambda g, i: (i, g)),
                   pl.BlockSpec((None, None, hpg, tm), lambda g, i: (g, i, 0, 0))],
        out_shape=[_sds((t, ad), BF16), _sds((t, ad), BF16), _sds((nkv, nt, hpg, tm), F32)],
        compiler_params=_cp("parallel", "parallel"))(qr, kr, vt, qgkv)


def gate_bwd(dog, o, qgkv, ad, kvd, tm):
    t = o.shape[0]
    nkv = kvd // HEAD_DIM
    hpg = ad // nkv // HEAD_DIM

    def body(dog_ref, o_ref, g_ref, do_ref, dg_ref, dl_ref):
        lane = lax.broadcasted_iota(jnp.int32, (tm, LANES), 1)
        for grp in range(nkv):
            blk = jnp.zeros((tm, LANES), F32)
            for hh in range(hpg):
                h = grp * hpg + hh
                sl = slice(h * HEAD_DIM, (h + 1) * HEAD_DIM)
                dv = dog_ref[:, sl].astype(F32)
                ov = o_ref[:, sl].astype(F32)
                gv = g_ref[:, sl].astype(F32)
                doh = dv * _silu(gv)
                do_ref[:, sl] = doh.astype(do_ref.dtype)
                dg_ref[:, sl] = (dv * ov * _dsilu(gv)).astype(dg_ref.dtype)
                blk = jnp.where(lane == hh, jnp.sum(doh * ov, axis=-1, keepdims=True), blk)
            dl_ref[grp] = blk

    return pl.pallas_call(
        body, name="gate_bwd", grid=(t // tm,),
        in_specs=[_row(tm, ad), _row(tm, ad), _colblk(tm, ad, 1)],
        out_specs=[_row(tm, ad), _row(tm, ad), pl.BlockSpec((nkv, tm, LANES), lambda j: (0, j, 0))],
        out_shape=[_sds((t, ad), BF16), _sds((t, ad), BF16), _sds((nkv, t, LANES), F32)],
        compiler_params=_cp("parallel"))(dog, o, qgkv)


def flash_bwd(qr, do, kr, kt, qgkv, lse_t, delta_t, ad, kvd, tm, scale):
    t = qr.shape[0]
    nkv = kvd // HEAD_DIM
    gw = ad // nkv
    hpg = gw // HEAD_DIM
    nt = t // tm
    v_blk0 = (2 * ad + kvd) // HEAD_DIM

    def body(q_ref, do_ref, k_ref, v_ref, kt_ref, lse_ref, dl_ref, dq_ref, dk_ref, dv_ref):
        qi = pl.program_id(1)

        @pl.when(qi == 0)
        def _():
            dk_ref[...] = jnp.zeros_like(dk_ref)
            dv_ref[...] = jnp.zeros_like(dv_ref)

        nkc = jnp.where(qi == 0, 1, nt)
        heads = [slice(h * HEAD_DIM, (h + 1) * HEAD_DIM) for h in range(hpg)]
        qs = [q_ref[:, sl] for sl in heads]
        dos = [do_ref[:, sl] for sl in heads]
        lses = [lse_ref[h:h + 1, :] for h in range(hpg)]
        dls = [dl_ref[h:h + 1, :] for h in range(hpg)]

        def step(c, dqts):
            rows = pl.ds(pl.multiple_of(c * tm, tm), tm)
            kc = k_ref[rows, :]
            vc = v_ref[rows, :]
            ktc = kt_ref[c]
            dk = jnp.zeros((tm, HEAD_DIM), F32)
            dv = jnp.zeros((tm, HEAD_DIM), F32)
            out = []
            sts = [lax.dot_general(kc, q, _NT, preferred_element_type=F32) for q in qs]
            dpts = [lax.dot_general(vc, d, _NT, preferred_element_type=F32) for d in dos]
            for h in range(hpg):
                pt = jnp.exp2(sts[h] - lses[h])
                dv = dv + jnp.dot(pt.astype(BF16), dos[h], preferred_element_type=F32)
                dst = (pt * (dpts[h] - dls[h])).astype(BF16)
                dk = dk + jnp.dot(dst, qs[h], preferred_element_type=F32)
                out.append(dqts[h] + jnp.dot(ktc, dst, preferred_element_type=F32))
            dk_ref[rows, :] += dk
            dv_ref[rows, :] += dv
            return tuple(out)

        res = lax.fori_loop(0, nkc, step, tuple(jnp.zeros((HEAD_DIM, tm), F32) for _ in range(hpg)))
        for h, sl in enumerate(heads):
            dq_ref[:, sl] = res[h].T * scale

        @pl.when(qi == nt - 1)
        def _():
            dk_ref[...] = dk_ref[...] * LN_2

    return pl.pallas_call(
        body, name="flash_bwd", grid=(nkv, nt),
        in_specs=[pl.BlockSpec((tm, gw), lambda g, i: (i, g)),
                  pl.BlockSpec((tm, gw), lambda g, i: (i, g)),
                  pl.BlockSpec((t, HEAD_DIM), lambda g, i: (0, g)),
                  pl.BlockSpec((t, HEAD_DIM), lambda g, i: (0, v_blk0 + g)),
                  pl.BlockSpec((None, nt, HEAD_DIM, tm), lambda g, i: (g, 0, 0, 0)),
                  pl.BlockSpec((None, None, hpg, tm), lambda g, i: (g, i, 0, 0)),
                  pl.BlockSpec((None, None, hpg, tm), lambda g, i: (g, i, 0, 0))],
        out_specs=[pl.BlockSpec((tm, gw), lambda g, i: (i, g)),
                   pl.BlockSpec((t, HEAD_DIM), lambda g, i: (0, g)),
                   pl.BlockSpec((t, HEAD_DIM), lambda g, i: (0, g))],
        out_shape=[_sds((t, ad), F32), _sds((t, kvd), F32), _sds((t, kvd), F32)],
        compiler_params=_cp("parallel", "arbitrary"))(qr, do, kr, qgkv, kt, lse_t, delta_t)


def _rows_per_head(a, tm, hpg):
    nkv, t, _ = a.shape
    return a[:, :, :hpg].reshape(nkv, t // tm, tm, hpg).transpose(0, 1, 3, 2)


def adaln_fwd(c16, w_mod):
    nlay, d, nl = w_mod.shape
    tn = _pick(nl, (512, 256, 128))

    def body(c_ref, w_ref, o_ref):
        o_ref[...] = jnp.dot(_silu(c_ref[...]), w_ref[...], preferred_element_type=F32,
                             precision=lax.Precision.HIGHEST)

    return pl.pallas_call(
        body, name="adaln_fwd", grid=(nlay, nl // tn),
        in_specs=[pl.BlockSpec((16, d), lambda l, j: (0, 0)),
                  pl.BlockSpec((None, d, tn), lambda l, j: (l, 0, j))],
        out_specs=pl.BlockSpec((None, 16, tn), lambda l, j: (l, 0, j)),
        out_shape=_sds((nlay, 16, nl), F32), compiler_params=_cp("parallel", "parallel"))(c16, w_mod)


def adaln_bwd(c16t, dm, w_mod):
    nlay, d, nl = w_mod.shape
    tn = _pick(nl, (512, 256, 128))

    def body(c_ref, dm_ref, w_ref, dw_ref, dc_ref):
        @pl.when(pl.program_id(1) == 0)
        def _():
            dc_ref[...] = jnp.zeros_like(dc_ref)

        dmv = dm_ref[...]
        dw_ref[...] = jnp.dot(_silu(c_ref[...]), dmv, preferred_element_type=F32,
                              precision=lax.Precision.HIGHEST)
        dc_ref[...] += lax.dot_general(dmv, w_ref[...], _NT, preferred_element_type=F32,
                                       precision=lax.Precision.HIGHEST)

    return pl.pallas_call(
        body, name="adaln_bwd", grid=(nlay, nl // tn),
        in_specs=[pl.BlockSpec((d, 16), lambda l, j: (0, 0)),
                  pl.BlockSpec((None, 16, tn), lambda l, j: (l, 0, j)),
                  pl.BlockSpec((None, d, tn), lambda l, j: (l, 0, j))],
        out_specs=[pl.BlockSpec((None, d, tn), lambda l, j: (l, 0, j)),
                   pl.BlockSpec((None, 16, d), lambda l, j: (l, 0, 0))],
        out_shape=[_sds((nlay, d, nl), F32), _sds((nlay, 16, d), F32)],
        compiler_params=_cp("parallel", "arbitrary"))(c16t, dm, w_mod)


def sum_leading(a, name):
    n = a.shape[0]

    def body(a_ref, o_ref):
        acc = a_ref[0]
        for i in range(1, n):
            acc = acc + a_ref[i]
        o_ref[...] = acc

    return pl.pallas_call(body, name=name, in_specs=[VMEM_FULL], out_specs=VMEM_FULL,
                          out_shape=_sds(a.shape[1:], F32),
                          compiler_params=pltpu.CompilerParams(vmem_limit_bytes=VMEM_LIMIT))(a)


def c_ctx_grad(parts, c_ctx):
    n = parts.shape[0]

    def body(p_ref, c_ref, o_ref):
        acc = p_ref[0]
        for i in range(1, n):
            acc = acc + p_ref[i]
        o_ref[...] = acc * _dsilu(c_ref[...])

    return pl.pallas_call(body, name="c_ctx_grad", in_specs=[VMEM_FULL, VMEM_FULL], out_specs=VMEM_FULL,
                          out_shape=_sds(c_ctx.shape, F32))(parts, c_ctx)


def _as2d(a):
    return a.reshape(-1, a.shape[-1])


def _row_tile(r, c):
    for tr in (1024, 512, 256, 128, 64, 32, 16, 8):
        if r % tr == 0 and tr * c * 4 <= (1 << 20):
            return tr
    return r


def add_n(arrs, name, out_dtypes=(F32,)):
    shape = arrs[0].shape
    flat = [_as2d(a) for a in arrs]
    r, c = flat[0].shape
    tr = _row_tile(r, c)
    n_in = len(flat)

    def body(*refs):
        acc = refs[0][...].astype(F32)
        for ref in refs[1:n_in]:
            acc = acc + ref[...].astype(F32)
        for ref in refs[n_in:]:
            ref[...] = acc.astype(ref.dtype)

    spec = pl.BlockSpec((tr, c), lambda i: (i, 0))
    outs = pl.pallas_call(
        body, name=name, grid=(r // tr,), in_specs=[spec] * n_in, out_specs=[spec] * len(out_dtypes),
        out_shape=[_sds((r, c), dt) for dt in out_dtypes], compiler_params=_cp("parallel"))(*flat)
    return [o.reshape(shape) for o in outs]


def adamw(w, g, m, v):
    shape = w.shape
    flat = [_as2d(a.reshape((1,) + shape) if len(shape) == 1 else a) for a in (w, g, m, v)]
    r, c = flat[0].shape
    tr = _row_tile(r, c)
    c1 = 1.0 - ADAM_B1 ** ADAM_STEP
    c2 = 1.0 - ADAM_B2 ** ADAM_STEP

    def body(w_ref, g_ref, m_ref, v_ref, d_ref, nm_ref, nv_ref):
        gv = g_ref[...]
        nm = ADAM_B1 * m_ref[...] + (1.0 - ADAM_B1) * gv
        nv = ADAM_B2 * v_ref[...] + (1.0 - ADAM_B2) * (gv * gv)
        d_ref[...] = -ADAM_LR * ((nm / c1) / (jnp.sqrt(nv / c2) + ADAM_EPS) + ADAM_WD * w_ref[...])
        nm_ref[...] = nm
        nv_ref[...] = nv

    spec = pl.BlockSpec((tr, c), lambda i: (i, 0))
    outs = pl.pallas_call(
        body, name="adamw", grid=(r // tr,), in_specs=[spec] * 4, out_specs=[spec] * 3,
        out_shape=[_sds((r, c), F32)] * 3, compiler_params=_cp("parallel"))(*flat)
    return tuple(o.reshape(shape) for o in outs)


def _place():
    return lax.axis_index("x"), lax.axis_index("y"), lax.axis_index("c")


def _remote(src, dst, ssem, rsem, dev):
    return pltpu.make_async_remote_copy(src_ref=src, dst_ref=dst, send_sem=ssem, recv_sem=rsem,
                                        device_id=dev, device_id_type=MESH)


def all_gather8(v, name):
    m_per, n = v.shape

    def body(x_ref, out_ref, send_sems, recv_sems, local_sem):
        x, y, c = _place()
        me, sibling = (x, y, c), (x, y, 1 - c)
        chips = [(1 - x, y), (x, 1 - y), (1 - x, 1 - y)]

        def rows(px, py, pc):
            return out_ref.at[pl.ds((4 * px + 2 * py + pc) * m_per, m_per), :]

        def copy(k, block, to, src=None):
            return _remote(rows(*block) if src is None else src, rows(*block),
                           send_sems.at[k], recv_sems.at[k], to)

        mine = pltpu.make_async_copy(x_ref, rows(*me), local_sem)
        mine.start()
        first = [copy(0, me, sibling, src=x_ref)]
        first += [copy(1 + j, me, (*chip, c), src=x_ref) for j, chip in enumerate(chips)]
        for cp in first:
            cp.start()
        passed = [copy(4 + j, (*chip, c), sibling) for j, chip in enumerate(chips)]
        for j, chip in enumerate(chips):
            copy(1 + j, (*chip, c), me).wait_recv()
            passed[j].start()
        copy(0, sibling, me).wait_recv()
        for j, chip in enumerate(chips):
            copy(4 + j, (*chip, 1 - c), me).wait_recv()
        for cp in first + passed:
            cp.wait_send()
        mine.wait()

    return pl.pallas_call(
        body, name=name, out_shape=_sds((8 * m_per, n), v.dtype),
        in_specs=[VMEM_FULL], out_specs=VMEM_FULL,
        scratch_shapes=[pltpu.SemaphoreType.DMA((7,)), pltpu.SemaphoreType.DMA((7,)), pltpu.SemaphoreType.DMA],
        compiler_params=pltpu.CompilerParams(vmem_limit_bytes=VMEM_LIMIT))(v)


def _slabs(ref, n):
    rows = ref.shape[0] // n
    return [ref.at[pl.ds(i * rows, rows)] for i in range(n)]


def gather_weight(w, name):
    _, r, cdim = w.shape
    n = D2D_PARTS

    def body(w_ref, out_ref, send_sems, recv_sems, local_sems):
        x, y, c = _place()
        sibling = (x, y, 1 - c)
        chips = [(1 - x, y), (x, 1 - y), (1 - x, 1 - y)]
        mine = 2 * x + y
        local = [pltpu.make_async_copy(w_ref.at[i], out_ref.at[i, mine], local_sems.at[i]) for i in range(2)]
        for cp in local:
            cp.start()
        src = _slabs(w_ref.at[c], n)
        first = []
        for k, chip in enumerate(chips):
            dst = _slabs(out_ref.at[c, mine], n)
            for j in range(n):
                cp = _remote(src[j], dst[j], send_sems.at[k * n + j], recv_sems.at[k * n + j], (*chip, c))
                cp.start()
                first.append(cp)
        passed = []
        for j in range(n):
            for k, (px, py) in enumerate(chips):
                theirs = _slabs(out_ref.at[c, 2 * px + py], n)[j]
                _remote(src[j], theirs, send_sems.at[k * n + j], recv_sems.at[k * n + j], sibling).wait_recv()
                cp = _remote(theirs, theirs, send_sems.at[(3 + k) * n + j], recv_sems.at[(3 + k) * n + j], sibling)
                cp.start()
                passed.append(cp)
        for j in range(n):
            for k, (px, py) in enumerate(chips):
                other = _slabs(out_ref.at[1 - c, 2 * px + py], n)[j]
                _remote(other, other, send_sems.at[(3 + k) * n + j], recv_sems.at[(3 + k) * n + j],
                        sibling).wait_recv()
        for cp in first + passed:
            cp.wait_send()
        for cp in local:
            cp.wait()

    return pl.pallas_call(
        body, name=name, out_shape=_sds((2, 4, r, cdim), w.dtype), in_specs=[ANY], out_specs=ANY,
        scratch_shapes=[pltpu.SemaphoreType.DMA((6 * n,)), pltpu.SemaphoreType.DMA((6 * n,)),
                        pltpu.SemaphoreType.DMA((2,))])(w)


def rs_chip_exchange(gb0, gb1, g0, g1, name):
    _, r, cdim = g0.shape

    def body(gb0_ref, gb1_ref, g0_ref, g1_ref, mine_ref, st_ref, send_sems, recv_sems, local_sems):
        x, y, c = _place()
        chips = [(1 - x, y), (x, 1 - y), (1 - x, 1 - y)]
        keep = [pltpu.make_async_copy(g_ref.at[2 * x + y], mine_ref.at[i], local_sems.at[i])
                for i, g_ref in enumerate((g0_ref, g1_ref))]
        for cp in keep:
            cp.start()
        cps = []
        for k, (px, py) in enumerate(chips):
            for i, gb_ref in enumerate((gb0_ref, gb1_ref)):
                cps.append(_remote(gb_ref.at[2 * px + py], st_ref.at[k, i], send_sems.at[2 * k + i],
                                   recv_sems.at[2 * k + i], (px, py, c)))
        for cp in cps:
            cp.start()
        for cp in cps:
            cp.wait()
        for cp in keep:
            cp.wait()

    return pl.pallas_call(
        body, name=name, out_shape=[_sds((2, r, cdim), F32), _sds((3, 2, r, cdim), BF16)],
        in_specs=[ANY] * 4, out_specs=[ANY, ANY],
        scratch_shapes=[pltpu.SemaphoreType.DMA((6,)), pltpu.SemaphoreType.DMA((6,)),
                        pltpu.SemaphoreType.DMA((2,))])(gb0, gb1, g0, g1)


def rs_pair_swap(part, part_b, name):
    _, r, cdim = part.shape
    n = 2 * D2D_PARTS

    def body(p_ref, pb_ref, own_ref, got_ref, send_sems, recv_sems, local_sem):
        x, y, c = _place()
        sibling = (x, y, 1 - c)
        keep = pltpu.make_async_copy(p_ref.at[c], own_ref, local_sem)
        keep.start()
        src, dst = _slabs(pb_ref.at[1 - c], n), _slabs(got_ref, n)
        cps = [_remote(src[j], dst[j], send_sems.at[j], recv_sems.at[j], sibling) for j in range(n)]
        for cp in cps:
            cp.start()
        for cp in cps:
            cp.wait()
        keep.wait()

    return pl.pallas_call(
        body, name=name, out_shape=[_sds((r, cdim), F32), _sds((r, cdim), BF16)],
        in_specs=[ANY, ANY], out_specs=[ANY, ANY],
        scratch_shapes=[pltpu.SemaphoreType.DMA((n,)), pltpu.SemaphoreType.DMA((n,)),
                        pltpu.SemaphoreType.DMA])(part, part_b)


def rs_pair_share(red, name):
    n = 2 * D2D_PARTS

    def body(red_ref, out_ref, send_sems, recv_sems, local_sem):
        x, y, c = _place()
        sibling = (x, y, 1 - c)
        keep = pltpu.make_async_copy(red_ref, out_ref.at[c], local_sem)
        keep.start()
        src, dst, got = _slabs(red_ref, n), _slabs(out_ref.at[c], n), _slabs(out_ref.at[1 - c], n)
        cps = [_remote(src[j], dst[j], send_sems.at[j], recv_sems.at[j], sibling) for j in range(n)]
        for cp in cps:
            cp.start()
        for cp in cps:
            cp.wait_send()
        for j in range(n):
            _remote(src[j], got[j], send_sems.at[j], recv_sems.at[j], sibling).wait_recv()
        keep.wait()

    return pl.pallas_call(
        body, name=name, out_shape=_sds((2,) + red.shape, red.dtype), in_specs=[ANY], out_specs=ANY,
        scratch_shapes=[pltpu.SemaphoreType.DMA((n,)), pltpu.SemaphoreType.DMA((n,)), pltpu.SemaphoreType.DMA])(red)


def reduce_scatter_grad(g0, g1, tag):
    mine, theirs = rs_chip_exchange(g0.astype(BF16), g1.astype(BF16), g0, g1, "rs_exchange_" + tag)
    part, part_b = add_n([mine, theirs[0], theirs[1], theirs[2]], "rs_chip_add_" + tag, (F32, BF16))
    own, got = rs_pair_swap(part, part_b, "rs_swap_" + tag)
    red, = add_n([own, got], "rs_pair_add_" + tag, (BF16,))
    return rs_pair_share(red, "rs_share_" + tag).astype(F32)


def _rope_tables(lc, s):
    rows_n = s // GRID_W
    row = jnp.repeat(jnp.arange(rows_n, dtype=F32), GRID_W)
    col = jnp.tile(jnp.arange(GRID_W, dtype=F32), rows_n)
    axis_dim = HEAD_DIM // 2
    inv_freq = ROPE_THETA ** (-jnp.arange(0, axis_dim, 2, dtype=F32) / axis_dim)
    ang_r = row[:, None] * inv_freq[None, :]
    ang_c = col[:, None] * inv_freq[None, :]
    cr, sr, cc, sc = jnp.cos(ang_r), jnp.sin(ang_r), jnp.cos(ang_c), jnp.sin(ang_c)
    cos_l = jnp.concatenate([cr, cr, cc, cc], axis=1)
    sin_l = jnp.concatenate([-sr, sr, -sc, sc], axis=1)
    cos_t = jnp.concatenate([jnp.ones((lc, HEAD_DIM), F32), cos_l], axis=0)
    sin_t = jnp.concatenate([jnp.zeros((lc, HEAD_DIM), F32), sin_l], axis=0)
    return cos_t, sin_t


def _pad_rows(a, rows):
    return jnp.concatenate([a, jnp.zeros((rows - a.shape[0],) + a.shape[1:], a.dtype)], axis=0)


def _pad_cols(a, cols):
    return jnp.concatenate([a, jnp.zeros(a.shape[:-1] + (cols - a.shape[-1],), a.dtype)], axis=-1)


def kernel(x, c, ctx, c_ctx, w_mod, b_mod, post_ln_g, post_ln_b, w_in_e, conv_a_w, conv_a_b, norm_a_g, norm_a_b, conv_b_w, w_out_e, w_in_o, q_norm_g, k_norm_g, w_out_o, loss_target, m_c_ctx, m_w_mod, m_b_mod, m_post_ln_g, m_post_ln_b, m_w_in_e, m_conv_a_w, m_conv_a_b, m_norm_a_g, m_norm_a_b, m_conv_b_w, m_w_out_e, m_w_in_o, m_q_norm_g, m_k_norm_g, m_w_out_o, v_c_ctx, v_w_mod, v_b_mod, v_post_ln_g, v_post_ln_b, v_w_in_e, v_conv_a_w, v_conv_a_b, v_norm_a_g, v_norm_a_b, v_conv_b_w, v_w_out_e, v_w_in_o, v_q_norm_g, v_k_norm_g, v_w_out_o):
    s, d = x.shape[1], x.shape[2]
    lc = ctx.shape[1]
    t = lc + s
    tm = lc
    depth = w_mod.shape[0]
    n_even, n_odd = w_in_e.shape[0], w_in_o.shape[0]
    ad = w_out_o.shape[1] * 4
    kvd = (w_in_o.shape[2] * 4 - 2 * ad) // 2
    nkv = kvd // HEAD_DIM
    hpg = ad // nkv // HEAD_DIM
    nlm = w_mod.shape[2]
    alpha = (2.0 * depth) ** 0.25
    scale = HEAD_DIM ** -0.5
    assert n_even == 2 and n_odd == 2 and depth == 4 and hpg == GQA_GROUP
    assert lc % CONV_ROWS == 0 and s % tm == 0 and d % LANES == 0

    xi, yi, ci = _place()
    shard = 2 * xi + yi
    dev = 4 * xi + 2 * yi + ci

    wg_in_e = gather_weight(w_in_e.astype(BF16), "gather_w_in_e")
    wg_out_e = gather_weight(w_out_e.astype(BF16), "gather_w_out_e")
    wg_in_o = gather_weight(w_in_o.astype(BF16), "gather_w_in_o")
    wg_out_o = gather_weight(w_out_o.astype(BF16), "gather_w_out_o")

    c_all = all_gather8(_pad_rows(c, 8), "gather_c")
    c16 = _pad_rows(jnp.concatenate([c_all[0::8], c_ctx[None, :]], axis=0), 16)
    m_part = adaln_fwd(c16, w_mod)
    m_all = all_gather8(m_part.reshape(depth * 16, nlm), "gather_mod")
    m_all = m_all.reshape(8, depth, 16, nlm)[0::2]
    m_full = m_all.transpose(1, 2, 0, 3).reshape(depth, 16, 4 * nlm) + b_mod[:, None, :]
    m_lat = lax.dynamic_index_in_dim(m_full, dev, axis=1, keepdims=False)
    m_ctx = m_full[:, 8]

    def seg2(l, part):
        return jnp.stack([m_ctx[l, part * d:(part + 1) * d], m_lat[l, part * d:(part + 1) * d]])[:, None, :]

    cos_t, sin_t = _rope_tables(lc, s)

    small_gathered = all_gather8(
        _pad_rows(jnp.concatenate([conv_a_w.reshape(n_even * CONV_A_TAPS, -1),
                                   conv_b_w.reshape(n_even * CONV_B_TAPS, -1)], axis=0), 72), "gather_taps")
    taps = small_gathered.reshape(8, 72, -1)[0::2]
    taps = taps.transpose(1, 0, 2).reshape(72, d)
    caw = taps[:n_even * CONV_A_TAPS].reshape(n_even, CONV_A_TAPS, d)
    cbw = taps[n_even * CONV_A_TAPS:n_even * (CONV_A_TAPS + CONV_B_TAPS)].reshape(n_even, CONV_B_TAPS, d)
    caw_pad = jnp.concatenate([caw, jnp.zeros((n_even, 32 - CONV_A_TAPS, d), F32)], axis=1)
    cbw_pad = jnp.concatenate([cbw, jnp.zeros((n_even, 8 - CONV_B_TAPS, d), F32)], axis=1)

    xc = jnp.concatenate([ctx[0], x[0]], axis=0)
    saved = []
    for l in range(depth):
        i = l // 2
        shift2, scale2, gate2 = seg2(l, 0), seg2(l, 1), seg2(l, 2)
        h = mod_fwd(xc, scale2, shift2, tm)
        if l % 2 == 0:
            p = mm_nn(h, wg_in_e[i], BF16)
            u1 = conv_a_fwd(p, caw_pad[i], conv_a_b[i][None, :], lc, d)
            a_out = ln_a_fwd(u1, p, norm_a_g[i][None, :], norm_a_b[i][None, :], tm)
            b_out = conv_b_fwd(p, cbw_pad[i], lc, d)
            ab = jnp.concatenate([a_out, b_out], axis=1)
            y = mm_nn(ab, wg_out_e[i].reshape(1, 2 * d, d), F32)
            saved.append(dict(xc=xc, h=h, p=p, u1=u1, ab=ab, y=y))
        else:
            qgkv = mm_nn(h, wg_in_o[i], BF16)
            qr, kr = qk_fwd(qgkv, q_norm_g[i][None, :], k_norm_g[i][None, :], cos_t, sin_t, ad, kvd, tm,
                            scale * LOG2_E)
            vt = _chunks_t(qgkv[:, 2 * ad + kvd:], nkv, tm)
            o, og, lse = flash_fwd(qr, kr, vt, qgkv, ad, kvd, tm)
            y = mm_nn(og, wg_out_o[i].reshape(1, ad, d), F32)
            saved.append(dict(xc=xc, h=h, qgkv=qgkv, qr=qr, kr=kr, o=o, og=og, lse=lse, y=y))
        xc = post_ln_fwd(xc, y, gate2, post_ln_g[l][None, :], post_ln_b[l][None, :], alpha, tm)

    dxc, loss_acc = loss_head(xc, loss_target[0], lc, tm)
    loss = lax.psum(0.5 / d * jnp.sum(loss_acc[0]), MESH_AXES)

    g_in_e, g_out_e, g_in_o, g_out_o = [None] * 2, [None] * 2, [None] * 2, [None] * 2
    d_mod_lat, d_mod_ctx = [None] * depth, [None] * depth
    d_pln_g, d_pln_b = [None] * depth, [None] * depth
    d_cab, d_nag, d_nab, d_caw, d_cbw = [None] * 2, [None] * 2, [None] * 2, [None] * 2, [None] * 2
    d_qg, d_kg = [None] * 2, [None] * 2
    for l in reversed(range(depth)):
        i = l // 2
        sv = saved[l]
        scale2, gate2 = seg2(l, 1), seg2(l, 2)
        dzx, dy, acc_ln = post_ln_bwd(dxc, sv["xc"], sv["y"], gate2, post_ln_g[l][None, :], alpha, tm)
        d_pln_g[l] = acc_ln[0, 1] + acc_ln[1, 1]
        d_pln_b[l] = acc_ln[0, 2] + acc_ln[1, 2]
        if l % 2 == 0:
            w_out3 = wg_out_e[i].reshape(1, 2 * d, d)
            dab = mm_nt(dy, w_out3, BF16)
            g_out_e[i] = mm_tn(sv["ab"], dy, 1).reshape(4, 2 * d // 4, d)
            du1, d_agate, acc_a = ln_a_bwd(dab, sv["u1"], sv["p"], norm_a_g[i][None, :], norm_a_b[i][None, :], tm)
            d_nag[i], d_nab[i], d_cab[i] = acc_a[0], acc_a[1], acc_a[2]
            d_aval, d_aglu, d_caw[i] = conv_a_bwd(du1, sv["p"], caw_pad[i], lc, d)
            d_bx, d_bb, d_bc, d_bg, d_cbw[i] = conv_b_bwd(dab, sv["p"], cbw_pad[i], lc, d)
            dp = jnp.concatenate([d_aval, d_aglu, d_agate, d_bx, d_bb, d_bc, d_bg], axis=1)
            dh = mm_nt(dp, wg_in_e[i], F32)
            g_in_e[i] = mm_tn(sv["h"], dp, 4)
        else:
            w_out3 = wg_out_o[i].reshape(1, ad, d)
            dog = mm_nt(dy, w_out3, BF16)
            g_out_o[i] = mm_tn(sv["og"], dy, 1).reshape(4, ad // 4, d)
            do, dgate, delta = gate_bwd(dog, sv["o"], sv["qgkv"], ad, kvd, tm)
            dqr, dkr, dv = flash_bwd(sv["qr"], do, sv["kr"], _chunks_t(sv["kr"], nkv, tm), sv["qgkv"], sv["lse"],
                                     _rows_per_head(delta, tm, hpg), ad, kvd, tm, scale)
            dq, dk, acc_qk = qk_bwd(dqr, dkr, sv["qgkv"], q_norm_g[i][None, :], k_norm_g[i][None, :],
                                    cos_t, sin_t, ad, kvd, tm)
            d_qg[i], d_kg[i] = acc_qk[0], acc_qk[1]
            dqgkv = jnp.concatenate([dq, dgate, dk, dv.astype(BF16)], axis=1)
            dh = mm_nt(dqgkv, wg_in_o[i], F32)
            g_in_o[i] = mm_tn(sv["h"], dqgkv, 4)
        dxc, acc_mod = mod_bwd(dh, dzx, sv["xc"], scale2, tm)
        d_mod_ctx[l] = jnp.stack([acc_mod[0, 0], acc_mod[0, 1], acc_ln[0, 0]])
        d_mod_lat[l] = jnp.stack([acc_mod[1, 0], acc_mod[1, 1], acc_ln[1, 0]])
    grad_x = dxc[lc:][None]

    pack = jnp.concatenate(
        [jnp.concatenate(d_mod_ctx, axis=0),
         jnp.stack(d_pln_g), jnp.stack(d_pln_b),
         jnp.stack(d_cab), jnp.stack(d_nag), jnp.stack(d_nab),
         jnp.concatenate(d_caw, axis=0),
         jnp.concatenate(d_cbw, axis=0),
         _pad_cols(jnp.stack(d_qg), d), _pad_cols(jnp.stack(d_kg), d),
         jnp.zeros((2, d), F32),
         jnp.concatenate(d_mod_lat, axis=0),
         jnp.zeros((4, d), F32)], axis=0)
    gathered = all_gather8(pack, "gather_small").reshape(8, 128, d)
    small = sum_leading(gathered, "sum_small")
    dm_ctx = small[0:12].reshape(depth, 1, 3 * d)
    dm_lat = gathered[:, 112:124].reshape(8, depth, 3 * d).transpose(1, 0, 2)
    dm = jnp.concatenate([dm_lat, dm_ctx, jnp.zeros((depth, 7, 3 * d), F32)], axis=1)
    g_b_mod = sum_leading(dm.transpose(1, 0, 2), "sum_b_mod")
    dm_shard = lax.dynamic_slice_in_dim(dm, shard * nlm, nlm, axis=2)
    g_w_mod, dc_part = adaln_bwd(c16.T, dm_shard, w_mod)
    dc_all = all_gather8(_pad_rows(dc_part[:, 8, :], 8), "gather_dc").reshape(8, 8, d)
    g_c_ctx = c_ctx_grad(dc_all[0::2, :depth].reshape(4 * depth, 1, d), c_ctx[None, :])[0]

    g_pln_g, g_pln_b = small[12:16], small[16:20]
    g_cab, g_nag, g_nab = small[20:22], small[22:24], small[24:26]
    dch = d // 4
    g_caw = lax.dynamic_slice_in_dim(small[26:90].reshape(2, 32, d)[:, :CONV_A_TAPS], shard * dch, dch, axis=2)
    g_cbw = lax.dynamic_slice_in_dim(small[90:106].reshape(2, 8, d)[:, :CONV_B_TAPS], shard * dch, dch, axis=2)
    g_qg, g_kg = small[106:108, :HEAD_DIM], small[108:110, :HEAD_DIM]

    g_w_in_e = reduce_scatter_grad(g_in_e[0], g_in_e[1], "in_e")
    g_w_out_e = reduce_scatter_grad(g_out_e[0], g_out_e[1], "out_e")
    g_w_in_o = reduce_scatter_grad(g_in_o[0], g_in_o[1], "in_o")
    g_w_out_o = reduce_scatter_grad(g_out_o[0], g_out_o[1], "out_o")

    grads = [g_c_ctx, g_w_mod, g_b_mod, g_pln_g, g_pln_b, g_w_in_e, g_caw, g_cab, g_nag, g_nab, g_cbw,
             g_w_out_e, g_w_in_o, g_qg, g_kg, g_w_out_o]
    weights = [c_ctx, w_mod, b_mod, post_ln_g, post_ln_b, w_in_e, conv_a_w, conv_a_b, norm_a_g, norm_a_b,
               conv_b_w, w_out_e, w_in_o, q_norm_g, k_norm_g, w_out_o]
    ms = [m_c_ctx, m_w_mod, m_b_mod, m_post_ln_g, m_post_ln_b, m_w_in_e, m_conv_a_w, m_conv_a_b, m_norm_a_g,
          m_norm_a_b, m_conv_b_w, m_w_out_e, m_w_in_o, m_q_norm_g, m_k_norm_g, m_w_out_o]
    vs = [v_c_ctx, v_w_mod, v_b_mod, v_post_ln_g, v_post_ln_b, v_w_in_e, v_conv_a_w, v_conv_a_b, v_norm_a_g,
          v_norm_a_b, v_conv_b_w, v_w_out_e, v_w_in_o, v_q_norm_g, v_k_norm_g, v_w_out_o]
    deltas, new_ms, new_vs = [], [], []
    for wv, gv, mv, vv in zip(weights, grads, ms, vs):
        dl, nm, nv = adamw(wv, gv, mv, vv)
        deltas.append(dl)
        new_ms.append(nm)
        new_vs.append(nv)
    return (loss, grad_x, *grads, *deltas, *new_ms, *new_vs)
```

```python
import functools

import jax
import jax.numpy as jnp
from jax import lax
from jax.experimental import pallas as pl
from jax.experimental.pallas import tpu as pltpu

F32 = jnp.float32
BF16 = jnp.bfloat16

LANES = 128
SUBLANES = 8
HEAD_DIM = 128
GQA_GROUP = 4
GRID_W = 64
ROPE_THETA = 10000.0
LN_EPS = 1e-5
RMS_EPS = 1e-6
CONV_A_TAPS = 31
CONV_B_TAPS = 3
HALO = 16
CONV_ROWS = 128
ADAM_LR = 0.001
ADAM_B1 = 0.9
ADAM_B2 = 0.999
ADAM_EPS = 1e-08
ADAM_WD = 0.01
ADAM_STEP = 10
VMEM_LIMIT = 56 * 1024 * 1024
D2D_PARTS = 4
LOG2_E = 1.4426950408889634
LN_2 = 0.6931471805599453
MESH_AXES = ("x", "y", "c")
MESH = pl.DeviceIdType.MESH
ANY = pl.BlockSpec(memory_space=pl.ANY)
VMEM_FULL = pl.BlockSpec(memory_space=pltpu.VMEM)


def _sds(shape, dtype):
    return jax.ShapeDtypeStruct(tuple(shape), dtype)


def _cp(*sem):
    return pltpu.CompilerParams(dimension_semantics=sem, vmem_limit_bytes=VMEM_LIMIT)


def _pick(n, cands):
    for c in cands:
        if n % c == 0:
            return c
    return n


def _sigmoid(x):
    return 1.0 / (1.0 + jnp.exp(-x))


def _silu(x):
    return x * _sigmoid(x)


def _dsilu(x):
    s = _sigmoid(x)
    return s * (1.0 + x * (1.0 - s))


def _row(tm, d):
    return pl.BlockSpec((tm, d), lambda j: (j, 0))


def _seg(d):
    return pl.BlockSpec((None, 1, d), lambda j: (jnp.minimum(j, 1), 0, 0))


def _vec(d):
    return pl.BlockSpec((1, d), lambda j: (0, 0))


def _colblk(tm, width, blk):
    return pl.BlockSpec((tm, width), lambda j: (j, blk))


def _seg_acc(d):
    return pl.BlockSpec((None, SUBLANES, d), lambda j: (jnp.minimum(j, 1), 0, 0))


def _ln_stats(z):
    mu = jnp.mean(z, axis=-1, keepdims=True)
    zc = z - mu
    var = jnp.mean(zc * zc, axis=-1, keepdims=True)
    rstd = lax.rsqrt(var + LN_EPS)
    return zc * rstd, rstd


def _ln_bwd(dxh, xhat, rstd):
    m1 = jnp.mean(dxh, axis=-1, keepdims=True)
    m2 = jnp.mean(dxh * xhat, axis=-1, keepdims=True)
    return rstd * (dxh - m1 - xhat * m2)


def _colsum(v):
    return jnp.sum(v, axis=0, keepdims=True)


def mod_fwd(xc, scale2, shift2, tm):
    t, d = xc.shape

    def body(x_ref, sc_ref, sh_ref, h_ref):
        h_ref[...] = (x_ref[...] * (1.0 + sc_ref[...]) + sh_ref[...]).astype(h_ref.dtype)

    return pl.pallas_call(
        body, name="mod_fwd", grid=(t // tm,),
        in_specs=[_row(tm, d), _seg(d), _seg(d)], out_specs=_row(tm, d),
        out_shape=_sds((t, d), BF16), compiler_params=_cp("parallel"))(xc, scale2, shift2)


def post_ln_fwd(xc, y, gate2, g, b, alpha, tm):
    t, d = xc.shape

    def body(x_ref, y_ref, gt_ref, g_ref, b_ref, o_ref):
        z = alpha * x_ref[...] + gt_ref[...] * y_ref[...]
        xhat, _ = _ln_stats(z)
        o_ref[...] = xhat * g_ref[...] + b_ref[...]

    return pl.pallas_call(
        body, name="post_ln_fwd", grid=(t // tm,),
        in_specs=[_row(tm, d), _row(tm, d), _seg(d), _vec(d), _vec(d)], out_specs=_row(tm, d),
        out_shape=_sds((t, d), F32), compiler_params=_cp("parallel"))(xc, y, gate2, g, b)


def post_ln_bwd(dout, xc, y, gate2, g, alpha, tm):
    t, d = xc.shape

    def body(do_ref, x_ref, y_ref, gt_ref, g_ref, dzx_ref, dy_ref, acc_ref):
        @pl.when(pl.program_id(0) <= 1)
        def _():
            acc_ref[...] = jnp.zeros_like(acc_ref)

        yv = y_ref[...]
        gate = gt_ref[...]
        xhat, rstd = _ln_stats(alpha * x_ref[...] + gate * yv)
        dout = do_ref[...]
        dz = _ln_bwd(dout * g_ref[...], xhat, rstd)
        dzx_ref[...] = alpha * dz
        dy_ref[...] = (gate * dz).astype(dy_ref.dtype)
        acc_ref[0:1, :] += _colsum(dz * yv)
        acc_ref[1:2, :] += _colsum(dout * xhat)
        acc_ref[2:3, :] += _colsum(dout)

    return pl.pallas_call(
        body, name="post_ln_bwd", grid=(t // tm,),
        in_specs=[_row(tm, d), _row(tm, d), _row(tm, d), _seg(d), _vec(d)],
        out_specs=[_row(tm, d), _row(tm, d), _seg_acc(d)],
        out_shape=[_sds((t, d), F32), _sds((t, d), BF16), _sds((2, SUBLANES, d), F32)],
        compiler_params=_cp("arbitrary"))(dout, xc, y, gate2, g)


def mod_bwd(dh, dzx, xc, scale2, tm):
    t, d = xc.shape

    def body(dh_ref, dzx_ref, x_ref, sc_ref, dx_ref, acc_ref):
        @pl.when(pl.program_id(0) <= 1)
        def _():
            acc_ref[...] = jnp.zeros_like(acc_ref)

        dhv = dh_ref[...].astype(F32)
        dx_ref[...] = dzx_ref[...] + dhv * (1.0 + sc_ref[...])
        acc_ref[0:1, :] += _colsum(dhv)
        acc_ref[1:2, :] += _colsum(dhv * x_ref[...])

    return pl.pallas_call(
        body, name="mod_bwd", grid=(t // tm,),
        in_specs=[_row(tm, d), _row(tm, d), _row(tm, d), _seg(d)],
        out_specs=[_row(tm, d), _seg_acc(d)],
        out_shape=[_sds((t, d), F32), _sds((2, SUBLANES, d), F32)],
        compiler_params=_cp("arbitrary"))(dh, dzx, xc, scale2)


def loss_head(xc, target, lc, tm):
    t, d = xc.shape

    def body(x_ref, t_ref, dx_ref, acc_ref):
        j = pl.program_id(0)

        @pl.when(j == 0)
        def _():
            acc_ref[...] = jnp.zeros_like(acc_ref)
            dx_ref[...] = jnp.zeros_like(dx_ref)

        @pl.when(j > 0)
        def _():
            err = x_ref[...] - t_ref[...]
            dx_ref[...] = err * (1.0 / d)
            col = _colsum(err * err)
            tot = col[:, 0:LANES]
            for k in range(1, d // LANES):
                tot = tot + col[:, k * LANES:(k + 1) * LANES]
            acc_ref[0:1, :] += tot

    nlc = lc // tm
    return pl.pallas_call(
        body, name="loss_head", grid=(t // tm,),
        in_specs=[_row(tm, d), pl.BlockSpec((tm, d), lambda j: (jnp.maximum(j - nlc, 0), 0))],
        out_specs=[_row(tm, d), pl.BlockSpec((SUBLANES, LANES), lambda j: (0, 0))],
        out_shape=[_sds((t, d), F32), _sds((SUBLANES, LANES), F32)],
        compiler_params=_cp("arbitrary"))(xc, target)


def mm_nn(a, w3, out_dtype):
    m, k = a.shape
    ns, _, nl = w3.shape
    tm = _pick(m, (768, 512, 256, 128))
    tn = _pick(nl, (512, 256, 128))
    npj = nl // tn

    def body(a_ref, w_ref, o_ref):
        o_ref[...] = jnp.dot(a_ref[...], w_ref[...], preferred_element_type=F32).astype(o_ref.dtype)

    return pl.pallas_call(
        body, name="mm_nn", grid=(m // tm, ns * npj),
        in_specs=[pl.BlockSpec((tm, k), lambda i, j: (i, 0)),
                  pl.BlockSpec((None, k, tn), lambda i, j: (j // npj, 0, j % npj))],
        out_specs=pl.BlockSpec((tm, tn), lambda i, j: (i, j)),
        out_shape=_sds((m, ns * nl), out_dtype), compiler_params=_cp("parallel", "parallel"))(a, w3)


def mm_nt(a, w3, out_dtype):
    m, _ = a.shape
    ns, k, nl = w3.shape
    tm = _pick(m, (768, 512, 256, 128))
    tk = _pick(k, (2048, 1024, 512, 256, 128))
    tn = _pick(nl, (512, 256, 128))
    npj = nl // tn
    nsteps = ns * npj

    def body(a_ref, w_ref, o_ref, acc_ref):
        n = pl.program_id(2)

        @pl.when(n == 0)
        def _():
            acc_ref[...] = jnp.zeros_like(acc_ref)

        acc_ref[...] += lax.dot_general(a_ref[...], w_ref[...], (((1,), (1,)), ((), ())),
                                        preferred_element_type=F32)

        @pl.when(n == nsteps - 1)
        def _():
            o_ref[...] = acc_ref[...].astype(o_ref.dtype)

    return pl.pallas_call(
        body, name="mm_nt", grid=(m // tm, k // tk, nsteps),
        in_specs=[pl.BlockSpec((tm, tn), lambda i, kk, n: (i, n)),
                  pl.BlockSpec((None, tk, tn), lambda i, kk, n: (n // npj, kk, n % npj))],
        out_specs=pl.BlockSpec((tm, tk), lambda i, kk, n: (i, kk)),
        out_shape=_sds((m, k), out_dtype), scratch_shapes=[pltpu.VMEM((tm, tk), F32)],
        compiler_params=_cp("parallel", "parallel", "arbitrary"))(a, w3)


def mm_tn(a, b, ns):
    m, k = a.shape
    nl = b.shape[1] // ns
    tm = _pick(m, (768, 512, 256, 128))
    tk = _pick(k, (1024, 512, 256, 128))
    tn = _pick(nl, (512, 256, 128))
    npj = nl // tn
    nsteps = m // tm

    def body(a_ref, b_ref, o_ref, ob_ref):
        r = pl.program_id(2)

        @pl.when(r == 0)
        def _():
            o_ref[...] = jnp.zeros_like(o_ref)

        o_ref[...] += lax.dot_general(a_ref[...], b_ref[...], (((0,), (0,)), ((), ())),
                                      preferred_element_type=F32)

        @pl.when(r == nsteps - 1)
        def _():
            ob_ref[...] = o_ref[...].astype(ob_ref.dtype)

    out_spec = pl.BlockSpec((None, tk, tn), lambda i, j, r: (j // npj, i, j % npj))
    return pl.pallas_call(
        body, name="mm_tn", grid=(k // tk, ns * npj, nsteps),
        in_specs=[pl.BlockSpec((tm, tk), lambda i, j, r: (r, i)),
                  pl.BlockSpec((tm, tn), lambda i, j, r: (r, j))],
        out_specs=[out_spec, out_spec],
        out_shape=[_sds((ns, k, nl), F32), _sds((ns, k, nl), BF16)],
        compiler_params=_cp("parallel", "parallel", "arbitrary"))(a, b)


def _win_start(j, ncc):
    return pl.multiple_of(j * CONV_ROWS + jnp.where(j >= ncc, HALO, 0), SUBLANES)


def _tok_start(j):
    return pl.multiple_of(j * CONV_ROWS, CONV_ROWS)


def _shifted(xw, off):
    n = xw.shape[0]
    sh = (n - off) % n
    y = pltpu.roll(xw, sh, 0) if sh else xw
    return y[:CONV_ROWS]


def _conv_fwd(xw, w_ref, ntaps):
    pad = ntaps // 2
    acc = None
    for k in range(ntaps):
        term = w_ref[k:k + 1, :] * _shifted(xw, HALO + k - pad)
        acc = term if acc is None else acc + term
    return acc


def _conv_bwd_data(xw, w_ref, ntaps):
    pad = ntaps // 2
    acc = None
    for k in range(ntaps):
        term = w_ref[k:k + 1, :] * _shifted(xw, HALO - k + pad)
        acc = term if acc is None else acc + term
    return acc


def _conv_bwd_weight(dw_ref, d, xw, ntaps):
    pad = ntaps // 2
    for k in range(ntaps):
        dw_ref[k:k + 1, :] += _colsum(d * _shifted(xw, HALO + k - pad))


def _zero_halos(pad_ref, lc, t):
    z = jnp.zeros((HALO, LANES), F32)
    pad_ref[0:HALO, :] = z
    pad_ref[HALO + lc:2 * HALO + lc, :] = z
    pad_ref[2 * HALO + t:3 * HALO + t, :] = z


def _pad_dst(j, ncc):
    return pl.multiple_of(j * CONV_ROWS + HALO + jnp.where(j >= ncc, HALO, 0), SUBLANES)


def _chan(t, blk0):
    return pl.BlockSpec((t, LANES), lambda ct: (0, blk0 + ct))


def _tapw(rows):
    return pl.BlockSpec((rows, LANES), lambda ct: (0, ct))


def conv_a_fwd(p, w_pad, bias, lc, d):
    t = p.shape[0]
    nct, nch, ncc = d // LANES, t // CONV_ROWS, lc // CONV_ROWS

    def body(av_ref, ag_ref, w_ref, b_ref, u1_ref, pad_ref):
        _zero_halos(pad_ref, lc, t)

        def fill(j, carry):
            rows = pl.ds(_tok_start(j), CONV_ROWS)
            u0 = av_ref[rows, :].astype(F32) * _sigmoid(ag_ref[rows, :].astype(F32))
            pad_ref[pl.ds(_pad_dst(j, ncc), CONV_ROWS), :] = u0
            return carry

        lax.fori_loop(0, nch, fill, 0)

        def conv(j, carry):
            xw = pad_ref[pl.ds(_win_start(j, ncc), CONV_ROWS + 2 * HALO), :]
            u1_ref[pl.ds(_tok_start(j), CONV_ROWS), :] = _conv_fwd(xw, w_ref, CONV_A_TAPS) + b_ref[...]
            return carry

        lax.fori_loop(0, nch, conv, 0)

    return pl.pallas_call(
        body, name="conv_a_fwd", grid=(nct,),
        in_specs=[_chan(t, 0), _chan(t, nct), _tapw(32), _tapw(1)],
        out_specs=_chan(t, 0), out_shape=_sds((t, d), F32),
        scratch_shapes=[pltpu.VMEM((t + 3 * HALO, LANES), F32)],
        compiler_params=_cp("parallel"))(p, p, w_pad, bias)


def conv_b_fwd(p, w_pad, lc, d):
    t = p.shape[0]
    nct, nch, ncc = d // LANES, t // CONV_ROWS, lc // CONV_ROWS

    def body(bx_ref, bb_ref, bc_ref, bg_ref, w_ref, o_ref, pad_ref):
        _zero_halos(pad_ref, lc, t)

        def fill(j, carry):
            rows = pl.ds(_tok_start(j), CONV_ROWS)
            pad_ref[pl.ds(_pad_dst(j, ncc), CONV_ROWS), :] = (
                bc_ref[rows, :].astype(F32) * bx_ref[rows, :].astype(F32))
            return carry

        lax.fori_loop(0, nch, fill, 0)

        def conv(j, carry):
            rows = pl.ds(_tok_start(j), CONV_ROWS)
            xw = pad_ref[pl.ds(_win_start(j, ncc), CONV_ROWS + 2 * HALO), :]
            v = _conv_fwd(xw, w_ref, CONV_B_TAPS)
            o_ref[rows, :] = (bb_ref[rows, :].astype(F32) * v
                              * _silu(bg_ref[rows, :].astype(F32))).astype(o_ref.dtype)
            return carry

        lax.fori_loop(0, nch, conv, 0)

    return pl.pallas_call(
        body, name="conv_b_fwd", grid=(nct,),
        in_specs=[_chan(t, 3 * nct), _chan(t, 4 * nct), _chan(t, 5 * nct), _chan(t, 6 * nct), _tapw(8)],
        out_specs=_chan(t, 0), out_shape=_sds((t, d), BF16),
        scratch_shapes=[pltpu.VMEM((t + 3 * HALO, LANES), F32)],
        compiler_params=_cp("parallel"))(p, p, p, p, w_pad)


def ln_a_fwd(u1, p, g, b, tm):
    t, d = u1.shape

    def body(u_ref, ag_ref, g_ref, b_ref, o_ref):
        xhat, _ = _ln_stats(u_ref[...])
        u2 = xhat * g_ref[...] + b_ref[...]
        o_ref[...] = (_silu(u2) * _silu(ag_ref[...].astype(F32))).astype(o_ref.dtype)

    return pl.pallas_call(
        body, name="ln_a_fwd", grid=(t // tm,),
        in_specs=[_row(tm, d), _colblk(tm, d, 2), _vec(d), _vec(d)], out_specs=_row(tm, d),
        out_shape=_sds((t, d), BF16), compiler_params=_cp("parallel"))(u1, p, g, b)


def ln_a_bwd(dab, u1, p, g, b, tm):
    t, d = u1.shape

    def body(da_ref, u_ref, ag_ref, g_ref, b_ref, du_ref, dag_ref, acc_ref):
        @pl.when(pl.program_id(0) == 0)
        def _():
            acc_ref[...] = jnp.zeros_like(acc_ref)

        xhat, rstd = _ln_stats(u_ref[...])
        u2 = xhat * g_ref[...] + b_ref[...]
        ag = ag_ref[...].astype(F32)
        da = da_ref[...].astype(F32)
        dag_ref[...] = (da * _silu(u2) * _dsilu(ag)).astype(dag_ref.dtype)
        du2 = da * _silu(ag) * _dsilu(u2)
        du1 = _ln_bwd(du2 * g_ref[...], xhat, rstd)
        du_ref[...] = du1
        acc_ref[0:1, :] += _colsum(du2 * xhat)
        acc_ref[1:2, :] += _colsum(du2)
        acc_ref[2:3, :] += _colsum(du1)

    return pl.pallas_call(
        body, name="ln_a_bwd", grid=(t // tm,),
        in_specs=[_colblk(tm, d, 0), _row(tm, d), _colblk(tm, d, 2), _vec(d), _vec(d)],
        out_specs=[_row(tm, d), _row(tm, d), pl.BlockSpec((SUBLANES, d), lambda j: (0, 0))],
        out_shape=[_sds((t, d), F32), _sds((t, d), BF16), _sds((SUBLANES, d), F32)],
        compiler_params=_cp("arbitrary"))(dab, u1, p, g, b)


def conv_a_bwd(du1, p, w_pad, lc, d):
    t = p.shape[0]
    nct, nch, ncc = d // LANES, t // CONV_ROWS, lc // CONV_ROWS

    def body(du_ref, av_ref, ag_ref, w_ref, dav_ref, dag_ref, dw_ref, pad_u, pad_d):
        _zero_halos(pad_u, lc, t)
        _zero_halos(pad_d, lc, t)
        dw_ref[...] = jnp.zeros_like(dw_ref)

        def fill(j, carry):
            rows = pl.ds(_tok_start(j), CONV_ROWS)
            dst = pl.ds(_pad_dst(j, ncc), CONV_ROWS)
            pad_u[dst, :] = av_ref[rows, :].astype(F32) * _sigmoid(ag_ref[rows, :].astype(F32))
            pad_d[dst, :] = du_ref[rows, :]
            return carry

        lax.fori_loop(0, nch, fill, 0)

        def step(j, carry):
            rows = pl.ds(_tok_start(j), CONV_ROWS)
            win = pl.ds(_win_start(j, ncc), CONV_ROWS + 2 * HALO)
            du0 = _conv_bwd_data(pad_d[win, :], w_ref, CONV_A_TAPS)
            sig = _sigmoid(ag_ref[rows, :].astype(F32))
            dav_ref[rows, :] = (du0 * sig).astype(dav_ref.dtype)
            dag_ref[rows, :] = (du0 * av_ref[rows, :].astype(F32) * sig * (1.0 - sig)).astype(dag_ref.dtype)
            _conv_bwd_weight(dw_ref, du_ref[rows, :], pad_u[win, :], CONV_A_TAPS)
            return carry

        lax.fori_loop(0, nch, step, 0)

    return pl.pallas_call(
        body, name="conv_a_bwd", grid=(nct,),
        in_specs=[_chan(t, 0), _chan(t, 0), _chan(t, nct), _tapw(32)],
        out_specs=[_chan(t, 0), _chan(t, 0), _tapw(32)],
        out_shape=[_sds((t, d), BF16), _sds((t, d), BF16), _sds((32, d), F32)],
        scratch_shapes=[pltpu.VMEM((t + 3 * HALO, LANES), F32), pltpu.VMEM((t + 3 * HALO, LANES), F32)],
        compiler_params=_cp("parallel"))(du1, p, p, w_pad)


def conv_b_bwd(dab, p, w_pad, lc, d):
    t = p.shape[0]
    nct, nch, ncc = d // LANES, t // CONV_ROWS, lc // CONV_ROWS

    def body(db_ref, bx_ref, bb_ref, bc_ref, bg_ref, w_ref,
             dbx_ref, dbb_ref, dbc_ref, dbg_ref, dw_ref, pad_t, pad_d):
        _zero_halos(pad_t, lc, t)
        _zero_halos(pad_d, lc, t)
        dw_ref[...] = jnp.zeros_like(dw_ref)

        def fill(j, carry):
            rows = pl.ds(_tok_start(j), CONV_ROWS)
            pad_t[pl.ds(_pad_dst(j, ncc), CONV_ROWS), :] = (
                bc_ref[rows, :].astype(F32) * bx_ref[rows, :].astype(F32))
            return carry

        lax.fori_loop(0, nch, fill, 0)

        def first(j, carry):
            rows = pl.ds(_tok_start(j), CONV_ROWS)
            xw = pad_t[pl.ds(_win_start(j, ncc), CONV_ROWS + 2 * HALO), :]
            v = _conv_fwd(xw, w_ref, CONV_B_TAPS)
            bg = bg_ref[rows, :].astype(F32)
            bb = bb_ref[rows, :].astype(F32)
            db = db_ref[rows, :].astype(F32)
            sg = _silu(bg)
            dbb_ref[rows, :] = (db * v * sg).astype(dbb_ref.dtype)
            dbg_ref[rows, :] = (db * bb * v * _dsilu(bg)).astype(dbg_ref.dtype)
            dv = db * bb * sg
            pad_d[pl.ds(_pad_dst(j, ncc), CONV_ROWS), :] = dv
            _conv_bwd_weight(dw_ref, dv, xw, CONV_B_TAPS)
            return carry

        lax.fori_loop(0, nch, first, 0)

        def second(j, carry):
            rows = pl.ds(_tok_start(j), CONV_ROWS)
            dt = _conv_bwd_data(pad_d[pl.ds(_win_start(j, ncc), CONV_ROWS + 2 * HALO), :], w_ref, CONV_B_TAPS)
            dbc_ref[rows, :] = (dt * bx_ref[rows, :].astype(F32)).astype(dbc_ref.dtype)
            dbx_ref[rows, :] = (dt * bc_ref[rows, :].astype(F32)).astype(dbx_ref.dtype)
            return carry

        lax.fori_loop(0, nch, second, 0)

    return pl.pallas_call(
        body, name="conv_b_bwd", grid=(nct,),
        in_specs=[_chan(t, nct), _chan(t, 3 * nct), _chan(t, 4 * nct), _chan(t, 5 * nct), _chan(t, 6 * nct),
                  _tapw(8)],
        out_specs=[_chan(t, 0)] * 4 + [_tapw(8)],
        out_shape=[_sds((t, d), BF16)] * 4 + [_sds((8, d), F32)],
        scratch_shapes=[pltpu.VMEM((t + 3 * HALO, LANES), F32), pltpu.VMEM((t + 3 * HALO, LANES), F32)],
        compiler_params=_cp("parallel"))(dab, p, p, p, p, w_pad)


def _swap_halves(z, first_half):
    return jnp.where(first_half, pltpu.roll(z, 96, 1), pltpu.roll(z, 32, 1))


def _first_half_mask(rows):
    lane = lax.broadcasted_iota(jnp.int32, (rows, HEAD_DIM), 1)
    return (lane & 32) == 0


def qk_fwd(qgkv, qg, kg, cos_t, sin_t, ad, kvd, tm, qscale):
    t = qgkv.shape[0]

    def body(q_ref, k_ref, qg_ref, kg_ref, c_ref, s_ref, qo_ref, ko_ref):
        first = _first_half_mask(tm)
        cosv, sinv = c_ref[...], s_ref[...]

        def head(x, gain):
            inv = lax.rsqrt(jnp.mean(x * x, axis=-1, keepdims=True) + RMS_EPS)
            yv = x * inv * gain
            return yv * cosv + _swap_halves(yv, first) * sinv

        for h in range(ad // HEAD_DIM):
            sl = slice(h * HEAD_DIM, (h + 1) * HEAD_DIM)
            qo_ref[:, sl] = (head(q_ref[:, sl].astype(F32), qg_ref[...]) * qscale).astype(qo_ref.dtype)
        for h in range(kvd // HEAD_DIM):
            sl = slice(h * HEAD_DIM, (h + 1) * HEAD_DIM)
            ko_ref[:, sl] = head(k_ref[:, sl].astype(F32), kg_ref[...]).astype(ko_ref.dtype)

    return pl.pallas_call(
        body, name="qk_fwd", grid=(t // tm,),
        in_specs=[_colblk(tm, ad, 0), _colblk(tm, kvd, 2 * ad // kvd), _vec(HEAD_DIM), _vec(HEAD_DIM),
                  _row(tm, HEAD_DIM), _row(tm, HEAD_DIM)],
        out_specs=[_row(tm, ad), _row(tm, kvd)],
        out_shape=[_sds((t, ad), BF16), _sds((t, kvd), BF16)],
        compiler_params=_cp("parallel"))(qgkv, qgkv, qg, kg, cos_t, sin_t)


def qk_bwd(dqr, dkr, qgkv, qg, kg, cos_t, sin_t, ad, kvd, tm):
    t = qgkv.shape[0]

    def body(dq_ref, dk_ref, q_ref, k_ref, qg_ref, kg_ref, c_ref, s_ref, dqo_ref, dko_ref, acc_ref):
        @pl.when(pl.program_id(0) == 0)
        def _():
            acc_ref[...] = jnp.zeros_like(acc_ref)

        first = _first_half_mask(tm)
        cosv, sinv = c_ref[...], s_ref[...]

        def head(x, gain, dout):
            inv = lax.rsqrt(jnp.mean(x * x, axis=-1, keepdims=True) + RMS_EPS)
            xn = x * inv
            dy = dout * cosv + _swap_halves(dout * sinv, first)
            dxn = dy * gain
            dx = inv * (dxn - xn * jnp.mean(dxn * xn, axis=-1, keepdims=True))
            return dx, _colsum(dy * xn)

        dqg = jnp.zeros((1, HEAD_DIM), F32)
        for h in range(ad // HEAD_DIM):
            sl = slice(h * HEAD_DIM, (h + 1) * HEAD_DIM)
            dx, dg = head(q_ref[:, sl].astype(F32), qg_ref[...], dq_ref[:, sl])
            dqo_ref[:, sl] = dx.astype(dqo_ref.dtype)
            dqg = dqg + dg
        dkg = jnp.zeros((1, HEAD_DIM), F32)
        for h in range(kvd // HEAD_DIM):
            sl = slice(h * HEAD_DIM, (h + 1) * HEAD_DIM)
            dx, dg = head(k_ref[:, sl].astype(F32), kg_ref[...], dk_ref[:, sl])
            dko_ref[:, sl] = dx.astype(dko_ref.dtype)
            dkg = dkg + dg
        acc_ref[0:1, :] += dqg
        acc_ref[1:2, :] += dkg

    return pl.pallas_call(
        body, name="qk_bwd", grid=(t // tm,),
        in_specs=[_row(tm, ad), _row(tm, kvd), _colblk(tm, ad, 0), _colblk(tm, kvd, 2 * ad // kvd),
                  _vec(HEAD_DIM), _vec(HEAD_DIM), _row(tm, HEAD_DIM), _row(tm, HEAD_DIM)],
        out_specs=[_row(tm, ad), _row(tm, kvd), pl.BlockSpec((SUBLANES, HEAD_DIM), lambda j: (0, 0))],
        out_shape=[_sds((t, ad), BF16), _sds((t, kvd), BF16), _sds((SUBLANES, HEAD_DIM), F32)],
        compiler_params=_cp("arbitrary"))(dqr, dkr, qgkv, qgkv, qg, kg, cos_t, sin_t)


_NT = (((1,), (1,)), ((), ()))


def _chunks_t(a, nkv, tm):
    t = a.shape[0]
    return a.reshape(t // tm, tm, nkv, HEAD_DIM).transpose(2, 0, 3, 1)


def _tree_rows(x, op):
    slabs = [x[i:i + SUBLANES] for i in range(0, x.shape[0], SUBLANES)]
    while len(slabs) > 1:
        slabs = [op(slabs[i], slabs[i + 1]) for i in range(0, len(slabs), 2)]
    return slabs[0]


def flash_fwd(qr, kr, vt, qgkv, ad, kvd, tm):
    t = qr.shape[0]
    nkv = kvd // HEAD_DIM
    gw = ad // nkv
    hpg = gw // HEAD_DIM
    nt = t // tm
    gate_blk0 = ad // gw

    def body(q_ref, k_ref, vt_ref, g_ref, o_ref, og_ref, lse_ref):
        qi = pl.program_id(1)
        nkc = jnp.where(qi == 0, 1, nt)
        heads = [slice(h * HEAD_DIM, (h + 1) * HEAD_DIM) for h in range(hpg)]
        qs = [q_ref[:, sl] for sl in heads]

        def step(c, carry):
            kc = k_ref[pl.ds(pl.multiple_of(c * tm, tm), tm), :]
            vtc = vt_ref[c]
            sts = [lax.dot_general(kc, q, _NT, preferred_element_type=F32) for q in qs]
            out = []
            for h in range(hpg):
                m, l, acc = carry[h]
                m_new = jnp.maximum(m, jnp.max(_tree_rows(sts[h], jnp.maximum), axis=0, keepdims=True))
                a = jnp.exp2(m - m_new)
                pt = jnp.exp2(sts[h] - m_new)
                l = a * l + jnp.sum(_tree_rows(pt, jnp.add), axis=0, keepdims=True)
                acc = a * acc + jnp.dot(vtc, pt.astype(BF16), preferred_element_type=F32)
                out.append((m_new, l, acc))
            return tuple(out)

        init = tuple((jnp.full((1, tm), -1e30, F32), jnp.zeros((1, tm), F32), jnp.zeros((HEAD_DIM, tm), F32))
                     for _ in range(hpg))
        res = lax.fori_loop(0, nkc, step, init)
        for h, sl in enumerate(heads):
            m, l, acc = res[h]
            o = (acc / l).T
            o_ref[:, sl] = o.astype(o_ref.dtype)
            og_ref[:, sl] = (o * _silu(g_ref[:, sl].astype(F32))).astype(og_ref.dtype)
            lse_ref[h:h + 1, :] = m + jnp.log(l) * LOG2_E

    return pl.pallas_call(
        body, name="flash_fwd", grid=(nkv, nt),
        in_specs=[pl.BlockSpec((tm, gw), lambda g, i: (i, g)),
                  pl.BlockSpec((t, HEAD_DIM), lambda g, i: (0, g)),
                  pl.BlockSpec((None, nt, HEAD_DIM, tm), lambda g, i: (g, 0, 0, 0)),
                  pl.BlockSpec((tm, gw), lambda g, i: (i, gate_blk0 + g))],
        out_specs=[pl.BlockSpec((tm, gw), lambda g, i: (i, g)),
                   pl.BlockSpec((tm, gw), lambda g, i: (i, g)),
                   pl.BlockSpec((None, None, hpg, tm), lambda g, i: (g, i, 0, 0))],
        out_shape=[_sds((t, ad), BF16), _sds((t, ad), BF16), _sds((nkv, nt, hpg, tm), F32)],
        compiler_params=_cp("parallel", "parallel"))(qr, kr, vt, qgkv)


def gate_bwd(dog, o, qgkv, ad, kvd, tm):
    t = o.shape[0]
    nkv = kvd // HEAD_DIM
    hpg = ad // nkv // HEAD_DIM

    def body(dog_ref, o_ref, g_ref, do_ref, dg_ref, dl_ref):
        lane = lax.broadcasted_iota(jnp.int32, (tm, LANES), 1)
        for grp in range(nkv):
            blk = jnp.zeros((tm, LANES), F32)
            for hh in range(hpg):
                h = grp * hpg + hh
                sl = slice(h * HEAD_DIM, (h + 1) * HEAD_DIM)
                dv = dog_ref[:, sl].astype(F32)
                ov = o_ref[:, sl].astype(F32)
                gv = g_ref[:, sl].astype(F32)
                doh = dv * _silu(gv)
                do_ref[:, sl] = doh.astype(do_ref.dtype)
                dg_ref[:, sl] = (dv * ov * _dsilu(gv)).astype(dg_ref.dtype)
                blk = jnp.where(lane == hh, jnp.sum(doh * ov, axis=-1, keepdims=True), blk)
            dl_ref[grp] = blk

    return pl.pallas_call(
        body, name="gate_bwd", grid=(t // tm,),
        in_specs=[_row(tm, ad), _row(tm, ad), _colblk(tm, ad, 1)],
        out_specs=[_row(tm, ad), _row(tm, ad), pl.BlockSpec((nkv, tm, LANES), lambda j: (0, j, 0))],
        out_shape=[_sds((t, ad), BF16), _sds((t, ad), BF16), _sds((nkv, t, LANES), F32)],
        compiler_params=_cp("parallel"))(dog, o, qgkv)


def flash_bwd(qr, do, kr, kt, qgkv, lse_t, delta_t, ad, kvd, tm, scale):
    t = qr.shape[0]
    nkv = kvd // HEAD_DIM
    gw = ad // nkv
    hpg = gw // HEAD_DIM
    nt = t // tm
    v_blk0 = (2 * ad + kvd) // HEAD_DIM

    def body(q_ref, do_ref, k_ref, v_ref, kt_ref, lse_ref, dl_ref, dq_ref, dk_ref, dv_ref):
        qi = pl.program_id(1)

        @pl.when(qi == 0)
        def _():
            dk_ref[...] = jnp.zeros_like(dk_ref)
            dv_ref[...] = jnp.zeros_like(dv_ref)

        nkc = jnp.where(qi == 0, 1, nt)
        heads = [slice(h * HEAD_DIM, (h + 1) * HEAD_DIM) for h in range(hpg)]
        qs = [q_ref[:, sl] for sl in heads]
        dos = [do_ref[:, sl] for sl in heads]
        lses = [lse_ref[h:h + 1, :] for h in range(hpg)]
        dls = [dl_ref[h:h + 1, :] for h in range(hpg)]

        def step(c, dqts):
            rows = pl.ds(pl.multiple_of(c * tm, tm), tm)
            kc = k_ref[rows, :]
            vc = v_ref[rows, :]
            ktc = kt_ref[c]
            dk = jnp.zeros((tm, HEAD_DIM), F32)
            dv = jnp.zeros((tm, HEAD_DIM), F32)
            out = []
            sts = [lax.dot_general(kc, q, _NT, preferred_element_type=F32) for q in qs]
            dpts = [lax.dot_general(vc, d, _NT, preferred_element_type=F32) for d in dos]
            for h in range(hpg):
                pt = jnp.exp2(sts[h] - lses[h])
                dv = dv + jnp.dot(pt.astype(BF16), dos[h], preferred_element_type=F32)
                dst = (pt * (dpts[h] - dls[h])).astype(BF16)
                dk = dk + jnp.dot(dst, qs[h], preferred_element_type=F32)
                out.append(dqts[h] + jnp.dot(ktc, dst, preferred_element_type=F32))
            dk_ref[rows, :] += dk
            dv_ref[rows, :] += dv
            return tuple(out)

        res = lax.fori_loop(0, nkc, step, tuple(jnp.zeros((HEAD_DIM, tm), F32) for _ in range(hpg)))
        for h, sl in enumerate(heads):
            dq_ref[:, sl] = res[h].T * scale

        @pl.when(qi == nt - 1)
        def _():
            dk_ref[...] = dk_ref[...] * LN_2

    return pl.pallas_call(
        body, name="flash_bwd", grid=(nkv, nt),
        in_specs=[pl.BlockSpec((tm, gw), lambda g, i: (i, g)),
                  pl.BlockSpec((tm, gw), lambda g, i: (i, g)),
                  pl.BlockSpec((t, HEAD_DIM), lambda g, i: (0, g)),
                  pl.BlockSpec((t, HEAD_DIM), lambda g, i: (0, v_blk0 + g)),
                  pl.BlockSpec((None, nt, HEAD_DIM, tm), lambda g, i: (g, 0, 0, 0)),
                  pl.BlockSpec((None, None, hpg, tm), lambda g, i: (g, i, 0, 0)),
                  pl.BlockSpec((None, None, hpg, tm), lambda g, i: (g, i, 0, 0))],
        out_specs=[pl.BlockSpec((tm, gw), lambda g, i: (i, g)),
                   pl.BlockSpec((t, HEAD_DIM), lambda g, i: (0, g)),
                   pl.BlockSpec((t, HEAD_DIM), lambda g, i: (0, g))],
        out_shape=[_sds((t, ad), F32), _sds((t, kvd), F32), _sds((t, kvd), F32)],
        compiler_params=_cp("parallel", "arbitrary"))(qr, do, kr, qgkv, kt, lse_t, delta_t)


def _rows_per_head(a, tm, hpg):
    nkv, t, _ = a.shape
    return a[:, :, :hpg].reshape(nkv, t // tm, tm, hpg).transpose(0, 1, 3, 2)


def adaln_fwd(c16, w_mod):
    nlay, d, nl = w_mod.shape
    tn = _pick(nl, (512, 256, 128))

    def body(c_ref, w_ref, o_ref):
        o_ref[...] = jnp.dot(_silu(c_ref[...]), w_ref[...], preferred_element_type=F32,
                             precision=lax.Precision.HIGHEST)

    return pl.pallas_call(
        body, name="adaln_fwd", grid=(nlay, nl // tn),
        in_specs=[pl.BlockSpec((16, d), lambda l, j: (0, 0)),
                  pl.BlockSpec((None, d, tn), lambda l, j: (l, 0, j))],
        out_specs=pl.BlockSpec((None, 16, tn), lambda l, j: (l, 0, j)),
        out_shape=_sds((nlay, 16, nl), F32), compiler_params=_cp("parallel", "parallel"))(c16, w_mod)


def adaln_bwd(c16t, dm, w_mod):
    nlay, d, nl = w_mod.shape
    tn = _pick(nl, (512, 256, 128))

    def body(c_ref, dm_ref, w_ref, dw_ref, dc_ref):
        @pl.when(pl.program_id(1) == 0)
        def _():
            dc_ref[...] = jnp.zeros_like(dc_ref)

        dmv = dm_ref[...]
        dw_ref[...] = jnp.dot(_silu(c_ref[...]), dmv, preferred_element_type=F32,
                              precision=lax.Precision.HIGHEST)
        dc_ref[...] += lax.dot_general(dmv, w_ref[...], _NT, preferred_element_type=F32,
                                       precision=lax.Precision.HIGHEST)

    return pl.pallas_call(
        body, name="adaln_bwd", grid=(nlay, nl // tn),
        in_specs=[pl.BlockSpec((d, 16), lambda l, j: (0, 0)),
                  pl.BlockSpec((None, 16, tn), lambda l, j: (l, 0, j)),
                  pl.BlockSpec((None, d, tn), lambda l, j: (l, 0, j))],
        out_specs=[pl.BlockSpec((None, d, tn), lambda l, j: (l, 0, j)),
                   pl.BlockSpec((None, 16, d), lambda l, j: (l, 0, 0))],
        out_shape=[_sds((nlay, d, nl), F32), _sds((nlay, 16, d), F32)],
        compiler_params=_cp("parallel", "arbitrary"))(c16t, dm, w_mod)


def sum_leading(a, name):
    n = a.shape[0]

    def body(a_ref, o_ref):
        acc = a_ref[0]
        for i in range(1, n):
            acc = acc + a_ref[i]
        o_ref[...] = acc

    return pl.pallas_call(body, name=name, in_specs=[VMEM_FULL], out_specs=VMEM_FULL,
                          out_shape=_sds(a.shape[1:], F32),
                          compiler_params=pltpu.CompilerParams(vmem_limit_bytes=VMEM_LIMIT))(a)


def c_ctx_grad(parts, c_ctx):
    n = parts.shape[0]

    def body(p_ref, c_ref, o_ref):
        acc = p_ref[0]
        for i in range(1, n):
            acc = acc + p_ref[i]
        o_ref[...] = acc * _dsilu(c_ref[...])

    return pl.pallas_call(body, name="c_ctx_grad", in_specs=[VMEM_FULL, VMEM_FULL], out_specs=VMEM_FULL,
                          out_shape=_sds(c_ctx.shape, F32))(parts, c_ctx)


def _as2d(a):
    return a.reshape(-1, a.shape[-1])


def _row_tile(r, c):
    for tr in (1024, 512, 256, 128, 64, 32, 16, 8):
        if r % tr == 0 and tr * c * 4 <= (1 << 20):
            return tr
    return r


def add_n(arrs, name, out_dtypes=(F32,)):
    shape = arrs[0].shape
    flat = [_as2d(a) for a in arrs]
    r, c = flat[0].shape
    tr = _row_tile(r, c)
    n_in = len(flat)

    def body(*refs):
        acc = refs[0][...].astype(F32)
        for ref in refs[1:n_in]:
            acc = acc + ref[...].astype(F32)
        for ref in refs[n_in:]:
            ref[...] = acc.astype(ref.dtype)

    spec = pl.BlockSpec((tr, c), lambda i: (i, 0))
    outs = pl.pallas_call(
        body, name=name, grid=(r // tr,), in_specs=[spec] * n_in, out_specs=[spec] * len(out_dtypes),
        out_shape=[_sds((r, c), dt) for dt in out_dtypes], compiler_params=_cp("parallel"))(*flat)
    return [o.reshape(shape) for o in outs]


def adamw(w, g, m, v):
    shape = w.shape
    flat = [_as2d(a.reshape((1,) + shape) if len(shape) == 1 else a) for a in (w, g, m, v)]
    r, c = flat[0].shape
    tr = _row_tile(r, c)
    c1 = 1.0 - ADAM_B1 ** ADAM_STEP
    c2 = 1.0 - ADAM_B2 ** ADAM_STEP

    def body(w_ref, g_ref, m_ref, v_ref, d_ref, nm_ref, nv_ref):
        gv = g_ref[...]
        nm = ADAM_B1 * m_ref[...] + (1.0 - ADAM_B1) * gv
        nv = ADAM_B2 * v_ref[...] + (1.0 - ADAM_B2) * (gv * gv)
        d_ref[...] = -ADAM_LR * ((nm / c1) / (jnp.sqrt(nv / c2) + ADAM_EPS) + ADAM_WD * w_ref[...])
        nm_ref[...] = nm
        nv_ref[...] = nv

    spec = pl.BlockSpec((tr, c), lambda i: (i, 0))
    outs = pl.pallas_call(
        body, name="adamw", grid=(r // tr,), in_specs=[spec] * 4, out_specs=[spec] * 3,
        out_shape=[_sds((r, c), F32)] * 3, compiler_params=_cp("parallel"))(*flat)
    return tuple(o.reshape(shape) for o in outs)


def _place():
    return lax.axis_index("x"), lax.axis_index("y"), lax.axis_index("c")


def _remote(src, dst, ssem, rsem, dev):
    return pltpu.make_async_remote_copy(src_ref=src, dst_ref=dst, send_sem=ssem, recv_sem=rsem,
                                        device_id=dev, device_id_type=MESH)


def all_gather8(v, name):
    m_per, n = v.shape

    def body(x_ref, out_ref, send_sems, recv_sems, local_sem):
        x, y, c = _place()
        me, sibling = (x, y, c), (x, y, 1 - c)
        chips = [(1 - x, y), (x, 1 - y), (1 - x, 1 - y)]

        def rows(px, py, pc):
            return out_ref.at[pl.ds((4 * px + 2 * py + pc) * m_per, m_per), :]

        def copy(k, block, to, src=None):
            return _remote(rows(*block) if src is None else src, rows(*block),
                           send_sems.at[k], recv_sems.at[k], to)

        mine = pltpu.make_async_copy(x_ref, rows(*me), local_sem)
        mine.start()
        first = [copy(0, me, sibling, src=x_ref)]
        first += [copy(1 + j, me, (*chip, c), src=x_ref) for j, chip in enumerate(chips)]
        for cp in first:
            cp.start()
        passed = [copy(4 + j, (*chip, c), sibling) for j, chip in enumerate(chips)]
        for j, chip in enumerate(chips):
            copy(1 + j, (*chip, c), me).wait_recv()
            passed[j].start()
        copy(0, sibling, me).wait_recv()
        for j, chip in enumerate(chips):
            copy(4 + j, (*chip, 1 - c), me).wait_recv()
        for cp in first + passed:
            cp.wait_send()
        mine.wait()

    return pl.pallas_call(
        body, name=name, out_shape=_sds((8 * m_per, n), v.dtype),
        in_specs=[VMEM_FULL], out_specs=VMEM_FULL,
        scratch_shapes=[pltpu.SemaphoreType.DMA((7,)), pltpu.SemaphoreType.DMA((7,)), pltpu.SemaphoreType.DMA],
        compiler_params=pltpu.CompilerParams(vmem_limit_bytes=VMEM_LIMIT))(v)


def _slabs(ref, n):
    rows = ref.shape[0] // n
    return [ref.at[pl.ds(i * rows, rows)] for i in range(n)]


def gather_weight(w, name):
    _, r, cdim = w.shape
    n = D2D_PARTS

    def body(w_ref, out_ref, send_sems, recv_sems, local_sems):
        x, y, c = _place()
        sibling = (x, y, 1 - c)
        chips = [(1 - x, y), (x, 1 - y), (1 - x, 1 - y)]
        mine = 2 * x + y
        local = [pltpu.make_async_copy(w_ref.at[i], out_ref.at[i, mine], local_sems.at[i]) for i in range(2)]
        for cp in local:
            cp.start()
        src = _slabs(w_ref.at[c], n)
        first = []
        for k, chip in enumerate(chips):
            dst = _slabs(out_ref.at[c, mine], n)
            for j in range(n):
                cp = _remote(src[j], dst[j], send_sems.at[k * n + j], recv_sems.at[k * n + j], (*chip, c))
                cp.start()
                first.append(cp)
        passed = []
        for j in range(n):
            for k, (px, py) in enumerate(chips):
                theirs = _slabs(out_ref.at[c, 2 * px + py], n)[j]
                _remote(src[j], theirs, send_sems.at[k * n + j], recv_sems.at[k * n + j], sibling).wait_recv()
                cp = _remote(theirs, theirs, send_sems.at[(3 + k) * n + j], recv_sems.at[(3 + k) * n + j], sibling)
                cp.start()
                passed.append(cp)
        for j in range(n):
            for k, (px, py) in enumerate(chips):
                other = _slabs(out_ref.at[1 - c, 2 * px + py], n)[j]
                _remote(other, other, send_sems.at[(3 + k) * n + j], recv_sems.at[(3 + k) * n + j],
                        sibling).wait_recv()
        for cp in first + passed:
            cp.wait_send()
        for cp in local:
            cp.wait()

    return pl.pallas_call(
        body, name=name, out_shape=_sds((2, 4, r, cdim), w.dtype), in_specs=[ANY], out_specs=ANY,
        scratch_shapes=[pltpu.SemaphoreType.DMA((6 * n,)), pltpu.SemaphoreType.DMA((6 * n,)),
                        pltpu.SemaphoreType.DMA((2,))])(w)


def rs_chip_exchange(gb0, gb1, g0, g1, name):
    _, r, cdim = g0.shape

    def body(gb0_ref, gb1_ref, g0_ref, g1_ref, mine_ref, st_ref, send_sems, recv_sems, local_sems):
        x, y, c = _place()
        chips = [(1 - x, y), (x, 1 - y), (1 - x, 1 - y)]
        keep = [pltpu.make_async_copy(g_ref.at[2 * x + y], mine_ref.at[i], local_sems.at[i])
                for i, g_ref in enumerate((g0_ref, g1_ref))]
        for cp in keep:
            cp.start()
        cps = []
        for k, (px, py) in enumerate(chips):
            for i, gb_ref in enumerate((gb0_ref, gb1_ref)):
                cps.append(_remote(gb_ref.at[2 * px + py], st_ref.at[k, i], send_sems.at[2 * k + i],
                                   recv_sems.at[2 * k + i], (px, py, c)))
        for cp in cps:
            cp.start()
        for cp in cps:
            cp.wait()
        for cp in keep:
            cp.wait()

    return pl.pallas_call(
        body, name=name, out_shape=[_sds((2, r, cdim), F32), _sds((3, 2, r, cdim), BF16)],
        in_specs=[ANY] * 4, out_specs=[ANY, ANY],
        scratch_shapes=[pltpu.SemaphoreType.DMA((6,)), pltpu.SemaphoreType.DMA((6,)),
                        pltpu.SemaphoreType.DMA((2,))])(gb0, gb1, g0, g1)


def rs_pair_swap(part, part_b, name):
    _, r, cdim = part.shape
    n = 2 * D2D_PARTS

    def body(p_ref, pb_ref, own_ref, got_ref, stage_ref, send_sems, recv_sems, local_sems):
        x, y, c = _place()
        sibling = (x, y, 1 - c)
        keep = pltpu.make_async_copy(p_ref.at[c], own_ref, local_sems.at[0])
        keep.start()
        load = pltpu.make_async_copy(pb_ref.at[1 - c], stage_ref, local_sems.at[1])
        load.start()
        load.wait()
        src, dst = _slabs(stage_ref, n), _slabs(got_ref, n)
        cps = [_remote(src[j], dst[j], send_sems.at[j], recv_sems.at[j], sibling) for j in range(n)]
        for cp in cps:
            cp.start()
        for cp in cps:
            cp.wait()
        keep.wait()

    return pl.pallas_call(
        body, name=name, out_shape=[_sds((r, cdim), F32), _sds((r, cdim), BF16)],
        in_specs=[ANY, ANY], out_specs=[ANY, VMEM_FULL],
        scratch_shapes=[pltpu.VMEM((r, cdim), BF16), pltpu.SemaphoreType.DMA((n,)), pltpu.SemaphoreType.DMA((n,)),
                        pltpu.SemaphoreType.DMA((2,))],
        compiler_params=pltpu.CompilerParams(vmem_limit_bytes=VMEM_LIMIT))(part, part_b)


def rs_pair_share(red, name):
    n = 2 * D2D_PARTS

    def body(red_ref, out_ref, send_sems, recv_sems, local_sem):
        x, y, c = _place()
        sibling = (x, y, 1 - c)
        keep = pltpu.make_async_copy(red_ref, out_ref.at[c], local_sem)
        keep.start()
        src, dst, got = _slabs(red_ref, n), _slabs(out_ref.at[c], n), _slabs(out_ref.at[1 - c], n)
        cps = [_remote(src[j], dst[j], send_sems.at[j], recv_sems.at[j], sibling) for j in range(n)]
        for cp in cps:
            cp.start()
        for cp in cps:
            cp.wait_send()
        for j in range(n):
            _remote(src[j], got[j], send_sems.at[j], recv_sems.at[j], sibling).wait_recv()
        keep.wait()

    return pl.pallas_call(
        body, name=name, out_shape=_sds((2,) + red.shape, red.dtype), in_specs=[VMEM_FULL], out_specs=VMEM_FULL,
        scratch_shapes=[pltpu.SemaphoreType.DMA((n,)), pltpu.SemaphoreType.DMA((n,)), pltpu.SemaphoreType.DMA],
        compiler_params=pltpu.CompilerParams(vmem_limit_bytes=VMEM_LIMIT))(red)


def reduce_scatter_grad(g0, g1, tag):
    mine, theirs = rs_chip_exchange(g0[1], g1[1], g0[0], g1[0], "rs_exchange_" + tag)
    part, part_b = add_n([mine, theirs[0], theirs[1], theirs[2]], "rs_chip_add_" + tag, (F32, BF16))
    own, got = rs_pair_swap(part, part_b, "rs_swap_" + tag)
    red, = add_n([own, got], "rs_pair_add_" + tag, (BF16,))
    return rs_pair_share(red, "rs_share_" + tag).astype(F32)


def _rope_tables(lc, s):
    rows_n = s // GRID_W
    row = jnp.repeat(jnp.arange(rows_n, dtype=F32), GRID_W)
    col = jnp.tile(jnp.arange(GRID_W, dtype=F32), rows_n)
    axis_dim = HEAD_DIM // 2
    inv_freq = ROPE_THETA ** (-jnp.arange(0, axis_dim, 2, dtype=F32) / axis_dim)
    ang_r = row[:, None] * inv_freq[None, :]
    ang_c = col[:, None] * inv_freq[None, :]
    cr, sr, cc, sc = jnp.cos(ang_r), jnp.sin(ang_r), jnp.cos(ang_c), jnp.sin(ang_c)
    cos_l = jnp.concatenate([cr, cr, cc, cc], axis=1)
    sin_l = jnp.concatenate([-sr, sr, -sc, sc], axis=1)
    cos_t = jnp.concatenate([jnp.ones((lc, HEAD_DIM), F32), cos_l], axis=0)
    sin_t = jnp.concatenate([jnp.zeros((lc, HEAD_DIM), F32), sin_l], axis=0)
    return cos_t, sin_t


def _pad_rows(a, rows):
    return jnp.concatenate([a, jnp.zeros((rows - a.shape[0],) + a.shape[1:], a.dtype)], axis=0)


def _pad_cols(a, cols):
    return jnp.concatenate([a, jnp.zeros(a.shape[:-1] + (cols - a.shape[-1],), a.dtype)], axis=-1)


def kernel(x, c, ctx, c_ctx, w_mod, b_mod, post_ln_g, post_ln_b, w_in_e, conv_a_w, conv_a_b, norm_a_g, norm_a_b, conv_b_w, w_out_e, w_in_o, q_norm_g, k_norm_g, w_out_o, loss_target, m_c_ctx, m_w_mod, m_b_mod, m_post_ln_g, m_post_ln_b, m_w_in_e, m_conv_a_w, m_conv_a_b, m_norm_a_g, m_norm_a_b, m_conv_b_w, m_w_out_e, m_w_in_o, m_q_norm_g, m_k_norm_g, m_w_out_o, v_c_ctx, v_w_mod, v_b_mod, v_post_ln_g, v_post_ln_b, v_w_in_e, v_conv_a_w, v_conv_a_b, v_norm_a_g, v_norm_a_b, v_conv_b_w, v_w_out_e, v_w_in_o, v_q_norm_g, v_k_norm_g, v_w_out_o):
    s, d = x.shape[1], x.shape[2]
    lc = ctx.shape[1]
    t = lc + s
    tm = lc
    depth = w_mod.shape[0]
    n_even, n_odd = w_in_e.shape[0], w_in_o.shape[0]
    ad = w_out_o.shape[1] * 4
    kvd = (w_in_o.shape[2] * 4 - 2 * ad) // 2
    nkv = kvd // HEAD_DIM
    hpg = ad // nkv // HEAD_DIM
    nlm = w_mod.shape[2]
    alpha = (2.0 * depth) ** 0.25
    scale = HEAD_DIM ** -0.5
    assert n_even == 2 and n_odd == 2 and depth == 4 and hpg == GQA_GROUP
    assert lc % CONV_ROWS == 0 and s % tm == 0 and d % LANES == 0

    xi, yi, ci = _place()
    shard = 2 * xi + yi
    dev = 4 * xi + 2 * yi + ci

    wg_in_e = gather_weight(w_in_e.astype(BF16), "gather_w_in_e")
    wg_out_e = gather_weight(w_out_e.astype(BF16), "gather_w_out_e")
    wg_in_o = gather_weight(w_in_o.astype(BF16), "gather_w_in_o")
    wg_out_o = gather_weight(w_out_o.astype(BF16), "gather_w_out_o")

    c_all = all_gather8(_pad_rows(c, 8), "gather_c")
    c16 = _pad_rows(jnp.concatenate([c_all[0::8], c_ctx[None, :]], axis=0), 16)
    m_part = adaln_fwd(c16, w_mod)
    m_all = all_gather8(m_part.reshape(depth * 16, nlm), "gather_mod")
    m_all = m_all.reshape(8, depth, 16, nlm)[0::2]
    m_full = m_all.transpose(1, 2, 0, 3).reshape(depth, 16, 4 * nlm) + b_mod[:, None, :]
    m_lat = lax.dynamic_index_in_dim(m_full, dev, axis=1, keepdims=False)
    m_ctx = m_full[:, 8]

    def seg2(l, part):
        return jnp.stack([m_ctx[l, part * d:(part + 1) * d], m_lat[l, part * d:(part + 1) * d]])[:, None, :]

    cos_t, sin_t = _rope_tables(lc, s)

    small_gathered = all_gather8(
        _pad_rows(jnp.concatenate([conv_a_w.reshape(n_even * CONV_A_TAPS, -1),
                                   conv_b_w.reshape(n_even * CONV_B_TAPS, -1)], axis=0), 72), "gather_taps")
    taps = small_gathered.reshape(8, 72, -1)[0::2]
    taps = taps.transpose(1, 0, 2).reshape(72, d)
    caw = taps[:n_even * CONV_A_TAPS].reshape(n_even, CONV_A_TAPS, d)
    cbw = taps[n_even * CONV_A_TAPS:n_even * (CONV_A_TAPS + CONV_B_TAPS)].reshape(n_even, CONV_B_TAPS, d)
    caw_pad = jnp.concatenate([caw, jnp.zeros((n_even, 32 - CONV_A_TAPS, d), F32)], axis=1)
    cbw_pad = jnp.concatenate([cbw, jnp.zeros((n_even, 8 - CONV_B_TAPS, d), F32)], axis=1)

    xc = jnp.concatenate([ctx[0], x[0]], axis=0)
    saved = []
    for l in range(depth):
        i = l // 2
        shift2, scale2, gate2 = seg2(l, 0), seg2(l, 1), seg2(l, 2)
        h = mod_fwd(xc, scale2, shift2, tm)
        if l % 2 == 0:
            p = mm_nn(h, wg_in_e[i], BF16)
            u1 = conv_a_fwd(p, caw_pad[i], conv_a_b[i][None, :], lc, d)
            a_out = ln_a_fwd(u1, p, norm_a_g[i][None, :], norm_a_b[i][None, :], tm)
            b_out = conv_b_fwd(p, cbw_pad[i], lc, d)
            ab = jnp.concatenate([a_out, b_out], axis=1)
            y = mm_nn(ab, wg_out_e[i].reshape(1, 2 * d, d), F32)
            saved.append(dict(xc=xc, h=h, p=p, u1=u1, ab=ab, y=y))
        else:
            qgkv = mm_nn(h, wg_in_o[i], BF16)
            qr, kr = qk_fwd(qgkv, q_norm_g[i][None, :], k_norm_g[i][None, :], cos_t, sin_t, ad, kvd, tm,
                            scale * LOG2_E)
            vt = _chunks_t(qgkv[:, 2 * ad + kvd:], nkv, tm)
            o, og, lse = flash_fwd(qr, kr, vt, qgkv, ad, kvd, tm)
            y = mm_nn(og, wg_out_o[i].reshape(1, ad, d), F32)
            saved.append(dict(xc=xc, h=h, qgkv=qgkv, qr=qr, kr=kr, o=o, og=og, lse=lse, y=y))
        xc = post_ln_fwd(xc, y, gate2, post_ln_g[l][None, :], post_ln_b[l][None, :], alpha, tm)

    dxc, loss_acc = loss_head(xc, loss_target[0], lc, tm)
    loss = lax.psum(0.5 / d * jnp.sum(loss_acc[0]), MESH_AXES)

    g_in_e, g_out_e, g_in_o, g_out_o = [None] * 2, [None] * 2, [None] * 2, [None] * 2
    d_mod_lat, d_mod_ctx = [None] * depth, [None] * depth
    d_pln_g, d_pln_b = [None] * depth, [None] * depth
    d_cab, d_nag, d_nab, d_caw, d_cbw = [None] * 2, [None] * 2, [None] * 2, [None] * 2, [None] * 2
    d_qg, d_kg = [None] * 2, [None] * 2
    for l in reversed(range(depth)):
        i = l // 2
        sv = saved[l]
        scale2, gate2 = seg2(l, 1), seg2(l, 2)
        dzx, dy, acc_ln = post_ln_bwd(dxc, sv["xc"], sv["y"], gate2, post_ln_g[l][None, :], alpha, tm)
        d_pln_g[l] = acc_ln[0, 1] + acc_ln[1, 1]
        d_pln_b[l] = acc_ln[0, 2] + acc_ln[1, 2]
        if l % 2 == 0:
            w_out3 = wg_out_e[i].reshape(1, 2 * d, d)
            dab = mm_nt(dy, w_out3, BF16)
            g_out_e[i] = tuple(g.reshape(4, 2 * d // 4, d) for g in mm_tn(sv["ab"], dy, 1))
            du1, d_agate, acc_a = ln_a_bwd(dab, sv["u1"], sv["p"], norm_a_g[i][None, :], norm_a_b[i][None, :], tm)
            d_nag[i], d_nab[i], d_cab[i] = acc_a[0], acc_a[1], acc_a[2]
            d_aval, d_aglu, d_caw[i] = conv_a_bwd(du1, sv["p"], caw_pad[i], lc, d)
            d_bx, d_bb, d_bc, d_bg, d_cbw[i] = conv_b_bwd(dab, sv["p"], cbw_pad[i], lc, d)
            dp = jnp.concatenate([d_aval, d_aglu, d_agate, d_bx, d_bb, d_bc, d_bg], axis=1)
            dh = mm_nt(dp, wg_in_e[i], F32)
            g_in_e[i] = tuple(mm_tn(sv["h"], dp, 4))
        else:
            w_out3 = wg_out_o[i].reshape(1, ad, d)
            dog = mm_nt(dy, w_out3, BF16)
            g_out_o[i] = tuple(g.reshape(4, ad // 4, d) for g in mm_tn(sv["og"], dy, 1))
            do, dgate, delta = gate_bwd(dog, sv["o"], sv["qgkv"], ad, kvd, tm)
            dqr, dkr, dv = flash_bwd(sv["qr"], do, sv["kr"], _chunks_t(sv["kr"], nkv, tm), sv["qgkv"], sv["lse"],
                                     _rows_per_head(delta, tm, hpg), ad, kvd, tm, scale)
            dq, dk, acc_qk = qk_bwd(dqr, dkr, sv["qgkv"], q_norm_g[i][None, :], k_norm_g[i][None, :],
                                    cos_t, sin_t, ad, kvd, tm)
            d_qg[i], d_kg[i] = acc_qk[0], acc_qk[1]
            dqgkv = jnp.concatenate([dq, dgate, dk, dv.astype(BF16)], axis=1)
            dh = mm_nt(dqgkv, wg_in_o[i], F32)
            g_in_o[i] = tuple(mm_tn(sv["h"], dqgkv, 4))
        dxc, acc_mod = mod_bwd(dh, dzx, sv["xc"], scale2, tm)
        d_mod_ctx[l] = jnp.stack([acc_mod[0, 0], acc_mod[0, 1], acc_ln[0, 0]])
        d_mod_lat[l] = jnp.stack([acc_mod[1, 0], acc_mod[1, 1], acc_ln[1, 0]])
    grad_x = dxc[lc:][None]

    parts = [jnp.concatenate(d_mod_ctx, axis=0), jnp.stack(d_pln_g), jnp.stack(d_pln_b), jnp.stack(d_cab),
             jnp.stack(d_nag), jnp.stack(d_nab), jnp.concatenate(d_caw, axis=0), jnp.concatenate(d_cbw, axis=0),
             _pad_cols(jnp.stack(d_qg), d), _pad_cols(jnp.stack(d_kg), d), jnp.concatenate(d_mod_lat, axis=0)]
    parts = [_pad_rows(p, -(-p.shape[0] // SUBLANES) * SUBLANES) for p in parts]
    offs = [0]
    for p in parts:
        offs.append(offs[-1] + p.shape[0])
    pack = jnp.concatenate(parts, axis=0)
    npack = offs[-1]
    gathered = all_gather8(pack, "gather_small").reshape(8, npack, d)
    small = sum_leading(gathered, "sum_small")

    def piece(k, rows):
        return small[offs[k]:offs[k] + rows]

    dm_ctx = piece(0, 3 * depth).reshape(depth, 1, 3 * d)
    dm_lat = gathered[:, offs[10]:offs[10] + 3 * depth].reshape(8, depth, 3 * d).transpose(1, 0, 2)
    dm = jnp.concatenate([dm_lat, dm_ctx, jnp.zeros((depth, 7, 3 * d), F32)], axis=1)
    g_b_mod = sum_leading(dm.transpose(1, 0, 2), "sum_b_mod")
    dm_shard = lax.dynamic_slice_in_dim(dm, shard * nlm, nlm, axis=2)
    g_w_mod, dc_part = adaln_bwd(c16.T, dm_shard, w_mod)
    dc_all = all_gather8(_pad_rows(dc_part[:, 8, :], 8), "gather_dc").reshape(8, 8, d)
    g_c_ctx = c_ctx_grad(dc_all[0::2, :depth].reshape(4 * depth, 1, d), c_ctx[None, :])[0]

    g_pln_g, g_pln_b = piece(1, depth), piece(2, depth)
    g_cab, g_nag, g_nab = piece(3, n_even), piece(4, n_even), piece(5, n_even)
    dch = d // 4
    g_caw = lax.dynamic_slice_in_dim(piece(6, 64).reshape(2, 32, d)[:, :CONV_A_TAPS], shard * dch, dch, axis=2)
    g_cbw = lax.dynamic_slice_in_dim(piece(7, 16).reshape(2, 8, d)[:, :CONV_B_TAPS], shard * dch, dch, axis=2)
    g_qg, g_kg = piece(8, n_odd)[:, :HEAD_DIM], piece(9, n_odd)[:, :HEAD_DIM]

    g_w_in_e = reduce_scatter_grad(g_in_e[0], g_in_e[1], "in_e")
    g_w_out_e = reduce_scatter_grad(g_out_e[0], g_out_e[1], "out_e")
    g_w_in_o = reduce_scatter_grad(g_in_o[0], g_in_o[1], "in_o")
    g_w_out_o = reduce_scatter_grad(g_out_o[0], g_out_o[1], "out_o")

    grads = [g_c_ctx, g_w_mod, g_b_mod, g_pln_g, g_pln_b, g_w_in_e, g_caw, g_cab, g_nag, g_nab, g_cbw,
             g_w_out_e, g_w_in_o, g_qg, g_kg, g_w_out_o]
    weights = [c_ctx, w_mod, b_mod, post_ln_g, post_ln_b, w_in_e, conv_a_w, conv_a_b, norm_a_g, norm_a_b,
               conv_b_w, w_out_e, w_in_o, q_norm_g, k_norm_g, w_out_o]
    ms = [m_c_ctx, m_w_mod, m_b_mod, m_post_ln_g, m_post_ln_b, m_w_in_e, m_conv_a_w, m_conv_a_b, m_norm_a_g,
          m_norm_a_b, m_conv_b_w, m_w_out_e, m_w_in_o, m_q_norm_g, m_k_norm_g, m_w_out_o]
    vs = [v_c_ctx, v_w_mod, v_b_mod, v_post_ln_g, v_post_ln_b, v_w_in_e, v_conv_a_w, v_conv_a_b, v_norm_a_g,
          v_norm_a_b, v_conv_b_w, v_w_out_e, v_w_in_o, v_q_norm_g, v_k_norm_g, v_w_out_o]
    deltas, new_ms, new_vs = [], [], []
    for wv, gv, mv, vv in zip(weights, grads, ms, vs):
        dl, nm, nv = adamw(wv, gv, mv, vv)
        deltas.append(dl)
        new_ms.append(nm)
        new_vs.append(nv)
    return (loss, grad_x, *grads, *deltas, *new_ms, *new_vs)
```

```python
import functools

import jax
import jax.numpy as jnp
from jax import lax
from jax.experimental import pallas as pl
from jax.experimental.pallas import tpu as pltpu

F32 = jnp.float32
BF16 = jnp.bfloat16

LANES = 128
SUBLANES = 8
HEAD_DIM = 128
GQA_GROUP = 4
GRID_W = 64
ROPE_THETA = 10000.0
LN_EPS = 1e-5
RMS_EPS = 1e-6
CONV_A_TAPS = 31
CONV_B_TAPS = 3
HALO = 16
CONV_ROWS = 128
ADAM_LR = 0.001
ADAM_B1 = 0.9
ADAM_B2 = 0.999
ADAM_EPS = 1e-08
ADAM_WD = 0.01
ADAM_STEP = 10
VMEM_LIMIT = 56 * 1024 * 1024
D2D_PARTS = 4
LOG2_E = 1.4426950408889634
LN_2 = 0.6931471805599453
MESH_AXES = ("x", "y", "c")
MESH = pl.DeviceIdType.MESH
ANY = pl.BlockSpec(memory_space=pl.ANY)
VMEM_FULL = pl.BlockSpec(memory_space=pltpu.VMEM)


def _sds(shape, dtype):
    return jax.ShapeDtypeStruct(tuple(shape), dtype)


def _cp(*sem):
    return pltpu.CompilerParams(dimension_semantics=sem, vmem_limit_bytes=VMEM_LIMIT)


def _pick(n, cands):
    for c in cands:
        if n % c == 0:
            return c
    return n


def _sigmoid(x):
    return 1.0 / (1.0 + jnp.exp(-x))


def _silu(x):
    return x * _sigmoid(x)


def _dsilu(x):
    s = _sigmoid(x)
    return s * (1.0 + x * (1.0 - s))


def _row(tm, d):
    return pl.BlockSpec((tm, d), lambda j: (j, 0))


def _seg(d):
    return pl.BlockSpec((None, 1, d), lambda j: (jnp.minimum(j, 1), 0, 0))


def _vec(d):
    return pl.BlockSpec((1, d), lambda j: (0, 0))


def _colblk(tm, width, blk):
    return pl.BlockSpec((tm, width), lambda j: (j, blk))


def _seg_acc(d):
    return pl.BlockSpec((None, SUBLANES, d), lambda j: (jnp.minimum(j, 1), 0, 0))


def _ln_stats(z):
    mu = jnp.mean(z, axis=-1, keepdims=True)
    zc = z - mu
    var = jnp.mean(zc * zc, axis=-1, keepdims=True)
    rstd = lax.rsqrt(var + LN_EPS)
    return zc * rstd, rstd


def _ln_bwd(dxh, xhat, rstd):
    m1 = jnp.mean(dxh, axis=-1, keepdims=True)
    m2 = jnp.mean(dxh * xhat, axis=-1, keepdims=True)
    return rstd * (dxh - m1 - xhat * m2)


def _colsum(v):
    return jnp.sum(v, axis=0, keepdims=True)


def mod_fwd(xc, scale2, shift2, tm):
    t, d = xc.shape

    def body(x_ref, sc_ref, sh_ref, h_ref):
        h_ref[...] = (x_ref[...] * (1.0 + sc_ref[...]) + sh_ref[...]).astype(h_ref.dtype)

    return pl.pallas_call(
        body, name="mod_fwd", grid=(t // tm,),
        in_specs=[_row(tm, d), _seg(d), _seg(d)], out_specs=_row(tm, d),
        out_shape=_sds((t, d), BF16), compiler_params=_cp("parallel"))(xc, scale2, shift2)


def post_ln_fwd(xc, y, gate2, g, b, alpha, tm):
    t, d = xc.shape

    def body(x_ref, y_ref, gt_ref, g_ref, b_ref, o_ref):
        z = alpha * x_ref[...] + gt_ref[...] * y_ref[...]
        xhat, _ = _ln_stats(z)
        o_ref[...] = xhat * g_ref[...] + b_ref[...]

    return pl.pallas_call(
        body, name="post_ln_fwd", grid=(t // tm,),
        in_specs=[_row(tm, d), _row(tm, d), _seg(d), _vec(d), _vec(d)], out_specs=_row(tm, d),
        out_shape=_sds((t, d), F32), compiler_params=_cp("parallel"))(xc, y, gate2, g, b)


def post_ln_bwd(dout, xc, y, gate2, g, alpha, tm):
    t, d = xc.shape

    def body(do_ref, x_ref, y_ref, gt_ref, g_ref, dzx_ref, dy_ref, acc_ref):
        @pl.when(pl.program_id(0) <= 1)
        def _():
            acc_ref[...] = jnp.zeros_like(acc_ref)

        yv = y_ref[...]
        gate = gt_ref[...]
        xhat, rstd = _ln_stats(alpha * x_ref[...] + gate * yv)
        dout = do_ref[...]
        dz = _ln_bwd(dout * g_ref[...], xhat, rstd)
        dzx_ref[...] = alpha * dz
        dy_ref[...] = (gate * dz).astype(dy_ref.dtype)
        acc_ref[0:1, :] += _colsum(dz * yv)
        acc_ref[1:2, :] += _colsum(dout * xhat)
        acc_ref[2:3, :] += _colsum(dout)

    return pl.pallas_call(
        body, name="post_ln_bwd", grid=(t // tm,),
        in_specs=[_row(tm, d), _row(tm, d), _row(tm, d), _seg(d), _vec(d)],
        out_specs=[_row(tm, d), _row(tm, d), _seg_acc(d)],
        out_shape=[_sds((t, d), F32), _sds((t, d), BF16), _sds((2, SUBLANES, d), F32)],
        compiler_params=_cp("arbitrary"))(dout, xc, y, gate2, g)


def mod_bwd(dh, dzx, xc, scale2, tm):
    t, d = xc.shape

    def body(dh_ref, dzx_ref, x_ref, sc_ref, dx_ref, acc_ref):
        @pl.when(pl.program_id(0) <= 1)
        def _():
            acc_ref[...] = jnp.zeros_like(acc_ref)

        dhv = dh_ref[...].astype(F32)
        dx_ref[...] = dzx_ref[...] + dhv * (1.0 + sc_ref[...])
        acc_ref[0:1, :] += _colsum(dhv)
        acc_ref[1:2, :] += _colsum(dhv * x_ref[...])

    return pl.pallas_call(
        body, name="mod_bwd", grid=(t // tm,),
        in_specs=[_row(tm, d), _row(tm, d), _row(tm, d), _seg(d)],
        out_specs=[_row(tm, d), _seg_acc(d)],
        out_shape=[_sds((t, d), F32), _sds((2, SUBLANES, d), F32)],
        compiler_params=_cp("arbitrary"))(dh, dzx, xc, scale2)


def loss_head(xc, target, lc, tm):
    t, d = xc.shape

    def body(x_ref, t_ref, dx_ref, acc_ref):
        j = pl.program_id(0)

        @pl.when(j == 0)
        def _():
            acc_ref[...] = jnp.zeros_like(acc_ref)
            dx_ref[...] = jnp.zeros_like(dx_ref)

        @pl.when(j > 0)
        def _():
            err = x_ref[...] - t_ref[...]
            dx_ref[...] = err * (1.0 / d)
            col = _colsum(err * err)
            tot = col[:, 0:LANES]
            for k in range(1, d // LANES):
                tot = tot + col[:, k * LANES:(k + 1) * LANES]
            acc_ref[0:1, :] += tot

    nlc = lc // tm
    return pl.pallas_call(
        body, name="loss_head", grid=(t // tm,),
        in_specs=[_row(tm, d), pl.BlockSpec((tm, d), lambda j: (jnp.maximum(j - nlc, 0), 0))],
        out_specs=[_row(tm, d), pl.BlockSpec((SUBLANES, LANES), lambda j: (0, 0))],
        out_shape=[_sds((t, d), F32), _sds((SUBLANES, LANES), F32)],
        compiler_params=_cp("arbitrary"))(xc, target)


def mm_nn(a, w3, out_dtype):
    m, k = a.shape
    ns, _, nl = w3.shape
    tm = _pick(m, (768, 512, 256, 128))
    tn = _pick(nl, (512, 256, 128))
    npj = nl // tn

    def body(a_ref, w_ref, o_ref):
        o_ref[...] = jnp.dot(a_ref[...], w_ref[...], preferred_element_type=F32).astype(o_ref.dtype)

    return pl.pallas_call(
        body, name="mm_nn", grid=(m // tm, ns * npj),
        in_specs=[pl.BlockSpec((tm, k), lambda i, j: (i, 0)),
                  pl.BlockSpec((None, k, tn), lambda i, j: (j // npj, 0, j % npj))],
        out_specs=pl.BlockSpec((tm, tn), lambda i, j: (i, j)),
        out_shape=_sds((m, ns * nl), out_dtype), compiler_params=_cp("parallel", "parallel"))(a, w3)


def mm_nt(a, w3, out_dtype):
    m, _ = a.shape
    ns, k, nl = w3.shape
    tm = _pick(m, (768, 512, 256, 128))
    tk = _pick(k, (2048, 1024, 512, 256, 128))
    tn = _pick(nl, (512, 256, 128))
    npj = nl // tn
    nsteps = ns * npj

    def body(a_ref, w_ref, o_ref, acc_ref):
        n = pl.program_id(2)

        @pl.when(n == 0)
        def _():
            acc_ref[...] = jnp.zeros_like(acc_ref)

        acc_ref[...] += lax.dot_general(a_ref[...], w_ref[...], (((1,), (1,)), ((), ())),
                                        preferred_element_type=F32)

        @pl.when(n == nsteps - 1)
        def _():
            o_ref[...] = acc_ref[...].astype(o_ref.dtype)

    return pl.pallas_call(
        body, name="mm_nt", grid=(m // tm, k // tk, nsteps),
        in_specs=[pl.BlockSpec((tm, tn), lambda i, kk, n: (i, n)),
                  pl.BlockSpec((None, tk, tn), lambda i, kk, n: (n // npj, kk, n % npj))],
        out_specs=pl.BlockSpec((tm, tk), lambda i, kk, n: (i, kk)),
        out_shape=_sds((m, k), out_dtype), scratch_shapes=[pltpu.VMEM((tm, tk), F32)],
        compiler_params=_cp("parallel", "parallel", "arbitrary"))(a, w3)


def mm_tn(a, b, ns):
    m, k = a.shape
    nl = b.shape[1] // ns
    tm = _pick(m, (768, 512, 256, 128))
    tk = _pick(k, (1024, 512, 256, 128))
    tn = _pick(nl, (512, 256, 128))
    npj = nl // tn
    nsteps = m // tm

    def body(a_ref, b_ref, o_ref, ob_ref):
        r = pl.program_id(2)

        @pl.when(r == 0)
        def _():
            o_ref[...] = jnp.zeros_like(o_ref)

        o_ref[...] += lax.dot_general(a_ref[...], b_ref[...], (((0,), (0,)), ((), ())),
                                      preferred_element_type=F32)

        @pl.when(r == nsteps - 1)
        def _():
            ob_ref[...] = o_ref[...].astype(ob_ref.dtype)

    out_spec = pl.BlockSpec((None, tk, tn), lambda i, j, r: (j // npj, i, j % npj))
    return pl.pallas_call(
        body, name="mm_tn", grid=(k // tk, ns * npj, nsteps),
        in_specs=[pl.BlockSpec((tm, tk), lambda i, j, r: (r, i)),
                  pl.BlockSpec((tm, tn), lambda i, j, r: (r, j))],
        out_specs=[out_spec, out_spec],
        out_shape=[_sds((ns, k, nl), F32), _sds((ns, k, nl), BF16)],
        compiler_params=_cp("parallel", "parallel", "arbitrary"))(a, b)


def _win_start(j, ncc):
    return pl.multiple_of(j * CONV_ROWS + jnp.where(j >= ncc, HALO, 0), SUBLANES)


def _tok_start(j):
    return pl.multiple_of(j * CONV_ROWS, CONV_ROWS)


def _shifted(xw, off):
    n = xw.shape[0]
    sh = (n - off) % n
    y = pltpu.roll(xw, sh, 0) if sh else xw
    return y[:CONV_ROWS]


def _conv_fwd(xw, w_ref, ntaps):
    pad = ntaps // 2
    acc = None
    for k in range(ntaps):
        term = w_ref[k:k + 1, :] * _shifted(xw, HALO + k - pad)
        acc = term if acc is None else acc + term
    return acc


def _conv_bwd_data(xw, w_ref, ntaps):
    pad = ntaps // 2
    acc = None
    for k in range(ntaps):
        term = w_ref[k:k + 1, :] * _shifted(xw, HALO - k + pad)
        acc = term if acc is None else acc + term
    return acc


def _conv_bwd_weight(dw_ref, d, xw, ntaps):
    pad = ntaps // 2
    for k in range(ntaps):
        dw_ref[k:k + 1, :] += _colsum(d * _shifted(xw, HALO + k - pad))


def _zero_halos(pad_ref, lc, t):
    z = jnp.zeros((HALO, LANES), F32)
    pad_ref[0:HALO, :] = z
    pad_ref[HALO + lc:2 * HALO + lc, :] = z
    pad_ref[2 * HALO + t:3 * HALO + t, :] = z


def _pad_dst(j, ncc):
    return pl.multiple_of(j * CONV_ROWS + HALO + jnp.where(j >= ncc, HALO, 0), SUBLANES)


def _chan(t, blk0):
    return pl.BlockSpec((t, LANES), lambda ct: (0, blk0 + ct))


def _tapw(rows):
    return pl.BlockSpec((rows, LANES), lambda ct: (0, ct))


def conv_a_fwd(p, w_pad, bias, lc, d):
    t = p.shape[0]
    nct, nch, ncc = d // LANES, t // CONV_ROWS, lc // CONV_ROWS

    def body(av_ref, ag_ref, w_ref, b_ref, u1_ref, pad_ref):
        _zero_halos(pad_ref, lc, t)

        def fill(j, carry):
            rows = pl.ds(_tok_start(j), CONV_ROWS)
            u0 = av_ref[rows, :].astype(F32) * _sigmoid(ag_ref[rows, :].astype(F32))
            pad_ref[pl.ds(_pad_dst(j, ncc), CONV_ROWS), :] = u0
            return carry

        lax.fori_loop(0, nch, fill, 0)

        def conv(j, carry):
            xw = pad_ref[pl.ds(_win_start(j, ncc), CONV_ROWS + 2 * HALO), :]
            u1_ref[pl.ds(_tok_start(j), CONV_ROWS), :] = _conv_fwd(xw, w_ref, CONV_A_TAPS) + b_ref[...]
            return carry

        lax.fori_loop(0, nch, conv, 0)

    return pl.pallas_call(
        body, name="conv_a_fwd", grid=(nct,),
        in_specs=[_chan(t, 0), _chan(t, nct), _tapw(32), _tapw(1)],
        out_specs=_chan(t, 0), out_shape=_sds((t, d), F32),
        scratch_shapes=[pltpu.VMEM((t + 3 * HALO, LANES), F32)],
        compiler_params=_cp("parallel"))(p, p, w_pad, bias)


def conv_b_fwd(p, w_pad, lc, d):
    t = p.shape[0]
    nct, nch, ncc = d // LANES, t // CONV_ROWS, lc // CONV_ROWS

    def body(bx_ref, bb_ref, bc_ref, bg_ref, w_ref, o_ref, pad_ref):
        _zero_halos(pad_ref, lc, t)

        def fill(j, carry):
            rows = pl.ds(_tok_start(j), CONV_ROWS)
            pad_ref[pl.ds(_pad_dst(j, ncc), CONV_ROWS), :] = (
                bc_ref[rows, :].astype(F32) * bx_ref[rows, :].astype(F32))
            return carry

        lax.fori_loop(0, nch, fill, 0)

        def conv(j, carry):
            rows = pl.ds(_tok_start(j), CONV_ROWS)
            xw = pad_ref[pl.ds(_win_start(j, ncc), CONV_ROWS + 2 * HALO), :]
            v = _conv_fwd(xw, w_ref, CONV_B_TAPS)
            o_ref[rows, :] = (bb_ref[rows, :].astype(F32) * v
                              * _silu(bg_ref[rows, :].astype(F32))).astype(o_ref.dtype)
            return carry

        lax.fori_loop(0, nch, conv, 0)

    return pl.pallas_call(
        body, name="conv_b_fwd", grid=(nct,),
        in_specs=[_chan(t, 3 * nct), _chan(t, 4 * nct), _chan(t, 5 * nct), _chan(t, 6 * nct), _tapw(8)],
        out_specs=_chan(t, 0), out_shape=_sds((t, d), BF16),
        scratch_shapes=[pltpu.VMEM((t + 3 * HALO, LANES), F32)],
        compiler_params=_cp("parallel"))(p, p, p, p, w_pad)


def ln_a_fwd(u1, p, g, b, tm):
    t, d = u1.shape

    def body(u_ref, ag_ref, g_ref, b_ref, o_ref):
        xhat, _ = _ln_stats(u_ref[...])
        u2 = xhat * g_ref[...] + b_ref[...]
        o_ref[...] = (_silu(u2) * _silu(ag_ref[...].astype(F32))).astype(o_ref.dtype)

    return pl.pallas_call(
        body, name="ln_a_fwd", grid=(t // tm,),
        in_specs=[_row(tm, d), _colblk(tm, d, 2), _vec(d), _vec(d)], out_specs=_row(tm, d),
        out_shape=_sds((t, d), BF16), compiler_params=_cp("parallel"))(u1, p, g, b)


def ln_a_bwd(dab, u1, p, g, b, tm):
    t, d = u1.shape

    def body(da_ref, u_ref, ag_ref, g_ref, b_ref, du_ref, dag_ref, acc_ref):
        @pl.when(pl.program_id(0) == 0)
        def _():
            acc_ref[...] = jnp.zeros_like(acc_ref)

        xhat, rstd = _ln_stats(u_ref[...])
        u2 = xhat * g_ref[...] + b_ref[...]
        ag = ag_ref[...].astype(F32)
        da = da_ref[...].astype(F32)
        dag_ref[...] = (da * _silu(u2) * _dsilu(ag)).astype(dag_ref.dtype)
        du2 = da * _silu(ag) * _dsilu(u2)
        du1 = _ln_bwd(du2 * g_ref[...], xhat, rstd)
        du_ref[...] = du1
        acc_ref[0:1, :] += _colsum(du2 * xhat)
        acc_ref[1:2, :] += _colsum(du2)
        acc_ref[2:3, :] += _colsum(du1)

    return pl.pallas_call(
        body, name="ln_a_bwd", grid=(t // tm,),
        in_specs=[_colblk(tm, d, 0), _row(tm, d), _colblk(tm, d, 2), _vec(d), _vec(d)],
        out_specs=[_row(tm, d), _row(tm, d), pl.BlockSpec((SUBLANES, d), lambda j: (0, 0))],
        out_shape=[_sds((t, d), F32), _sds((t, d), BF16), _sds((SUBLANES, d), F32)],
        compiler_params=_cp("arbitrary"))(dab, u1, p, g, b)


def conv_a_bwd(du1, p, w_pad, lc, d):
    t = p.shape[0]
    nct, nch, ncc = d // LANES, t // CONV_ROWS, lc // CONV_ROWS

    def body(du_ref, av_ref, ag_ref, w_ref, dav_ref, dag_ref, dw_ref, pad_u, pad_d):
        _zero_halos(pad_u, lc, t)
        _zero_halos(pad_d, lc, t)
        dw_ref[...] = jnp.zeros_like(dw_ref)

        def fill(j, carry):
            rows = pl.ds(_tok_start(j), CONV_ROWS)
            dst = pl.ds(_pad_dst(j, ncc), CONV_ROWS)
            pad_u[dst, :] = av_ref[rows, :].astype(F32) * _sigmoid(ag_ref[rows, :].astype(F32))
            pad_d[dst, :] = du_ref[rows, :]
            return carry

        lax.fori_loop(0, nch, fill, 0)

        def step(j, carry):
            rows = pl.ds(_tok_start(j), CONV_ROWS)
            win = pl.ds(_win_start(j, ncc), CONV_ROWS + 2 * HALO)
            du0 = _conv_bwd_data(pad_d[win, :], w_ref, CONV_A_TAPS)
            sig = _sigmoid(ag_ref[rows, :].astype(F32))
            dav_ref[rows, :] = (du0 * sig).astype(dav_ref.dtype)
            dag_ref[rows, :] = (du0 * av_ref[rows, :].astype(F32) * sig * (1.0 - sig)).astype(dag_ref.dtype)
            _conv_bwd_weight(dw_ref, du_ref[rows, :], pad_u[win, :], CONV_A_TAPS)
            return carry

        lax.fori_loop(0, nch, step, 0)

    return pl.pallas_call(
        body, name="conv_a_bwd", grid=(nct,),
        in_specs=[_chan(t, 0), _chan(t, 0), _chan(t, nct), _tapw(32)],
        out_specs=[_chan(t, 0), _chan(t, 0), _tapw(32)],
        out_shape=[_sds((t, d), BF16), _sds((t, d), BF16), _sds((32, d), F32)],
        scratch_shapes=[pltpu.VMEM((t + 3 * HALO, LANES), F32), pltpu.VMEM((t + 3 * HALO, LANES), F32)],
        compiler_params=_cp("parallel"))(du1, p, p, w_pad)


def conv_b_bwd(dab, p, w_pad, lc, d):
    t = p.shape[0]
    nct, nch, ncc = d // LANES, t // CONV_ROWS, lc // CONV_ROWS

    def body(db_ref, bx_ref, bb_ref, bc_ref, bg_ref, w_ref,
             dbx_ref, dbb_ref, dbc_ref, dbg_ref, dw_ref, pad_t, pad_d):
        _zero_halos(pad_t, lc, t)
        _zero_halos(pad_d, lc, t)
        dw_ref[...] = jnp.zeros_like(dw_ref)

        def fill(j, carry):
            rows = pl.ds(_tok_start(j), CONV_ROWS)
            pad_t[pl.ds(_pad_dst(j, ncc), CONV_ROWS), :] = (
                bc_ref[rows, :].astype(F32) * bx_ref[rows, :].astype(F32))
            return carry

        lax.fori_loop(0, nch, fill, 0)

        def first(j, carry):
            rows = pl.ds(_tok_start(j), CONV_ROWS)
            xw = pad_t[pl.ds(_win_start(j, ncc), CONV_ROWS + 2 * HALO), :]
            v = _conv_fwd(xw, w_ref, CONV_B_TAPS)
            bg = bg_ref[rows, :].astype(F32)
            bb = bb_ref[rows, :].astype(F32)
            db = db_ref[rows, :].astype(F32)
            sg = _silu(bg)
            dbb_ref[rows, :] = (db * v * sg).astype(dbb_ref.dtype)
            dbg_ref[rows, :] = (db * bb * v * _dsilu(bg)).astype(dbg_ref.dtype)
            dv = db * bb * sg
            pad_d[pl.ds(_pad_dst(j, ncc), CONV_ROWS), :] = dv
            _conv_bwd_weight(dw_ref, dv, xw, CONV_B_TAPS)
            return carry

        lax.fori_loop(0, nch, first, 0)

        def second(j, carry):
            rows = pl.ds(_tok_start(j), CONV_ROWS)
            dt = _conv_bwd_data(pad_d[pl.ds(_win_start(j, ncc), CONV_ROWS + 2 * HALO), :], w_ref, CONV_B_TAPS)
            dbc_ref[rows, :] = (dt * bx_ref[rows, :].astype(F32)).astype(dbc_ref.dtype)
            dbx_ref[rows, :] = (dt * bc_ref[rows, :].astype(F32)).astype(dbx_ref.dtype)
            return carry

        lax.fori_loop(0, nch, second, 0)

    return pl.pallas_call(
        body, name="conv_b_bwd", grid=(nct,),
        in_specs=[_chan(t, nct), _chan(t, 3 * nct), _chan(t, 4 * nct), _chan(t, 5 * nct), _chan(t, 6 * nct),
                  _tapw(8)],
        out_specs=[_chan(t, 0)] * 4 + [_tapw(8)],
        out_shape=[_sds((t, d), BF16)] * 4 + [_sds((8, d), F32)],
        scratch_shapes=[pltpu.VMEM((t + 3 * HALO, LANES), F32), pltpu.VMEM((t + 3 * HALO, LANES), F32)],
        compiler_params=_cp("parallel"))(dab, p, p, p, p, w_pad)


def _swap_halves(z, first_half):
    return jnp.where(first_half, pltpu.roll(z, 96, 1), pltpu.roll(z, 32, 1))


def _first_half_mask(rows):
    lane = lax.broadcasted_iota(jnp.int32, (rows, HEAD_DIM), 1)
    return (lane & 32) == 0


def qk_fwd(qgkv, qg, kg, cos_t, sin_t, ad, kvd, tm, qscale):
    t = qgkv.shape[0]

    def body(q_ref, k_ref, qg_ref, kg_ref, c_ref, s_ref, qo_ref, ko_ref):
        first = _first_half_mask(tm)
        cosv, sinv = c_ref[...], s_ref[...]

        def head(x, gain):
            inv = lax.rsqrt(jnp.mean(x * x, axis=-1, keepdims=True) + RMS_EPS)
            yv = x * inv * gain
            return yv * cosv + _swap_halves(yv, first) * sinv

        for h in range(ad // HEAD_DIM):
            sl = slice(h * HEAD_DIM, (h + 1) * HEAD_DIM)
            qo_ref[:, sl] = (head(q_ref[:, sl].astype(F32), qg_ref[...]) * qscale).astype(qo_ref.dtype)
        for h in range(kvd // HEAD_DIM):
            sl = slice(h * HEAD_DIM, (h + 1) * HEAD_DIM)
            ko_ref[:, sl] = head(k_ref[:, sl].astype(F32), kg_ref[...]).astype(ko_ref.dtype)

    return pl.pallas_call(
        body, name="qk_fwd", grid=(t // tm,),
        in_specs=[_colblk(tm, ad, 0), _colblk(tm, kvd, 2 * ad // kvd), _vec(HEAD_DIM), _vec(HEAD_DIM),
                  _row(tm, HEAD_DIM), _row(tm, HEAD_DIM)],
        out_specs=[_row(tm, ad), _row(tm, kvd)],
        out_shape=[_sds((t, ad), BF16), _sds((t, kvd), BF16)],
        compiler_params=_cp("parallel"))(qgkv, qgkv, qg, kg, cos_t, sin_t)


def qk_bwd(dqr, dkr, qgkv, qg, kg, cos_t, sin_t, ad, kvd, tm):
    t = qgkv.shape[0]

    def body(dq_ref, dk_ref, q_ref, k_ref, qg_ref, kg_ref, c_ref, s_ref, dqo_ref, dko_ref, acc_ref):
        @pl.when(pl.program_id(0) == 0)
        def _():
            acc_ref[...] = jnp.zeros_like(acc_ref)

        first = _first_half_mask(tm)
        cosv, sinv = c_ref[...], s_ref[...]

        def head(x, gain, dout):
            inv = lax.rsqrt(jnp.mean(x * x, axis=-1, keepdims=True) + RMS_EPS)
            xn = x * inv
            dy = dout * cosv + _swap_halves(dout * sinv, first)
            dxn = dy * gain
            dx = inv * (dxn - xn * jnp.mean(dxn * xn, axis=-1, keepdims=True))
            return dx, _colsum(dy * xn)

        dqg = jnp.zeros((1, HEAD_DIM), F32)
        for h in range(ad // HEAD_DIM):
            sl = slice(h * HEAD_DIM, (h + 1) * HEAD_DIM)
            dx, dg = head(q_ref[:, sl].astype(F32), qg_ref[...], dq_ref[:, sl])
            dqo_ref[:, sl] = dx.astype(dqo_ref.dtype)
            dqg = dqg + dg
        dkg = jnp.zeros((1, HEAD_DIM), F32)
        for h in range(kvd // HEAD_DIM):
            sl = slice(h * HEAD_DIM, (h + 1) * HEAD_DIM)
            dx, dg = head(k_ref[:, sl].astype(F32), kg_ref[...], dk_ref[:, sl])
            dko_ref[:, sl] = dx.astype(dko_ref.dtype)
            dkg = dkg + dg
        acc_ref[0:1, :] += dqg
        acc_ref[1:2, :] += dkg

    return pl.pallas_call(
        body, name="qk_bwd", grid=(t // tm,),
        in_specs=[_row(tm, ad), _row(tm, kvd), _colblk(tm, ad, 0), _colblk(tm, kvd, 2 * ad // kvd),
                  _vec(HEAD_DIM), _vec(HEAD_DIM), _row(tm, HEAD_DIM), _row(tm, HEAD_DIM)],
        out_specs=[_row(tm, ad), _row(tm, kvd), pl.BlockSpec((SUBLANES, HEAD_DIM), lambda j: (0, 0))],
        out_shape=[_sds((t, ad), BF16), _sds((t, kvd), BF16), _sds((SUBLANES, HEAD_DIM), F32)],
        compiler_params=_cp("arbitrary"))(dqr, dkr, qgkv, qgkv, qg, kg, cos_t, sin_t)


_NT = (((1,), (1,)), ((), ()))


def _chunks_t(a, nkv, tm):
    t = a.shape[0]
    return a.reshape(t // tm, tm, nkv, HEAD_DIM).transpose(2, 0, 3, 1)


def _tree_rows(x, op):
    slabs = [x[i:i + SUBLANES] for i in range(0, x.shape[0], SUBLANES)]
    while len(slabs) > 1:
        slabs = [op(slabs[i], slabs[i + 1]) for i in range(0, len(slabs), 2)]
    return slabs[0]


def flash_fwd(qr, kr, vt, qgkv, ad, kvd, tm):
    t = qr.shape[0]
    nkv = kvd // HEAD_DIM
    gw = ad // nkv
    hpg = gw // HEAD_DIM
    nt = t // tm
    gate_blk0 = ad // gw

    def body(q_ref, k_ref, vt_ref, g_ref, o_ref, og_ref, lse_ref):
        qi = pl.program_id(1)
        nkc = jnp.where(qi == 0, 1, nt)
        heads = [slice(h * HEAD_DIM, (h + 1) * HEAD_DIM) for h in range(hpg)]
        qs = [q_ref[:, sl] for sl in heads]

        def step(c, carry):
            kc = k_ref[pl.ds(pl.multiple_of(c * tm, tm), tm), :]
            vtc = vt_ref[c]
            sts = [lax.dot_general(kc, q, _NT, preferred_element_type=F32) for q in qs]
            out = []
            for h in range(hpg):
                m, l, acc = carry[h]
                m_new = jnp.maximum(m, jnp.max(_tree_rows(sts[h], jnp.maximum), axis=0, keepdims=True))
                a = jnp.exp2(m - m_new)
                pt = jnp.exp2(sts[h] - m_new)
                l = a * l + jnp.sum(_tree_rows(pt, jnp.add), axis=0, keepdims=True)
                acc = a * acc + jnp.dot(vtc, pt.astype(BF16), preferred_element_type=F32)
                out.append((m_new, l, acc))
            return tuple(out)

        init = tuple((jnp.full((1, tm), -1e30, F32), jnp.zeros((1, tm), F32), jnp.zeros((HEAD_DIM, tm), F32))
                     for _ in range(hpg))
        res = lax.fori_loop(0, nkc, step, init)
        for h, sl in enumerate(heads):
            m, l, acc = res[h]
            o = (acc / l).T
            o_ref[:, sl] = o.astype(o_ref.dtype)
            og_ref[:, sl] = (o * _silu(g_ref[:, sl].astype(F32))).astype(og_ref.dtype)
            lse_ref[h:h + 1, :] = m + jnp.log(l) * LOG2_E

    return pl.pallas_call(
        body, name="flash_fwd", grid=(nkv, nt),
        in_specs=[pl.BlockSpec((tm, gw), lambda g, i: (i, g)),
                  pl.BlockSpec((t, HEAD_DIM), lambda g, i: (0, g)),
                  pl.BlockSpec((None, nt, HEAD_DIM, tm), lambda g, i: (g, 0, 0, 0)),
                  pl.BlockSpec((tm, gw), lambda g, i: (i, gate_blk0 + g))],
        out_specs=[pl.BlockSpec((tm, gw), lambda g, i: (i, g)),
                   pl.BlockSpec((tm, gw), lambda g, i: (i, g)),
                   pl.BlockSpec((None, None, hpg, tm), lambda g, i: (g, i, 0, 0))],
        out_shape=[_sds((t, ad), BF16), _sds((t, ad), BF16), _sds((nkv, nt, hpg, tm), F32)],
        compiler_params=_cp("parallel", "parallel"))(qr, kr, vt, qgkv)


def gate_bwd(dog, o, qgkv, ad, kvd, tm):
    t = o.shape[0]
    nkv = kvd // HEAD_DIM
    hpg = ad // nkv // HEAD_DIM

    def body(dog_ref, o_ref, g_ref, do_ref, dg_ref, dl_ref):
        lane = lax.broadcasted_iota(jnp.int32, (tm, LANES), 1)
        for grp in range(nkv):
            blk = jnp.zeros((tm, LANES), F32)
            for hh in range(hpg):
                h = grp * hpg + hh
                sl = slice(h * HEAD_DIM, (h + 1) * HEAD_DIM)
                dv = dog_ref[:, sl].astype(F32)
                ov = o_ref[:, sl].astype(F32)
                gv = g_ref[:, sl].astype(F32)
                doh = dv * _silu(gv)
                do_ref[:, sl] = doh.astype(do_ref.dtype)
                dg_ref[:, sl] = (dv * ov * _dsilu(gv)).astype(dg_ref.dtype)
                blk = jnp.where(lane == hh, jnp.sum(doh * ov, axis=-1, keepdims=True), blk)
            dl_ref[grp] = blk

    return pl.pallas_call(
        body, name="gate_bwd", grid=(t // tm,),
        in_specs=[_row(tm, ad), _row(tm, ad), _colblk(tm, ad, 1)],
        out_specs=[_row(tm, ad), _row(tm, ad), pl.BlockSpec((nkv, tm, LANES), lambda j: (0, j, 0))],
        out_shape=[_sds((t, ad), BF16), _sds((t, ad), BF16), _sds((nkv, t, LANES), F32)],
        compiler_params=_cp("parallel"))(dog, o, qgkv)


def flash_bwd(qr, do, kr, kt, qgkv, lse_t, delta_t, ad, kvd, tm, scale):
    t = qr.shape[0]
    nkv = kvd // HEAD_DIM
    gw = ad // nkv
    hpg = gw // HEAD_DIM
    nt = t // tm
    v_blk0 = (2 * ad + kvd) // HEAD_DIM

    def body(q_ref, do_ref, k_ref, v_ref, kt_ref, lse_ref, dl_ref, dq_ref, dk_ref, dv_ref):
        qi = pl.program_id(1)

        @pl.when(qi == 0)
        def _():
            dk_ref[...] = jnp.zeros_like(dk_ref)
            dv_ref[...] = jnp.zeros_like(dv_ref)

        nkc = jnp.where(qi == 0, 1, nt)
        heads = [slice(h * HEAD_DIM, (h + 1) * HEAD_DIM) for h in range(hpg)]
        qs = [q_ref[:, sl] for sl in heads]
        dos = [do_ref[:, sl] for sl in heads]
        lses = [lse_ref[h:h + 1, :] for h in range(hpg)]
        dls = [dl_ref[h:h + 1, :] for h in range(hpg)]

        def step(c, dqts):
            rows = pl.ds(pl.multiple_of(c * tm, tm), tm)
            kc = k_ref[rows, :]
            vc = v_ref[rows, :]
            ktc = kt_ref[c]
            dk = jnp.zeros((tm, HEAD_DIM), F32)
            dv = jnp.zeros((tm, HEAD_DIM), F32)
            out = []
            sts = [lax.dot_general(kc, q, _NT, preferred_element_type=F32) for q in qs]
            dpts = [lax.dot_general(vc, d, _NT, preferred_element_type=F32) for d in dos]
            for h in range(hpg):
                pt = jnp.exp2(sts[h] - lses[h])
                dv = dv + jnp.dot(pt.astype(BF16), dos[h], preferred_element_type=F32)
                dst = (pt * (dpts[h] - dls[h])).astype(BF16)
                dk = dk + jnp.dot(dst, qs[h], preferred_element_type=F32)
                out.append(dqts[h] + jnp.dot(ktc, dst, preferred_element_type=F32))
            dk_ref[rows, :] += dk
            dv_ref[rows, :] += dv
            return tuple(out)

        res = lax.fori_loop(0, nkc, step, tuple(jnp.zeros((HEAD_DIM, tm), F32) for _ in range(hpg)))
        for h, sl in enumerate(heads):
            dq_ref[:, sl] = res[h].T * scale

        @pl.when(qi == nt - 1)
        def _():
            dk_ref[...] = dk_ref[...] * LN_2

    return pl.pallas_call(
        body, name="flash_bwd", grid=(nkv, nt),
        in_specs=[pl.BlockSpec((tm, gw), lambda g, i: (i, g)),
                  pl.BlockSpec((tm, gw), lambda g, i: (i, g)),
                  pl.BlockSpec((t, HEAD_DIM), lambda g, i: (0, g)),
                  pl.BlockSpec((t, HEAD_DIM), lambda g, i: (0, v_blk0 + g)),
                  pl.BlockSpec((None, nt, HEAD_DIM, tm), lambda g, i: (g, 0, 0, 0)),
                  pl.BlockSpec((None, None, hpg, tm), lambda g, i: (g, i, 0, 0)),
                  pl.BlockSpec((None, None, hpg, tm), lambda g, i: (g, i, 0, 0))],
        out_specs=[pl.BlockSpec((tm, gw), lambda g, i: (i, g)),
                   pl.BlockSpec((t, HEAD_DIM), lambda g, i: (0, g)),
                   pl.BlockSpec((t, HEAD_DIM), lambda g, i: (0, g))],
        out_shape=[_sds((t, ad), F32), _sds((t, kvd), F32), _sds((t, kvd), F32)],
        compiler_params=_cp("parallel", "arbitrary"))(qr, do, kr, qgkv, kt, lse_t, delta_t)


def _rows_per_head(a, tm, hpg):
    nkv, t, _ = a.shape
    return a[:, :, :hpg].reshape(nkv, t // tm, tm, hpg).transpose(0, 1, 3, 2)


def adaln_fwd(c16, w_mod):
    nlay, d, nl = w_mod.shape
    tn = _pick(nl, (512, 256, 128))

    def body(c_ref, w_ref, o_ref):
        o_ref[...] = jnp.dot(_silu(c_ref[...]), w_ref[...], preferred_element_type=F32,
                             precision=lax.Precision.HIGHEST)

    return pl.pallas_call(
        body, name="adaln_fwd", grid=(nlay, nl // tn),
        in_specs=[pl.BlockSpec((16, d), lambda l, j: (0, 0)),
                  pl.BlockSpec((None, d, tn), lambda l, j: (l, 0, j))],
        out_specs=pl.BlockSpec((None, 16, tn), lambda l, j: (l, 0, j)),
        out_shape=_sds((nlay, 16, nl), F32), compiler_params=_cp("parallel", "parallel"))(c16, w_mod)


def adaln_bwd(c16t, dm, w_mod):
    nlay, d, nl = w_mod.shape
    tn = _pick(nl, (512, 256, 128))

    def body(c_ref, dm_ref, w_ref, dw_ref, dc_ref):
        @pl.when(pl.program_id(1) == 0)
        def _():
            dc_ref[...] = jnp.zeros_like(dc_ref)

        dmv = dm_ref[...]
        dw_ref[...] = jnp.dot(_silu(c_ref[...]), dmv, preferred_element_type=F32,
                              precision=lax.Precision.HIGHEST)
        dc_ref[...] += lax.dot_general(dmv, w_ref[...], _NT, preferred_element_type=F32,
                                       precision=lax.Precision.HIGHEST)

    return pl.pallas_call(
        body, name="adaln_bwd", grid=(nlay, nl // tn),
        in_specs=[pl.BlockSpec((d, 16), lambda l, j: (0, 0)),
                  pl.BlockSpec((None, 16, tn), lambda l, j: (l, 0, j)),
                  pl.BlockSpec((None, d, tn), lambda l, j: (l, 0, j))],
        out_specs=[pl.BlockSpec((None, d, tn), lambda l, j: (l, 0, j)),
                   pl.BlockSpec((None, 16, d), lambda l, j: (l, 0, 0))],
        out_shape=[_sds((nlay, d, nl), F32), _sds((nlay, 16, d), F32)],
        compiler_params=_cp("parallel", "arbitrary"))(c16t, dm, w_mod)


def sum_leading(a, name):
    n = a.shape[0]

    def body(a_ref, o_ref):
        acc = a_ref[0]
        for i in range(1, n):
            acc = acc + a_ref[i]
        o_ref[...] = acc

    return pl.pallas_call(body, name=name, in_specs=[VMEM_FULL], out_specs=VMEM_FULL,
                          out_shape=_sds(a.shape[1:], F32),
                          compiler_params=pltpu.CompilerParams(vmem_limit_bytes=VMEM_LIMIT))(a)


def c_ctx_grad(parts, c_ctx):
    n = parts.shape[0]

    def body(p_ref, c_ref, o_ref):
        acc = p_ref[0]
        for i in range(1, n):
            acc = acc + p_ref[i]
        o_ref[...] = acc * _dsilu(c_ref[...])

    return pl.pallas_call(body, name="c_ctx_grad", in_specs=[VMEM_FULL, VMEM_FULL], out_specs=VMEM_FULL,
                          out_shape=_sds(c_ctx.shape, F32))(parts, c_ctx)


def _as2d(a):
    return a.reshape(-1, a.shape[-1])


def _row_tile(r, c):
    for tr in (1024, 512, 256, 128, 64, 32, 16, 8):
        if r % tr == 0 and tr * c * 4 <= (1 << 20):
            return tr
    return r


def add_n(items, name, out_dtypes=(F32,)):
    norm = [it if isinstance(it, tuple) else (it, ()) for it in items]
    shape = norm[0][0].shape[len(norm[0][1]):]
    r, c = _as2d(jnp.zeros(shape, BF16)).shape
    tr = _row_tile(r, c)
    dyn, specs, flat = [], [], []
    for arr, lead in norm:
        slots = []
        for ix in lead:
            if isinstance(ix, int):
                slots.append(ix)
            else:
                slots.append((len(dyn),))
                dyn.append(ix)

        def imap(i, s, slots=tuple(slots)):
            return tuple(s[k[0]] if isinstance(k, tuple) else k for k in slots) + (i, 0)

        specs.append(pl.BlockSpec((None,) * len(lead) + (tr, c), imap))
        flat.append(arr.reshape(arr.shape[:len(lead)] + (r, c)))
    sel = jnp.stack([jnp.asarray(v, jnp.int32) for v in dyn]) if dyn else jnp.zeros((1,), jnp.int32)
    n_in = len(flat)

    def body(s_ref, *refs):
        acc = refs[0][...].astype(F32)
        for ref in refs[1:n_in]:
            acc = acc + ref[...].astype(F32)
        for ref in refs[n_in:]:
            ref[...] = acc.astype(ref.dtype)

    out_spec = pl.BlockSpec((tr, c), lambda i, s: (i, 0))
    outs = pl.pallas_call(
        body, name=name,
        grid_spec=pltpu.PrefetchScalarGridSpec(num_scalar_prefetch=1, grid=(r // tr,), in_specs=specs,
                                               out_specs=[out_spec] * len(out_dtypes)),
        out_shape=[_sds((r, c), dt) for dt in out_dtypes], compiler_params=_cp("parallel"))(sel, *flat)
    return [o.reshape(shape) for o in outs]


def pair_add(p0, p1, got, which, name):
    shape = p0.shape
    flat = [_as2d(a) for a in (p0, p1, got)]
    r, c = flat[0].shape
    tr = _row_tile(r, c)

    def body(s_ref, p0_ref, p1_ref, g_ref, o_ref):
        own = jnp.where(s_ref[0] == 0, p0_ref[...], p1_ref[...])
        o_ref[...] = (own + g_ref[...].astype(F32)).astype(o_ref.dtype)

    spec = pl.BlockSpec((tr, c), lambda i, s: (i, 0))
    out = pl.pallas_call(
        body, name=name,
        grid_spec=pltpu.PrefetchScalarGridSpec(num_scalar_prefetch=1, grid=(r // tr,), in_specs=[spec] * 3,
                                               out_specs=spec),
        out_shape=_sds((r, c), BF16), compiler_params=_cp("parallel"))(
            jnp.asarray(which, jnp.int32).reshape(1), *flat)
    return out.reshape(shape)


def adamw(w, g, m, v):
    shape = w.shape
    flat = [_as2d(a.reshape((1,) + shape) if len(shape) == 1 else a) for a in (w, g, m, v)]
    r, c = flat[0].shape
    tr = _row_tile(r, c)
    c1 = 1.0 - ADAM_B1 ** ADAM_STEP
    c2 = 1.0 - ADAM_B2 ** ADAM_STEP

    def body(w_ref, g_ref, m_ref, v_ref, d_ref, nm_ref, nv_ref):
        gv = g_ref[...]
        nm = ADAM_B1 * m_ref[...] + (1.0 - ADAM_B1) * gv
        nv = ADAM_B2 * v_ref[...] + (1.0 - ADAM_B2) * (gv * gv)
        d_ref[...] = -ADAM_LR * ((nm / c1) / (jnp.sqrt(nv / c2) + ADAM_EPS) + ADAM_WD * w_ref[...])
        nm_ref[...] = nm
        nv_ref[...] = nv

    spec = pl.BlockSpec((tr, c), lambda i: (i, 0))
    outs = pl.pallas_call(
        body, name="adamw", grid=(r // tr,), in_specs=[spec] * 4, out_specs=[spec] * 3,
        out_shape=[_sds((r, c), F32)] * 3, compiler_params=_cp("parallel"))(*flat)
    return tuple(o.reshape(shape) for o in outs)


def _place():
    return lax.axis_index("x"), lax.axis_index("y"), lax.axis_index("c")


def _remote(src, dst, ssem, rsem, dev):
    return pltpu.make_async_remote_copy(src_ref=src, dst_ref=dst, send_sem=ssem, recv_sem=rsem,
                                        device_id=dev, device_id_type=MESH)


def all_gather8(v, name):
    m_per, n = v.shape

    def body(x_ref, out_ref, send_sems, recv_sems, local_sem):
        x, y, c = _place()
        me, sibling = (x, y, c), (x, y, 1 - c)
        chips = [(1 - x, y), (x, 1 - y), (1 - x, 1 - y)]

        def rows(px, py, pc):
            return out_ref.at[pl.ds((4 * px + 2 * py + pc) * m_per, m_per), :]

        def copy(k, block, to, src=None):
            return _remote(rows(*block) if src is None else src, rows(*block),
                           send_sems.at[k], recv_sems.at[k], to)

        mine = pltpu.make_async_copy(x_ref, rows(*me), local_sem)
        mine.start()
        first = [copy(0, me, sibling, src=x_ref)]
        first += [copy(1 + j, me, (*chip, c), src=x_ref) for j, chip in enumerate(chips)]
        for cp in first:
            cp.start()
        passed = [copy(4 + j, (*chip, c), sibling) for j, chip in enumerate(chips)]
        for j, chip in enumerate(chips):
            copy(1 + j, (*chip, c), me).wait_recv()
            passed[j].start()
        copy(0, sibling, me).wait_recv()
        for j, chip in enumerate(chips):
            copy(4 + j, (*chip, 1 - c), me).wait_recv()
        for cp in first + passed:
            cp.wait_send()
        mine.wait()

    return pl.pallas_call(
        body, name=name, out_shape=_sds((8 * m_per, n), v.dtype),
        in_specs=[VMEM_FULL], out_specs=VMEM_FULL,
        scratch_shapes=[pltpu.SemaphoreType.DMA((7,)), pltpu.SemaphoreType.DMA((7,)), pltpu.SemaphoreType.DMA],
        compiler_params=pltpu.CompilerParams(vmem_limit_bytes=VMEM_LIMIT))(v)


def _slabs(ref, n):
    rows = ref.shape[0] // n
    return [ref.at[pl.ds(i * rows, rows)] for i in range(n)]


def gather_weight(w, name):
    _, r, cdim = w.shape
    n = D2D_PARTS

    def body(w_ref, out_ref, slab_ref, send_sems, recv_sems, local_sem):
        x, y, c = _place()
        sibling = (x, y, 1 - c)
        chips = [(1 - x, y), (x, 1 - y), (1 - x, 1 - y)]
        mine = 2 * x + y
        src = _slabs(w_ref.at[c], n)
        first = []
        for k, chip in enumerate(chips):
            dst = _slabs(out_ref.at[c, mine], n)
            for j in range(n):
                cp = _remote(src[j], dst[j], send_sems.at[k * n + j], recv_sems.at[k * n + j], (*chip, c))
                cp.start()
                first.append(cp)
        for i in range(2):
            for s_in, s_out in zip(_slabs(w_ref.at[i], n), _slabs(out_ref.at[i, mine], n)):
                cp = pltpu.make_async_copy(s_in, slab_ref, local_sem)
                cp.start()
                cp.wait()
                cp = pltpu.make_async_copy(slab_ref, s_out, local_sem)
                cp.start()
                cp.wait()
        passed = []
        for j in range(n):
            for k, (px, py) in enumerate(chips):
                theirs = _slabs(out_ref.at[c, 2 * px + py], n)[j]
                _remote(src[j], theirs, send_sems.at[k * n + j], recv_sems.at[k * n + j], sibling).wait_recv()
                cp = _remote(theirs, theirs, send_sems.at[(3 + k) * n + j], recv_sems.at[(3 + k) * n + j], sibling)
                cp.start()
                passed.append(cp)
        for j in range(n):
            for k, (px, py) in enumerate(chips):
                other = _slabs(out_ref.at[1 - c, 2 * px + py], n)[j]
                _remote(other, other, send_sems.at[(3 + k) * n + j], recv_sems.at[(3 + k) * n + j],
                        sibling).wait_recv()
        for cp in first + passed:
            cp.wait_send()

    return pl.pallas_call(
        body, name=name, out_shape=_sds((2, 4, r, cdim), w.dtype), in_specs=[ANY], out_specs=ANY,
        scratch_shapes=[pltpu.VMEM((r // n, cdim), w.dtype), pltpu.SemaphoreType.DMA((6 * n,)),
                        pltpu.SemaphoreType.DMA((6 * n,)), pltpu.SemaphoreType.DMA],
        compiler_params=pltpu.CompilerParams(vmem_limit_bytes=VMEM_LIMIT))(w)


def rs_chip_exchange(gb0, gb1, name):
    _, r, cdim = gb0.shape

    def body(gb0_ref, gb1_ref, st_ref, send_sems, recv_sems):
        x, y, c = _place()
        chips = [(1 - x, y), (x, 1 - y), (1 - x, 1 - y)]
        cps = []
        for k, (px, py) in enumerate(chips):
            for i, gb_ref in enumerate((gb0_ref, gb1_ref)):
                cps.append(_remote(gb_ref.at[2 * px + py], st_ref.at[k, i], send_sems.at[2 * k + i],
                                   recv_sems.at[2 * k + i], (px, py, c)))
        for cp in cps:
            cp.start()
        for cp in cps:
            cp.wait()

    return pl.pallas_call(
        body, name=name, out_shape=_sds((3, 2, r, cdim), BF16), in_specs=[ANY] * 2, out_specs=ANY,
        scratch_shapes=[pltpu.SemaphoreType.DMA((6,)), pltpu.SemaphoreType.DMA((6,))])(gb0, gb1)


def rs_pair_swap(pb0, pb1, name):
    r, cdim = pb0.shape
    n = 2 * D2D_PARTS

    def body(pb0_ref, pb1_ref, got_ref, stage_ref, send_sems, recv_sems, local_sem):
        x, y, c = _place()
        sibling = (x, y, 1 - c)

        @pl.when(c == 0)
        def _():
            load = pltpu.make_async_copy(pb1_ref, stage_ref, local_sem)
            load.start()
            load.wait()

        @pl.when(c == 1)
        def _():
            load = pltpu.make_async_copy(pb0_ref, stage_ref, local_sem)
            load.start()
            load.wait()

        src, dst = _slabs(stage_ref, n), _slabs(got_ref, n)
        cps = [_remote(src[j], dst[j], send_sems.at[j], recv_sems.at[j], sibling) for j in range(n)]
        for cp in cps:
            cp.start()
        for cp in cps:
            cp.wait()

    return pl.pallas_call(
        body, name=name, out_shape=_sds((r, cdim), BF16), in_specs=[ANY, ANY], out_specs=VMEM_FULL,
        scratch_shapes=[pltpu.VMEM((r, cdim), BF16), pltpu.SemaphoreType.DMA((n,)), pltpu.SemaphoreType.DMA((n,)),
                        pltpu.SemaphoreType.DMA],
        compiler_params=pltpu.CompilerParams(vmem_limit_bytes=VMEM_LIMIT))(pb0, pb1)


def rs_pair_share(red, name):
    n = 2 * D2D_PARTS

    def body(red_ref, out_ref, send_sems, recv_sems, local_sem):
        x, y, c = _place()
        sibling = (x, y, 1 - c)
        keep = pltpu.make_async_copy(red_ref, out_ref.at[c], local_sem)
        keep.start()
        src, dst, got = _slabs(red_ref, n), _slabs(out_ref.at[c], n), _slabs(out_ref.at[1 - c], n)
        cps = [_remote(src[j], dst[j], send_sems.at[j], recv_sems.at[j], sibling) for j in range(n)]
        for cp in cps:
            cp.start()
        for cp in cps:
            cp.wait_send()
        for j in range(n):
            _remote(src[j], got[j], send_sems.at[j], recv_sems.at[j], sibling).wait_recv()
        keep.wait()

    return pl.pallas_call(
        body, name=name, out_shape=_sds((2,) + red.shape, red.dtype), in_specs=[VMEM_FULL], out_specs=VMEM_FULL,
        scratch_shapes=[pltpu.SemaphoreType.DMA((n,)), pltpu.SemaphoreType.DMA((n,)), pltpu.SemaphoreType.DMA],
        compiler_params=pltpu.CompilerParams(vmem_limit_bytes=VMEM_LIMIT))(red)


def reduce_scatter_grad(g0, g1, shard, core, tag):
    theirs = rs_chip_exchange(g0[1], g1[1], "rs_exchange_" + tag)
    p0, pb0 = add_n([(g0[0], (shard,)), (theirs, (0, 0)), (theirs, (1, 0)), (theirs, (2, 0))],
                    "rs_chip_add0_" + tag, (F32, BF16))
    p1, pb1 = add_n([(g1[0], (shard,)), (theirs, (0, 1)), (theirs, (1, 1)), (theirs, (2, 1))],
                    "rs_chip_add1_" + tag, (F32, BF16))
    got = rs_pair_swap(pb0, pb1, "rs_swap_" + tag)
    red = pair_add(p0, p1, got, core, "rs_pair_add_" + tag)
    return rs_pair_share(red, "rs_share_" + tag).astype(F32)


def _rope_tables(lc, s):
    rows_n = s // GRID_W
    row = jnp.repeat(jnp.arange(rows_n, dtype=F32), GRID_W)
    col = jnp.tile(jnp.arange(GRID_W, dtype=F32), rows_n)
    axis_dim = HEAD_DIM // 2
    inv_freq = ROPE_THETA ** (-jnp.arange(0, axis_dim, 2, dtype=F32) / axis_dim)
    ang_r = row[:, None] * inv_freq[None, :]
    ang_c = col[:, None] * inv_freq[None, :]
    cr, sr, cc, sc = jnp.cos(ang_r), jnp.sin(ang_r), jnp.cos(ang_c), jnp.sin(ang_c)
    cos_l = jnp.concatenate([cr, cr, cc, cc], axis=1)
    sin_l = jnp.concatenate([-sr, sr, -sc, sc], axis=1)
    cos_t = jnp.concatenate([jnp.ones((lc, HEAD_DIM), F32), cos_l], axis=0)
    sin_t = jnp.concatenate([jnp.zeros((lc, HEAD_DIM), F32), sin_l], axis=0)
    return cos_t, sin_t


def _pad_rows(a, rows):
    return jnp.concatenate([a, jnp.zeros((rows - a.shape[0],) + a.shape[1:], a.dtype)], axis=0)


def _pad_cols(a, cols):
    return jnp.concatenate([a, jnp.zeros(a.shape[:-1] + (cols - a.shape[-1],), a.dtype)], axis=-1)


def kernel(x, c, ctx, c_ctx, w_mod, b_mod, post_ln_g, post_ln_b, w_in_e, conv_a_w, conv_a_b, norm_a_g, norm_a_b, conv_b_w, w_out_e, w_in_o, q_norm_g, k_norm_g, w_out_o, loss_target, m_c_ctx, m_w_mod, m_b_mod, m_post_ln_g, m_post_ln_b, m_w_in_e, m_conv_a_w, m_conv_a_b, m_norm_a_g, m_norm_a_b, m_conv_b_w, m_w_out_e, m_w_in_o, m_q_norm_g, m_k_norm_g, m_w_out_o, v_c_ctx, v_w_mod, v_b_mod, v_post_ln_g, v_post_ln_b, v_w_in_e, v_conv_a_w, v_conv_a_b, v_norm_a_g, v_norm_a_b, v_conv_b_w, v_w_out_e, v_w_in_o, v_q_norm_g, v_k_norm_g, v_w_out_o):
    s, d = x.shape[1], x.shape[2]
    lc = ctx.shape[1]
    t = lc + s
    tm = lc
    depth = w_mod.shape[0]
    n_even, n_odd = w_in_e.shape[0], w_in_o.shape[0]
    ad = w_out_o.shape[1] * 4
    kvd = (w_in_o.shape[2] * 4 - 2 * ad) // 2
    nkv = kvd // HEAD_DIM
    hpg = ad // nkv // HEAD_DIM
    nlm = w_mod.shape[2]
    alpha = (2.0 * depth) ** 0.25
    scale = HEAD_DIM ** -0.5
    assert n_even == 2 and n_odd == 2 and depth == 4 and hpg == GQA_GROUP
    assert lc % CONV_ROWS == 0 and s % tm == 0 and d % LANES == 0

    xi, yi, ci = _place()
    shard = 2 * xi + yi
    dev = 4 * xi + 2 * yi + ci

    wg_in_e = gather_weight(w_in_e.astype(BF16), "gather_w_in_e")
    wg_out_e = gather_weight(w_out_e.astype(BF16), "gather_w_out_e")
    wg_in_o = gather_weight(w_in_o.astype(BF16), "gather_w_in_o")
    wg_out_o = gather_weight(w_out_o.astype(BF16), "gather_w_out_o")

    c_all = all_gather8(_pad_rows(c, 8), "gather_c")
    c16 = _pad_rows(jnp.concatenate([c_all[0::8], c_ctx[None, :]], axis=0), 16)
    m_part = adaln_fwd(c16, w_mod)
    m_all = all_gather8(m_part.reshape(depth * 16, nlm), "gather_mod")
    m_all = m_all.reshape(8, depth, 16, nlm)[0::2]
    m_full = m_all.transpose(1, 2, 0, 3).reshape(depth, 16, 4 * nlm) + b_mod[:, None, :]
    m_lat = lax.dynamic_index_in_dim(m_full, dev, axis=1, keepdims=False)
    m_ctx = m_full[:, 8]

    def seg2(l, part):
        return jnp.stack([m_ctx[l, part * d:(part + 1) * d], m_lat[l, part * d:(part + 1) * d]])[:, None, :]

    cos_t, sin_t = _rope_tables(lc, s)

    small_gathered = all_gather8(
        _pad_rows(jnp.concatenate([conv_a_w.reshape(n_even * CONV_A_TAPS, -1),
                                   conv_b_w.reshape(n_even * CONV_B_TAPS, -1)], axis=0), 72), "gather_taps")
    taps = small_gathered.reshape(8, 72, -1)[0::2]
    taps = taps.transpose(1, 0, 2).reshape(72, d)
    caw = taps[:n_even * CONV_A_TAPS].reshape(n_even, CONV_A_TAPS, d)
    cbw = taps[n_even * CONV_A_TAPS:n_even * (CONV_A_TAPS + CONV_B_TAPS)].reshape(n_even, CONV_B_TAPS, d)
    caw_pad = jnp.concatenate([caw, jnp.zeros((n_even, 32 - CONV_A_TAPS, d), F32)], axis=1)
    cbw_pad = jnp.concatenate([cbw, jnp.zeros((n_even, 8 - CONV_B_TAPS, d), F32)], axis=1)

    xc = jnp.concatenate([ctx[0], x[0]], axis=0)
    saved = []
    for l in range(depth):
        i = l // 2
        shift2, scale2, gate2 = seg2(l, 0), seg2(l, 1), seg2(l, 2)
        h = mod_fwd(xc, scale2, shift2, tm)
        if l % 2 == 0:
            p = mm_nn(h, wg_in_e[i], BF16)
            u1 = conv_a_fwd(p, caw_pad[i], conv_a_b[i][None, :], lc, d)
            a_out = ln_a_fwd(u1, p, norm_a_g[i][None, :], norm_a_b[i][None, :], tm)
            b_out = conv_b_fwd(p, cbw_pad[i], lc, d)
            ab = jnp.concatenate([a_out, b_out], axis=1)
            y = mm_nn(ab, wg_out_e[i].reshape(1, 2 * d, d), F32)
            saved.append(dict(xc=xc, h=h, p=p, u1=u1, ab=ab, y=y))
        else:
            qgkv = mm_nn(h, wg_in_o[i], BF16)
            qr, kr = qk_fwd(qgkv, q_norm_g[i][None, :], k_norm_g[i][None, :], cos_t, sin_t, ad, kvd, tm,
                            scale * LOG2_E)
            vt = _chunks_t(qgkv[:, 2 * ad + kvd:], nkv, tm)
            o, og, lse = flash_fwd(qr, kr, vt, qgkv, ad, kvd, tm)
            y = mm_nn(og, wg_out_o[i].reshape(1, ad, d), F32)
            saved.append(dict(xc=xc, h=h, qgkv=qgkv, qr=qr, kr=kr, o=o, og=og, lse=lse, y=y))
        xc = post_ln_fwd(xc, y, gate2, post_ln_g[l][None, :], post_ln_b[l][None, :], alpha, tm)

    dxc, loss_acc = loss_head(xc, loss_target[0], lc, tm)
    loss = lax.psum(0.5 / d * jnp.sum(loss_acc[0]), MESH_AXES)

    g_in_e, g_out_e, g_in_o, g_out_o = [None] * 2, [None] * 2, [None] * 2, [None] * 2
    d_mod_lat, d_mod_ctx = [None] * depth, [None] * depth
    d_pln_g, d_pln_b = [None] * depth, [None] * depth
    d_cab, d_nag, d_nab, d_caw, d_cbw = [None] * 2, [None] * 2, [None] * 2, [None] * 2, [None] * 2
    d_qg, d_kg = [None] * 2, [None] * 2
    for l in reversed(range(depth)):
        i = l // 2
        sv = saved[l]
        scale2, gate2 = seg2(l, 1), seg2(l, 2)
        dzx, dy, acc_ln = post_ln_bwd(dxc, sv["xc"], sv["y"], gate2, post_ln_g[l][None, :], alpha, tm)
        d_pln_g[l] = acc_ln[0, 1] + acc_ln[1, 1]
        d_pln_b[l] = acc_ln[0, 2] + acc_ln[1, 2]
        if l % 2 == 0:
            w_out3 = wg_out_e[i].reshape(1, 2 * d, d)
            dab = mm_nt(dy, w_out3, BF16)
            g_out_e[i] = tuple(g.reshape(4, 2 * d // 4, d) for g in mm_tn(sv["ab"], dy, 1))
            du1, d_agate, acc_a = ln_a_bwd(dab, sv["u1"], sv["p"], norm_a_g[i][None, :], norm_a_b[i][None, :], tm)
            d_nag[i], d_nab[i], d_cab[i] = acc_a[0], acc_a[1], acc_a[2]
            d_aval, d_aglu, d_caw[i] = conv_a_bwd(du1, sv["p"], caw_pad[i], lc, d)
            d_bx, d_bb, d_bc, d_bg, d_cbw[i] = conv_b_bwd(dab, sv["p"], cbw_pad[i], lc, d)
            dp = jnp.concatenate([d_aval, d_aglu, d_agate, d_bx, d_bb, d_bc, d_bg], axis=1)
            dh = mm_nt(dp, wg_in_e[i], F32)
            g_in_e[i] = tuple(mm_tn(sv["h"], dp, 4))
        else:
            w_out3 = wg_out_o[i].reshape(1, ad, d)
            dog = mm_nt(dy, w_out3, BF16)
            g_out_o[i] = tuple(g.reshape(4, ad // 4, d) for g in mm_tn(sv["og"], dy, 1))
            do, dgate, delta = gate_bwd(dog, sv["o"], sv["qgkv"], ad, kvd, tm)
            dqr, dkr, dv = flash_bwd(sv["qr"], do, sv["kr"], _chunks_t(sv["kr"], nkv, tm), sv["qgkv"], sv["lse"],
                                     _rows_per_head(delta, tm, hpg), ad, kvd, tm, scale)
            dq, dk, acc_qk = qk_bwd(dqr, dkr, sv["qgkv"], q_norm_g[i][None, :], k_norm_g[i][None, :],
                                    cos_t, sin_t, ad, kvd, tm)
            d_qg[i], d_kg[i] = acc_qk[0], acc_qk[1]
            dqgkv = jnp.concatenate([dq, dgate, dk, dv.astype(BF16)], axis=1)
            dh = mm_nt(dqgkv, wg_in_o[i], F32)
            g_in_o[i] = tuple(mm_tn(sv["h"], dqgkv, 4))
        dxc, acc_mod = mod_bwd(dh, dzx, sv["xc"], scale2, tm)
        d_mod_ctx[l] = jnp.stack([acc_mod[0, 0], acc_mod[0, 1], acc_ln[0, 0]])
        d_mod_lat[l] = jnp.stack([acc_mod[1, 0], acc_mod[1, 1], acc_ln[1, 0]])
    grad_x = dxc[lc:][None]

    parts = [jnp.concatenate(d_mod_ctx, axis=0), jnp.stack(d_pln_g), jnp.stack(d_pln_b), jnp.stack(d_cab),
             jnp.stack(d_nag), jnp.stack(d_nab), jnp.concatenate(d_caw, axis=0), jnp.concatenate(d_cbw, axis=0),
             _pad_cols(jnp.stack(d_qg), d), _pad_cols(jnp.stack(d_kg), d), jnp.concatenate(d_mod_lat, axis=0)]
    parts = [_pad_rows(p, -(-p.shape[0] // SUBLANES) * SUBLANES) for p in parts]
    offs = [0]
    for p in parts:
        offs.append(offs[-1] + p.shape[0])
    pack = jnp.concatenate(parts, axis=0)
    npack = offs[-1]
    gathered = all_gather8(pack, "gather_small").reshape(8, npack, d)
    small = sum_leading(gathered, "sum_small")

    def piece(k, rows):
        return small[offs[k]:offs[k] + rows]

    dm_ctx = piece(0, 3 * depth).reshape(depth, 1, 3 * d)
    dm_lat = gathered[:, offs[10]:offs[10] + 3 * depth].reshape(8, depth, 3 * d).transpose(1, 0, 2)
    dm = jnp.concatenate([dm_lat, dm_ctx, jnp.zeros((depth, 7, 3 * d), F32)], axis=1)
    g_b_mod = sum_leading(dm.transpose(1, 0, 2), "sum_b_mod")
    dm_shard = lax.dynamic_slice_in_dim(dm, shard * nlm, nlm, axis=2)
    g_w_mod, dc_part = adaln_bwd(c16.T, dm_shard, w_mod)
    dc_all = all_gather8(_pad_rows(dc_part[:, 8, :], 8), "gather_dc").reshape(8, 8, d)
    g_c_ctx = c_ctx_grad(dc_all[0::2, :depth].reshape(4 * depth, 1, d), c_ctx[None, :])[0]

    g_pln_g, g_pln_b = piece(1, depth), piece(2, depth)
    g_cab, g_nag, g_nab = piece(3, n_even), piece(4, n_even), piece(5, n_even)
    dch = d // 4
    g_caw = lax.dynamic_slice_in_dim(piece(6, 64).reshape(2, 32, d)[:, :CONV_A_TAPS], shard * dch, dch, axis=2)
    g_cbw = lax.dynamic_slice_in_dim(piece(7, 16).reshape(2, 8, d)[:, :CONV_B_TAPS], shard * dch, dch, axis=2)
    g_qg, g_kg = piece(8, n_odd)[:, :HEAD_DIM], piece(9, n_odd)[:, :HEAD_DIM]

    g_w_in_e = reduce_scatter_grad(g_in_e[0], g_in_e[1], shard, ci, "in_e")
    g_w_out_e = reduce_scatter_grad(g_out_e[0], g_out_e[1], shard, ci, "out_e")
    g_w_in_o = reduce_scatter_grad(g_in_o[0], g_in_o[1], shard, ci, "in_o")
    g_w_out_o = reduce_scatter_grad(g_out_o[0], g_out_o[1], shard, ci, "out_o")

    grads = [g_c_ctx, g_w_mod, g_b_mod, g_pln_g, g_pln_b, g_w_in_e, g_caw, g_cab, g_nag, g_nab, g_cbw,
             g_w_out_e, g_w_in_o, g_qg, g_kg, g_w_out_o]
    weights = [c_ctx, w_mod, b_mod, post_ln_g, post_ln_b, w_in_e, conv_a_w, conv_a_b, norm_a_g, norm_a_b,
               conv_b_w, w_out_e, w_in_o, q_norm_g, k_norm_g, w_out_o]
    ms = [m_c_ctx, m_w_mod, m_b_mod, m_post_ln_g, m_post_ln_b, m_w_in_e, m_conv_a_w, m_conv_a_b, m_norm_a_g,
          m_norm_a_b, m_conv_b_w, m_w_out_e, m_w_in_o, m_q_norm_g, m_k_norm_g, m_w_out_o]
    vs = [v_c_ctx, v_w_mod, v_b_mod, v_post_ln_g, v_post_ln_b, v_w_in_e, v_conv_a_w, v_conv_a_b, v_norm_a_g,
          v_norm_a_b, v_conv_b_w, v_w_out_e, v_w_in_o, v_q_norm_g, v_k_norm_g, v_w_out_o]
    deltas, new_ms, new_vs = [], [], []
    for wv, gv, mv, vv in zip(weights, grads, ms, vs):
        dl, nm, nv = adamw(wv, gv, mv, vv)
        deltas.append(dl)
        new_ms.append(nm)
        new_vs.append(nv)
    return (loss, grad_x, *grads, *deltas, *new_ms, *new_vs)
```

```python
import functools

import jax
import jax.numpy as jnp
from jax import lax
from jax.experimental import pallas as pl
from jax.experimental.pallas import tpu as pltpu

F32 = jnp.float32
BF16 = jnp.bfloat16

LANES = 128
SUBLANES = 8
HEAD_DIM = 128
GQA_GROUP = 4
GRID_W = 64
ROPE_THETA = 10000.0
LN_EPS = 1e-5
RMS_EPS = 1e-6
CONV_A_TAPS = 31
CONV_B_TAPS = 3
HALO = 16
CONV_ROWS = 128
ADAM_LR = 0.001
ADAM_B1 = 0.9
ADAM_B2 = 0.999
ADAM_EPS = 1e-08
ADAM_WD = 0.01
ADAM_STEP = 10
VMEM_LIMIT = 56 * 1024 * 1024
D2D_PARTS = 4
LOG2_E = 1.4426950408889634
LN_2 = 0.6931471805599453
MESH = pl.DeviceIdType.MESH
ANY = pl.BlockSpec(memory_space=pl.ANY)
VMEM_FULL = pl.BlockSpec(memory_space=pltpu.VMEM)


def _sds(shape, dtype):
    return jax.ShapeDtypeStruct(tuple(shape), dtype)


def _cp(*sem):
    return pltpu.CompilerParams(dimension_semantics=sem, vmem_limit_bytes=VMEM_LIMIT)


def _pick(n, cands):
    for c in cands:
        if n % c == 0:
            return c
    return n


def _sigmoid(x):
    return 1.0 / (1.0 + jnp.exp(-x))


def _silu(x):
    return x * _sigmoid(x)


def _dsilu(x):
    s = _sigmoid(x)
    return s * (1.0 + x * (1.0 - s))


def _row(tm, d):
    return pl.BlockSpec((tm, d), lambda j: (j, 0))


def _seg(d):
    return pl.BlockSpec((None, 1, d), lambda j: (jnp.minimum(j, 1), 0, 0))


def _vec(d):
    return pl.BlockSpec((1, d), lambda j: (0, 0))


def _colblk(tm, width, blk):
    return pl.BlockSpec((tm, width), lambda j: (j, blk))


def _seg_acc(d):
    return pl.BlockSpec((None, SUBLANES, d), lambda j: (jnp.minimum(j, 1), 0, 0))


def _ln_stats(z):
    mu = jnp.mean(z, axis=-1, keepdims=True)
    zc = z - mu
    var = jnp.mean(zc * zc, axis=-1, keepdims=True)
    rstd = lax.rsqrt(var + LN_EPS)
    return zc * rstd, rstd


def _ln_bwd(dxh, xhat, rstd):
    m1 = jnp.mean(dxh, axis=-1, keepdims=True)
    m2 = jnp.mean(dxh * xhat, axis=-1, keepdims=True)
    return rstd * (dxh - m1 - xhat * m2)


def _colsum(v):
    return jnp.sum(v, axis=0, keepdims=True)


def mod_fwd(xc, scale2, shift2, tm):
    t, d = xc.shape

    def body(x_ref, sc_ref, sh_ref, h_ref):
        h_ref[...] = (x_ref[...] * (1.0 + sc_ref[...]) + sh_ref[...]).astype(h_ref.dtype)

    return pl.pallas_call(
        body, name="mod_fwd", grid=(t // tm,),
        in_specs=[_row(tm, d), _seg(d), _seg(d)], out_specs=_row(tm, d),
        out_shape=_sds((t, d), BF16), compiler_params=_cp("parallel"))(xc, scale2, shift2)


def post_ln_fwd(xc, y, gate2, g, b, alpha, tm):
    t, d = xc.shape

    def body(x_ref, y_ref, gt_ref, g_ref, b_ref, o_ref):
        z = alpha * x_ref[...] + gt_ref[...] * y_ref[...]
        xhat, _ = _ln_stats(z)
        o_ref[...] = xhat * g_ref[...] + b_ref[...]

    return pl.pallas_call(
        body, name="post_ln_fwd", grid=(t // tm,),
        in_specs=[_row(tm, d), _row(tm, d), _seg(d), _vec(d), _vec(d)], out_specs=_row(tm, d),
        out_shape=_sds((t, d), F32), compiler_params=_cp("parallel"))(xc, y, gate2, g, b)


def post_ln_bwd(dout, xc, y, gate2, g, alpha, tm):
    t, d = xc.shape

    def body(do_ref, x_ref, y_ref, gt_ref, g_ref, dzx_ref, dy_ref, acc_ref):
        @pl.when(pl.program_id(0) <= 1)
        def _():
            acc_ref[...] = jnp.zeros_like(acc_ref)

        yv = y_ref[...]
        gate = gt_ref[...]
        xhat, rstd = _ln_stats(alpha * x_ref[...] + gate * yv)
        dout = do_ref[...]
        dz = _ln_bwd(dout * g_ref[...], xhat, rstd)
        dzx_ref[...] = alpha * dz
        dy_ref[...] = (gate * dz).astype(dy_ref.dtype)
        acc_ref[0:1, :] += _colsum(dz * yv)
        acc_ref[1:2, :] += _colsum(dout * xhat)
        acc_ref[2:3, :] += _colsum(dout)

    return pl.pallas_call(
        body, name="post_ln_bwd", grid=(t // tm,),
        in_specs=[_row(tm, d), _row(tm, d), _row(tm, d), _seg(d), _vec(d)],
        out_specs=[_row(tm, d), _row(tm, d), _seg_acc(d)],
        out_shape=[_sds((t, d), F32), _sds((t, d), BF16), _sds((2, SUBLANES, d), F32)],
        compiler_params=_cp("arbitrary"))(dout, xc, y, gate2, g)


def mod_bwd(dh, dzx, xc, scale2, tm):
    t, d = xc.shape

    def body(dh_ref, dzx_ref, x_ref, sc_ref, dx_ref, acc_ref):
        @pl.when(pl.program_id(0) <= 1)
        def _():
            acc_ref[...] = jnp.zeros_like(acc_ref)

        dhv = dh_ref[...].astype(F32)
        dx_ref[...] = dzx_ref[...] + dhv * (1.0 + sc_ref[...])
        acc_ref[0:1, :] += _colsum(dhv)
        acc_ref[1:2, :] += _colsum(dhv * x_ref[...])

    return pl.pallas_call(
        body, name="mod_bwd", grid=(t // tm,),
        in_specs=[_row(tm, d), _row(tm, d), _row(tm, d), _seg(d)],
        out_specs=[_row(tm, d), _seg_acc(d)],
        out_shape=[_sds((t, d), F32), _sds((2, SUBLANES, d), F32)],
        compiler_params=_cp("arbitrary"))(dh, dzx, xc, scale2)


def loss_head(xc, target, lc, tm):
    t, d = xc.shape

    def body(x_ref, t_ref, dx_ref, acc_ref):
        j = pl.program_id(0)

        @pl.when(j == 0)
        def _():
            acc_ref[...] = jnp.zeros_like(acc_ref)
            dx_ref[...] = jnp.zeros_like(dx_ref)

        @pl.when(j > 0)
        def _():
            err = x_ref[...] - t_ref[...]
            dx_ref[...] = err * (1.0 / d)
            col = _colsum(err * err)
            tot = col[:, 0:LANES]
            for k in range(1, d // LANES):
                tot = tot + col[:, k * LANES:(k + 1) * LANES]
            acc_ref[0:1, :] += tot

    nlc = lc // tm
    return pl.pallas_call(
        body, name="loss_head", grid=(t // tm,),
        in_specs=[_row(tm, d), pl.BlockSpec((tm, d), lambda j: (jnp.maximum(j - nlc, 0), 0))],
        out_specs=[_row(tm, d), pl.BlockSpec((SUBLANES, LANES), lambda j: (0, 0))],
        out_shape=[_sds((t, d), F32), _sds((SUBLANES, LANES), F32)],
        compiler_params=_cp("arbitrary"))(xc, target)


def mm_nn(a, w3, out_dtype):
    m, k = a.shape
    ns, _, nl = w3.shape
    tm = _pick(m, (768, 512, 256, 128))
    tn = _pick(nl, (512, 256, 128))
    npj = nl // tn

    def body(a_ref, w_ref, o_ref):
        o_ref[...] = jnp.dot(a_ref[...], w_ref[...], preferred_element_type=F32).astype(o_ref.dtype)

    return pl.pallas_call(
        body, name="mm_nn", grid=(m // tm, ns * npj),
        in_specs=[pl.BlockSpec((tm, k), lambda i, j: (i, 0)),
                  pl.BlockSpec((None, k, tn), lambda i, j: (j // npj, 0, j % npj))],
        out_specs=pl.BlockSpec((tm, tn), lambda i, j: (i, j)),
        out_shape=_sds((m, ns * nl), out_dtype), compiler_params=_cp("parallel", "parallel"))(a, w3)


def mm_nt(a, w3, out_dtype):
    m, _ = a.shape
    ns, k, nl = w3.shape
    tm = _pick(m, (768, 512, 256, 128))
    tk = _pick(k, (2048, 1024, 512, 256, 128))
    tn = _pick(nl, (512, 256, 128))
    npj = nl // tn
    nsteps = ns * npj

    def body(a_ref, w_ref, o_ref, acc_ref):
        n = pl.program_id(2)

        @pl.when(n == 0)
        def _():
            acc_ref[...] = jnp.zeros_like(acc_ref)

        acc_ref[...] += lax.dot_general(a_ref[...], w_ref[...], (((1,), (1,)), ((), ())),
                                        preferred_element_type=F32)

        @pl.when(n == nsteps - 1)
        def _():
            o_ref[...] = acc_ref[...].astype(o_ref.dtype)

    return pl.pallas_call(
        body, name="mm_nt", grid=(m // tm, k // tk, nsteps),
        in_specs=[pl.BlockSpec((tm, tn), lambda i, kk, n: (i, n)),
                  pl.BlockSpec((None, tk, tn), lambda i, kk, n: (n // npj, kk, n % npj))],
        out_specs=pl.BlockSpec((tm, tk), lambda i, kk, n: (i, kk)),
        out_shape=_sds((m, k), out_dtype), scratch_shapes=[pltpu.VMEM((tm, tk), F32)],
        compiler_params=_cp("parallel", "parallel", "arbitrary"))(a, w3)


def mm_tn(a, b, ns):
    m, k = a.shape
    nl = b.shape[1] // ns
    tm = _pick(m, (1408, 768, 512, 256, 128))
    tk = _pick(k, (1024, 512, 256, 128))
    tn = _pick(nl, (1792, 1280, 1024, 512, 256, 128))
    npj = nl // tn
    nsteps = m // tm

    def body(a_ref, b_ref, o_ref, ob_ref):
        r = pl.program_id(2)

        @pl.when(r == 0)
        def _():
            o_ref[...] = jnp.zeros_like(o_ref)

        o_ref[...] += lax.dot_general(a_ref[...], b_ref[...], (((0,), (0,)), ((), ())),
                                      preferred_element_type=F32)

        @pl.when(r == nsteps - 1)
        def _():
            ob_ref[...] = o_ref[...].astype(ob_ref.dtype)

    out_spec = pl.BlockSpec((None, tk, tn), lambda i, j, r: (j // npj, i, j % npj))
    return pl.pallas_call(
        body, name="mm_tn", grid=(k // tk, ns * npj, nsteps),
        in_specs=[pl.BlockSpec((tm, tk), lambda i, j, r: (r, i)),
                  pl.BlockSpec((tm, tn), lambda i, j, r: (r, j))],
        out_specs=[out_spec, out_spec],
        out_shape=[_sds((ns, k, nl), F32), _sds((ns, k, nl), BF16)],
        compiler_params=_cp("parallel", "parallel", "arbitrary"))(a, b)


def _win_start(j, ncc):
    return pl.multiple_of(j * CONV_ROWS + jnp.where(j >= ncc, HALO, 0), SUBLANES)


def _tok_start(j):
    return pl.multiple_of(j * CONV_ROWS, CONV_ROWS)


def _shifted(xw, off):
    n = xw.shape[0]
    sh = (n - off) % n
    y = pltpu.roll(xw, sh, 0) if sh else xw
    return y[:CONV_ROWS]


def _conv_fwd(xw, w_ref, ntaps):
    pad = ntaps // 2
    acc = None
    for k in range(ntaps):
        term = w_ref[k:k + 1, :] * _shifted(xw, HALO + k - pad)
        acc = term if acc is None else acc + term
    return acc


def _conv_bwd_data(xw, w_ref, ntaps):
    pad = ntaps // 2
    acc = None
    for k in range(ntaps):
        term = w_ref[k:k + 1, :] * _shifted(xw, HALO - k + pad)
        acc = term if acc is None else acc + term
    return acc


def _conv_bwd_weight(dw_ref, d, xw, ntaps):
    pad = ntaps // 2
    for k in range(ntaps):
        dw_ref[k:k + 1, :] += _colsum(d * _shifted(xw, HALO + k - pad))


def _zero_halos(pad_ref, lc, t):
    z = jnp.zeros((HALO, LANES), F32)
    pad_ref[0:HALO, :] = z
    pad_ref[HALO + lc:2 * HALO + lc, :] = z
    pad_ref[2 * HALO + t:3 * HALO + t, :] = z


def _pad_dst(j, ncc):
    return pl.multiple_of(j * CONV_ROWS + HALO + jnp.where(j >= ncc, HALO, 0), SUBLANES)


def _chan(t, blk0):
    return pl.BlockSpec((t, LANES), lambda ct: (0, blk0 + ct))


def _tapw(rows):
    return pl.BlockSpec((rows, LANES), lambda ct: (0, ct))


def conv_a_fwd(p, w_pad, bias, lc, d):
    t = p.shape[0]
    nct, nch, ncc = d // LANES, t // CONV_ROWS, lc // CONV_ROWS

    def body(av_ref, ag_ref, w_ref, b_ref, u1_ref, pad_ref):
        _zero_halos(pad_ref, lc, t)

        def fill(j, carry):
            rows = pl.ds(_tok_start(j), CONV_ROWS)
            u0 = av_ref[rows, :].astype(F32) * _sigmoid(ag_ref[rows, :].astype(F32))
            pad_ref[pl.ds(_pad_dst(j, ncc), CONV_ROWS), :] = u0
            return carry

        lax.fori_loop(0, nch, fill, 0)

        def conv(j, carry):
            xw = pad_ref[pl.ds(_win_start(j, ncc), CONV_ROWS + 2 * HALO), :]
            u1_ref[pl.ds(_tok_start(j), CONV_ROWS), :] = _conv_fwd(xw, w_ref, CONV_A_TAPS) + b_ref[...]
            return carry

        lax.fori_loop(0, nch, conv, 0)

    return pl.pallas_call(
        body, name="conv_a_fwd", grid=(nct,),
        in_specs=[_chan(t, 0), _chan(t, nct), _tapw(32), _tapw(1)],
        out_specs=_chan(t, 0), out_shape=_sds((t, d), F32),
        scratch_shapes=[pltpu.VMEM((t + 3 * HALO, LANES), F32)],
        compiler_params=_cp("parallel"))(p, p, w_pad, bias)


def conv_b_fwd(p, w_pad, lc, d):
    t = p.shape[0]
    nct, nch, ncc = d // LANES, t // CONV_ROWS, lc // CONV_ROWS

    def body(bx_ref, bb_ref, bc_ref, bg_ref, w_ref, o_ref, pad_ref):
        _zero_halos(pad_ref, lc, t)

        def fill(j, carry):
            rows = pl.ds(_tok_start(j), CONV_ROWS)
            pad_ref[pl.ds(_pad_dst(j, ncc), CONV_ROWS), :] = (
                bc_ref[rows, :].astype(F32) * bx_ref[rows, :].astype(F32))
            return carry

        lax.fori_loop(0, nch, fill, 0)

        def conv(j, carry):
            rows = pl.ds(_tok_start(j), CONV_ROWS)
            xw = pad_ref[pl.ds(_win_start(j, ncc), CONV_ROWS + 2 * HALO), :]
            v = _conv_fwd(xw, w_ref, CONV_B_TAPS)
            o_ref[rows, :] = (bb_ref[rows, :].astype(F32) * v
                              * _silu(bg_ref[rows, :].astype(F32))).astype(o_ref.dtype)
            return carry

        lax.fori_loop(0, nch, conv, 0)

    return pl.pallas_call(
        body, name="conv_b_fwd", grid=(nct,),
        in_specs=[_chan(t, 3 * nct), _chan(t, 4 * nct), _chan(t, 5 * nct), _chan(t, 6 * nct), _tapw(8)],
        out_specs=_chan(t, 0), out_shape=_sds((t, d), BF16),
        scratch_shapes=[pltpu.VMEM((t + 3 * HALO, LANES), F32)],
        compiler_params=_cp("parallel"))(p, p, p, p, w_pad)


def ln_a_fwd(u1, p, g, b, tm):
    t, d = u1.shape

    def body(u_ref, ag_ref, g_ref, b_ref, o_ref):
        xhat, _ = _ln_stats(u_ref[...])
        u2 = xhat * g_ref[...] + b_ref[...]
        o_ref[...] = (_silu(u2) * _silu(ag_ref[...].astype(F32))).astype(o_ref.dtype)

    return pl.pallas_call(
        body, name="ln_a_fwd", grid=(t // tm,),
        in_specs=[_row(tm, d), _colblk(tm, d, 2), _vec(d), _vec(d)], out_specs=_row(tm, d),
        out_shape=_sds((t, d), BF16), compiler_params=_cp("parallel"))(u1, p, g, b)


def ln_a_bwd(dab, u1, p, g, b, tm):
    t, d = u1.shape

    def body(da_ref, u_ref, ag_ref, g_ref, b_ref, du_ref, dag_ref, acc_ref):
        @pl.when(pl.program_id(0) == 0)
        def _():
            acc_ref[...] = jnp.zeros_like(acc_ref)

        xhat, rstd = _ln_stats(u_ref[...])
        u2 = xhat * g_ref[...] + b_ref[...]
        ag = ag_ref[...].astype(F32)
        da = da_ref[...].astype(F32)
        dag_ref[...] = (da * _silu(u2) * _dsilu(ag)).astype(dag_ref.dtype)
        du2 = da * _silu(ag) * _dsilu(u2)
        du1 = _ln_bwd(du2 * g_ref[...], xhat, rstd)
        du_ref[...] = du1
        acc_ref[0:1, :] += _colsum(du2 * xhat)
        acc_ref[1:2, :] += _colsum(du2)
        acc_ref[2:3, :] += _colsum(du1)

    return pl.pallas_call(
        body, name="ln_a_bwd", grid=(t // tm,),
        in_specs=[_colblk(tm, d, 0), _row(tm, d), _colblk(tm, d, 2), _vec(d), _vec(d)],
        out_specs=[_row(tm, d), _row(tm, d), pl.BlockSpec((SUBLANES, d), lambda j: (0, 0))],
        out_shape=[_sds((t, d), F32), _sds((t, d), BF16), _sds((SUBLANES, d), F32)],
        compiler_params=_cp("arbitrary"))(dab, u1, p, g, b)


def conv_a_bwd(du1, p, w_pad, lc, d):
    t = p.shape[0]
    nct, nch, ncc = d // LANES, t // CONV_ROWS, lc // CONV_ROWS

    def body(du_ref, av_ref, ag_ref, w_ref, dav_ref, dag_ref, dw_ref, pad_u, pad_d):
        _zero_halos(pad_u, lc, t)
        _zero_halos(pad_d, lc, t)
        dw_ref[...] = jnp.zeros_like(dw_ref)

        def fill(j, carry):
            rows = pl.ds(_tok_start(j), CONV_ROWS)
            dst = pl.ds(_pad_dst(j, ncc), CONV_ROWS)
            pad_u[dst, :] = av_ref[rows, :].astype(F32) * _sigmoid(ag_ref[rows, :].astype(F32))
            pad_d[dst, :] = du_ref[rows, :]
            return carry

        lax.fori_loop(0, nch, fill, 0)

        def step(j, carry):
            rows = pl.ds(_tok_start(j), CONV_ROWS)
            win = pl.ds(_win_start(j, ncc), CONV_ROWS + 2 * HALO)
            du0 = _conv_bwd_data(pad_d[win, :], w_ref, CONV_A_TAPS)
            sig = _sigmoid(ag_ref[rows, :].astype(F32))
            dav_ref[rows, :] = (du0 * sig).astype(dav_ref.dtype)
            dag_ref[rows, :] = (du0 * av_ref[rows, :].astype(F32) * sig * (1.0 - sig)).astype(dag_ref.dtype)
            _conv_bwd_weight(dw_ref, du_ref[rows, :], pad_u[win, :], CONV_A_TAPS)
            return carry

        lax.fori_loop(0, nch, step, 0)

    return pl.pallas_call(
        body, name="conv_a_bwd", grid=(nct,),
        in_specs=[_chan(t, 0), _chan(t, 0), _chan(t, nct), _tapw(32)],
        out_specs=[_chan(t, 0), _chan(t, 0), _tapw(32)],
        out_shape=[_sds((t, d), BF16), _sds((t, d), BF16), _sds((32, d), F32)],
        scratch_shapes=[pltpu.VMEM((t + 3 * HALO, LANES), F32), pltpu.VMEM((t + 3 * HALO, LANES), F32)],
        compiler_params=_cp("parallel"))(du1, p, p, w_pad)


def conv_b_bwd(dab, p, w_pad, lc, d):
    t = p.shape[0]
    nct, nch, ncc = d // LANES, t // CONV_ROWS, lc // CONV_ROWS

    def body(db_ref, bx_ref, bb_ref, bc_ref, bg_ref, w_ref,
             dbx_ref, dbb_ref, dbc_ref, dbg_ref, dw_ref, pad_t, pad_d):
        _zero_halos(pad_t, lc, t)
        _zero_halos(pad_d, lc, t)
        dw_ref[...] = jnp.zeros_like(dw_ref)

        def fill(j, carry):
            rows = pl.ds(_tok_start(j), CONV_ROWS)
            pad_t[pl.ds(_pad_dst(j, ncc), CONV_ROWS), :] = (
                bc_ref[rows, :].astype(F32) * bx_ref[rows, :].astype(F32))
            return carry

        lax.fori_loop(0, nch, fill, 0)

        def first(j, carry):
            rows = pl.ds(_tok_start(j), CONV_ROWS)
            xw = pad_t[pl.ds(_win_start(j, ncc), CONV_ROWS + 2 * HALO), :]
            v = _conv_fwd(xw, w_ref, CONV_B_TAPS)
            bg = bg_ref[rows, :].astype(F32)
            bb = bb_ref[rows, :].astype(F32)
            db = db_ref[rows, :].astype(F32)
            sg = _silu(bg)
            dbb_ref[rows, :] = (db * v * sg).astype(dbb_ref.dtype)
            dbg_ref[rows, :] = (db * bb * v * _dsilu(bg)).astype(dbg_ref.dtype)
            dv = db * bb * sg
            pad_d[pl.ds(_pad_dst(j, ncc), CONV_ROWS), :] = dv
            _conv_bwd_weight(dw_ref, dv, xw, CONV_B_TAPS)
            return carry

        lax.fori_loop(0, nch, first, 0)

        def second(j, carry):
            rows = pl.ds(_tok_start(j), CONV_ROWS)
            dt = _conv_bwd_data(pad_d[pl.ds(_win_start(j, ncc), CONV_ROWS + 2 * HALO), :], w_ref, CONV_B_TAPS)
            dbc_ref[rows, :] = (dt * bx_ref[rows, :].astype(F32)).astype(dbc_ref.dtype)
            dbx_ref[rows, :] = (dt * bc_ref[rows, :].astype(F32)).astype(dbx_ref.dtype)
            return carry

        lax.fori_loop(0, nch, second, 0)

    return pl.pallas_call(
        body, name="conv_b_bwd", grid=(nct,),
        in_specs=[_chan(t, nct), _chan(t, 3 * nct), _chan(t, 4 * nct), _chan(t, 5 * nct), _chan(t, 6 * nct),
                  _tapw(8)],
        out_specs=[_chan(t, 0)] * 4 + [_tapw(8)],
        out_shape=[_sds((t, d), BF16)] * 4 + [_sds((8, d), F32)],
        scratch_shapes=[pltpu.VMEM((t + 3 * HALO, LANES), F32), pltpu.VMEM((t + 3 * HALO, LANES), F32)],
        compiler_params=_cp("parallel"))(dab, p, p, p, p, w_pad)


def _swap_halves(z, first_half):
    return jnp.where(first_half, pltpu.roll(z, 96, 1), pltpu.roll(z, 32, 1))


def _first_half_mask(rows):
    lane = lax.broadcasted_iota(jnp.int32, (rows, HEAD_DIM), 1)
    return (lane & 32) == 0


def qk_fwd(qgkv, qg, kg, cos_t, sin_t, ad, kvd, tm, qscale):
    t = qgkv.shape[0]

    def body(q_ref, k_ref, qg_ref, kg_ref, c_ref, s_ref, qo_ref, ko_ref):
        first = _first_half_mask(tm)
        cosv, sinv = c_ref[...], s_ref[...]

        def head(x, gain):
            inv = lax.rsqrt(jnp.mean(x * x, axis=-1, keepdims=True) + RMS_EPS)
            yv = x * inv * gain
            return yv * cosv + _swap_halves(yv, first) * sinv

        for h in range(ad // HEAD_DIM):
            sl = slice(h * HEAD_DIM, (h + 1) * HEAD_DIM)
            qo_ref[:, sl] = (head(q_ref[:, sl].astype(F32), qg_ref[...]) * qscale).astype(qo_ref.dtype)
        for h in range(kvd // HEAD_DIM):
            sl = slice(h * HEAD_DIM, (h + 1) * HEAD_DIM)
            ko_ref[:, sl] = head(k_ref[:, sl].astype(F32), kg_ref[...]).astype(ko_ref.dtype)

    return pl.pallas_call(
        body, name="qk_fwd", grid=(t // tm,),
        in_specs=[_colblk(tm, ad, 0), _colblk(tm, kvd, 2 * ad // kvd), _vec(HEAD_DIM), _vec(HEAD_DIM),
                  _row(tm, HEAD_DIM), _row(tm, HEAD_DIM)],
        out_specs=[_row(tm, ad), _row(tm, kvd)],
        out_shape=[_sds((t, ad), BF16), _sds((t, kvd), BF16)],
        compiler_params=_cp("parallel"))(qgkv, qgkv, qg, kg, cos_t, sin_t)


def qk_bwd(dqr, dkr, qgkv, qg, kg, cos_t, sin_t, ad, kvd, tm):
    t = qgkv.shape[0]

    def body(dq_ref, dk_ref, q_ref, k_ref, qg_ref, kg_ref, c_ref, s_ref, dqo_ref, dko_ref, acc_ref):
        @pl.when(pl.program_id(0) == 0)
        def _():
            acc_ref[...] = jnp.zeros_like(acc_ref)

        first = _first_half_mask(tm)
        cosv, sinv = c_ref[...], s_ref[...]

        def head(x, gain, dout):
            inv = lax.rsqrt(jnp.mean(x * x, axis=-1, keepdims=True) + RMS_EPS)
            xn = x * inv
            dy = dout * cosv + _swap_halves(dout * sinv, first)
            dxn = dy * gain
            dx = inv * (dxn - xn * jnp.mean(dxn * xn, axis=-1, keepdims=True))
            return dx, _colsum(dy * xn)

        dqg = jnp.zeros((1, HEAD_DIM), F32)
        for h in range(ad // HEAD_DIM):
            sl = slice(h * HEAD_DIM, (h + 1) * HEAD_DIM)
            dx, dg = head(q_ref[:, sl].astype(F32), qg_ref[...], dq_ref[:, sl])
            dqo_ref[:, sl] = dx.astype(dqo_ref.dtype)
            dqg = dqg + dg
        dkg = jnp.zeros((1, HEAD_DIM), F32)
        for h in range(kvd // HEAD_DIM):
            sl = slice(h * HEAD_DIM, (h + 1) * HEAD_DIM)
            dx, dg = head(k_ref[:, sl].astype(F32), kg_ref[...], dk_ref[:, sl])
            dko_ref[:, sl] = dx.astype(dko_ref.dtype)
            dkg = dkg + dg
        acc_ref[0:1, :] += dqg
        acc_ref[1:2, :] += dkg

    return pl.pallas_call(
        body, name="qk_bwd", grid=(t // tm,),
        in_specs=[_row(tm, ad), _row(tm, kvd), _colblk(tm, ad, 0), _colblk(tm, kvd, 2 * ad // kvd),
                  _vec(HEAD_DIM), _vec(HEAD_DIM), _row(tm, HEAD_DIM), _row(tm, HEAD_DIM)],
        out_specs=[_row(tm, ad), _row(tm, kvd), pl.BlockSpec((SUBLANES, HEAD_DIM), lambda j: (0, 0))],
        out_shape=[_sds((t, ad), BF16), _sds((t, kvd), BF16), _sds((SUBLANES, HEAD_DIM), F32)],
        compiler_params=_cp("arbitrary"))(dqr, dkr, qgkv, qgkv, qg, kg, cos_t, sin_t)


_NT = (((1,), (1,)), ((), ()))


def _chunks_t(a, nkv, tm):
    t = a.shape[0]
    return a.reshape(t // tm, tm, nkv, HEAD_DIM).transpose(2, 0, 3, 1)


def _tree_rows(x, op):
    slabs = [x[i:i + SUBLANES] for i in range(0, x.shape[0], SUBLANES)]
    while len(slabs) > 1:
        slabs = [op(slabs[i], slabs[i + 1]) for i in range(0, len(slabs), 2)]
    return slabs[0]


def flash_fwd(qr, kr, vt, qgkv, ad, kvd, tm):
    t = qr.shape[0]
    nkv = kvd // HEAD_DIM
    gw = ad // nkv
    hpg = gw // HEAD_DIM
    nt = t // tm
    gate_blk0 = ad // gw

    def body(q_ref, k_ref, vt_ref, g_ref, o_ref, og_ref, lse_ref):
        qi = pl.program_id(1)
        nkc = jnp.where(qi == 0, 1, nt)
        heads = [slice(h * HEAD_DIM, (h + 1) * HEAD_DIM) for h in range(hpg)]
        qs = [q_ref[:, sl] for sl in heads]

        def step(c, carry):
            kc = k_ref[pl.ds(pl.multiple_of(c * tm, tm), tm), :]
            vtc = vt_ref[c]
            sts = [lax.dot_general(kc, q, _NT, preferred_element_type=F32) for q in qs]
            out = []
            for h in range(hpg):
                m, l, acc = carry[h]
                m_new = jnp.maximum(m, jnp.max(_tree_rows(sts[h], jnp.maximum), axis=0, keepdims=True))
                a = jnp.exp2(m - m_new)
                pt = jnp.exp2(sts[h] - m_new)
                l = a * l + jnp.sum(_tree_rows(pt, jnp.add), axis=0, keepdims=True)
                acc = a * acc + jnp.dot(vtc, pt.astype(BF16), preferred_element_type=F32)
                out.append((m_new, l, acc))
            return tuple(out)

        init = tuple((jnp.full((1, tm), -1e30, F32), jnp.zeros((1, tm), F32), jnp.zeros((HEAD_DIM, tm), F32))
                     for _ in range(hpg))
        res = lax.fori_loop(0, nkc, step, init)
        for h, sl in enumerate(heads):
            m, l, acc = res[h]
            o = (acc / l).T
            o_ref[:, sl] = o.astype(o_ref.dtype)
            og_ref[:, sl] = (o * _silu(g_ref[:, sl].astype(F32))).astype(og_ref.dtype)
            lse_ref[h:h + 1, :] = m + jnp.log(l) * LOG2_E

    return pl.pallas_call(
        body, name="flash_fwd", grid=(nkv, nt),
        in_specs=[pl.BlockSpec((tm, gw), lambda g, i: (i, g)),
                  pl.BlockSpec((t, HEAD_DIM), lambda g, i: (0, g)),
                  pl.BlockSpec((None, nt, HEAD_DIM, tm), lambda g, i: (g, 0, 0, 0)),
                  pl.BlockSpec((tm, gw), lambda g, i: (i, gate_blk0 + g))],
        out_specs=[pl.BlockSpec((tm, gw), lambda g, i: (i, g)),
                   pl.BlockSpec((tm, gw), lambda g, i: (i, g)),
                   pl.BlockSpec((None, None, hpg, tm), lambda g, i: (g, i, 0, 0))],
        out_shape=[_sds((t, ad), BF16), _sds((t, ad), BF16), _sds((nkv, nt, hpg, tm), F32)],
        compiler_params=_cp("parallel", "parallel"))(qr, kr, vt, qgkv)


def gate_bwd(dog, o, qgkv, ad, kvd, tm):
    t = o.shape[0]
    nkv = kvd // HEAD_DIM
    hpg = ad // nkv // HEAD_DIM

    def body(dog_ref, o_ref, g_ref, do_ref, dg_ref, dl_ref):
        lane = lax.broadcasted_iota(jnp.int32, (tm, LANES), 1)
        for grp in range(nkv):
            blk = jnp.zeros((tm, LANES), F32)
            for hh in range(hpg):
                h = grp * hpg + hh
                sl = slice(h * HEAD_DIM, (h + 1) * HEAD_DIM)
                dv = dog_ref[:, sl].astype(F32)
                ov = o_ref[:, sl].astype(F32)
                gv = g_ref[:, sl].astype(F32)
                doh = dv * _silu(gv)
                do_ref[:, sl] = doh.astype(do_ref.dtype)
                dg_ref[:, sl] = (dv * ov * _dsilu(gv)).astype(dg_ref.dtype)
                blk = jnp.where(lane == hh, jnp.sum(doh * ov, axis=-1, keepdims=True), blk)
            dl_ref[grp] = blk

    return pl.pallas_call(
        body, name="gate_bwd", grid=(t // tm,),
        in_specs=[_row(tm, ad), _row(tm, ad), _colblk(tm, ad, 1)],
        out_specs=[_row(tm, ad), _row(tm, ad), pl.BlockSpec((nkv, tm, LANES), lambda j: (0, j, 0))],
        out_shape=[_sds((t, ad), BF16), _sds((t, ad), BF16), _sds((nkv, t, LANES), F32)],
        compiler_params=_cp("parallel"))(dog, o, qgkv)


def flash_bwd(qr, do, kr, kt, qgkv, lse_t, delta_t, ad, kvd, tm, scale):
    t = qr.shape[0]
    nkv = kvd // HEAD_DIM
    gw = ad // nkv
    hpg = gw // HEAD_DIM
    nt = t // tm
    v_blk0 = (2 * ad + kvd) // HEAD_DIM

    def body(q_ref, do_ref, k_ref, v_ref, kt_ref, lse_ref, dl_ref, dq_ref, dk_ref, dv_ref):
        qi = pl.program_id(1)

        @pl.when(qi == 0)
        def _():
            dk_ref[...] = jnp.zeros_like(dk_ref)
            dv_ref[...] = jnp.zeros_like(dv_ref)

        nkc = jnp.where(qi == 0, 1, nt)
        heads = [slice(h * HEAD_DIM, (h + 1) * HEAD_DIM) for h in range(hpg)]
        qs = [q_ref[:, sl] for sl in heads]
        dos = [do_ref[:, sl] for sl in heads]
        lses = [lse_ref[h:h + 1, :] for h in range(hpg)]
        dls = [dl_ref[h:h + 1, :] for h in range(hpg)]

        def step(c, dqts):
            rows = pl.ds(pl.multiple_of(c * tm, tm), tm)
            kc = k_ref[rows, :]
            vc = v_ref[rows, :]
            ktc = kt_ref[c]
            dk = jnp.zeros((tm, HEAD_DIM), F32)
            dv = jnp.zeros((tm, HEAD_DIM), F32)
            out = []
            sts = [lax.dot_general(kc, q, _NT, preferred_element_type=F32) for q in qs]
            dpts = [lax.dot_general(vc, d, _NT, preferred_element_type=F32) for d in dos]
            for h in range(hpg):
                pt = jnp.exp2(sts[h] - lses[h])
                dv = dv + jnp.dot(pt.astype(BF16), dos[h], preferred_element_type=F32)
                dst = (pt * (dpts[h] - dls[h])).astype(BF16)
                dk = dk + jnp.dot(dst, qs[h], preferred_element_type=F32)
                out.append(dqts[h] + jnp.dot(ktc, dst, preferred_element_type=F32))
            dk_ref[rows, :] += dk
            dv_ref[rows, :] += dv
            return tuple(out)

        res = lax.fori_loop(0, nkc, step, tuple(jnp.zeros((HEAD_DIM, tm), F32) for _ in range(hpg)))
        for h, sl in enumerate(heads):
            dq_ref[:, sl] = res[h].T * scale

        @pl.when(qi == nt - 1)
        def _():
            dk_ref[...] = dk_ref[...] * LN_2

    return pl.pallas_call(
        body, name="flash_bwd", grid=(nkv, nt),
        in_specs=[pl.BlockSpec((tm, gw), lambda g, i: (i, g)),
                  pl.BlockSpec((tm, gw), lambda g, i: (i, g)),
                  pl.BlockSpec((t, HEAD_DIM), lambda g, i: (0, g)),
                  pl.BlockSpec((t, HEAD_DIM), lambda g, i: (0, v_blk0 + g)),
                  pl.BlockSpec((None, nt, HEAD_DIM, tm), lambda g, i: (g, 0, 0, 0)),
                  pl.BlockSpec((None, None, hpg, tm), lambda g, i: (g, i, 0, 0)),
                  pl.BlockSpec((None, None, hpg, tm), lambda g, i: (g, i, 0, 0))],
        out_specs=[pl.BlockSpec((tm, gw), lambda g, i: (i, g)),
                   pl.BlockSpec((t, HEAD_DIM), lambda g, i: (0, g)),
                   pl.BlockSpec((t, HEAD_DIM), lambda g, i: (0, g))],
        out_shape=[_sds((t, ad), F32), _sds((t, kvd), F32), _sds((t, kvd), F32)],
        compiler_params=_cp("parallel", "arbitrary"))(qr, do, kr, qgkv, kt, lse_t, delta_t)


def _rows_per_head(a, tm, hpg):
    nkv, t, _ = a.shape
    return a[:, :, :hpg].reshape(nkv, t // tm, tm, hpg).transpose(0, 1, 3, 2)


def adaln_fwd(c16, w_mod):
    nlay, d, nl = w_mod.shape
    tn = _pick(nl, (512, 256, 128))

    def body(c_ref, w_ref, o_ref):
        o_ref[...] = jnp.dot(_silu(c_ref[...]), w_ref[...], preferred_element_type=F32,
                             precision=lax.Precision.HIGHEST)

    return pl.pallas_call(
        body, name="adaln_fwd", grid=(nlay, nl // tn),
        in_specs=[pl.BlockSpec((16, d), lambda l, j: (0, 0)),
                  pl.BlockSpec((None, d, tn), lambda l, j: (l, 0, j))],
        out_specs=pl.BlockSpec((None, 16, tn), lambda l, j: (l, 0, j)),
        out_shape=_sds((nlay, 16, nl), F32), compiler_params=_cp("parallel", "parallel"))(c16, w_mod)


def adaln_bwd(c16t, dm, w_mod):
    nlay, d, nl = w_mod.shape
    tn = _pick(nl, (512, 256, 128))

    def body(c_ref, dm_ref, w_ref, dw_ref, dc_ref):
        @pl.when(pl.program_id(1) == 0)
        def _():
            dc_ref[...] = jnp.zeros_like(dc_ref)

        dmv = dm_ref[...]
        dw_ref[...] = jnp.dot(_silu(c_ref[...]), dmv, preferred_element_type=F32,
                              precision=lax.Precision.HIGHEST)
        dc_ref[...] += lax.dot_general(dmv, w_ref[...], _NT, preferred_element_type=F32,
                                       precision=lax.Precision.HIGHEST)

    return pl.pallas_call(
        body, name="adaln_bwd", grid=(nlay, nl // tn),
        in_specs=[pl.BlockSpec((d, 16), lambda l, j: (0, 0)),
                  pl.BlockSpec((None, 16, tn), lambda l, j: (l, 0, j)),
                  pl.BlockSpec((None, d, tn), lambda l, j: (l, 0, j))],
        out_specs=[pl.BlockSpec((None, d, tn), lambda l, j: (l, 0, j)),
                   pl.BlockSpec((None, 16, d), lambda l, j: (l, 0, 0))],
        out_shape=[_sds((nlay, d, nl), F32), _sds((nlay, 16, d), F32)],
        compiler_params=_cp("parallel", "arbitrary"))(c16t, dm, w_mod)


def sum_leading(a, name):
    n = a.shape[0]

    def body(a_ref, o_ref):
        acc = a_ref[0]
        for i in range(1, n):
            acc = acc + a_ref[i]
        o_ref[...] = acc

    return pl.pallas_call(body, name=name, in_specs=[VMEM_FULL], out_specs=VMEM_FULL,
                          out_shape=_sds(a.shape[1:], F32),
                          compiler_params=pltpu.CompilerParams(vmem_limit_bytes=VMEM_LIMIT))(a)


def c_ctx_grad(parts, c_ctx):
    n = parts.shape[0]

    def body(p_ref, c_ref, o_ref):
        acc = p_ref[0]
        for i in range(1, n):
            acc = acc + p_ref[i]
        o_ref[...] = acc * _dsilu(c_ref[...])

    return pl.pallas_call(body, name="c_ctx_grad", in_specs=[VMEM_FULL, VMEM_FULL], out_specs=VMEM_FULL,
                          out_shape=_sds(c_ctx.shape, F32))(parts, c_ctx)


def _as2d(a):
    return a.reshape(-1, a.shape[-1])


def _row_tile(r, c):
    for tr in (1024, 512, 256, 128, 64, 32, 16, 8):
        if r % tr == 0 and tr * c * 4 <= (1 << 20):
            return tr
    return r


def add_n(items, name, out_dtypes=(F32,)):
    norm = [it if isinstance(it, tuple) else (it, ()) for it in items]
    shape = norm[0][0].shape[len(norm[0][1]):]
    r, c = _as2d(jnp.zeros(shape, BF16)).shape
    tr = _row_tile(r, c)
    dyn, specs, flat = [], [], []
    for arr, lead in norm:
        slots = []
        for ix in lead:
            if isinstance(ix, int):
                slots.append(ix)
            else:
                slots.append((len(dyn),))
                dyn.append(ix)

        def imap(i, s, slots=tuple(slots)):
            return tuple(s[k[0]] if isinstance(k, tuple) else k for k in slots) + (i, 0)

        specs.append(pl.BlockSpec((None,) * len(lead) + (tr, c), imap))
        flat.append(arr.reshape(arr.shape[:len(lead)] + (r, c)))
    sel = jnp.stack([jnp.asarray(v, jnp.int32) for v in dyn]) if dyn else jnp.zeros((1,), jnp.int32)
    n_in = len(flat)

    def body(s_ref, *refs):
        acc = refs[0][...].astype(F32)
        for ref in refs[1:n_in]:
            acc = acc + ref[...].astype(F32)
        for ref in refs[n_in:]:
            ref[...] = acc.astype(ref.dtype)

    out_spec = pl.BlockSpec((tr, c), lambda i, s: (i, 0))
    outs = pl.pallas_call(
        body, name=name,
        grid_spec=pltpu.PrefetchScalarGridSpec(num_scalar_prefetch=1, grid=(r // tr,), in_specs=specs,
                                               out_specs=[out_spec] * len(out_dtypes)),
        out_shape=[_sds((r, c), dt) for dt in out_dtypes], compiler_params=_cp("parallel"))(sel, *flat)
    return [o.reshape(shape) for o in outs]


def pair_add(p0, p1, got, which, name):
    shape = p0.shape
    flat = [_as2d(a) for a in (p0, p1, got)]
    r, c = flat[0].shape
    tr = _row_tile(r, c)

    def body(s_ref, p0_ref, p1_ref, g_ref, o_ref):
        own = jnp.where(s_ref[0] == 0, p0_ref[...], p1_ref[...])
        o_ref[...] = (own + g_ref[...].astype(F32)).astype(o_ref.dtype)

    spec = pl.BlockSpec((tr, c), lambda i, s: (i, 0))
    out = pl.pallas_call(
        body, name=name,
        grid_spec=pltpu.PrefetchScalarGridSpec(num_scalar_prefetch=1, grid=(r // tr,), in_specs=[spec] * 3,
                                               out_specs=spec),
        out_shape=_sds((r, c), BF16), compiler_params=_cp("parallel"))(
            jnp.asarray(which, jnp.int32).reshape(1), *flat)
    return out.reshape(shape)


def adamw(w, g, m, v):
    shape = w.shape
    flat = [_as2d(a.reshape((1,) + shape) if len(shape) == 1 else a) for a in (w, g, m, v)]
    r, c = flat[0].shape
    tr = _row_tile(r, c)
    c1 = 1.0 - ADAM_B1 ** ADAM_STEP
    c2 = 1.0 - ADAM_B2 ** ADAM_STEP

    def body(w_ref, g_ref, m_ref, v_ref, d_ref, nm_ref, nv_ref):
        gv = g_ref[...]
        nm = ADAM_B1 * m_ref[...] + (1.0 - ADAM_B1) * gv
        nv = ADAM_B2 * v_ref[...] + (1.0 - ADAM_B2) * (gv * gv)
        d_ref[...] = -ADAM_LR * ((nm / c1) / (jnp.sqrt(nv / c2) + ADAM_EPS) + ADAM_WD * w_ref[...])
        nm_ref[...] = nm
        nv_ref[...] = nv

    spec = pl.BlockSpec((tr, c), lambda i: (i, 0))
    outs = pl.pallas_call(
        body, name="adamw", grid=(r // tr,), in_specs=[spec] * 4, out_specs=[spec] * 3,
        out_shape=[_sds((r, c), F32)] * 3, compiler_params=_cp("parallel"))(*flat)
    return tuple(o.reshape(shape) for o in outs)


def _place():
    return lax.axis_index("x"), lax.axis_index("y"), lax.axis_index("c")


def _remote(src, dst, ssem, rsem, dev):
    return pltpu.make_async_remote_copy(src_ref=src, dst_ref=dst, send_sem=ssem, recv_sem=rsem,
                                        device_id=dev, device_id_type=MESH)


def all_gather8(v, name):
    m_per, n = v.shape

    def body(x_ref, out_ref, send_sems, recv_sems, local_sem):
        x, y, c = _place()
        me, sibling = (x, y, c), (x, y, 1 - c)
        chips = [(1 - x, y), (x, 1 - y), (1 - x, 1 - y)]

        def rows(px, py, pc):
            return out_ref.at[pl.ds((4 * px + 2 * py + pc) * m_per, m_per), :]

        def copy(k, block, to, src=None):
            return _remote(rows(*block) if src is None else src, rows(*block),
                           send_sems.at[k], recv_sems.at[k], to)

        mine = pltpu.make_async_copy(x_ref, rows(*me), local_sem)
        mine.start()
        first = [copy(0, me, sibling, src=x_ref)]
        first += [copy(1 + j, me, (*chip, c), src=x_ref) for j, chip in enumerate(chips)]
        for cp in first:
            cp.start()
        passed = [copy(4 + j, (*chip, c), sibling) for j, chip in enumerate(chips)]
        for j, chip in enumerate(chips):
            copy(1 + j, (*chip, c), me).wait_recv()
            passed[j].start()
        copy(0, sibling, me).wait_recv()
        for j, chip in enumerate(chips):
            copy(4 + j, (*chip, 1 - c), me).wait_recv()
        for cp in first + passed:
            cp.wait_send()
        mine.wait()

    return pl.pallas_call(
        body, name=name, out_shape=_sds((8 * m_per, n), v.dtype),
        in_specs=[VMEM_FULL], out_specs=VMEM_FULL,
        scratch_shapes=[pltpu.SemaphoreType.DMA((7,)), pltpu.SemaphoreType.DMA((7,)), pltpu.SemaphoreType.DMA],
        compiler_params=pltpu.CompilerParams(vmem_limit_bytes=VMEM_LIMIT))(v)


def _slabs(ref, n):
    rows = ref.shape[0] // n
    return [ref.at[pl.ds(i * rows, rows)] for i in range(n)]


def gather_weight(w, name):
    _, r, cdim = w.shape
    n = D2D_PARTS

    def body(w_ref, out_ref, slab_ref, send_sems, recv_sems, local_sem):
        x, y, c = _place()
        sibling = (x, y, 1 - c)
        chips = [(1 - x, y), (x, 1 - y), (1 - x, 1 - y)]
        mine = 2 * x + y
        src = _slabs(w_ref.at[c], n)
        first = []
        for k, chip in enumerate(chips):
            dst = _slabs(out_ref.at[c, mine], n)
            for j in range(n):
                cp = _remote(src[j], dst[j], send_sems.at[k * n + j], recv_sems.at[k * n + j], (*chip, c))
                cp.start()
                first.append(cp)
        for i in range(2):
            for s_in, s_out in zip(_slabs(w_ref.at[i], n), _slabs(out_ref.at[i, mine], n)):
                cp = pltpu.make_async_copy(s_in, slab_ref, local_sem)
                cp.start()
                cp.wait()
                cp = pltpu.make_async_copy(slab_ref, s_out, local_sem)
                cp.start()
                cp.wait()
        passed = []
        for j in range(n):
            for k, (px, py) in enumerate(chips):
                theirs = _slabs(out_ref.at[c, 2 * px + py], n)[j]
                _remote(src[j], theirs, send_sems.at[k * n + j], recv_sems.at[k * n + j], sibling).wait_recv()
                cp = _remote(theirs, theirs, send_sems.at[(3 + k) * n + j], recv_sems.at[(3 + k) * n + j], sibling)
                cp.start()
                passed.append(cp)
        for j in range(n):
            for k, (px, py) in enumerate(chips):
                other = _slabs(out_ref.at[1 - c, 2 * px + py], n)[j]
                _remote(other, other, send_sems.at[(3 + k) * n + j], recv_sems.at[(3 + k) * n + j],
                        sibling).wait_recv()
        for cp in first + passed:
            cp.wait_send()

    return pl.pallas_call(
        body, name=name, out_shape=_sds((2, 4, r, cdim), w.dtype), in_specs=[ANY], out_specs=ANY,
        scratch_shapes=[pltpu.VMEM((r // n, cdim), w.dtype), pltpu.SemaphoreType.DMA((6 * n,)),
                        pltpu.SemaphoreType.DMA((6 * n,)), pltpu.SemaphoreType.DMA],
        compiler_params=pltpu.CompilerParams(vmem_limit_bytes=VMEM_LIMIT))(w)


def rs_chip_exchange(gb0, gb1, name):
    _, r, cdim = gb0.shape

    def body(gb0_ref, gb1_ref, st_ref, send_sems, recv_sems):
        x, y, c = _place()
        chips = [(1 - x, y), (x, 1 - y), (1 - x, 1 - y)]
        cps = []
        for k, (px, py) in enumerate(chips):
            for i, gb_ref in enumerate((gb0_ref, gb1_ref)):
                cps.append(_remote(gb_ref.at[2 * px + py], st_ref.at[k, i], send_sems.at[2 * k + i],
                                   recv_sems.at[2 * k + i], (px, py, c)))
        for cp in cps:
            cp.start()
        for cp in cps:
            cp.wait()

    return pl.pallas_call(
        body, name=name, out_shape=_sds((3, 2, r, cdim), BF16), in_specs=[ANY] * 2, out_specs=ANY,
        scratch_shapes=[pltpu.SemaphoreType.DMA((6,)), pltpu.SemaphoreType.DMA((6,))])(gb0, gb1)


def rs_pair_swap(pb0, pb1, name):
    r, cdim = pb0.shape
    n = 2 * D2D_PARTS

    def body(pb0_ref, pb1_ref, got_ref, stage_ref, send_sems, recv_sems, local_sem):
        x, y, c = _place()
        sibling = (x, y, 1 - c)

        @pl.when(c == 0)
        def _():
            load = pltpu.make_async_copy(pb1_ref, stage_ref, local_sem)
            load.start()
            load.wait()

        @pl.when(c == 1)
        def _():
            load = pltpu.make_async_copy(pb0_ref, stage_ref, local_sem)
            load.start()
            load.wait()

        src, dst = _slabs(stage_ref, n), _slabs(got_ref, n)
        cps = [_remote(src[j], dst[j], send_sems.at[j], recv_sems.at[j], sibling) for j in range(n)]
        for cp in cps:
            cp.start()
        for cp in cps:
            cp.wait()

    return pl.pallas_call(
        body, name=name, out_shape=_sds((r, cdim), BF16), in_specs=[ANY, ANY], out_specs=VMEM_FULL,
        scratch_shapes=[pltpu.VMEM((r, cdim), BF16), pltpu.SemaphoreType.DMA((n,)), pltpu.SemaphoreType.DMA((n,)),
                        pltpu.SemaphoreType.DMA],
        compiler_params=pltpu.CompilerParams(vmem_limit_bytes=VMEM_LIMIT))(pb0, pb1)


def rs_pair_share(red, name):
    n = 2 * D2D_PARTS

    def body(red_ref, out_ref, send_sems, recv_sems, local_sem):
        x, y, c = _place()
        sibling = (x, y, 1 - c)
        keep = pltpu.make_async_copy(red_ref, out_ref.at[c], local_sem)
        keep.start()
        src, dst, got = _slabs(red_ref, n), _slabs(out_ref.at[c], n), _slabs(out_ref.at[1 - c], n)
        cps = [_remote(src[j], dst[j], send_sems.at[j], recv_sems.at[j], sibling) for j in range(n)]
        for cp in cps:
            cp.start()
        for cp in cps:
            cp.wait_send()
        for j in range(n):
            _remote(src[j], got[j], send_sems.at[j], recv_sems.at[j], sibling).wait_recv()
        keep.wait()

    return pl.pallas_call(
        body, name=name, out_shape=_sds((2,) + red.shape, red.dtype), in_specs=[VMEM_FULL], out_specs=VMEM_FULL,
        scratch_shapes=[pltpu.SemaphoreType.DMA((n,)), pltpu.SemaphoreType.DMA((n,)), pltpu.SemaphoreType.DMA],
        compiler_params=pltpu.CompilerParams(vmem_limit_bytes=VMEM_LIMIT))(red)


def reduce_scatter_grad(g0, g1, shard, core, tag):
    theirs = rs_chip_exchange(g0[1], g1[1], "rs_exchange_" + tag)
    p0, pb0 = add_n([(g0[0], (shard,)), (theirs, (0, 0)), (theirs, (1, 0)), (theirs, (2, 0))],
                    "rs_chip_add0_" + tag, (F32, BF16))
    p1, pb1 = add_n([(g1[0], (shard,)), (theirs, (0, 1)), (theirs, (1, 1)), (theirs, (2, 1))],
                    "rs_chip_add1_" + tag, (F32, BF16))
    got = rs_pair_swap(pb0, pb1, "rs_swap_" + tag)
    red = pair_add(p0, p1, got, core, "rs_pair_add_" + tag)
    return rs_pair_share(red, "rs_share_" + tag).astype(F32)


def _rope_tables(lc, s):
    rows_n = s // GRID_W
    row = jnp.repeat(jnp.arange(rows_n, dtype=F32), GRID_W)
    col = jnp.tile(jnp.arange(GRID_W, dtype=F32), rows_n)
    axis_dim = HEAD_DIM // 2
    inv_freq = ROPE_THETA ** (-jnp.arange(0, axis_dim, 2, dtype=F32) / axis_dim)
    ang_r = row[:, None] * inv_freq[None, :]
    ang_c = col[:, None] * inv_freq[None, :]
    cr, sr, cc, sc = jnp.cos(ang_r), jnp.sin(ang_r), jnp.cos(ang_c), jnp.sin(ang_c)
    cos_l = jnp.concatenate([cr, cr, cc, cc], axis=1)
    sin_l = jnp.concatenate([-sr, sr, -sc, sc], axis=1)
    cos_t = jnp.concatenate([jnp.ones((lc, HEAD_DIM), F32), cos_l], axis=0)
    sin_t = jnp.concatenate([jnp.zeros((lc, HEAD_DIM), F32), sin_l], axis=0)
    return cos_t, sin_t


def _pad_rows(a, rows):
    return jnp.concatenate([a, jnp.zeros((rows - a.shape[0],) + a.shape[1:], a.dtype)], axis=0)


def _pad_cols(a, cols):
    return jnp.concatenate([a, jnp.zeros(a.shape[:-1] + (cols - a.shape[-1],), a.dtype)], axis=-1)


def kernel(x, c, ctx, c_ctx, w_mod, b_mod, post_ln_g, post_ln_b, w_in_e, conv_a_w, conv_a_b, norm_a_g, norm_a_b, conv_b_w, w_out_e, w_in_o, q_norm_g, k_norm_g, w_out_o, loss_target, m_c_ctx, m_w_mod, m_b_mod, m_post_ln_g, m_post_ln_b, m_w_in_e, m_conv_a_w, m_conv_a_b, m_norm_a_g, m_norm_a_b, m_conv_b_w, m_w_out_e, m_w_in_o, m_q_norm_g, m_k_norm_g, m_w_out_o, v_c_ctx, v_w_mod, v_b_mod, v_post_ln_g, v_post_ln_b, v_w_in_e, v_conv_a_w, v_conv_a_b, v_norm_a_g, v_norm_a_b, v_conv_b_w, v_w_out_e, v_w_in_o, v_q_norm_g, v_k_norm_g, v_w_out_o):
    s, d = x.shape[1], x.shape[2]
    lc = ctx.shape[1]
    t = lc + s
    tm = lc
    depth = w_mod.shape[0]
    n_even, n_odd = w_in_e.shape[0], w_in_o.shape[0]
    ad = w_out_o.shape[1] * 4
    kvd = (w_in_o.shape[2] * 4 - 2 * ad) // 2
    nkv = kvd // HEAD_DIM
    hpg = ad // nkv // HEAD_DIM
    nlm = w_mod.shape[2]
    alpha = (2.0 * depth) ** 0.25
    scale = HEAD_DIM ** -0.5
    assert n_even == 2 and n_odd == 2 and depth == 4 and hpg == GQA_GROUP
    assert lc % CONV_ROWS == 0 and s % tm == 0 and d % LANES == 0

    xi, yi, ci = _place()
    shard = 2 * xi + yi
    dev = 4 * xi + 2 * yi + ci

    wg_in_e = gather_weight(w_in_e.astype(BF16), "gather_w_in_e")
    wg_out_e = gather_weight(w_out_e.astype(BF16), "gather_w_out_e")
    wg_in_o = gather_weight(w_in_o.astype(BF16), "gather_w_in_o")
    wg_out_o = gather_weight(w_out_o.astype(BF16), "gather_w_out_o")

    c_all = all_gather8(_pad_rows(c, 8), "gather_c")
    c16 = _pad_rows(jnp.concatenate([c_all[0::8], c_ctx[None, :]], axis=0), 16)
    m_part = adaln_fwd(c16, w_mod)
    m_all = all_gather8(m_part.reshape(depth * 16, nlm), "gather_mod")
    m_all = m_all.reshape(8, depth, 16, nlm)[0::2]
    m_full = m_all.transpose(1, 2, 0, 3).reshape(depth, 16, 4 * nlm) + b_mod[:, None, :]
    m_lat = lax.dynamic_index_in_dim(m_full, dev, axis=1, keepdims=False)
    m_ctx = m_full[:, 8]

    def seg2(l, part):
        return jnp.stack([m_ctx[l, part * d:(part + 1) * d], m_lat[l, part * d:(part + 1) * d]])[:, None, :]

    cos_t, sin_t = _rope_tables(lc, s)

    small_gathered = all_gather8(
        _pad_rows(jnp.concatenate([conv_a_w.reshape(n_even * CONV_A_TAPS, -1),
                                   conv_b_w.reshape(n_even * CONV_B_TAPS, -1)], axis=0), 72), "gather_taps")
    taps = small_gathered.reshape(8, 72, -1)[0::2]
    taps = taps.transpose(1, 0, 2).reshape(72, d)
    caw = taps[:n_even * CONV_A_TAPS].reshape(n_even, CONV_A_TAPS, d)
    cbw = taps[n_even * CONV_A_TAPS:n_even * (CONV_A_TAPS + CONV_B_TAPS)].reshape(n_even, CONV_B_TAPS, d)
    caw_pad = jnp.concatenate([caw, jnp.zeros((n_even, 32 - CONV_A_TAPS, d), F32)], axis=1)
    cbw_pad = jnp.concatenate([cbw, jnp.zeros((n_even, 8 - CONV_B_TAPS, d), F32)], axis=1)

    xc = jnp.concatenate([ctx[0], x[0]], axis=0)
    saved = []
    for l in range(depth):
        i = l // 2
        shift2, scale2, gate2 = seg2(l, 0), seg2(l, 1), seg2(l, 2)
        h = mod_fwd(xc, scale2, shift2, tm)
        if l % 2 == 0:
            p = mm_nn(h, wg_in_e[i], BF16)
            u1 = conv_a_fwd(p, caw_pad[i], conv_a_b[i][None, :], lc, d)
            a_out = ln_a_fwd(u1, p, norm_a_g[i][None, :], norm_a_b[i][None, :], tm)
            b_out = conv_b_fwd(p, cbw_pad[i], lc, d)
            ab = jnp.concatenate([a_out, b_out], axis=1)
            y = mm_nn(ab, wg_out_e[i].reshape(1, 2 * d, d), F32)
            saved.append(dict(xc=xc, h=h, p=p, u1=u1, ab=ab, y=y))
        else:
            qgkv = mm_nn(h, wg_in_o[i], BF16)
            qr, kr = qk_fwd(qgkv, q_norm_g[i][None, :], k_norm_g[i][None, :], cos_t, sin_t, ad, kvd, tm,
                            scale * LOG2_E)
            vt = _chunks_t(qgkv[:, 2 * ad + kvd:], nkv, tm)
            o, og, lse = flash_fwd(qr, kr, vt, qgkv, ad, kvd, tm)
            y = mm_nn(og, wg_out_o[i].reshape(1, ad, d), F32)
            saved.append(dict(xc=xc, h=h, qgkv=qgkv, qr=qr, kr=kr, o=o, og=og, lse=lse, y=y))
        xc = post_ln_fwd(xc, y, gate2, post_ln_g[l][None, :], post_ln_b[l][None, :], alpha, tm)

    dxc, loss_acc = loss_head(xc, loss_target[0], lc, tm)

    g_in_e, g_out_e, g_in_o, g_out_o = [None] * 2, [None] * 2, [None] * 2, [None] * 2
    d_mod_lat, d_mod_ctx = [None] * depth, [None] * depth
    d_pln_g, d_pln_b = [None] * depth, [None] * depth
    d_cab, d_nag, d_nab, d_caw, d_cbw = [None] * 2, [None] * 2, [None] * 2, [None] * 2, [None] * 2
    d_qg, d_kg = [None] * 2, [None] * 2
    for l in reversed(range(depth)):
        i = l // 2
        sv = saved[l]
        scale2, gate2 = seg2(l, 1), seg2(l, 2)
        dzx, dy, acc_ln = post_ln_bwd(dxc, sv["xc"], sv["y"], gate2, post_ln_g[l][None, :], alpha, tm)
        d_pln_g[l] = acc_ln[0, 1] + acc_ln[1, 1]
        d_pln_b[l] = acc_ln[0, 2] + acc_ln[1, 2]
        if l % 2 == 0:
            w_out3 = wg_out_e[i].reshape(1, 2 * d, d)
            dab = mm_nt(dy, w_out3, BF16)
            g_out_e[i] = tuple(g.reshape(4, 2 * d // 4, d) for g in mm_tn(sv["ab"], dy, 1))
            du1, d_agate, acc_a = ln_a_bwd(dab, sv["u1"], sv["p"], norm_a_g[i][None, :], norm_a_b[i][None, :], tm)
            d_nag[i], d_nab[i], d_cab[i] = acc_a[0], acc_a[1], acc_a[2]
            d_aval, d_aglu, d_caw[i] = conv_a_bwd(du1, sv["p"], caw_pad[i], lc, d)
            d_bx, d_bb, d_bc, d_bg, d_cbw[i] = conv_b_bwd(dab, sv["p"], cbw_pad[i], lc, d)
            dp = jnp.concatenate([d_aval, d_aglu, d_agate, d_bx, d_bb, d_bc, d_bg], axis=1)
            dh = mm_nt(dp, wg_in_e[i], F32)
            g_in_e[i] = tuple(mm_tn(sv["h"], dp, 4))
        else:
            w_out3 = wg_out_o[i].reshape(1, ad, d)
            dog = mm_nt(dy, w_out3, BF16)
            g_out_o[i] = tuple(g.reshape(4, ad // 4, d) for g in mm_tn(sv["og"], dy, 1))
            do, dgate, delta = gate_bwd(dog, sv["o"], sv["qgkv"], ad, kvd, tm)
            dqr, dkr, dv = flash_bwd(sv["qr"], do, sv["kr"], _chunks_t(sv["kr"], nkv, tm), sv["qgkv"], sv["lse"],
                                     _rows_per_head(delta, tm, hpg), ad, kvd, tm, scale)
            dq, dk, acc_qk = qk_bwd(dqr, dkr, sv["qgkv"], q_norm_g[i][None, :], k_norm_g[i][None, :],
                                    cos_t, sin_t, ad, kvd, tm)
            d_qg[i], d_kg[i] = acc_qk[0], acc_qk[1]
            dqgkv = jnp.concatenate([dq, dgate, dk, dv.astype(BF16)], axis=1)
            dh = mm_nt(dqgkv, wg_in_o[i], F32)
            g_in_o[i] = tuple(mm_tn(sv["h"], dqgkv, 4))
        dxc, acc_mod = mod_bwd(dh, dzx, sv["xc"], scale2, tm)
        d_mod_ctx[l] = jnp.stack([acc_mod[0, 0], acc_mod[0, 1], acc_ln[0, 0]])
        d_mod_lat[l] = jnp.stack([acc_mod[1, 0], acc_mod[1, 1], acc_ln[1, 0]])
    grad_x = dxc[lc:][None]

    parts = [jnp.concatenate(d_mod_ctx, axis=0), jnp.stack(d_pln_g), jnp.stack(d_pln_b), jnp.stack(d_cab),
             jnp.stack(d_nag), jnp.stack(d_nab), jnp.concatenate(d_caw, axis=0), jnp.concatenate(d_cbw, axis=0),
             _pad_cols(jnp.stack(d_qg), d), _pad_cols(jnp.stack(d_kg), d), jnp.concatenate(d_mod_lat, axis=0),
             _pad_cols(loss_acc, d)]
    parts = [_pad_rows(p, -(-p.shape[0] // SUBLANES) * SUBLANES) for p in parts]
    offs = [0]
    for p in parts:
        offs.append(offs[-1] + p.shape[0])
    pack = jnp.concatenate(parts, axis=0)
    npack = offs[-1]
    gathered = all_gather8(pack, "gather_small").reshape(8, npack, d)
    small = sum_leading(gathered, "sum_small")

    def piece(k, rows):
        return small[offs[k]:offs[k] + rows]

    loss = 0.5 / d * jnp.sum(piece(11, 1)[0, :LANES])
    dm_ctx = piece(0, 3 * depth).reshape(depth, 1, 3 * d)
    dm_lat = gathered[:, offs[10]:offs[10] + 3 * depth].reshape(8, depth, 3 * d).transpose(1, 0, 2)
    dm = jnp.concatenate([dm_lat, dm_ctx, jnp.zeros((depth, 7, 3 * d), F32)], axis=1)
    g_b_mod = sum_leading(dm.transpose(1, 0, 2), "sum_b_mod")
    dm_shard = lax.dynamic_slice_in_dim(dm, shard * nlm, nlm, axis=2)
    g_w_mod, dc_part = adaln_bwd(c16.T, dm_shard, w_mod)
    dc_all = all_gather8(_pad_rows(dc_part[:, 8, :], 8), "gather_dc").reshape(8, 8, d)
    g_c_ctx = c_ctx_grad(dc_all[0::2, :depth].reshape(4 * depth, 1, d), c_ctx[None, :])[0]

    g_pln_g, g_pln_b = piece(1, depth), piece(2, depth)
    g_cab, g_nag, g_nab = piece(3, n_even), piece(4, n_even), piece(5, n_even)
    dch = d // 4
    g_caw = lax.dynamic_slice_in_dim(piece(6, 64).reshape(2, 32, d)[:, :CONV_A_TAPS], shard * dch, dch, axis=2)
    g_cbw = lax.dynamic_slice_in_dim(piece(7, 16).reshape(2, 8, d)[:, :CONV_B_TAPS], shard * dch, dch, axis=2)
    g_qg, g_kg = piece(8, n_odd)[:, :HEAD_DIM], piece(9, n_odd)[:, :HEAD_DIM]

    g_w_in_e = reduce_scatter_grad(g_in_e[0], g_in_e[1], shard, ci, "in_e")
    g_w_out_e = reduce_scatter_grad(g_out_e[0], g_out_e[1], shard, ci, "out_e")
    g_w_in_o = reduce_scatter_grad(g_in_o[0], g_in_o[1], shard, ci, "in_o")
    g_w_out_o = reduce_scatter_grad(g_out_o[0], g_out_o[1], shard, ci, "out_o")

    grads = [g_c_ctx, g_w_mod, g_b_mod, g_pln_g, g_pln_b, g_w_in_e, g_caw, g_cab, g_nag, g_nab, g_cbw,
             g_w_out_e, g_w_in_o, g_qg, g_kg, g_w_out_o]
    weights = [c_ctx, w_mod, b_mod, post_ln_g, post_ln_b, w_in_e, conv_a_w, conv_a_b, norm_a_g, norm_a_b,
               conv_b_w, w_out_e, w_in_o, q_norm_g, k_norm_g, w_out_o]
    ms = [m_c_ctx, m_w_mod, m_b_mod, m_post_ln_g, m_post_ln_b, m_w_in_e, m_conv_a_w, m_conv_a_b, m_norm_a_g,
          m_norm_a_b, m_conv_b_w, m_w_out_e, m_w_in_o, m_q_norm_g, m_k_norm_g, m_w_out_o]
    vs = [v_c_ctx, v_w_mod, v_b_mod, v_post_ln_g, v_post_ln_b, v_w_in_e, v_conv_a_w, v_conv_a_b, v_norm_a_g,
          v_norm_a_b, v_conv_b_w, v_w_out_e, v_w_in_o, v_q_norm_g, v_k_norm_g, v_w_out_o]
    deltas, new_ms, new_vs = [], [], []
    for wv, gv, mv, vv in zip(weights, grads, ms, vs):
        dl, nm, nv = adamw(wv, gv, mv, vv)
        deltas.append(dl)
        new_ms.append(nm)
        new_vs.append(nv)
    return (loss, grad_x, *grads, *deltas, *new_ms, *new_vs)
```

```python
import functools

import jax
import jax.numpy as jnp
from jax import lax
from jax.experimental import pallas as pl
from jax.experimental.pallas import tpu as pltpu

F32 = jnp.float32
BF16 = jnp.bfloat16

LANES = 128
SUBLANES = 8
HEAD_DIM = 128
GQA_GROUP = 4
GRID_W = 64
ROPE_THETA = 10000.0
LN_EPS = 1e-5
RMS_EPS = 1e-6
CONV_A_TAPS = 31
CONV_B_TAPS = 3
HALO = 16
CONV_ROWS = 128
ADAM_LR = 0.001
ADAM_B1 = 0.9
ADAM_B2 = 0.999
ADAM_EPS = 1e-08
ADAM_WD = 0.01
ADAM_STEP = 10
VMEM_LIMIT = 56 * 1024 * 1024
D2D_PARTS = 4
KV_CHUNKS_PER_STEP = 4
LOG2_E = 1.4426950408889634
LN_2 = 0.6931471805599453
MESH = pl.DeviceIdType.MESH
ANY = pl.BlockSpec(memory_space=pl.ANY)
VMEM_FULL = pl.BlockSpec(memory_space=pltpu.VMEM)


def _sds(shape, dtype):
    return jax.ShapeDtypeStruct(tuple(shape), dtype)


def _cp(*sem):
    return pltpu.CompilerParams(dimension_semantics=sem, vmem_limit_bytes=VMEM_LIMIT)


def _pick(n, cands):
    for c in cands:
        if n % c == 0:
            return c
    return n


def _sigmoid(x):
    return 1.0 / (1.0 + jnp.exp(-x))


def _silu(x):
    return x * _sigmoid(x)


def _dsilu(x):
    s = _sigmoid(x)
    return s * (1.0 + x * (1.0 - s))


def _row(tm, d):
    return pl.BlockSpec((tm, d), lambda j: (j, 0))


def _seg(d):
    return pl.BlockSpec((None, 1, d), lambda j: (jnp.minimum(j, 1), 0, 0))


def _vec(d):
    return pl.BlockSpec((1, d), lambda j: (0, 0))


def _colblk(tm, width, blk):
    return pl.BlockSpec((tm, width), lambda j: (j, blk))


def _seg_acc(d):
    return pl.BlockSpec((None, SUBLANES, d), lambda j: (jnp.minimum(j, 1), 0, 0))


def _ln_stats(z):
    mu = jnp.mean(z, axis=-1, keepdims=True)
    zc = z - mu
    var = jnp.mean(zc * zc, axis=-1, keepdims=True)
    rstd = lax.rsqrt(var + LN_EPS)
    return zc * rstd, rstd


def _ln_bwd(dxh, xhat, rstd):
    m1 = jnp.mean(dxh, axis=-1, keepdims=True)
    m2 = jnp.mean(dxh * xhat, axis=-1, keepdims=True)
    return rstd * (dxh - m1 - xhat * m2)


def _colsum(v):
    return jnp.sum(v, axis=0, keepdims=True)


def mod_fwd(xc, scale2, shift2, tm):
    t, d = xc.shape

    def body(x_ref, sc_ref, sh_ref, h_ref):
        h_ref[...] = (x_ref[...] * (1.0 + sc_ref[...]) + sh_ref[...]).astype(h_ref.dtype)

    return pl.pallas_call(
        body, name="mod_fwd", grid=(t // tm,),
        in_specs=[_row(tm, d), _seg(d), _seg(d)], out_specs=_row(tm, d),
        out_shape=_sds((t, d), BF16), compiler_params=_cp("parallel"))(xc, scale2, shift2)


def post_ln_fwd(xc, y, gate2, g, b, alpha, tm):
    t, d = xc.shape

    def body(x_ref, y_ref, gt_ref, g_ref, b_ref, o_ref):
        z = alpha * x_ref[...] + gt_ref[...] * y_ref[...]
        xhat, _ = _ln_stats(z)
        o_ref[...] = xhat * g_ref[...] + b_ref[...]

    return pl.pallas_call(
        body, name="post_ln_fwd", grid=(t // tm,),
        in_specs=[_row(tm, d), _row(tm, d), _seg(d), _vec(d), _vec(d)], out_specs=_row(tm, d),
        out_shape=_sds((t, d), F32), compiler_params=_cp("parallel"))(xc, y, gate2, g, b)


def post_ln_bwd(dout, xc, y, gate2, g, alpha, tm):
    t, d = xc.shape

    def body(do_ref, x_ref, y_ref, gt_ref, g_ref, dzx_ref, dy_ref, acc_ref):
        @pl.when(pl.program_id(0) <= 1)
        def _():
            acc_ref[...] = jnp.zeros_like(acc_ref)

        yv = y_ref[...]
        gate = gt_ref[...]
        xhat, rstd = _ln_stats(alpha * x_ref[...] + gate * yv)
        dout = do_ref[...]
        dz = _ln_bwd(dout * g_ref[...], xhat, rstd)
        dzx_ref[...] = alpha * dz
        dy_ref[...] = (gate * dz).astype(dy_ref.dtype)
        acc_ref[0:1, :] += _colsum(dz * yv)
        acc_ref[1:2, :] += _colsum(dout * xhat)
        acc_ref[2:3, :] += _colsum(dout)

    return pl.pallas_call(
        body, name="post_ln_bwd", grid=(t // tm,),
        in_specs=[_row(tm, d), _row(tm, d), _row(tm, d), _seg(d), _vec(d)],
        out_specs=[_row(tm, d), _row(tm, d), _seg_acc(d)],
        out_shape=[_sds((t, d), F32), _sds((t, d), BF16), _sds((2, SUBLANES, d), F32)],
        compiler_params=_cp("arbitrary"))(dout, xc, y, gate2, g)


def mod_bwd(dh, dzx, xc, scale2, tm):
    t, d = xc.shape

    def body(dh_ref, dzx_ref, x_ref, sc_ref, dx_ref, acc_ref):
        @pl.when(pl.program_id(0) <= 1)
        def _():
            acc_ref[...] = jnp.zeros_like(acc_ref)

        dhv = dh_ref[...].astype(F32)
        dx_ref[...] = dzx_ref[...] + dhv * (1.0 + sc_ref[...])
        acc_ref[0:1, :] += _colsum(dhv)
        acc_ref[1:2, :] += _colsum(dhv * x_ref[...])

    return pl.pallas_call(
        body, name="mod_bwd", grid=(t // tm,),
        in_specs=[_row(tm, d), _row(tm, d), _row(tm, d), _seg(d)],
        out_specs=[_row(tm, d), _seg_acc(d)],
        out_shape=[_sds((t, d), F32), _sds((2, SUBLANES, d), F32)],
        compiler_params=_cp("arbitrary"))(dh, dzx, xc, scale2)


def loss_head(xc, target, lc, tm):
    t, d = xc.shape

    def body(x_ref, t_ref, dx_ref, acc_ref):
        j = pl.program_id(0)

        @pl.when(j == 0)
        def _():
            acc_ref[...] = jnp.zeros_like(acc_ref)
            dx_ref[...] = jnp.zeros_like(dx_ref)

        @pl.when(j > 0)
        def _():
            err = x_ref[...] - t_ref[...]
            dx_ref[...] = err * (1.0 / d)
            col = _colsum(err * err)
            tot = col[:, 0:LANES]
            for k in range(1, d // LANES):
                tot = tot + col[:, k * LANES:(k + 1) * LANES]
            acc_ref[0:1, :] += tot

    nlc = lc // tm
    return pl.pallas_call(
        body, name="loss_head", grid=(t // tm,),
        in_specs=[_row(tm, d), pl.BlockSpec((tm, d), lambda j: (jnp.maximum(j - nlc, 0), 0))],
        out_specs=[_row(tm, d), pl.BlockSpec((SUBLANES, LANES), lambda j: (0, 0))],
        out_shape=[_sds((t, d), F32), _sds((SUBLANES, LANES), F32)],
        compiler_params=_cp("arbitrary"))(xc, target)


def mm_nn(a, w3, out_dtype):
    m, k = a.shape
    ns, _, nl = w3.shape
    tm = _pick(m, (768, 512, 256, 128))
    tn = _pick(nl, (512, 256, 128))
    npj = nl // tn

    def body(a_ref, w_ref, o_ref):
        o_ref[...] = jnp.dot(a_ref[...], w_ref[...], preferred_element_type=F32).astype(o_ref.dtype)

    return pl.pallas_call(
        body, name="mm_nn", grid=(m // tm, ns * npj),
        in_specs=[pl.BlockSpec((tm, k), lambda i, j: (i, 0)),
                  pl.BlockSpec((None, k, tn), lambda i, j: (j // npj, 0, j % npj))],
        out_specs=pl.BlockSpec((tm, tn), lambda i, j: (i, j)),
        out_shape=_sds((m, ns * nl), out_dtype), compiler_params=_cp("parallel", "parallel"))(a, w3)


def mm_nt(a, w3, out_dtype):
    m, _ = a.shape
    ns, k, nl = w3.shape
    tm = _pick(m, (768, 512, 256, 128))
    tk = _pick(k, (2048, 1024, 512, 256, 128))
    tn = _pick(nl, (1792, 1280, 1024, 512, 256, 128))
    npj = nl // tn
    nsteps = ns * npj

    def body(a_ref, w_ref, o_ref, acc_ref):
        n = pl.program_id(2)

        @pl.when(n == 0)
        def _():
            acc_ref[...] = jnp.zeros_like(acc_ref)

        acc_ref[...] += lax.dot_general(a_ref[...], w_ref[...], (((1,), (1,)), ((), ())),
                                        preferred_element_type=F32)

        @pl.when(n == nsteps - 1)
        def _():
            o_ref[...] = acc_ref[...].astype(o_ref.dtype)

    return pl.pallas_call(
        body, name="mm_nt", grid=(m // tm, k // tk, nsteps),
        in_specs=[pl.BlockSpec((tm, tn), lambda i, kk, n: (i, n)),
                  pl.BlockSpec((None, tk, tn), lambda i, kk, n: (n // npj, kk, n % npj))],
        out_specs=pl.BlockSpec((tm, tk), lambda i, kk, n: (i, kk)),
        out_shape=_sds((m, k), out_dtype), scratch_shapes=[pltpu.VMEM((tm, tk), F32)],
        compiler_params=_cp("parallel", "parallel", "arbitrary"))(a, w3)


def mm_tn(a, b, ns):
    m, k = a.shape
    nl = b.shape[1] // ns
    tm = _pick(m, (1408, 768, 512, 256, 128))
    tk = _pick(k, (1024, 512, 256, 128))
    tn = _pick(nl, (1792, 1280, 1024, 512, 256, 128))
    npj = nl // tn
    nsteps = m // tm

    def body(a_ref, b_ref, o_ref, ob_ref):
        r = pl.program_id(2)

        @pl.when(r == 0)
        def _():
            o_ref[...] = jnp.zeros_like(o_ref)

        o_ref[...] += lax.dot_general(a_ref[...], b_ref[...], (((0,), (0,)), ((), ())),
                                      preferred_element_type=F32)

        @pl.when(r == nsteps - 1)
        def _():
            ob_ref[...] = o_ref[...].astype(ob_ref.dtype)

    out_spec = pl.BlockSpec((None, tk, tn), lambda i, j, r: (j // npj, i, j % npj))
    return pl.pallas_call(
        body, name="mm_tn", grid=(k // tk, ns * npj, nsteps),
        in_specs=[pl.BlockSpec((tm, tk), lambda i, j, r: (r, i)),
                  pl.BlockSpec((tm, tn), lambda i, j, r: (r, j))],
        out_specs=[out_spec, out_spec],
        out_shape=[_sds((ns, k, nl), F32), _sds((ns, k, nl), BF16)],
        compiler_params=_cp("parallel", "parallel", "arbitrary"))(a, b)


def _win_start(j, ncc):
    return pl.multiple_of(j * CONV_ROWS + jnp.where(j >= ncc, HALO, 0), SUBLANES)


def _tok_start(j):
    return pl.multiple_of(j * CONV_ROWS, CONV_ROWS)


def _shifted(xw, off):
    n = xw.shape[0]
    sh = (n - off) % n
    y = pltpu.roll(xw, sh, 0) if sh else xw
    return y[:CONV_ROWS]


def _conv_fwd(xw, w_ref, ntaps):
    pad = ntaps // 2
    acc = None
    for k in range(ntaps):
        term = w_ref[k:k + 1, :] * _shifted(xw, HALO + k - pad)
        acc = term if acc is None else acc + term
    return acc


def _conv_bwd_data(xw, w_ref, ntaps):
    pad = ntaps // 2
    acc = None
    for k in range(ntaps):
        term = w_ref[k:k + 1, :] * _shifted(xw, HALO - k + pad)
        acc = term if acc is None else acc + term
    return acc


def _conv_bwd_weight(dw_ref, d, xw, ntaps):
    pad = ntaps // 2
    for k in range(ntaps):
        dw_ref[k:k + 1, :] += _colsum(d * _shifted(xw, HALO + k - pad))


def _zero_halos(pad_ref, lc, t):
    z = jnp.zeros((HALO, LANES), F32)
    pad_ref[0:HALO, :] = z
    pad_ref[HALO + lc:2 * HALO + lc, :] = z
    pad_ref[2 * HALO + t:3 * HALO + t, :] = z


def _pad_dst(j, ncc):
    return pl.multiple_of(j * CONV_ROWS + HALO + jnp.where(j >= ncc, HALO, 0), SUBLANES)


def _chan(t, blk0):
    return pl.BlockSpec((t, LANES), lambda ct: (0, blk0 + ct))


def _tapw(rows):
    return pl.BlockSpec((rows, LANES), lambda ct: (0, ct))


def conv_a_fwd(p, w_pad, bias, lc, d):
    t = p.shape[0]
    nct, nch, ncc = d // LANES, t // CONV_ROWS, lc // CONV_ROWS

    def body(av_ref, ag_ref, w_ref, b_ref, u1_ref, pad_ref):
        _zero_halos(pad_ref, lc, t)

        def fill(j, carry):
            rows = pl.ds(_tok_start(j), CONV_ROWS)
            u0 = av_ref[rows, :].astype(F32) * _sigmoid(ag_ref[rows, :].astype(F32))
            pad_ref[pl.ds(_pad_dst(j, ncc), CONV_ROWS), :] = u0
            return carry

        lax.fori_loop(0, nch, fill, 0)

        def conv(j, carry):
            xw = pad_ref[pl.ds(_win_start(j, ncc), CONV_ROWS + 2 * HALO), :]
            u1_ref[pl.ds(_tok_start(j), CONV_ROWS), :] = _conv_fwd(xw, w_ref, CONV_A_TAPS) + b_ref[...]
            return carry

        lax.fori_loop(0, nch, conv, 0)

    return pl.pallas_call(
        body, name="conv_a_fwd", grid=(nct,),
        in_specs=[_chan(t, 0), _chan(t, nct), _tapw(32), _tapw(1)],
        out_specs=_chan(t, 0), out_shape=_sds((t, d), F32),
        scratch_shapes=[pltpu.VMEM((t + 3 * HALO, LANES), F32)],
        compiler_params=_cp("parallel"))(p, p, w_pad, bias)


def conv_b_fwd(p, w_pad, lc, d):
    t = p.shape[0]
    nct, nch, ncc = d // LANES, t // CONV_ROWS, lc // CONV_ROWS

    def body(bx_ref, bb_ref, bc_ref, bg_ref, w_ref, o_ref, pad_ref):
        _zero_halos(pad_ref, lc, t)

        def fill(j, carry):
            rows = pl.ds(_tok_start(j), CONV_ROWS)
            pad_ref[pl.ds(_pad_dst(j, ncc), CONV_ROWS), :] = (
                bc_ref[rows, :].astype(F32) * bx_ref[rows, :].astype(F32))
            return carry

        lax.fori_loop(0, nch, fill, 0)

        def conv(j, carry):
            rows = pl.ds(_tok_start(j), CONV_ROWS)
            xw = pad_ref[pl.ds(_win_start(j, ncc), CONV_ROWS + 2 * HALO), :]
            v = _conv_fwd(xw, w_ref, CONV_B_TAPS)
            o_ref[rows, :] = (bb_ref[rows, :].astype(F32) * v
                              * _silu(bg_ref[rows, :].astype(F32))).astype(o_ref.dtype)
            return carry

        lax.fori_loop(0, nch, conv, 0)

    return pl.pallas_call(
        body, name="conv_b_fwd", grid=(nct,),
        in_specs=[_chan(t, 3 * nct), _chan(t, 4 * nct), _chan(t, 5 * nct), _chan(t, 6 * nct), _tapw(8)],
        out_specs=_chan(t, 0), out_shape=_sds((t, d), BF16),
        scratch_shapes=[pltpu.VMEM((t + 3 * HALO, LANES), F32)],
        compiler_params=_cp("parallel"))(p, p, p, p, w_pad)


def ln_a_fwd(u1, p, g, b, tm):
    t, d = u1.shape

    def body(u_ref, ag_ref, g_ref, b_ref, o_ref):
        xhat, _ = _ln_stats(u_ref[...])
        u2 = xhat * g_ref[...] + b_ref[...]
        o_ref[...] = (_silu(u2) * _silu(ag_ref[...].astype(F32))).astype(o_ref.dtype)

    return pl.pallas_call(
        body, name="ln_a_fwd", grid=(t // tm,),
        in_specs=[_row(tm, d), _colblk(tm, d, 2), _vec(d), _vec(d)], out_specs=_row(tm, d),
        out_shape=_sds((t, d), BF16), compiler_params=_cp("parallel"))(u1, p, g, b)


def ln_a_bwd(dab, u1, p, g, b, tm):
    t, d = u1.shape

    def body(da_ref, u_ref, ag_ref, g_ref, b_ref, du_ref, dag_ref, acc_ref):
        @pl.when(pl.program_id(0) == 0)
        def _():
            acc_ref[...] = jnp.zeros_like(acc_ref)

        xhat, rstd = _ln_stats(u_ref[...])
        u2 = xhat * g_ref[...] + b_ref[...]
        ag = ag_ref[...].astype(F32)
        da = da_ref[...].astype(F32)
        dag_ref[...] = (da * _silu(u2) * _dsilu(ag)).astype(dag_ref.dtype)
        du2 = da * _silu(ag) * _dsilu(u2)
        du1 = _ln_bwd(du2 * g_ref[...], xhat, rstd)
        du_ref[...] = du1
        acc_ref[0:1, :] += _colsum(du2 * xhat)
        acc_ref[1:2, :] += _colsum(du2)
        acc_ref[2:3, :] += _colsum(du1)

    return pl.pallas_call(
        body, name="ln_a_bwd", grid=(t // tm,),
        in_specs=[_colblk(tm, d, 0), _row(tm, d), _colblk(tm, d, 2), _vec(d), _vec(d)],
        out_specs=[_row(tm, d), _row(tm, d), pl.BlockSpec((SUBLANES, d), lambda j: (0, 0))],
        out_shape=[_sds((t, d), F32), _sds((t, d), BF16), _sds((SUBLANES, d), F32)],
        compiler_params=_cp("arbitrary"))(dab, u1, p, g, b)


def conv_a_bwd(du1, p, w_pad, lc, d):
    t = p.shape[0]
    nct, nch, ncc = d // LANES, t // CONV_ROWS, lc // CONV_ROWS

    def body(du_ref, av_ref, ag_ref, w_ref, dav_ref, dag_ref, dw_ref, pad_u, pad_d):
        _zero_halos(pad_u, lc, t)
        _zero_halos(pad_d, lc, t)
        dw_ref[...] = jnp.zeros_like(dw_ref)

        def fill(j, carry):
            rows = pl.ds(_tok_start(j), CONV_ROWS)
            dst = pl.ds(_pad_dst(j, ncc), CONV_ROWS)
            pad_u[dst, :] = av_ref[rows, :].astype(F32) * _sigmoid(ag_ref[rows, :].astype(F32))
            pad_d[dst, :] = du_ref[rows, :]
            return carry

        lax.fori_loop(0, nch, fill, 0)

        def step(j, carry):
            rows = pl.ds(_tok_start(j), CONV_ROWS)
            win = pl.ds(_win_start(j, ncc), CONV_ROWS + 2 * HALO)
            du0 = _conv_bwd_data(pad_d[win, :], w_ref, CONV_A_TAPS)
            sig = _sigmoid(ag_ref[rows, :].astype(F32))
            dav_ref[rows, :] = (du0 * sig).astype(dav_ref.dtype)
            dag_ref[rows, :] = (du0 * av_ref[rows, :].astype(F32) * sig * (1.0 - sig)).astype(dag_ref.dtype)
            _conv_bwd_weight(dw_ref, du_ref[rows, :], pad_u[win, :], CONV_A_TAPS)
            return carry

        lax.fori_loop(0, nch, step, 0)

    return pl.pallas_call(
        body, name="conv_a_bwd", grid=(nct,),
        in_specs=[_chan(t, 0), _chan(t, 0), _chan(t, nct), _tapw(32)],
        out_specs=[_chan(t, 0), _chan(t, 0), _tapw(32)],
        out_shape=[_sds((t, d), BF16), _sds((t, d), BF16), _sds((32, d), F32)],
        scratch_shapes=[pltpu.VMEM((t + 3 * HALO, LANES), F32), pltpu.VMEM((t + 3 * HALO, LANES), F32)],
        compiler_params=_cp("parallel"))(du1, p, p, w_pad)


def conv_b_bwd(dab, p, w_pad, lc, d):
    t = p.shape[0]
    nct, nch, ncc = d // LANES, t // CONV_ROWS, lc // CONV_ROWS

    def body(db_ref, bx_ref, bb_ref, bc_ref, bg_ref, w_ref,
             dbx_ref, dbb_ref, dbc_ref, dbg_ref, dw_ref, pad_t, pad_d):
        _zero_halos(pad_t, lc, t)
        _zero_halos(pad_d, lc, t)
        dw_ref[...] = jnp.zeros_like(dw_ref)

        def fill(j, carry):
            rows = pl.ds(_tok_start(j), CONV_ROWS)
            pad_t[pl.ds(_pad_dst(j, ncc), CONV_ROWS), :] = (
                bc_ref[rows, :].astype(F32) * bx_ref[rows, :].astype(F32))
            return carry

        lax.fori_loop(0, nch, fill, 0)

        def first(j, carry):
            rows = pl.ds(_tok_start(j), CONV_ROWS)
            xw = pad_t[pl.ds(_win_start(j, ncc), CONV_ROWS + 2 * HALO), :]
            v = _conv_fwd(xw, w_ref, CONV_B_TAPS)
            bg = bg_ref[rows, :].astype(F32)
            bb = bb_ref[rows, :].astype(F32)
            db = db_ref[rows, :].astype(F32)
            sg = _silu(bg)
            dbb_ref[rows, :] = (db * v * sg).astype(dbb_ref.dtype)
            dbg_ref[rows, :] = (db * bb * v * _dsilu(bg)).astype(dbg_ref.dtype)
            dv = db * bb * sg
            pad_d[pl.ds(_pad_dst(j, ncc), CONV_ROWS), :] = dv
            _conv_bwd_weight(dw_ref, dv, xw, CONV_B_TAPS)
            return carry

        lax.fori_loop(0, nch, first, 0)

        def second(j, carry):
            rows = pl.ds(_tok_start(j), CONV_ROWS)
            dt = _conv_bwd_data(pad_d[pl.ds(_win_start(j, ncc), CONV_ROWS + 2 * HALO), :], w_ref, CONV_B_TAPS)
            dbc_ref[rows, :] = (dt * bx_ref[rows, :].astype(F32)).astype(dbc_ref.dtype)
            dbx_ref[rows, :] = (dt * bc_ref[rows, :].astype(F32)).astype(dbx_ref.dtype)
            return carry

        lax.fori_loop(0, nch, second, 0)

    return pl.pallas_call(
        body, name="conv_b_bwd", grid=(nct,),
        in_specs=[_chan(t, nct), _chan(t, 3 * nct), _chan(t, 4 * nct), _chan(t, 5 * nct), _chan(t, 6 * nct),
                  _tapw(8)],
        out_specs=[_chan(t, 0)] * 4 + [_tapw(8)],
        out_shape=[_sds((t, d), BF16)] * 4 + [_sds((8, d), F32)],
        scratch_shapes=[pltpu.VMEM((t + 3 * HALO, LANES), F32), pltpu.VMEM((t + 3 * HALO, LANES), F32)],
        compiler_params=_cp("parallel"))(dab, p, p, p, p, w_pad)


def _swap_halves(z, first_half):
    return jnp.where(first_half, pltpu.roll(z, 96, 1), pltpu.roll(z, 32, 1))


def _first_half_mask(rows):
    lane = lax.broadcasted_iota(jnp.int32, (rows, HEAD_DIM), 1)
    return (lane & 32) == 0


def qk_fwd(qgkv, qg, kg, cos_t, sin_t, ad, kvd, tm, qscale):
    t = qgkv.shape[0]

    def body(q_ref, k_ref, qg_ref, kg_ref, c_ref, s_ref, qo_ref, ko_ref):
        first = _first_half_mask(tm)
        cosv, sinv = c_ref[...], s_ref[...]

        def head(x, gain):
            inv = lax.rsqrt(jnp.mean(x * x, axis=-1, keepdims=True) + RMS_EPS)
            yv = x * inv * gain
            return yv * cosv + _swap_halves(yv, first) * sinv

        for h in range(ad // HEAD_DIM):
            sl = slice(h * HEAD_DIM, (h + 1) * HEAD_DIM)
            qo_ref[:, sl] = (head(q_ref[:, sl].astype(F32), qg_ref[...]) * qscale).astype(qo_ref.dtype)
        for h in range(kvd // HEAD_DIM):
            sl = slice(h * HEAD_DIM, (h + 1) * HEAD_DIM)
            ko_ref[:, sl] = head(k_ref[:, sl].astype(F32), kg_ref[...]).astype(ko_ref.dtype)

    return pl.pallas_call(
        body, name="qk_fwd", grid=(t // tm,),
        in_specs=[_colblk(tm, ad, 0), _colblk(tm, kvd, 2 * ad // kvd), _vec(HEAD_DIM), _vec(HEAD_DIM),
                  _row(tm, HEAD_DIM), _row(tm, HEAD_DIM)],
        out_specs=[_row(tm, ad), _row(tm, kvd)],
        out_shape=[_sds((t, ad), BF16), _sds((t, kvd), BF16)],
        compiler_params=_cp("parallel"))(qgkv, qgkv, qg, kg, cos_t, sin_t)


def qk_bwd(dqr, dkr, qgkv, qg, kg, cos_t, sin_t, ad, kvd, tm):
    t = qgkv.shape[0]

    def body(dq_ref, dk_ref, q_ref, k_ref, qg_ref, kg_ref, c_ref, s_ref, dqo_ref, dko_ref, acc_ref):
        @pl.when(pl.program_id(0) == 0)
        def _():
            acc_ref[...] = jnp.zeros_like(acc_ref)

        first = _first_half_mask(tm)
        cosv, sinv = c_ref[...], s_ref[...]

        def head(x, gain, dout):
            inv = lax.rsqrt(jnp.mean(x * x, axis=-1, keepdims=True) + RMS_EPS)
            xn = x * inv
            dy = dout * cosv + _swap_halves(dout * sinv, first)
            dxn = dy * gain
            dx = inv * (dxn - xn * jnp.mean(dxn * xn, axis=-1, keepdims=True))
            return dx, _colsum(dy * xn)

        dqg = jnp.zeros((1, HEAD_DIM), F32)
        for h in range(ad // HEAD_DIM):
            sl = slice(h * HEAD_DIM, (h + 1) * HEAD_DIM)
            dx, dg = head(q_ref[:, sl].astype(F32), qg_ref[...], dq_ref[:, sl])
            dqo_ref[:, sl] = dx.astype(dqo_ref.dtype)
            dqg = dqg + dg
        dkg = jnp.zeros((1, HEAD_DIM), F32)
        for h in range(kvd // HEAD_DIM):
            sl = slice(h * HEAD_DIM, (h + 1) * HEAD_DIM)
            dx, dg = head(k_ref[:, sl].astype(F32), kg_ref[...], dk_ref[:, sl])
            dko_ref[:, sl] = dx.astype(dko_ref.dtype)
            dkg = dkg + dg
        acc_ref[0:1, :] += dqg
        acc_ref[1:2, :] += dkg

    return pl.pallas_call(
        body, name="qk_bwd", grid=(t // tm,),
        in_specs=[_row(tm, ad), _row(tm, kvd), _colblk(tm, ad, 0), _colblk(tm, kvd, 2 * ad // kvd),
                  _vec(HEAD_DIM), _vec(HEAD_DIM), _row(tm, HEAD_DIM), _row(tm, HEAD_DIM)],
        out_specs=[_row(tm, ad), _row(tm, kvd), pl.BlockSpec((SUBLANES, HEAD_DIM), lambda j: (0, 0))],
        out_shape=[_sds((t, ad), BF16), _sds((t, kvd), BF16), _sds((SUBLANES, HEAD_DIM), F32)],
        compiler_params=_cp("arbitrary"))(dqr, dkr, qgkv, qgkv, qg, kg, cos_t, sin_t)


_NT = (((1,), (1,)), ((), ()))


def _chunks_t(a, nkv, tm):
    t = a.shape[0]
    return a.reshape(t // tm, tm, nkv, HEAD_DIM).transpose(2, 0, 3, 1)


def _tree_rows(x, op):
    slabs = [x[i:i + SUBLANES] for i in range(0, x.shape[0], SUBLANES)]
    while len(slabs) > 1:
        slabs = [op(slabs[i], slabs[i + 1]) for i in range(0, len(slabs), 2)]
    return slabs[0]


def flash_fwd(qr, kr, vt, qgkv, ad, kvd, tm):
    t = qr.shape[0]
    nkv = kvd // HEAD_DIM
    gw = ad // nkv
    hpg = gw // HEAD_DIM
    nt = t // tm
    gate_blk0 = ad // gw

    def body(q_ref, k_ref, vt_ref, g_ref, o_ref, og_ref, lse_ref):
        qi = pl.program_id(1)
        heads = [slice(h * HEAD_DIM, (h + 1) * HEAD_DIM) for h in range(hpg)]
        qs = [q_ref[:, sl] for sl in heads]

        def attend(kc, vtcs, carry):
            sts = [lax.dot_general(kc, q, _NT, preferred_element_type=F32) for q in qs]
            out = []
            for h in range(hpg):
                m, l, acc = carry[h]
                m_new = jnp.maximum(m, jnp.max(_tree_rows(sts[h], jnp.maximum), axis=0, keepdims=True))
                a = jnp.exp2(m - m_new)
                pt = jnp.exp2(sts[h] - m_new)
                l = a * l + jnp.sum(_tree_rows(pt, jnp.add), axis=0, keepdims=True)
                ptb = pt.astype(BF16)
                acc = a * acc
                for j, vtc in enumerate(vtcs):
                    acc = acc + jnp.dot(vtc, ptb[j * tm:(j + 1) * tm], preferred_element_type=F32)
                out.append((m_new, l, acc))
            return tuple(out)

        def step(c, carry):
            first = 1 + KV_CHUNKS_PER_STEP * c
            kc = k_ref[pl.ds(pl.multiple_of(first * tm, tm), KV_CHUNKS_PER_STEP * tm), :]
            return attend(kc, [vt_ref[first + j] for j in range(KV_CHUNKS_PER_STEP)], carry)

        init = tuple((jnp.full((1, tm), -1e30, F32), jnp.zeros((1, tm), F32), jnp.zeros((HEAD_DIM, tm), F32))
                     for _ in range(hpg))
        ctx_done = attend(k_ref[0:tm, :], [vt_ref[0]], init)
        res = lax.fori_loop(0, jnp.where(qi == 0, 0, (nt - 1) // KV_CHUNKS_PER_STEP), step, ctx_done)
        for h, sl in enumerate(heads):
            m, l, acc = res[h]
            o = (acc / l).T
            o_ref[:, sl] = o.astype(o_ref.dtype)
            og_ref[:, sl] = (o * _silu(g_ref[:, sl].astype(F32))).astype(og_ref.dtype)
            lse_ref[h:h + 1, :] = m + jnp.log(l) * LOG2_E

    return pl.pallas_call(
        body, name="flash_fwd", grid=(nkv, nt),
        in_specs=[pl.BlockSpec((tm, gw), lambda g, i: (i, g)),
                  pl.BlockSpec((t, HEAD_DIM), lambda g, i: (0, g)),
                  pl.BlockSpec((None, nt, HEAD_DIM, tm), lambda g, i: (g, 0, 0, 0)),
                  pl.BlockSpec((tm, gw), lambda g, i: (i, gate_blk0 + g))],
        out_specs=[pl.BlockSpec((tm, gw), lambda g, i: (i, g)),
                   pl.BlockSpec((tm, gw), lambda g, i: (i, g)),
                   pl.BlockSpec((None, None, hpg, tm), lambda g, i: (g, i, 0, 0))],
        out_shape=[_sds((t, ad), BF16), _sds((t, ad), BF16), _sds((nkv, nt, hpg, tm), F32)],
        compiler_params=_cp("parallel", "parallel"))(qr, kr, vt, qgkv)


def gate_bwd(dog, o, qgkv, ad, kvd, tm):
    t = o.shape[0]
    nkv = kvd // HEAD_DIM
    hpg = ad // nkv // HEAD_DIM

    def body(dog_ref, o_ref, g_ref, do_ref, dg_ref, dl_ref):
        lane = lax.broadcasted_iota(jnp.int32, (tm, LANES), 1)
        for grp in range(nkv):
            blk = jnp.zeros((tm, LANES), F32)
            for hh in range(hpg):
                h = grp * hpg + hh
                sl = slice(h * HEAD_DIM, (h + 1) * HEAD_DIM)
                dv = dog_ref[:, sl].astype(F32)
                ov = o_ref[:, sl].astype(F32)
                gv = g_ref[:, sl].astype(F32)
                doh = dv * _silu(gv)
                do_ref[:, sl] = doh.astype(do_ref.dtype)
                dg_ref[:, sl] = (dv * ov * _dsilu(gv)).astype(dg_ref.dtype)
                blk = jnp.where(lane == hh, jnp.sum(doh * ov, axis=-1, keepdims=True), blk)
            dl_ref[grp] = blk

    return pl.pallas_call(
        body, name="gate_bwd", grid=(t // tm,),
        in_specs=[_row(tm, ad), _row(tm, ad), _colblk(tm, ad, 1)],
        out_specs=[_row(tm, ad), _row(tm, ad), pl.BlockSpec((nkv, tm, LANES), lambda j: (0, j, 0))],
        out_shape=[_sds((t, ad), BF16), _sds((t, ad), BF16), _sds((nkv, t, LANES), F32)],
        compiler_params=_cp("parallel"))(dog, o, qgkv)


def flash_bwd(qr, do, kr, kt, qgkv, lse_t, delta_t, ad, kvd, tm, scale):
    t = qr.shape[0]
    nkv = kvd // HEAD_DIM
    gw = ad // nkv
    hpg = gw // HEAD_DIM
    nt = t // tm
    v_blk0 = (2 * ad + kvd) // HEAD_DIM

    def body(q_ref, do_ref, k_ref, v_ref, kt_ref, lse_ref, dl_ref, dq_ref, dk_ref, dv_ref):
        qi = pl.program_id(1)

        @pl.when(qi == 0)
        def _():
            dk_ref[...] = jnp.zeros_like(dk_ref)
            dv_ref[...] = jnp.zeros_like(dv_ref)

        nkc = jnp.where(qi == 0, 1, nt)
        heads = [slice(h * HEAD_DIM, (h + 1) * HEAD_DIM) for h in range(hpg)]
        qs = [q_ref[:, sl] for sl in heads]
        dos = [do_ref[:, sl] for sl in heads]
        lses = [lse_ref[h:h + 1, :] for h in range(hpg)]
        dls = [dl_ref[h:h + 1, :] for h in range(hpg)]

        def step(c, dqts):
            rows = pl.ds(pl.multiple_of(c * tm, tm), tm)
            kc = k_ref[rows, :]
            vc = v_ref[rows, :]
            ktc = kt_ref[c]
            dk = jnp.zeros((tm, HEAD_DIM), F32)
            dv = jnp.zeros((tm, HEAD_DIM), F32)
            out = []
            sts = [lax.dot_general(kc, q, _NT, preferred_element_type=F32) for q in qs]
            dpts = [lax.dot_general(vc, d, _NT, preferred_element_type=F32) for d in dos]
            for h in range(hpg):
                pt = jnp.exp2(sts[h] - lses[h])
                dv = dv + jnp.dot(pt.astype(BF16), dos[h], preferred_element_type=F32)
                dst = (pt * (dpts[h] - dls[h])).astype(BF16)
                dk = dk + jnp.dot(dst, qs[h], preferred_element_type=F32)
                out.append(dqts[h] + jnp.dot(ktc, dst, preferred_element_type=F32))
            dk_ref[rows, :] += dk
            dv_ref[rows, :] += dv
            return tuple(out)

        res = lax.fori_loop(0, nkc, step, tuple(jnp.zeros((HEAD_DIM, tm), F32) for _ in range(hpg)))
        for h, sl in enumerate(heads):
            dq_ref[:, sl] = res[h].T * scale

        @pl.when(qi == nt - 1)
        def _():
            dk_ref[...] = dk_ref[...] * LN_2

    return pl.pallas_call(
        body, name="flash_bwd", grid=(nkv, nt),
        in_specs=[pl.BlockSpec((tm, gw), lambda g, i: (i, g)),
                  pl.BlockSpec((tm, gw), lambda g, i: (i, g)),
                  pl.BlockSpec((t, HEAD_DIM), lambda g, i: (0, g)),
                  pl.BlockSpec((t, HEAD_DIM), lambda g, i: (0, v_blk0 + g)),
                  pl.BlockSpec((None, nt, HEAD_DIM, tm), lambda g, i: (g, 0, 0, 0)),
                  pl.BlockSpec((None, None, hpg, tm), lambda g, i: (g, i, 0, 0)),
                  pl.BlockSpec((None, None, hpg, tm), lambda g, i: (g, i, 0, 0))],
        out_specs=[pl.BlockSpec((tm, gw), lambda g, i: (i, g)),
                   pl.BlockSpec((t, HEAD_DIM), lambda g, i: (0, g)),
                   pl.BlockSpec((t, HEAD_DIM), lambda g, i: (0, g))],
        out_shape=[_sds((t, ad), F32), _sds((t, kvd), F32), _sds((t, kvd), F32)],
        compiler_params=_cp("parallel", "arbitrary"))(qr, do, kr, qgkv, kt, lse_t, delta_t)


def _rows_per_head(a, tm, hpg):
    nkv, t, _ = a.shape
    return a[:, :, :hpg].reshape(nkv, t // tm, tm, hpg).transpose(0, 1, 3, 2)


def adaln_fwd(c16, w_mod):
    nlay, d, nl = w_mod.shape
    tn = _pick(nl, (512, 256, 128))

    def body(c_ref, w_ref, o_ref):
        o_ref[...] = jnp.dot(_silu(c_ref[...]), w_ref[...], preferred_element_type=F32,
                             precision=lax.Precision.HIGHEST)

    return pl.pallas_call(
        body, name="adaln_fwd", grid=(nlay, nl // tn),
        in_specs=[pl.BlockSpec((16, d), lambda l, j: (0, 0)),
                  pl.BlockSpec((None, d, tn), lambda l, j: (l, 0, j))],
        out_specs=pl.BlockSpec((None, 16, tn), lambda l, j: (l, 0, j)),
        out_shape=_sds((nlay, 16, nl), F32), compiler_params=_cp("parallel", "parallel"))(c16, w_mod)


def adaln_bwd(c16t, dm, w_mod):
    nlay, d, nl = w_mod.shape
    tn = _pick(nl, (512, 256, 128))

    def body(c_ref, dm_ref, w_ref, dw_ref, dc_ref):
        @pl.when(pl.program_id(1) == 0)
        def _():
            dc_ref[...] = jnp.zeros_like(dc_ref)

        dmv = dm_ref[...]
        dw_ref[...] = jnp.dot(_silu(c_ref[...]), dmv, preferred_element_type=F32,
                              precision=lax.Precision.HIGHEST)
        dc_ref[...] += lax.dot_general(dmv, w_ref[...], _NT, preferred_element_type=F32,
                                       precision=lax.Precision.HIGHEST)

    return pl.pallas_call(
        body, name="adaln_bwd", grid=(nlay, nl // tn),
        in_specs=[pl.BlockSpec((d, 16), lambda l, j: (0, 0)),
                  pl.BlockSpec((None, 16, tn), lambda l, j: (l, 0, j)),
                  pl.BlockSpec((None, d, tn), lambda l, j: (l, 0, j))],
        out_specs=[pl.BlockSpec((None, d, tn), lambda l, j: (l, 0, j)),
                   pl.BlockSpec((None, 16, d), lambda l, j: (l, 0, 0))],
        out_shape=[_sds((nlay, d, nl), F32), _sds((nlay, 16, d), F32)],
        compiler_params=_cp("parallel", "arbitrary"))(c16t, dm, w_mod)


def sum_leading(a, name):
    n = a.shape[0]

    def body(a_ref, o_ref):
        acc = a_ref[0]
        for i in range(1, n):
            acc = acc + a_ref[i]
        o_ref[...] = acc

    return pl.pallas_call(body, name=name, in_specs=[VMEM_FULL], out_specs=VMEM_FULL,
                          out_shape=_sds(a.shape[1:], F32),
                          compiler_params=pltpu.CompilerParams(vmem_limit_bytes=VMEM_LIMIT))(a)


def c_ctx_grad(parts, c_ctx):
    n = parts.shape[0]

    def body(p_ref, c_ref, o_ref):
        acc = p_ref[0]
        for i in range(1, n):
            acc = acc + p_ref[i]
        o_ref[...] = acc * _dsilu(c_ref[...])

    return pl.pallas_call(body, name="c_ctx_grad", in_specs=[VMEM_FULL, VMEM_FULL], out_specs=VMEM_FULL,
                          out_shape=_sds(c_ctx.shape, F32))(parts, c_ctx)


def _as2d(a):
    return a.reshape(-1, a.shape[-1])


def _row_tile(r, c):
    for tr in (1024, 512, 256, 128, 64, 32, 16, 8):
        if r % tr == 0 and tr * c * 4 <= (1 << 20):
            return tr
    return r


def add_n(items, name, out_dtypes=(F32,)):
    norm = [it if isinstance(it, tuple) else (it, ()) for it in items]
    shape = norm[0][0].shape[len(norm[0][1]):]
    r, c = _as2d(jnp.zeros(shape, BF16)).shape
    tr = _row_tile(r, c)
    dyn, specs, flat = [], [], []
    for arr, lead in norm:
        slots = []
        for ix in lead:
            if isinstance(ix, int):
                slots.append(ix)
            else:
                slots.append((len(dyn),))
                dyn.append(ix)

        def imap(i, s, slots=tuple(slots)):
            return tuple(s[k[0]] if isinstance(k, tuple) else k for k in slots) + (i, 0)

        specs.append(pl.BlockSpec((None,) * len(lead) + (tr, c), imap))
        flat.append(arr.reshape(arr.shape[:len(lead)] + (r, c)))
    sel = jnp.stack([jnp.asarray(v, jnp.int32) for v in dyn]) if dyn else jnp.zeros((1,), jnp.int32)
    n_in = len(flat)

    def body(s_ref, *refs):
        acc = refs[0][...].astype(F32)
        for ref in refs[1:n_in]:
            acc = acc + ref[...].astype(F32)
        for ref in refs[n_in:]:
            ref[...] = acc.astype(ref.dtype)

    out_spec = pl.BlockSpec((tr, c), lambda i, s: (i, 0))
    outs = pl.pallas_call(
        body, name=name,
        grid_spec=pltpu.PrefetchScalarGridSpec(num_scalar_prefetch=1, grid=(r // tr,), in_specs=specs,
                                               out_specs=[out_spec] * len(out_dtypes)),
        out_shape=[_sds((r, c), dt) for dt in out_dtypes], compiler_params=_cp("parallel"))(sel, *flat)
    return [o.reshape(shape) for o in outs]


def pair_add(p0, p1, got, which, name):
    shape = p0.shape
    flat = [_as2d(a) for a in (p0, p1, got)]
    r, c = flat[0].shape
    tr = _row_tile(r, c)

    def body(s_ref, p0_ref, p1_ref, g_ref, o_ref):
        own = jnp.where(s_ref[0] == 0, p0_ref[...], p1_ref[...])
        o_ref[...] = (own + g_ref[...].astype(F32)).astype(o_ref.dtype)

    spec = pl.BlockSpec((tr, c), lambda i, s: (i, 0))
    out = pl.pallas_call(
        body, name=name,
        grid_spec=pltpu.PrefetchScalarGridSpec(num_scalar_prefetch=1, grid=(r // tr,), in_specs=[spec] * 3,
                                               out_specs=spec),
        out_shape=_sds((r, c), BF16), compiler_params=_cp("parallel"))(
            jnp.asarray(which, jnp.int32).reshape(1), *flat)
    return out.reshape(shape)


def adamw(w, g, m, v):
    shape = w.shape
    flat = [_as2d(a.reshape((1,) + shape) if len(shape) == 1 else a) for a in (w, g, m, v)]
    r, c = flat[0].shape
    tr = _row_tile(r, c)
    c1 = 1.0 - ADAM_B1 ** ADAM_STEP
    c2 = 1.0 - ADAM_B2 ** ADAM_STEP

    def body(w_ref, g_ref, m_ref, v_ref, d_ref, nm_ref, nv_ref):
        gv = g_ref[...]
        nm = ADAM_B1 * m_ref[...] + (1.0 - ADAM_B1) * gv
        nv = ADAM_B2 * v_ref[...] + (1.0 - ADAM_B2) * (gv * gv)
        d_ref[...] = -ADAM_LR * ((nm / c1) / (jnp.sqrt(nv / c2) + ADAM_EPS) + ADAM_WD * w_ref[...])
        nm_ref[...] = nm
        nv_ref[...] = nv

    spec = pl.BlockSpec((tr, c), lambda i: (i, 0))
    outs = pl.pallas_call(
        body, name="adamw", grid=(r // tr,), in_specs=[spec] * 4, out_specs=[spec] * 3,
        out_shape=[_sds((r, c), F32)] * 3, compiler_params=_cp("parallel"))(*flat)
    return tuple(o.reshape(shape) for o in outs)


def _place():
    return lax.axis_index("x"), lax.axis_index("y"), lax.axis_index("c")


def _remote(src, dst, ssem, rsem, dev):
    return pltpu.make_async_remote_copy(src_ref=src, dst_ref=dst, send_sem=ssem, recv_sem=rsem,
                                        device_id=dev, device_id_type=MESH)


def all_gather8(v, name):
    m_per, n = v.shape

    def body(x_ref, out_ref, send_sems, recv_sems, local_sem):
        x, y, c = _place()
        me, sibling = (x, y, c), (x, y, 1 - c)
        chips = [(1 - x, y), (x, 1 - y), (1 - x, 1 - y)]

        def rows(px, py, pc):
            return out_ref.at[pl.ds((4 * px + 2 * py + pc) * m_per, m_per), :]

        def copy(k, block, to, src=None):
            return _remote(rows(*block) if src is None else src, rows(*block),
                           send_sems.at[k], recv_sems.at[k], to)

        mine = pltpu.make_async_copy(x_ref, rows(*me), local_sem)
        mine.start()
        first = [copy(0, me, sibling, src=x_ref)]
        first += [copy(1 + j, me, (*chip, c), src=x_ref) for j, chip in enumerate(chips)]
        for cp in first:
            cp.start()
        passed = [copy(4 + j, (*chip, c), sibling) for j, chip in enumerate(chips)]
        for j, chip in enumerate(chips):
            copy(1 + j, (*chip, c), me).wait_recv()
            passed[j].start()
        copy(0, sibling, me).wait_recv()
        for j, chip in enumerate(chips):
            copy(4 + j, (*chip, 1 - c), me).wait_recv()
        for cp in first + passed:
            cp.wait_send()
        mine.wait()

    return pl.pallas_call(
        body, name=name, out_shape=_sds((8 * m_per, n), v.dtype),
        in_specs=[VMEM_FULL], out_specs=VMEM_FULL,
        scratch_shapes=[pltpu.SemaphoreType.DMA((7,)), pltpu.SemaphoreType.DMA((7,)), pltpu.SemaphoreType.DMA],
        compiler_params=pltpu.CompilerParams(vmem_limit_bytes=VMEM_LIMIT))(v)


def _slabs(ref, n):
    rows = ref.shape[0] // n
    return [ref.at[pl.ds(i * rows, rows)] for i in range(n)]


def gather_weight(w, name):
    _, r, cdim = w.shape
    n = D2D_PARTS

    def body(w_ref, out_ref, slab_ref, send_sems, recv_sems, local_sem):
        x, y, c = _place()
        sibling = (x, y, 1 - c)
        chips = [(1 - x, y), (x, 1 - y), (1 - x, 1 - y)]
        mine = 2 * x + y
        src = _slabs(w_ref.at[c], n)
        first = []
        for k, chip in enumerate(chips):
            dst = _slabs(out_ref.at[c, mine], n)
            for j in range(n):
                cp = _remote(src[j], dst[j], send_sems.at[k * n + j], recv_sems.at[k * n + j], (*chip, c))
                cp.start()
                first.append(cp)
        for i in range(2):
            for s_in, s_out in zip(_slabs(w_ref.at[i], n), _slabs(out_ref.at[i, mine], n)):
                cp = pltpu.make_async_copy(s_in, slab_ref, local_sem)
                cp.start()
                cp.wait()
                cp = pltpu.make_async_copy(slab_ref, s_out, local_sem)
                cp.start()
                cp.wait()
        passed = []
        for j in range(n):
            for k, (px, py) in enumerate(chips):
                theirs = _slabs(out_ref.at[c, 2 * px + py], n)[j]
                _remote(src[j], theirs, send_sems.at[k * n + j], recv_sems.at[k * n + j], sibling).wait_recv()
                cp = _remote(theirs, theirs, send_sems.at[(3 + k) * n + j], recv_sems.at[(3 + k) * n + j], sibling)
                cp.start()
                passed.append(cp)
        for j in range(n):
            for k, (px, py) in enumerate(chips):
                other = _slabs(out_ref.at[1 - c, 2 * px + py], n)[j]
                _remote(other, other, send_sems.at[(3 + k) * n + j], recv_sems.at[(3 + k) * n + j],
                        sibling).wait_recv()
        for cp in first + passed:
            cp.wait_send()

    return pl.pallas_call(
        body, name=name, out_shape=_sds((2, 4, r, cdim), w.dtype), in_specs=[ANY], out_specs=ANY,
        scratch_shapes=[pltpu.VMEM((r // n, cdim), w.dtype), pltpu.SemaphoreType.DMA((6 * n,)),
                        pltpu.SemaphoreType.DMA((6 * n,)), pltpu.SemaphoreType.DMA],
        compiler_params=pltpu.CompilerParams(vmem_limit_bytes=VMEM_LIMIT))(w)


def rs_chip_exchange(gb0, gb1, name):
    _, r, cdim = gb0.shape

    def body(gb0_ref, gb1_ref, st_ref, send_sems, recv_sems):
        x, y, c = _place()
        chips = [(1 - x, y), (x, 1 - y), (1 - x, 1 - y)]
        cps = []
        for k, (px, py) in enumerate(chips):
            for i, gb_ref in enumerate((gb0_ref, gb1_ref)):
                cps.append(_remote(gb_ref.at[2 * px + py], st_ref.at[k, i], send_sems.at[2 * k + i],
                                   recv_sems.at[2 * k + i], (px, py, c)))
        for cp in cps:
            cp.start()
        for cp in cps:
            cp.wait()

    return pl.pallas_call(
        body, name=name, out_shape=_sds((3, 2, r, cdim), BF16), in_specs=[ANY] * 2, out_specs=ANY,
        scratch_shapes=[pltpu.SemaphoreType.DMA((6,)), pltpu.SemaphoreType.DMA((6,))])(gb0, gb1)


def rs_pair_swap(pb0, pb1, name):
    r, cdim = pb0.shape
    n = 2 * D2D_PARTS

    def body(pb0_ref, pb1_ref, got_ref, stage_ref, send_sems, recv_sems, local_sem):
        x, y, c = _place()
        sibling = (x, y, 1 - c)

        @pl.when(c == 0)
        def _():
            load = pltpu.make_async_copy(pb1_ref, stage_ref, local_sem)
            load.start()
            load.wait()

        @pl.when(c == 1)
        def _():
            load = pltpu.make_async_copy(pb0_ref, stage_ref, local_sem)
            load.start()
            load.wait()

        src, dst = _slabs(stage_ref, n), _slabs(got_ref, n)
        cps = [_remote(src[j], dst[j], send_sems.at[j], recv_sems.at[j], sibling) for j in range(n)]
        for cp in cps:
            cp.start()
        for cp in cps:
            cp.wait()

    return pl.pallas_call(
        body, name=name, out_shape=_sds((r, cdim), BF16), in_specs=[ANY, ANY], out_specs=VMEM_FULL,
        scratch_shapes=[pltpu.VMEM((r, cdim), BF16), pltpu.SemaphoreType.DMA((n,)), pltpu.SemaphoreType.DMA((n,)),
                        pltpu.SemaphoreType.DMA],
        compiler_params=pltpu.CompilerParams(vmem_limit_bytes=VMEM_LIMIT))(pb0, pb1)


def rs_pair_share(red, name):
    n = 2 * D2D_PARTS

    def body(red_ref, out_ref, send_sems, recv_sems, local_sem):
        x, y, c = _place()
        sibling = (x, y, 1 - c)
        keep = pltpu.make_async_copy(red_ref, out_ref.at[c], local_sem)
        keep.start()
        src, dst, got = _slabs(red_ref, n), _slabs(out_ref.at[c], n), _slabs(out_ref.at[1 - c], n)
        cps = [_remote(src[j], dst[j], send_sems.at[j], recv_sems.at[j], sibling) for j in range(n)]
        for cp in cps:
            cp.start()
        for cp in cps:
            cp.wait_send()
        for j in range(n):
            _remote(src[j], got[j], send_sems.at[j], recv_sems.at[j], sibling).wait_recv()
        keep.wait()

    return pl.pallas_call(
        body, name=name, out_shape=_sds((2,) + red.shape, red.dtype), in_specs=[VMEM_FULL], out_specs=VMEM_FULL,
        scratch_shapes=[pltpu.SemaphoreType.DMA((n,)), pltpu.SemaphoreType.DMA((n,)), pltpu.SemaphoreType.DMA],
        compiler_params=pltpu.CompilerParams(vmem_limit_bytes=VMEM_LIMIT))(red)


def reduce_scatter_grad(g0, g1, shard, core, tag):
    theirs = rs_chip_exchange(g0[1], g1[1], "rs_exchange_" + tag)
    p0, pb0 = add_n([(g0[0], (shard,)), (theirs, (0, 0)), (theirs, (1, 0)), (theirs, (2, 0))],
                    "rs_chip_add0_" + tag, (F32, BF16))
    p1, pb1 = add_n([(g1[0], (shard,)), (theirs, (0, 1)), (theirs, (1, 1)), (theirs, (2, 1))],
                    "rs_chip_add1_" + tag, (F32, BF16))
    got = rs_pair_swap(pb0, pb1, "rs_swap_" + tag)
    red = pair_add(p0, p1, got, core, "rs_pair_add_" + tag)
    return rs_pair_share(red, "rs_share_" + tag).astype(F32)


def _rope_tables(lc, s):
    rows_n = s // GRID_W
    row = jnp.repeat(jnp.arange(rows_n, dtype=F32), GRID_W)
    col = jnp.tile(jnp.arange(GRID_W, dtype=F32), rows_n)
    axis_dim = HEAD_DIM // 2
    inv_freq = ROPE_THETA ** (-jnp.arange(0, axis_dim, 2, dtype=F32) / axis_dim)
    ang_r = row[:, None] * inv_freq[None, :]
    ang_c = col[:, None] * inv_freq[None, :]
    cr, sr, cc, sc = jnp.cos(ang_r), jnp.sin(ang_r), jnp.cos(ang_c), jnp.sin(ang_c)
    cos_l = jnp.concatenate([cr, cr, cc, cc], axis=1)
    sin_l = jnp.concatenate([-sr, sr, -sc, sc], axis=1)
    cos_t = jnp.concatenate([jnp.ones((lc, HEAD_DIM), F32), cos_l], axis=0)
    sin_t = jnp.concatenate([jnp.zeros((lc, HEAD_DIM), F32), sin_l], axis=0)
    return cos_t, sin_t


def _pad_rows(a, rows):
    return jnp.concatenate([a, jnp.zeros((rows - a.shape[0],) + a.shape[1:], a.dtype)], axis=0)


def _pad_cols(a, cols):
    return jnp.concatenate([a, jnp.zeros(a.shape[:-1] + (cols - a.shape[-1],), a.dtype)], axis=-1)


def kernel(x, c, ctx, c_ctx, w_mod, b_mod, post_ln_g, post_ln_b, w_in_e, conv_a_w, conv_a_b, norm_a_g, norm_a_b, conv_b_w, w_out_e, w_in_o, q_norm_g, k_norm_g, w_out_o, loss_target, m_c_ctx, m_w_mod, m_b_mod, m_post_ln_g, m_post_ln_b, m_w_in_e, m_conv_a_w, m_conv_a_b, m_norm_a_g, m_norm_a_b, m_conv_b_w, m_w_out_e, m_w_in_o, m_q_norm_g, m_k_norm_g, m_w_out_o, v_c_ctx, v_w_mod, v_b_mod, v_post_ln_g, v_post_ln_b, v_w_in_e, v_conv_a_w, v_conv_a_b, v_norm_a_g, v_norm_a_b, v_conv_b_w, v_w_out_e, v_w_in_o, v_q_norm_g, v_k_norm_g, v_w_out_o):
    s, d = x.shape[1], x.shape[2]
    lc = ctx.shape[1]
    t = lc + s
    tm = lc
    depth = w_mod.shape[0]
    n_even, n_odd = w_in_e.shape[0], w_in_o.shape[0]
    ad = w_out_o.shape[1] * 4
    kvd = (w_in_o.shape[2] * 4 - 2 * ad) // 2
    nkv = kvd // HEAD_DIM
    hpg = ad // nkv // HEAD_DIM
    nlm = w_mod.shape[2]
    alpha = (2.0 * depth) ** 0.25
    scale = HEAD_DIM ** -0.5
    assert n_even == 2 and n_odd == 2 and depth == 4 and hpg == GQA_GROUP
    assert lc % CONV_ROWS == 0 and s % (tm * KV_CHUNKS_PER_STEP) == 0 and d % LANES == 0

    xi, yi, ci = _place()
    shard = 2 * xi + yi
    dev = 4 * xi + 2 * yi + ci

    wg_in_e = gather_weight(w_in_e.astype(BF16), "gather_w_in_e")
    wg_out_e = gather_weight(w_out_e.astype(BF16), "gather_w_out_e")
    wg_in_o = gather_weight(w_in_o.astype(BF16), "gather_w_in_o")
    wg_out_o = gather_weight(w_out_o.astype(BF16), "gather_w_out_o")

    c_all = all_gather8(_pad_rows(c, 8), "gather_c")
    c16 = _pad_rows(jnp.concatenate([c_all[0::8], c_ctx[None, :]], axis=0), 16)
    m_part = adaln_fwd(c16, w_mod)
    m_all = all_gather8(m_part.reshape(depth * 16, nlm), "gather_mod")
    m_all = m_all.reshape(8, depth, 16, nlm)[0::2]
    m_full = m_all.transpose(1, 2, 0, 3).reshape(depth, 16, 4 * nlm) + b_mod[:, None, :]
    m_lat = lax.dynamic_index_in_dim(m_full, dev, axis=1, keepdims=False)
    m_ctx = m_full[:, 8]

    def seg2(l, part):
        return jnp.stack([m_ctx[l, part * d:(part + 1) * d], m_lat[l, part * d:(part + 1) * d]])[:, None, :]

    cos_t, sin_t = _rope_tables(lc, s)

    small_gathered = all_gather8(
        _pad_rows(jnp.concatenate([conv_a_w.reshape(n_even * CONV_A_TAPS, -1),
                                   conv_b_w.reshape(n_even * CONV_B_TAPS, -1)], axis=0), 72), "gather_taps")
    taps = small_gathered.reshape(8, 72, -1)[0::2]
    taps = taps.transpose(1, 0, 2).reshape(72, d)
    caw = taps[:n_even * CONV_A_TAPS].reshape(n_even, CONV_A_TAPS, d)
    cbw = taps[n_even * CONV_A_TAPS:n_even * (CONV_A_TAPS + CONV_B_TAPS)].reshape(n_even, CONV_B_TAPS, d)
    caw_pad = jnp.concatenate([caw, jnp.zeros((n_even, 32 - CONV_A_TAPS, d), F32)], axis=1)
    cbw_pad = jnp.concatenate([cbw, jnp.zeros((n_even, 8 - CONV_B_TAPS, d), F32)], axis=1)

    xc = jnp.concatenate([ctx[0], x[0]], axis=0)
    saved = []
    for l in range(depth):
        i = l // 2
        shift2, scale2, gate2 = seg2(l, 0), seg2(l, 1), seg2(l, 2)
        h = mod_fwd(xc, scale2, shift2, tm)
        if l % 2 == 0:
            p = mm_nn(h, wg_in_e[i], BF16)
            u1 = conv_a_fwd(p, caw_pad[i], conv_a_b[i][None, :], lc, d)
            a_out = ln_a_fwd(u1, p, norm_a_g[i][None, :], norm_a_b[i][None, :], tm)
            b_out = conv_b_fwd(p, cbw_pad[i], lc, d)
            ab = jnp.concatenate([a_out, b_out], axis=1)
            y = mm_nn(ab, wg_out_e[i].reshape(1, 2 * d, d), F32)
            saved.append(dict(xc=xc, h=h, p=p, u1=u1, ab=ab, y=y))
        else:
            qgkv = mm_nn(h, wg_in_o[i], BF16)
            qr, kr = qk_fwd(qgkv, q_norm_g[i][None, :], k_norm_g[i][None, :], cos_t, sin_t, ad, kvd, tm,
                            scale * LOG2_E)
            vt = _chunks_t(qgkv[:, 2 * ad + kvd:], nkv, tm)
            o, og, lse = flash_fwd(qr, kr, vt, qgkv, ad, kvd, tm)
            y = mm_nn(og, wg_out_o[i].reshape(1, ad, d), F32)
            saved.append(dict(xc=xc, h=h, qgkv=qgkv, qr=qr, kr=kr, o=o, og=og, lse=lse, y=y))
        xc = post_ln_fwd(xc, y, gate2, post_ln_g[l][None, :], post_ln_b[l][None, :], alpha, tm)

    dxc, loss_acc = loss_head(xc, loss_target[0], lc, tm)

    g_in_e, g_out_e, g_in_o, g_out_o = [None] * 2, [None] * 2, [None] * 2, [None] * 2
    d_mod_lat, d_mod_ctx = [None] * depth, [None] * depth
    d_pln_g, d_pln_b = [None] * depth, [None] * depth
    d_cab, d_nag, d_nab, d_caw, d_cbw = [None] * 2, [None] * 2, [None] * 2, [None] * 2, [None] * 2
    d_qg, d_kg = [None] * 2, [None] * 2
    for l in reversed(range(depth)):
        i = l // 2
        sv = saved[l]
        scale2, gate2 = seg2(l, 1), seg2(l, 2)
        dzx, dy, acc_ln = post_ln_bwd(dxc, sv["xc"], sv["y"], gate2, post_ln_g[l][None, :], alpha, tm)
        d_pln_g[l] = acc_ln[0, 1] + acc_ln[1, 1]
        d_pln_b[l] = acc_ln[0, 2] + acc_ln[1, 2]
        if l % 2 == 0:
            w_out3 = wg_out_e[i].reshape(1, 2 * d, d)
            dab = mm_nt(dy, w_out3, BF16)
            g_out_e[i] = tuple(g.reshape(4, 2 * d // 4, d) for g in mm_tn(sv["ab"], dy, 1))
            du1, d_agate, acc_a = ln_a_bwd(dab, sv["u1"], sv["p"], norm_a_g[i][None, :], norm_a_b[i][None, :], tm)
            d_nag[i], d_nab[i], d_cab[i] = acc_a[0], acc_a[1], acc_a[2]
            d_aval, d_aglu, d_caw[i] = conv_a_bwd(du1, sv["p"], caw_pad[i], lc, d)
            d_bx, d_bb, d_bc, d_bg, d_cbw[i] = conv_b_bwd(dab, sv["p"], cbw_pad[i], lc, d)
            dp = jnp.concatenate([d_aval, d_aglu, d_agate, d_bx, d_bb, d_bc, d_bg], axis=1)
            dh = mm_nt(dp, wg_in_e[i], F32)
            g_in_e[i] = tuple(mm_tn(sv["h"], dp, 4))
        else:
            w_out3 = wg_out_o[i].reshape(1, ad, d)
            dog = mm_nt(dy, w_out3, BF16)
            g_out_o[i] = tuple(g.reshape(4, ad // 4, d) for g in mm_tn(sv["og"], dy, 1))
            do, dgate, delta = gate_bwd(dog, sv["o"], sv["qgkv"], ad, kvd, tm)
            dqr, dkr, dv = flash_bwd(sv["qr"], do, sv["kr"], _chunks_t(sv["kr"], nkv, tm), sv["qgkv"], sv["lse"],
                                     _rows_per_head(delta, tm, hpg), ad, kvd, tm, scale)
            dq, dk, acc_qk = qk_bwd(dqr, dkr, sv["qgkv"], q_norm_g[i][None, :], k_norm_g[i][None, :],
                                    cos_t, sin_t, ad, kvd, tm)
            d_qg[i], d_kg[i] = acc_qk[0], acc_qk[1]
            dqgkv = jnp.concatenate([dq, dgate, dk, dv.astype(BF16)], axis=1)
            dh = mm_nt(dqgkv, wg_in_o[i], F32)
            g_in_o[i] = tuple(mm_tn(sv["h"], dqgkv, 4))
        dxc, acc_mod = mod_bwd(dh, dzx, sv["xc"], scale2, tm)
        d_mod_ctx[l] = jnp.stack([acc_mod[0, 0], acc_mod[0, 1], acc_ln[0, 0]])
        d_mod_lat[l] = jnp.stack([acc_mod[1, 0], acc_mod[1, 1], acc_ln[1, 0]])
    grad_x = dxc[lc:][None]

    parts = [jnp.concatenate(d_mod_ctx, axis=0), jnp.stack(d_pln_g), jnp.stack(d_pln_b), jnp.stack(d_cab),
             jnp.stack(d_nag), jnp.stack(d_nab), jnp.concatenate(d_caw, axis=0), jnp.concatenate(d_cbw, axis=0),
             _pad_cols(jnp.stack(d_qg), d), _pad_cols(jnp.stack(d_kg), d), jnp.concatenate(d_mod_lat, axis=0),
             _pad_cols(loss_acc, d)]
    parts = [_pad_rows(p, -(-p.shape[0] // SUBLANES) * SUBLANES) for p in parts]
    offs = [0]
    for p in parts:
        offs.append(offs[-1] + p.shape[0])
    pack = jnp.concatenate(parts, axis=0)
    npack = offs[-1]
    gathered = all_gather8(pack, "gather_small").reshape(8, npack, d)
    small = sum_leading(gathered, "sum_small")

    def piece(k, rows):
        return small[offs[k]:offs[k] + rows]

    loss = 0.5 / d * jnp.sum(piece(11, 1)[0, :LANES])
    dm_ctx = piece(0, 3 * depth).reshape(depth, 1, 3 * d)
    dm_lat = gathered[:, offs[10]:offs[10] + 3 * depth].reshape(8, depth, 3 * d).transpose(1, 0, 2)
    dm = jnp.concatenate([dm_lat, dm_ctx, jnp.zeros((depth, 7, 3 * d), F32)], axis=1)
    g_b_mod = sum_leading(dm.transpose(1, 0, 2), "sum_b_mod")
    dm_shard = lax.dynamic_slice_in_dim(dm, shard * nlm, nlm, axis=2)
    g_w_mod, dc_part = adaln_bwd(c16.T, dm_shard, w_mod)
    dc_all = all_gather8(_pad_rows(dc_part[:, 8, :], 8), "gather_dc").reshape(8, 8, d)
    g_c_ctx = c_ctx_grad(dc_all[0::2, :depth].reshape(4 * depth, 1, d), c_ctx[None, :])[0]

    g_pln_g, g_pln_b = piece(1, depth), piece(2, depth)
    g_cab, g_nag, g_nab = piece(3, n_even), piece(4, n_even), piece(5, n_even)
    dch = d // 4
    g_caw = lax.dynamic_slice_in_dim(piece(6, 64).reshape(2, 32, d)[:, :CONV_A_TAPS], shard * dch, dch, axis=2)
    g_cbw = lax.dynamic_slice_in_dim(piece(7, 16).reshape(2, 8, d)[:, :CONV_B_TAPS], shard * dch, dch, axis=2)
    g_qg, g_kg = piece(8, n_odd)[:, :HEAD_DIM], piece(9, n_odd)[:, :HEAD_DIM]

    g_w_in_e = reduce_scatter_grad(g_in_e[0], g_in_e[1], shard, ci, "in_e")
    g_w_out_e = reduce_scatter_grad(g_out_e[0], g_out_e[1], shard, ci, "out_e")
    g_w_in_o = reduce_scatter_grad(g_in_o[0], g_in_o[1], shard, ci, "in_o")
    g_w_out_o = reduce_scatter_grad(g_out_o[0], g_out_o[1], shard, ci, "out_o")

    grads = [g_c_ctx, g_w_mod, g_b_mod, g_pln_g, g_pln_b, g_w_in_e, g_caw, g_cab, g_nag, g_nab, g_cbw,
             g_w_out_e, g_w_in_o, g_qg, g_kg, g_w_out_o]
    weights = [c_ctx, w_mod, b_mod, post_ln_g, post_ln_b, w_in_e, conv_a_w, conv_a_b, norm_a_g, norm_a_b,
               conv_b_w, w_out_e, w_in_o, q_norm_g, k_norm_g, w_out_o]
    ms = [m_c_ctx, m_w_mod, m_b_mod, m_post_ln_g, m_post_ln_b, m_w_in_e, m_conv_a_w, m_conv_a_b, m_norm_a_g,
          m_norm_a_b, m_conv_b_w, m_w_out_e, m_w_in_o, m_q_norm_g, m_k_norm_g, m_w_out_o]
    vs = [v_c_ctx, v_w_mod, v_b_mod, v_post_ln_g, v_post_ln_b, v_w_in_e, v_conv_a_w, v_conv_a_b, v_norm_a_g,
          v_norm_a_b, v_conv_b_w, v_w_out_e, v_w_in_o, v_q_norm_g, v_k_norm_g, v_w_out_o]
    deltas, new_ms, new_vs = [], [], []
    for wv, gv, mv, vv in zip(weights, grads, ms, vs):
        dl, nm, nv = adamw(wv, gv, mv, vv)
        deltas.append(dl)
        new_ms.append(nm)
        new_vs.append(nv)
    return (loss, grad_x, *grads, *deltas, *new_ms, *new_vs)
```

```python
import functools

import jax
import jax.numpy as jnp
from jax import lax
from jax.experimental import pallas as pl
from jax.experimental.pallas import tpu as pltpu

F32 = jnp.float32
BF16 = jnp.bfloat16

LANES = 128
SUBLANES = 8
HEAD_DIM = 128
GQA_GROUP = 4
GRID_W = 64
ROPE_THETA = 10000.0
LN_EPS = 1e-5
RMS_EPS = 1e-6
CONV_A_TAPS = 31
CONV_B_TAPS = 3
HALO = 16
CONV_ROWS = 128
ADAM_LR = 0.001
ADAM_B1 = 0.9
ADAM_B2 = 0.999
ADAM_EPS = 1e-08
ADAM_WD = 0.01
ADAM_STEP = 10
VMEM_LIMIT = 56 * 1024 * 1024
D2D_PARTS = 4
KV_CHUNKS_PER_STEP = 4
LOG2_E = 1.4426950408889634
LN_2 = 0.6931471805599453
MESH = pl.DeviceIdType.MESH
ANY = pl.BlockSpec(memory_space=pl.ANY)
VMEM_FULL = pl.BlockSpec(memory_space=pltpu.VMEM)


def _sds(shape, dtype):
    return jax.ShapeDtypeStruct(tuple(shape), dtype)


def _cp(*sem):
    return pltpu.CompilerParams(dimension_semantics=sem, vmem_limit_bytes=VMEM_LIMIT)


def _pick(n, cands):
    for c in cands:
        if n % c == 0:
            return c
    return n


def _sigmoid(x):
    return 1.0 / (1.0 + jnp.exp(-x))


def _silu(x):
    return x * _sigmoid(x)


def _dsilu(x):
    s = _sigmoid(x)
    return s * (1.0 + x * (1.0 - s))


def _row(tm, d):
    return pl.BlockSpec((tm, d), lambda j: (j, 0))


def _seg(d):
    return pl.BlockSpec((None, 1, d), lambda j: (jnp.minimum(j, 1), 0, 0))


def _vec(d):
    return pl.BlockSpec((1, d), lambda j: (0, 0))


def _colblk(tm, width, blk):
    return pl.BlockSpec((tm, width), lambda j: (j, blk))


def _seg_acc(d):
    return pl.BlockSpec((None, SUBLANES, d), lambda j: (jnp.minimum(j, 1), 0, 0))


def _ln_stats(z):
    mu = jnp.mean(z, axis=-1, keepdims=True)
    zc = z - mu
    var = jnp.mean(zc * zc, axis=-1, keepdims=True)
    rstd = lax.rsqrt(var + LN_EPS)
    return zc * rstd, rstd


def _ln_bwd(dxh, xhat, rstd):
    m1 = jnp.mean(dxh, axis=-1, keepdims=True)
    m2 = jnp.mean(dxh * xhat, axis=-1, keepdims=True)
    return rstd * (dxh - m1 - xhat * m2)


def _colsum(v):
    return jnp.sum(v, axis=0, keepdims=True)


def mod_fwd(xc, scale2, shift2, tm):
    t, d = xc.shape

    def body(x_ref, sc_ref, sh_ref, h_ref):
        h_ref[...] = (x_ref[...] * (1.0 + sc_ref[...]) + sh_ref[...]).astype(h_ref.dtype)

    return pl.pallas_call(
        body, name="mod_fwd", grid=(t // tm,),
        in_specs=[_row(tm, d), _seg(d), _seg(d)], out_specs=_row(tm, d),
        out_shape=_sds((t, d), BF16), compiler_params=_cp("parallel"))(xc, scale2, shift2)


def post_ln_fwd(xc, y, gate2, g, b, alpha, tm):
    t, d = xc.shape

    def body(x_ref, y_ref, gt_ref, g_ref, b_ref, o_ref):
        z = alpha * x_ref[...] + gt_ref[...] * y_ref[...]
        xhat, _ = _ln_stats(z)
        o_ref[...] = xhat * g_ref[...] + b_ref[...]

    return pl.pallas_call(
        body, name="post_ln_fwd", grid=(t // tm,),
        in_specs=[_row(tm, d), _row(tm, d), _seg(d), _vec(d), _vec(d)], out_specs=_row(tm, d),
        out_shape=_sds((t, d), F32), compiler_params=_cp("parallel"))(xc, y, gate2, g, b)


def post_ln_bwd(dout, xc, y, gate2, g, alpha, tm):
    t, d = xc.shape

    def body(do_ref, x_ref, y_ref, gt_ref, g_ref, dzx_ref, dy_ref, acc_ref):
        @pl.when(pl.program_id(0) <= 1)
        def _():
            acc_ref[...] = jnp.zeros_like(acc_ref)

        yv = y_ref[...]
        gate = gt_ref[...]
        xhat, rstd = _ln_stats(alpha * x_ref[...] + gate * yv)
        dout = do_ref[...]
        dz = _ln_bwd(dout * g_ref[...], xhat, rstd)
        dzx_ref[...] = alpha * dz
        dy_ref[...] = (gate * dz).astype(dy_ref.dtype)
        acc_ref[0:1, :] += _colsum(dz * yv)
        acc_ref[1:2, :] += _colsum(dout * xhat)
        acc_ref[2:3, :] += _colsum(dout)

    return pl.pallas_call(
        body, name="post_ln_bwd", grid=(t // tm,),
        in_specs=[_row(tm, d), _row(tm, d), _row(tm, d), _seg(d), _vec(d)],
        out_specs=[_row(tm, d), _row(tm, d), _seg_acc(d)],
        out_shape=[_sds((t, d), F32), _sds((t, d), BF16), _sds((2, SUBLANES, d), F32)],
        compiler_params=_cp("arbitrary"))(dout, xc, y, gate2, g)


def mod_bwd(dh, dzx, xc, scale2, tm):
    t, d = xc.shape

    def body(dh_ref, dzx_ref, x_ref, sc_ref, dx_ref, acc_ref):
        @pl.when(pl.program_id(0) <= 1)
        def _():
            acc_ref[...] = jnp.zeros_like(acc_ref)

        dhv = dh_ref[...].astype(F32)
        dx_ref[...] = dzx_ref[...] + dhv * (1.0 + sc_ref[...])
        acc_ref[0:1, :] += _colsum(dhv)
        acc_ref[1:2, :] += _colsum(dhv * x_ref[...])

    return pl.pallas_call(
        body, name="mod_bwd", grid=(t // tm,),
        in_specs=[_row(tm, d), _row(tm, d), _row(tm, d), _seg(d)],
        out_specs=[_row(tm, d), _seg_acc(d)],
        out_shape=[_sds((t, d), F32), _sds((2, SUBLANES, d), F32)],
        compiler_params=_cp("arbitrary"))(dh, dzx, xc, scale2)


def loss_head(xc, target, lc, tm):
    t, d = xc.shape

    def body(x_ref, t_ref, dx_ref, acc_ref):
        j = pl.program_id(0)

        @pl.when(j == 0)
        def _():
            acc_ref[...] = jnp.zeros_like(acc_ref)
            dx_ref[...] = jnp.zeros_like(dx_ref)

        @pl.when(j > 0)
        def _():
            err = x_ref[...] - t_ref[...]
            dx_ref[...] = err * (1.0 / d)
            col = _colsum(err * err)
            tot = col[:, 0:LANES]
            for k in range(1, d // LANES):
                tot = tot + col[:, k * LANES:(k + 1) * LANES]
            acc_ref[0:1, :] += tot

    nlc = lc // tm
    return pl.pallas_call(
        body, name="loss_head", grid=(t // tm,),
        in_specs=[_row(tm, d), pl.BlockSpec((tm, d), lambda j: (jnp.maximum(j - nlc, 0), 0))],
        out_specs=[_row(tm, d), pl.BlockSpec((SUBLANES, LANES), lambda j: (0, 0))],
        out_shape=[_sds((t, d), F32), _sds((SUBLANES, LANES), F32)],
        compiler_params=_cp("arbitrary"))(xc, target)


def mm_nn(a, w3, out_dtype):
    m, k = a.shape
    ns, _, nl = w3.shape
    tm = _pick(m, (768, 512, 256, 128))
    tn = _pick(nl, (512, 256, 128))
    npj = nl // tn

    def body(a_ref, w_ref, o_ref):
        o_ref[...] = jnp.dot(a_ref[...], w_ref[...], preferred_element_type=F32).astype(o_ref.dtype)

    return pl.pallas_call(
        body, name="mm_nn", grid=(m // tm, ns * npj),
        in_specs=[pl.BlockSpec((tm, k), lambda i, j: (i, 0)),
                  pl.BlockSpec((None, k, tn), lambda i, j: (j // npj, 0, j % npj))],
        out_specs=pl.BlockSpec((tm, tn), lambda i, j: (i, j)),
        out_shape=_sds((m, ns * nl), out_dtype), compiler_params=_cp("parallel", "parallel"))(a, w3)


def mm_nt(a, w3, out_dtype):
    m, _ = a.shape
    ns, k, nl = w3.shape
    tm = _pick(m, (768, 512, 256, 128))
    tk = _pick(k, (2048, 1024, 512, 256, 128))
    tn = _pick(nl, (1792, 1280, 1024, 512, 256, 128))
    npj = nl // tn
    nsteps = ns * npj

    def body(a_ref, w_ref, o_ref, acc_ref):
        n = pl.program_id(2)

        @pl.when(n == 0)
        def _():
            acc_ref[...] = jnp.zeros_like(acc_ref)

        acc_ref[...] += lax.dot_general(a_ref[...], w_ref[...], (((1,), (1,)), ((), ())),
                                        preferred_element_type=F32)

        @pl.when(n == nsteps - 1)
        def _():
            o_ref[...] = acc_ref[...].astype(o_ref.dtype)

    return pl.pallas_call(
        body, name="mm_nt", grid=(m // tm, k // tk, nsteps),
        in_specs=[pl.BlockSpec((tm, tn), lambda i, kk, n: (i, n)),
                  pl.BlockSpec((None, tk, tn), lambda i, kk, n: (n // npj, kk, n % npj))],
        out_specs=pl.BlockSpec((tm, tk), lambda i, kk, n: (i, kk)),
        out_shape=_sds((m, k), out_dtype), scratch_shapes=[pltpu.VMEM((tm, tk), F32)],
        compiler_params=_cp("parallel", "parallel", "arbitrary"))(a, w3)


def mm_tn(a, b, ns):
    m, k = a.shape
    nl = b.shape[1] // ns
    tm = _pick(m, (1408, 768, 512, 256, 128))
    tk = _pick(k, (1024, 512, 256, 128))
    tn = _pick(nl, (1792, 1280, 1024, 512, 256, 128))
    npj = nl // tn
    nsteps = m // tm

    def body(a_ref, b_ref, o_ref, ob_ref):
        r = pl.program_id(2)

        @pl.when(r == 0)
        def _():
            o_ref[...] = jnp.zeros_like(o_ref)

        o_ref[...] += lax.dot_general(a_ref[...], b_ref[...], (((0,), (0,)), ((), ())),
                                      preferred_element_type=F32)

        @pl.when(r == nsteps - 1)
        def _():
            ob_ref[...] = o_ref[...].astype(ob_ref.dtype)

    out_spec = pl.BlockSpec((None, tk, tn), lambda i, j, r: (j // npj, i, j % npj))
    return pl.pallas_call(
        body, name="mm_tn", grid=(k // tk, ns * npj, nsteps),
        in_specs=[pl.BlockSpec((tm, tk), lambda i, j, r: (r, i)),
                  pl.BlockSpec((tm, tn), lambda i, j, r: (r, j))],
        out_specs=[out_spec, out_spec],
        out_shape=[_sds((ns, k, nl), F32), _sds((ns, k, nl), BF16)],
        compiler_params=_cp("parallel", "parallel", "arbitrary"))(a, b)


def _win_start(j, ncc):
    return pl.multiple_of(j * CONV_ROWS + jnp.where(j >= ncc, HALO, 0), SUBLANES)


def _tok_start(j):
    return pl.multiple_of(j * CONV_ROWS, CONV_ROWS)


def _shifted(xw, off):
    n = xw.shape[0]
    sh = (n - off) % n
    y = pltpu.roll(xw, sh, 0) if sh else xw
    return y[:CONV_ROWS]


def _conv_fwd(xw, w_ref, ntaps):
    pad = ntaps // 2
    acc = None
    for k in range(ntaps):
        term = w_ref[k:k + 1, :] * _shifted(xw, HALO + k - pad)
        acc = term if acc is None else acc + term
    return acc


def _conv_bwd_data(xw, w_ref, ntaps):
    pad = ntaps // 2
    acc = None
    for k in range(ntaps):
        term = w_ref[k:k + 1, :] * _shifted(xw, HALO - k + pad)
        acc = term if acc is None else acc + term
    return acc


def _conv_bwd_weight(dw_ref, d, xw, ntaps):
    pad = ntaps // 2
    for k in range(ntaps):
        dw_ref[k:k + 1, :] += _colsum(d * _shifted(xw, HALO + k - pad))


def _zero_halos(pad_ref, lc, t):
    z = jnp.zeros((HALO, LANES), F32)
    pad_ref[0:HALO, :] = z
    pad_ref[HALO + lc:2 * HALO + lc, :] = z
    pad_ref[2 * HALO + t:3 * HALO + t, :] = z


def _pad_dst(j, ncc):
    return pl.multiple_of(j * CONV_ROWS + HALO + jnp.where(j >= ncc, HALO, 0), SUBLANES)


def _chan(t, blk0):
    return pl.BlockSpec((t, LANES), lambda ct: (0, blk0 + ct))


def _tapw(rows):
    return pl.BlockSpec((rows, LANES), lambda ct: (0, ct))


def conv_a_fwd(p, w_pad, bias, lc, d):
    t = p.shape[0]
    nct, nch, ncc = d // LANES, t // CONV_ROWS, lc // CONV_ROWS

    def body(av_ref, ag_ref, w_ref, b_ref, u1_ref, pad_ref):
        _zero_halos(pad_ref, lc, t)

        def fill(j, carry):
            rows = pl.ds(_tok_start(j), CONV_ROWS)
            u0 = av_ref[rows, :].astype(F32) * _sigmoid(ag_ref[rows, :].astype(F32))
            pad_ref[pl.ds(_pad_dst(j, ncc), CONV_ROWS), :] = u0
            return carry

        lax.fori_loop(0, nch, fill, 0)

        def conv(j, carry):
            xw = pad_ref[pl.ds(_win_start(j, ncc), CONV_ROWS + 2 * HALO), :]
            u1_ref[pl.ds(_tok_start(j), CONV_ROWS), :] = _conv_fwd(xw, w_ref, CONV_A_TAPS) + b_ref[...]
            return carry

        lax.fori_loop(0, nch, conv, 0)

    return pl.pallas_call(
        body, name="conv_a_fwd", grid=(nct,),
        in_specs=[_chan(t, 0), _chan(t, nct), _tapw(32), _tapw(1)],
        out_specs=_chan(t, 0), out_shape=_sds((t, d), F32),
        scratch_shapes=[pltpu.VMEM((t + 3 * HALO, LANES), F32)],
        compiler_params=_cp("parallel"))(p, p, w_pad, bias)


def conv_b_fwd(p, w_pad, lc, d):
    t = p.shape[0]
    nct, nch, ncc = d // LANES, t // CONV_ROWS, lc // CONV_ROWS

    def body(bx_ref, bb_ref, bc_ref, bg_ref, w_ref, o_ref, pad_ref):
        _zero_halos(pad_ref, lc, t)

        def fill(j, carry):
            rows = pl.ds(_tok_start(j), CONV_ROWS)
            pad_ref[pl.ds(_pad_dst(j, ncc), CONV_ROWS), :] = (
                bc_ref[rows, :].astype(F32) * bx_ref[rows, :].astype(F32))
            return carry

        lax.fori_loop(0, nch, fill, 0)

        def conv(j, carry):
            rows = pl.ds(_tok_start(j), CONV_ROWS)
            xw = pad_ref[pl.ds(_win_start(j, ncc), CONV_ROWS + 2 * HALO), :]
            v = _conv_fwd(xw, w_ref, CONV_B_TAPS)
            o_ref[rows, :] = (bb_ref[rows, :].astype(F32) * v
                              * _silu(bg_ref[rows, :].astype(F32))).astype(o_ref.dtype)
            return carry

        lax.fori_loop(0, nch, conv, 0)

    return pl.pallas_call(
        body, name="conv_b_fwd", grid=(nct,),
        in_specs=[_chan(t, 3 * nct), _chan(t, 4 * nct), _chan(t, 5 * nct), _chan(t, 6 * nct), _tapw(8)],
        out_specs=_chan(t, 0), out_shape=_sds((t, d), BF16),
        scratch_shapes=[pltpu.VMEM((t + 3 * HALO, LANES), F32)],
        compiler_params=_cp("parallel"))(p, p, p, p, w_pad)


def ln_a_fwd(u1, p, g, b, tm):
    t, d = u1.shape

    def body(u_ref, ag_ref, g_ref, b_ref, o_ref):
        xhat, _ = _ln_stats(u_ref[...])
        u2 = xhat * g_ref[...] + b_ref[...]
        o_ref[...] = (_silu(u2) * _silu(ag_ref[...].astype(F32))).astype(o_ref.dtype)

    return pl.pallas_call(
        body, name="ln_a_fwd", grid=(t // tm,),
        in_specs=[_row(tm, d), _colblk(tm, d, 2), _vec(d), _vec(d)], out_specs=_row(tm, d),
        out_shape=_sds((t, d), BF16), compiler_params=_cp("parallel"))(u1, p, g, b)


def ln_a_bwd(dab, u1, p, g, b, tm):
    t, d = u1.shape

    def body(da_ref, u_ref, ag_ref, g_ref, b_ref, du_ref, dag_ref, acc_ref):
        @pl.when(pl.program_id(0) == 0)
        def _():
            acc_ref[...] = jnp.zeros_like(acc_ref)

        xhat, rstd = _ln_stats(u_ref[...])
        u2 = xhat * g_ref[...] + b_ref[...]
        ag = ag_ref[...].astype(F32)
        da = da_ref[...].astype(F32)
        dag_ref[...] = (da * _silu(u2) * _dsilu(ag)).astype(dag_ref.dtype)
        du2 = da * _silu(ag) * _dsilu(u2)
        du1 = _ln_bwd(du2 * g_ref[...], xhat, rstd)
        du_ref[...] = du1
        acc_ref[0:1, :] += _colsum(du2 * xhat)
        acc_ref[1:2, :] += _colsum(du2)
        acc_ref[2:3, :] += _colsum(du1)

    return pl.pallas_call(
        body, name="ln_a_bwd", grid=(t // tm,),
        in_specs=[_colblk(tm, d, 0), _row(tm, d), _colblk(tm, d, 2), _vec(d), _vec(d)],
        out_specs=[_row(tm, d), _row(tm, d), pl.BlockSpec((SUBLANES, d), lambda j: (0, 0))],
        out_shape=[_sds((t, d), F32), _sds((t, d), BF16), _sds((SUBLANES, d), F32)],
        compiler_params=_cp("arbitrary"))(dab, u1, p, g, b)


def conv_a_bwd(du1, p, w_pad, lc, d):
    t = p.shape[0]
    nct, nch, ncc = d // LANES, t // CONV_ROWS, lc // CONV_ROWS

    def body(du_ref, av_ref, ag_ref, w_ref, dav_ref, dag_ref, dw_ref, pad_u, pad_d):
        _zero_halos(pad_u, lc, t)
        _zero_halos(pad_d, lc, t)
        dw_ref[...] = jnp.zeros_like(dw_ref)

        def fill(j, carry):
            rows = pl.ds(_tok_start(j), CONV_ROWS)
            dst = pl.ds(_pad_dst(j, ncc), CONV_ROWS)
            pad_u[dst, :] = av_ref[rows, :].astype(F32) * _sigmoid(ag_ref[rows, :].astype(F32))
            pad_d[dst, :] = du_ref[rows, :]
            return carry

        lax.fori_loop(0, nch, fill, 0)

        def step(j, carry):
            rows = pl.ds(_tok_start(j), CONV_ROWS)
            win = pl.ds(_win_start(j, ncc), CONV_ROWS + 2 * HALO)
            du0 = _conv_bwd_data(pad_d[win, :], w_ref, CONV_A_TAPS)
            sig = _sigmoid(ag_ref[rows, :].astype(F32))
            dav_ref[rows, :] = (du0 * sig).astype(dav_ref.dtype)
            dag_ref[rows, :] = (du0 * av_ref[rows, :].astype(F32) * sig * (1.0 - sig)).astype(dag_ref.dtype)
            _conv_bwd_weight(dw_ref, du_ref[rows, :], pad_u[win, :], CONV_A_TAPS)
            return carry

        lax.fori_loop(0, nch, step, 0)

    return pl.pallas_call(
        body, name="conv_a_bwd", grid=(nct,),
        in_specs=[_chan(t, 0), _chan(t, 0), _chan(t, nct), _tapw(32)],
        out_specs=[_chan(t, 0), _chan(t, 0), _tapw(32)],
        out_shape=[_sds((t, d), BF16), _sds((t, d), BF16), _sds((32, d), F32)],
        scratch_shapes=[pltpu.VMEM((t + 3 * HALO, LANES), F32), pltpu.VMEM((t + 3 * HALO, LANES), F32)],
        compiler_params=_cp("parallel"))(du1, p, p, w_pad)


def conv_b_bwd(dab, p, w_pad, lc, d):
    t = p.shape[0]
    nct, nch, ncc = d // LANES, t // CONV_ROWS, lc // CONV_ROWS

    def body(db_ref, bx_ref, bb_ref, bc_ref, bg_ref, w_ref,
             dbx_ref, dbb_ref, dbc_ref, dbg_ref, dw_ref, pad_t, pad_d):
        _zero_halos(pad_t, lc, t)
        _zero_halos(pad_d, lc, t)
        dw_ref[...] = jnp.zeros_like(dw_ref)

        def fill(j, carry):
            rows = pl.ds(_tok_start(j), CONV_ROWS)
            pad_t[pl.ds(_pad_dst(j, ncc), CONV_ROWS), :] = (
                bc_ref[rows, :].astype(F32) * bx_ref[rows, :].astype(F32))
            return carry

        lax.fori_loop(0, nch, fill, 0)

        def first(j, carry):
            rows = pl.ds(_tok_start(j), CONV_ROWS)
            xw = pad_t[pl.ds(_win_start(j, ncc), CONV_ROWS + 2 * HALO), :]
            v = _conv_fwd(xw, w_ref, CONV_B_TAPS)
            bg = bg_ref[rows, :].astype(F32)
            bb = bb_ref[rows, :].astype(F32)
            db = db_ref[rows, :].astype(F32)
            sg = _silu(bg)
            dbb_ref[rows, :] = (db * v * sg).astype(dbb_ref.dtype)
            dbg_ref[rows, :] = (db * bb * v * _dsilu(bg)).astype(dbg_ref.dtype)
            dv = db * bb * sg
            pad_d[pl.ds(_pad_dst(j, ncc), CONV_ROWS), :] = dv
            _conv_bwd_weight(dw_ref, dv, xw, CONV_B_TAPS)
            return carry

        lax.fori_loop(0, nch, first, 0)

        def second(j, carry):
            rows = pl.ds(_tok_start(j), CONV_ROWS)
            dt = _conv_bwd_data(pad_d[pl.ds(_win_start(j, ncc), CONV_ROWS + 2 * HALO), :], w_ref, CONV_B_TAPS)
            dbc_ref[rows, :] = (dt * bx_ref[rows, :].astype(F32)).astype(dbc_ref.dtype)
            dbx_ref[rows, :] = (dt * bc_ref[rows, :].astype(F32)).astype(dbx_ref.dtype)
            return carry

        lax.fori_loop(0, nch, second, 0)

    return pl.pallas_call(
        body, name="conv_b_bwd", grid=(nct,),
        in_specs=[_chan(t, nct), _chan(t, 3 * nct), _chan(t, 4 * nct), _chan(t, 5 * nct), _chan(t, 6 * nct),
                  _tapw(8)],
        out_specs=[_chan(t, 0)] * 4 + [_tapw(8)],
        out_shape=[_sds((t, d), BF16)] * 4 + [_sds((8, d), F32)],
        scratch_shapes=[pltpu.VMEM((t + 3 * HALO, LANES), F32), pltpu.VMEM((t + 3 * HALO, LANES), F32)],
        compiler_params=_cp("parallel"))(dab, p, p, p, p, w_pad)


def _swap_halves(z, first_half):
    return jnp.where(first_half, pltpu.roll(z, 96, 1), pltpu.roll(z, 32, 1))


def _first_half_mask(rows):
    lane = lax.broadcasted_iota(jnp.int32, (rows, HEAD_DIM), 1)
    return (lane & 32) == 0


def qk_fwd(qgkv, qg, kg, cos_t, sin_t, ad, kvd, tm, qscale):
    t = qgkv.shape[0]

    def body(q_ref, k_ref, qg_ref, kg_ref, c_ref, s_ref, qo_ref, ko_ref):
        first = _first_half_mask(tm)
        cosv, sinv = c_ref[...], s_ref[...]

        def head(x, gain):
            inv = lax.rsqrt(jnp.mean(x * x, axis=-1, keepdims=True) + RMS_EPS)
            yv = x * inv * gain
            return yv * cosv + _swap_halves(yv, first) * sinv

        for h in range(ad // HEAD_DIM):
            sl = slice(h * HEAD_DIM, (h + 1) * HEAD_DIM)
            qo_ref[:, sl] = (head(q_ref[:, sl].astype(F32), qg_ref[...]) * qscale).astype(qo_ref.dtype)
        for h in range(kvd // HEAD_DIM):
            sl = slice(h * HEAD_DIM, (h + 1) * HEAD_DIM)
            ko_ref[:, sl] = head(k_ref[:, sl].astype(F32), kg_ref[...]).astype(ko_ref.dtype)

    return pl.pallas_call(
        body, name="qk_fwd", grid=(t // tm,),
        in_specs=[_colblk(tm, ad, 0), _colblk(tm, kvd, 2 * ad // kvd), _vec(HEAD_DIM), _vec(HEAD_DIM),
                  _row(tm, HEAD_DIM), _row(tm, HEAD_DIM)],
        out_specs=[_row(tm, ad), _row(tm, kvd)],
        out_shape=[_sds((t, ad), BF16), _sds((t, kvd), BF16)],
        compiler_params=_cp("parallel"))(qgkv, qgkv, qg, kg, cos_t, sin_t)


def qk_bwd(dqr, dkr, qgkv, qg, kg, cos_t, sin_t, ad, kvd, tm):
    t = qgkv.shape[0]

    def body(dq_ref, dk_ref, q_ref, k_ref, qg_ref, kg_ref, c_ref, s_ref, dqo_ref, dko_ref, acc_ref):
        @pl.when(pl.program_id(0) == 0)
        def _():
            acc_ref[...] = jnp.zeros_like(acc_ref)

        first = _first_half_mask(tm)
        cosv, sinv = c_ref[...], s_ref[...]

        def head(x, gain, dout):
            inv = lax.rsqrt(jnp.mean(x * x, axis=-1, keepdims=True) + RMS_EPS)
            xn = x * inv
            dy = dout * cosv + _swap_halves(dout * sinv, first)
            dxn = dy * gain
            dx = inv * (dxn - xn * jnp.mean(dxn * xn, axis=-1, keepdims=True))
            return dx, _colsum(dy * xn)

        dqg = jnp.zeros((1, HEAD_DIM), F32)
        for h in range(ad // HEAD_DIM):
            sl = slice(h * HEAD_DIM, (h + 1) * HEAD_DIM)
            dx, dg = head(q_ref[:, sl].astype(F32), qg_ref[...], dq_ref[:, sl])
            dqo_ref[:, sl] = dx.astype(dqo_ref.dtype)
            dqg = dqg + dg
        dkg = jnp.zeros((1, HEAD_DIM), F32)
        for h in range(kvd // HEAD_DIM):
            sl = slice(h * HEAD_DIM, (h + 1) * HEAD_DIM)
            dx, dg = head(k_ref[:, sl].astype(F32), kg_ref[...], dk_ref[:, sl])
            dko_ref[:, sl] = dx.astype(dko_ref.dtype)
            dkg = dkg + dg
        acc_ref[0:1, :] += dqg
        acc_ref[1:2, :] += dkg

    return pl.pallas_call(
        body, name="qk_bwd", grid=(t // tm,),
        in_specs=[_row(tm, ad), _row(tm, kvd), _colblk(tm, ad, 0), _colblk(tm, kvd, 2 * ad // kvd),
                  _vec(HEAD_DIM), _vec(HEAD_DIM), _row(tm, HEAD_DIM), _row(tm, HEAD_DIM)],
        out_specs=[_row(tm, ad), _row(tm, kvd), pl.BlockSpec((SUBLANES, HEAD_DIM), lambda j: (0, 0))],
        out_shape=[_sds((t, ad), BF16), _sds((t, kvd), BF16), _sds((SUBLANES, HEAD_DIM), F32)],
        compiler_params=_cp("arbitrary"))(dqr, dkr, qgkv, qgkv, qg, kg, cos_t, sin_t)


_NT = (((1,), (1,)), ((), ()))


def _chunks_t(a, nkv, tm):
    t = a.shape[0]
    return a.reshape(t // tm, tm, nkv, HEAD_DIM).transpose(2, 0, 3, 1)


def _tree_rows(x, op):
    slabs = [x[i:i + SUBLANES] for i in range(0, x.shape[0], SUBLANES)]
    while len(slabs) > 1:
        slabs = [op(slabs[i], slabs[i + 1]) for i in range(0, len(slabs), 2)]
    return slabs[0]


def flash_fwd(qr, kr, vt, qgkv, ad, kvd, tm):
    t = qr.shape[0]
    nkv = kvd // HEAD_DIM
    gw = ad // nkv
    hpg = gw // HEAD_DIM
    nt = t // tm
    gate_blk0 = ad // gw

    def body(q_ref, k_ref, vt_ref, g_ref, o_ref, og_ref, lse_ref):
        qi = pl.program_id(1)
        heads = [slice(h * HEAD_DIM, (h + 1) * HEAD_DIM) for h in range(hpg)]
        qs = [q_ref[:, sl] for sl in heads]

        def attend(kc, vtcs, carry):
            sts = [lax.dot_general(kc, q, _NT, preferred_element_type=F32) for q in qs]
            out = []
            for h in range(hpg):
                m, l, acc = carry[h]
                m_new = jnp.maximum(m, jnp.max(_tree_rows(sts[h], jnp.maximum), axis=0, keepdims=True))
                a = jnp.exp2(m - m_new)
                pt = jnp.exp2(sts[h] - m_new)
                l = a * l + jnp.sum(_tree_rows(pt, jnp.add), axis=0, keepdims=True)
                ptb = pt.astype(BF16)
                acc = a * acc
                for j, vtc in enumerate(vtcs):
                    acc = acc + jnp.dot(vtc, ptb[j * tm:(j + 1) * tm], preferred_element_type=F32)
                out.append((m_new, l, acc))
            return tuple(out)

        def step(c, carry):
            first = 1 + KV_CHUNKS_PER_STEP * c
            kc = k_ref[pl.ds(pl.multiple_of(first * tm, tm), KV_CHUNKS_PER_STEP * tm), :]
            return attend(kc, [vt_ref[first + j] for j in range(KV_CHUNKS_PER_STEP)], carry)

        init = tuple((jnp.full((1, tm), -1e30, F32), jnp.zeros((1, tm), F32), jnp.zeros((HEAD_DIM, tm), F32))
                     for _ in range(hpg))
        ctx_done = attend(k_ref[0:tm, :], [vt_ref[0]], init)
        res = lax.fori_loop(0, jnp.where(qi == 0, 0, (nt - 1) // KV_CHUNKS_PER_STEP), step, ctx_done)
        for h, sl in enumerate(heads):
            m, l, acc = res[h]
            o = (acc / l).T
            o_ref[:, sl] = o.astype(o_ref.dtype)
            og_ref[:, sl] = (o * _silu(g_ref[:, sl].astype(F32))).astype(og_ref.dtype)
            lse_ref[h:h + 1, :] = m + jnp.log(l) * LOG2_E

    return pl.pallas_call(
        body, name="flash_fwd", grid=(nkv, nt),
        in_specs=[pl.BlockSpec((tm, gw), lambda g, i: (i, g)),
                  pl.BlockSpec((t, HEAD_DIM), lambda g, i: (0, g)),
                  pl.BlockSpec((None, nt, HEAD_DIM, tm), lambda g, i: (g, 0, 0, 0)),
                  pl.BlockSpec((tm, gw), lambda g, i: (i, gate_blk0 + g))],
        out_specs=[pl.BlockSpec((tm, gw), lambda g, i: (i, g)),
                   pl.BlockSpec((tm, gw), lambda g, i: (i, g)),
                   pl.BlockSpec((None, None, hpg, tm), lambda g, i: (g, i, 0, 0))],
        out_shape=[_sds((t, ad), BF16), _sds((t, ad), BF16), _sds((nkv, nt, hpg, tm), F32)],
        compiler_params=_cp("parallel", "parallel"))(qr, kr, vt, qgkv)


def gate_bwd(dog, o, qgkv, ad, kvd, tm):
    t = o.shape[0]
    nkv = kvd // HEAD_DIM
    hpg = ad // nkv // HEAD_DIM

    def body(dog_ref, o_ref, g_ref, do_ref, dg_ref, dl_ref):
        lane = lax.broadcasted_iota(jnp.int32, (tm, LANES), 1)
        for grp in range(nkv):
            blk = jnp.zeros((tm, LANES), F32)
            for hh in range(hpg):
                h = grp * hpg + hh
                sl = slice(h * HEAD_DIM, (h + 1) * HEAD_DIM)
                dv = dog_ref[:, sl].astype(F32)
                ov = o_ref[:, sl].astype(F32)
                gv = g_ref[:, sl].astype(F32)
                doh = dv * _silu(gv)
                do_ref[:, sl] = doh.astype(do_ref.dtype)
                dg_ref[:, sl] = (dv * ov * _dsilu(gv)).astype(dg_ref.dtype)
                blk = jnp.where(lane == hh, jnp.sum(doh * ov, axis=-1, keepdims=True), blk)
            dl_ref[grp] = blk

    return pl.pallas_call(
        body, name="gate_bwd", grid=(t // tm,),
        in_specs=[_row(tm, ad), _row(tm, ad), _colblk(tm, ad, 1)],
        out_specs=[_row(tm, ad), _row(tm, ad), pl.BlockSpec((nkv, tm, LANES), lambda j: (0, j, 0))],
        out_shape=[_sds((t, ad), BF16), _sds((t, ad), BF16), _sds((nkv, t, LANES), F32)],
        compiler_params=_cp("parallel"))(dog, o, qgkv)


def flash_bwd(qr, do, kr, kt, qgkv, lse_t, delta_t, ad, kvd, tm, scale):
    t = qr.shape[0]
    nkv = kvd // HEAD_DIM
    gw = ad // nkv
    hpg = gw // HEAD_DIM
    nt = t // tm
    v_blk0 = (2 * ad + kvd) // HEAD_DIM

    def body(q_ref, do_ref, k_ref, v_ref, kt_ref, lse_ref, dl_ref, dq_ref, dk_ref, dv_ref):
        qi = pl.program_id(1)

        @pl.when(qi == 0)
        def _():
            dk_ref[...] = jnp.zeros_like(dk_ref)
            dv_ref[...] = jnp.zeros_like(dv_ref)

        heads = [slice(h * HEAD_DIM, (h + 1) * HEAD_DIM) for h in range(hpg)]
        qs = [q_ref[:, sl] for sl in heads]
        dos = [do_ref[:, sl] for sl in heads]
        lses = [lse_ref[h:h + 1, :] for h in range(hpg)]
        dls = [dl_ref[h:h + 1, :] for h in range(hpg)]

        def attend(rows, ktcs, dqts):
            kc = k_ref[rows, :]
            vc = v_ref[rows, :]
            dk = jnp.zeros((len(ktcs) * tm, HEAD_DIM), F32)
            dv = jnp.zeros((len(ktcs) * tm, HEAD_DIM), F32)
            out = []
            sts = [lax.dot_general(kc, q, _NT, preferred_element_type=F32) for q in qs]
            dpts = [lax.dot_general(vc, d, _NT, preferred_element_type=F32) for d in dos]
            for h in range(hpg):
                pt = jnp.exp2(sts[h] - lses[h])
                dv = dv + jnp.dot(pt.astype(BF16), dos[h], preferred_element_type=F32)
                dst = (pt * (dpts[h] - dls[h])).astype(BF16)
                dk = dk + jnp.dot(dst, qs[h], preferred_element_type=F32)
                dqt = dqts[h]
                for j, ktc in enumerate(ktcs):
                    dqt = dqt + jnp.dot(ktc, dst[j * tm:(j + 1) * tm], preferred_element_type=F32)
                out.append(dqt)
            dk_ref[rows, :] += dk
            dv_ref[rows, :] += dv
            return tuple(out)

        def step(c, dqts):
            first = 1 + KV_CHUNKS_PER_STEP * c
            rows = pl.ds(pl.multiple_of(first * tm, tm), KV_CHUNKS_PER_STEP * tm)
            return attend(rows, [kt_ref[first + j] for j in range(KV_CHUNKS_PER_STEP)], dqts)

        ctx_done = attend(pl.ds(0, tm), [kt_ref[0]], tuple(jnp.zeros((HEAD_DIM, tm), F32) for _ in range(hpg)))
        res = lax.fori_loop(0, jnp.where(qi == 0, 0, (nt - 1) // KV_CHUNKS_PER_STEP), step, ctx_done)
        for h, sl in enumerate(heads):
            dq_ref[:, sl] = res[h].T * scale

        @pl.when(qi == nt - 1)
        def _():
            dk_ref[...] = dk_ref[...] * LN_2

    return pl.pallas_call(
        body, name="flash_bwd", grid=(nkv, nt),
        in_specs=[pl.BlockSpec((tm, gw), lambda g, i: (i, g)),
                  pl.BlockSpec((tm, gw), lambda g, i: (i, g)),
                  pl.BlockSpec((t, HEAD_DIM), lambda g, i: (0, g)),
                  pl.BlockSpec((t, HEAD_DIM), lambda g, i: (0, v_blk0 + g)),
                  pl.BlockSpec((None, nt, HEAD_DIM, tm), lambda g, i: (g, 0, 0, 0)),
                  pl.BlockSpec((None, None, hpg, tm), lambda g, i: (g, i, 0, 0)),
                  pl.BlockSpec((None, None, hpg, tm), lambda g, i: (g, i, 0, 0))],
        out_specs=[pl.BlockSpec((tm, gw), lambda g, i: (i, g)),
                   pl.BlockSpec((t, HEAD_DIM), lambda g, i: (0, g)),
                   pl.BlockSpec((t, HEAD_DIM), lambda g, i: (0, g))],
        out_shape=[_sds((t, ad), F32), _sds((t, kvd), F32), _sds((t, kvd), F32)],
        compiler_params=_cp("parallel", "arbitrary"))(qr, do, kr, qgkv, kt, lse_t, delta_t)


def _rows_per_head(a, tm, hpg):
    nkv, t, _ = a.shape
    return a[:, :, :hpg].reshape(nkv, t // tm, tm, hpg).transpose(0, 1, 3, 2)


def adaln_fwd(c16, w_mod):
    nlay, d, nl = w_mod.shape
    tn = _pick(nl, (512, 256, 128))

    def body(c_ref, w_ref, o_ref):
        o_ref[...] = jnp.dot(_silu(c_ref[...]), w_ref[...], preferred_element_type=F32,
                             precision=lax.Precision.HIGHEST)

    return pl.pallas_call(
        body, name="adaln_fwd", grid=(nlay, nl // tn),
        in_specs=[pl.BlockSpec((16, d), lambda l, j: (0, 0)),
                  pl.BlockSpec((None, d, tn), lambda l, j: (l, 0, j))],
        out_specs=pl.BlockSpec((None, 16, tn), lambda l, j: (l, 0, j)),
        out_shape=_sds((nlay, 16, nl), F32), compiler_params=_cp("parallel", "parallel"))(c16, w_mod)


def adaln_bwd(c16t, dm, w_mod):
    nlay, d, nl = w_mod.shape
    tn = _pick(nl, (512, 256, 128))

    def body(c_ref, dm_ref, w_ref, dw_ref, dc_ref):
        @pl.when(pl.program_id(1) == 0)
        def _():
            dc_ref[...] = jnp.zeros_like(dc_ref)

        dmv = dm_ref[...]
        dw_ref[...] = jnp.dot(_silu(c_ref[...]), dmv, preferred_element_type=F32,
                              precision=lax.Precision.HIGHEST)
        dc_ref[...] += lax.dot_general(dmv, w_ref[...], _NT, preferred_element_type=F32,
                                       precision=lax.Precision.HIGHEST)

    return pl.pallas_call(
        body, name="adaln_bwd", grid=(nlay, nl // tn),
        in_specs=[pl.BlockSpec((d, 16), lambda l, j: (0, 0)),
                  pl.BlockSpec((None, 16, tn), lambda l, j: (l, 0, j)),
                  pl.BlockSpec((None, d, tn), lambda l, j: (l, 0, j))],
        out_specs=[pl.BlockSpec((None, d, tn), lambda l, j: (l, 0, j)),
                   pl.BlockSpec((None, 16, d), lambda l, j: (l, 0, 0))],
        out_shape=[_sds((nlay, d, nl), F32), _sds((nlay, 16, d), F32)],
        compiler_params=_cp("parallel", "arbitrary"))(c16t, dm, w_mod)


def sum_leading(a, name):
    n = a.shape[0]

    def body(a_ref, o_ref):
        acc = a_ref[0]
        for i in range(1, n):
            acc = acc + a_ref[i]
        o_ref[...] = acc

    return pl.pallas_call(body, name=name, in_specs=[VMEM_FULL], out_specs=VMEM_FULL,
                          out_shape=_sds(a.shape[1:], F32),
                          compiler_params=pltpu.CompilerParams(vmem_limit_bytes=VMEM_LIMIT))(a)


def c_ctx_grad(parts, c_ctx):
    n = parts.shape[0]

    def body(p_ref, c_ref, o_ref):
        acc = p_ref[0]
        for i in range(1, n):
            acc = acc + p_ref[i]
        o_ref[...] = acc * _dsilu(c_ref[...])

    return pl.pallas_call(body, name="c_ctx_grad", in_specs=[VMEM_FULL, VMEM_FULL], out_specs=VMEM_FULL,
                          out_shape=_sds(c_ctx.shape, F32))(parts, c_ctx)


def _as2d(a):
    return a.reshape(-1, a.shape[-1])


def _row_tile(r, c):
    for tr in (1024, 512, 256, 128, 64, 32, 16, 8):
        if r % tr == 0 and tr * c * 4 <= (1 << 20):
            return tr
    return r


def add_n(items, name, out_dtypes=(F32,)):
    norm = [it if isinstance(it, tuple) else (it, ()) for it in items]
    shape = norm[0][0].shape[len(norm[0][1]):]
    r, c = _as2d(jnp.zeros(shape, BF16)).shape
    tr = _row_tile(r, c)
    dyn, specs, flat = [], [], []
    for arr, lead in norm:
        slots = []
        for ix in lead:
            if isinstance(ix, int):
                slots.append(ix)
            else:
                slots.append((len(dyn),))
                dyn.append(ix)

        def imap(i, s, slots=tuple(slots)):
            return tuple(s[k[0]] if isinstance(k, tuple) else k for k in slots) + (i, 0)

        specs.append(pl.BlockSpec((None,) * len(lead) + (tr, c), imap))
        flat.append(arr.reshape(arr.shape[:len(lead)] + (r, c)))
    sel = jnp.stack([jnp.asarray(v, jnp.int32) for v in dyn]) if dyn else jnp.zeros((1,), jnp.int32)
    n_in = len(flat)

    def body(s_ref, *refs):
        acc = refs[0][...].astype(F32)
        for ref in refs[1:n_in]:
            acc = acc + ref[...].astype(F32)
        for ref in refs[n_in:]:
            ref[...] = acc.astype(ref.dtype)

    out_spec = pl.BlockSpec((tr, c), lambda i, s: (i, 0))
    outs = pl.pallas_call(
        body, name=name,
        grid_spec=pltpu.PrefetchScalarGridSpec(num_scalar_prefetch=1, grid=(r // tr,), in_specs=specs,
                                               out_specs=[out_spec] * len(out_dtypes)),
        out_shape=[_sds((r, c), dt) for dt in out_dtypes], compiler_params=_cp("parallel"))(sel, *flat)
    return [o.reshape(shape) for o in outs]


def pair_add(p0, p1, got, which, name):
    shape = p0.shape
    flat = [_as2d(a) for a in (p0, p1, got)]
    r, c = flat[0].shape
    tr = _row_tile(r, c)

    def body(s_ref, p0_ref, p1_ref, g_ref, o_ref):
        own = jnp.where(s_ref[0] == 0, p0_ref[...], p1_ref[...])
        o_ref[...] = (own + g_ref[...].astype(F32)).astype(o_ref.dtype)

    spec = pl.BlockSpec((tr, c), lambda i, s: (i, 0))
    out = pl.pallas_call(
        body, name=name,
        grid_spec=pltpu.PrefetchScalarGridSpec(num_scalar_prefetch=1, grid=(r // tr,), in_specs=[spec] * 3,
                                               out_specs=spec),
        out_shape=_sds((r, c), BF16), compiler_params=_cp("parallel"))(
            jnp.asarray(which, jnp.int32).reshape(1), *flat)
    return out.reshape(shape)


def adamw(w, g, m, v):
    shape = w.shape
    flat = [_as2d(a.reshape((1,) + shape) if len(shape) == 1 else a) for a in (w, g, m, v)]
    r, c = flat[0].shape
    tr = _row_tile(r, c)
    c1 = 1.0 - ADAM_B1 ** ADAM_STEP
    c2 = 1.0 - ADAM_B2 ** ADAM_STEP

    def body(w_ref, g_ref, m_ref, v_ref, d_ref, nm_ref, nv_ref):
        gv = g_ref[...]
        nm = ADAM_B1 * m_ref[...] + (1.0 - ADAM_B1) * gv
        nv = ADAM_B2 * v_ref[...] + (1.0 - ADAM_B2) * (gv * gv)
        d_ref[...] = -ADAM_LR * ((nm / c1) / (jnp.sqrt(nv / c2) + ADAM_EPS) + ADAM_WD * w_ref[...])
        nm_ref[...] = nm
        nv_ref[...] = nv

    spec = pl.BlockSpec((tr, c), lambda i: (i, 0))
    outs = pl.pallas_call(
        body, name="adamw", grid=(r // tr,), in_specs=[spec] * 4, out_specs=[spec] * 3,
        out_shape=[_sds((r, c), F32)] * 3, compiler_params=_cp("parallel"))(*flat)
    return tuple(o.reshape(shape) for o in outs)


def _place():
    return lax.axis_index("x"), lax.axis_index("y"), lax.axis_index("c")


def _remote(src, dst, ssem, rsem, dev):
    return pltpu.make_async_remote_copy(src_ref=src, dst_ref=dst, send_sem=ssem, recv_sem=rsem,
                                        device_id=dev, device_id_type=MESH)


def all_gather8(v, name):
    m_per, n = v.shape

    def body(x_ref, out_ref, send_sems, recv_sems, local_sem):
        x, y, c = _place()
        me, sibling = (x, y, c), (x, y, 1 - c)
        chips = [(1 - x, y), (x, 1 - y), (1 - x, 1 - y)]

        def rows(px, py, pc):
            return out_ref.at[pl.ds((4 * px + 2 * py + pc) * m_per, m_per), :]

        def copy(k, block, to, src=None):
            return _remote(rows(*block) if src is None else src, rows(*block),
                           send_sems.at[k], recv_sems.at[k], to)

        mine = pltpu.make_async_copy(x_ref, rows(*me), local_sem)
        mine.start()
        first = [copy(0, me, sibling, src=x_ref)]
        first += [copy(1 + j, me, (*chip, c), src=x_ref) for j, chip in enumerate(chips)]
        for cp in first:
            cp.start()
        passed = [copy(4 + j, (*chip, c), sibling) for j, chip in enumerate(chips)]
        for j, chip in enumerate(chips):
            copy(1 + j, (*chip, c), me).wait_recv()
            passed[j].start()
        copy(0, sibling, me).wait_recv()
        for j, chip in enumerate(chips):
            copy(4 + j, (*chip, 1 - c), me).wait_recv()
        for cp in first + passed:
            cp.wait_send()
        mine.wait()

    return pl.pallas_call(
        body, name=name, out_shape=_sds((8 * m_per, n), v.dtype),
        in_specs=[VMEM_FULL], out_specs=VMEM_FULL,
        scratch_shapes=[pltpu.SemaphoreType.DMA((7,)), pltpu.SemaphoreType.DMA((7,)), pltpu.SemaphoreType.DMA],
        compiler_params=pltpu.CompilerParams(vmem_limit_bytes=VMEM_LIMIT))(v)


def _slabs(ref, n):
    rows = ref.shape[0] // n
    return [ref.at[pl.ds(i * rows, rows)] for i in range(n)]


def gather_weight(w, name):
    _, r, cdim = w.shape
    n = D2D_PARTS

    def body(w_ref, out_ref, slab_ref, send_sems, recv_sems, local_sem):
        x, y, c = _place()
        sibling = (x, y, 1 - c)
        chips = [(1 - x, y), (x, 1 - y), (1 - x, 1 - y)]
        mine = 2 * x + y
        src = _slabs(w_ref.at[c], n)
        first = []
        for k, chip in enumerate(chips):
            dst = _slabs(out_ref.at[c, mine], n)
            for j in range(n):
                cp = _remote(src[j], dst[j], send_sems.at[k * n + j], recv_sems.at[k * n + j], (*chip, c))
                cp.start()
                first.append(cp)
        for i in range(2):
            for s_in, s_out in zip(_slabs(w_ref.at[i], n), _slabs(out_ref.at[i, mine], n)):
                cp = pltpu.make_async_copy(s_in, slab_ref, local_sem)
                cp.start()
                cp.wait()
                cp = pltpu.make_async_copy(slab_ref, s_out, local_sem)
                cp.start()
                cp.wait()
        passed = []
        for j in range(n):
            for k, (px, py) in enumerate(chips):
                theirs = _slabs(out_ref.at[c, 2 * px + py], n)[j]
                _remote(src[j], theirs, send_sems.at[k * n + j], recv_sems.at[k * n + j], sibling).wait_recv()
                cp = _remote(theirs, theirs, send_sems.at[(3 + k) * n + j], recv_sems.at[(3 + k) * n + j], sibling)
                cp.start()
                passed.append(cp)
        for j in range(n):
            for k, (px, py) in enumerate(chips):
                other = _slabs(out_ref.at[1 - c, 2 * px + py], n)[j]
                _remote(other, other, send_sems.at[(3 + k) * n + j], recv_sems.at[(3 + k) * n + j],
                        sibling).wait_recv()
        for cp in first + passed:
            cp.wait_send()

    return pl.pallas_call(
        body, name=name, out_shape=_sds((2, 4, r, cdim), w.dtype), in_specs=[ANY], out_specs=ANY,
        scratch_shapes=[pltpu.VMEM((r // n, cdim), w.dtype), pltpu.SemaphoreType.DMA((6 * n,)),
                        pltpu.SemaphoreType.DMA((6 * n,)), pltpu.SemaphoreType.DMA],
        compiler_params=pltpu.CompilerParams(vmem_limit_bytes=VMEM_LIMIT))(w)


def rs_chip_exchange(gb0, gb1, name):
    _, r, cdim = gb0.shape

    def body(gb0_ref, gb1_ref, st_ref, send_sems, recv_sems):
        x, y, c = _place()
        chips = [(1 - x, y), (x, 1 - y), (1 - x, 1 - y)]
        cps = []
        for k, (px, py) in enumerate(chips):
            for i, gb_ref in enumerate((gb0_ref, gb1_ref)):
                cps.append(_remote(gb_ref.at[2 * px + py], st_ref.at[k, i], send_sems.at[2 * k + i],
                                   recv_sems.at[2 * k + i], (px, py, c)))
        for cp in cps:
            cp.start()
        for cp in cps:
            cp.wait()

    return pl.pallas_call(
        body, name=name, out_shape=_sds((3, 2, r, cdim), BF16), in_specs=[ANY] * 2, out_specs=ANY,
        scratch_shapes=[pltpu.SemaphoreType.DMA((6,)), pltpu.SemaphoreType.DMA((6,))])(gb0, gb1)


def rs_pair_swap(pb0, pb1, name):
    r, cdim = pb0.shape
    n = 2 * D2D_PARTS

    def body(pb0_ref, pb1_ref, got_ref, stage_ref, send_sems, recv_sems, local_sem):
        x, y, c = _place()
        sibling = (x, y, 1 - c)

        @pl.when(c == 0)
        def _():
            load = pltpu.make_async_copy(pb1_ref, stage_ref, local_sem)
            load.start()
            load.wait()

        @pl.when(c == 1)
        def _():
            load = pltpu.make_async_copy(pb0_ref, stage_ref, local_sem)
            load.start()
            load.wait()

        src, dst = _slabs(stage_ref, n), _slabs(got_ref, n)
        cps = [_remote(src[j], dst[j], send_sems.at[j], recv_sems.at[j], sibling) for j in range(n)]
        for cp in cps:
            cp.start()
        for cp in cps:
            cp.wait()

    return pl.pallas_call(
        body, name=name, out_shape=_sds((r, cdim), BF16), in_specs=[ANY, ANY], out_specs=VMEM_FULL,
        scratch_shapes=[pltpu.VMEM((r, cdim), BF16), pltpu.SemaphoreType.DMA((n,)), pltpu.SemaphoreType.DMA((n,)),
                        pltpu.SemaphoreType.DMA],
        compiler_params=pltpu.CompilerParams(vmem_limit_bytes=VMEM_LIMIT))(pb0, pb1)


def rs_pair_share(red, name):
    n = 2 * D2D_PARTS

    def body(red_ref, out_ref, send_sems, recv_sems, local_sem):
        x, y, c = _place()
        sibling = (x, y, 1 - c)
        keep = pltpu.make_async_copy(red_ref, out_ref.at[c], local_sem)
        keep.start()
        src, dst, got = _slabs(red_ref, n), _slabs(out_ref.at[c], n), _slabs(out_ref.at[1 - c], n)
        cps = [_remote(src[j], dst[j], send_sems.at[j], recv_sems.at[j], sibling) for j in range(n)]
        for cp in cps:
            cp.start()
        for cp in cps:
            cp.wait_send()
        for j in range(n):
            _remote(src[j], got[j], send_sems.at[j], recv_sems.at[j], sibling).wait_recv()
        keep.wait()

    return pl.pallas_call(
        body, name=name, out_shape=_sds((2,) + red.shape, red.dtype), in_specs=[VMEM_FULL], out_specs=VMEM_FULL,
        scratch_shapes=[pltpu.SemaphoreType.DMA((n,)), pltpu.SemaphoreType.DMA((n,)), pltpu.SemaphoreType.DMA],
        compiler_params=pltpu.CompilerParams(vmem_limit_bytes=VMEM_LIMIT))(red)


def reduce_scatter_grad(g0, g1, shard, core, tag):
    theirs = rs_chip_exchange(g0[1], g1[1], "rs_exchange_" + tag)
    p0, pb0 = add_n([(g0[0], (shard,)), (theirs, (0, 0)), (theirs, (1, 0)), (theirs, (2, 0))],
                    "rs_chip_add0_" + tag, (F32, BF16))
    p1, pb1 = add_n([(g1[0], (shard,)), (theirs, (0, 1)), (theirs, (1, 1)), (theirs, (2, 1))],
                    "rs_chip_add1_" + tag, (F32, BF16))
    got = rs_pair_swap(pb0, pb1, "rs_swap_" + tag)
    red = pair_add(p0, p1, got, core, "rs_pair_add_" + tag)
    return rs_pair_share(red, "rs_share_" + tag).astype(F32)


def _rope_tables(lc, s):
    rows_n = s // GRID_W
    row = jnp.repeat(jnp.arange(rows_n, dtype=F32), GRID_W)
    col = jnp.tile(jnp.arange(GRID_W, dtype=F32), rows_n)
    axis_dim = HEAD_DIM // 2
    inv_freq = ROPE_THETA ** (-jnp.arange(0, axis_dim, 2, dtype=F32) / axis_dim)
    ang_r = row[:, None] * inv_freq[None, :]
    ang_c = col[:, None] * inv_freq[None, :]
    cr, sr, cc, sc = jnp.cos(ang_r), jnp.sin(ang_r), jnp.cos(ang_c), jnp.sin(ang_c)
    cos_l = jnp.concatenate([cr, cr, cc, cc], axis=1)
    sin_l = jnp.concatenate([-sr, sr, -sc, sc], axis=1)
    cos_t = jnp.concatenate([jnp.ones((lc, HEAD_DIM), F32), cos_l], axis=0)
    sin_t = jnp.concatenate([jnp.zeros((lc, HEAD_DIM), F32), sin_l], axis=0)
    return cos_t, sin_t


def _pad_rows(a, rows):
    return jnp.concatenate([a, jnp.zeros((rows - a.shape[0],) + a.shape[1:], a.dtype)], axis=0)


def _pad_cols(a, cols):
    return jnp.concatenate([a, jnp.zeros(a.shape[:-1] + (cols - a.shape[-1],), a.dtype)], axis=-1)


def kernel(x, c, ctx, c_ctx, w_mod, b_mod, post_ln_g, post_ln_b, w_in_e, conv_a_w, conv_a_b, norm_a_g, norm_a_b, conv_b_w, w_out_e, w_in_o, q_norm_g, k_norm_g, w_out_o, loss_target, m_c_ctx, m_w_mod, m_b_mod, m_post_ln_g, m_post_ln_b, m_w_in_e, m_conv_a_w, m_conv_a_b, m_norm_a_g, m_norm_a_b, m_conv_b_w, m_w_out_e, m_w_in_o, m_q_norm_g, m_k_norm_g, m_w_out_o, v_c_ctx, v_w_mod, v_b_mod, v_post_ln_g, v_post_ln_b, v_w_in_e, v_conv_a_w, v_conv_a_b, v_norm_a_g, v_norm_a_b, v_conv_b_w, v_w_out_e, v_w_in_o, v_q_norm_g, v_k_norm_g, v_w_out_o):
    s, d = x.shape[1], x.shape[2]
    lc = ctx.shape[1]
    t = lc + s
    tm = lc
    depth = w_mod.shape[0]
    n_even, n_odd = w_in_e.shape[0], w_in_o.shape[0]
    ad = w_out_o.shape[1] * 4
    kvd = (w_in_o.shape[2] * 4 - 2 * ad) // 2
    nkv = kvd // HEAD_DIM
    hpg = ad // nkv // HEAD_DIM
    nlm = w_mod.shape[2]
    alpha = (2.0 * depth) ** 0.25
    scale = HEAD_DIM ** -0.5
    assert n_even == 2 and n_odd == 2 and depth == 4 and hpg == GQA_GROUP
    assert lc % CONV_ROWS == 0 and s % (tm * KV_CHUNKS_PER_STEP) == 0 and d % LANES == 0

    xi, yi, ci = _place()
    shard = 2 * xi + yi
    dev = 4 * xi + 2 * yi + ci

    wg_in_e = gather_weight(w_in_e.astype(BF16), "gather_w_in_e")
    wg_out_e = gather_weight(w_out_e.astype(BF16), "gather_w_out_e")
    wg_in_o = gather_weight(w_in_o.astype(BF16), "gather_w_in_o")
    wg_out_o = gather_weight(w_out_o.astype(BF16), "gather_w_out_o")

    c_all = all_gather8(_pad_rows(c, 8), "gather_c")
    c16 = _pad_rows(jnp.concatenate([c_all[0::8], c_ctx[None, :]], axis=0), 16)
    m_part = adaln_fwd(c16, w_mod)
    m_all = all_gather8(m_part.reshape(depth * 16, nlm), "gather_mod")
    m_all = m_all.reshape(8, depth, 16, nlm)[0::2]
    m_full = m_all.transpose(1, 2, 0, 3).reshape(depth, 16, 4 * nlm) + b_mod[:, None, :]
    m_lat = lax.dynamic_index_in_dim(m_full, dev, axis=1, keepdims=False)
    m_ctx = m_full[:, 8]

    def seg2(l, part):
        return jnp.stack([m_ctx[l, part * d:(part + 1) * d], m_lat[l, part * d:(part + 1) * d]])[:, None, :]

    cos_t, sin_t = _rope_tables(lc, s)

    small_gathered = all_gather8(
        _pad_rows(jnp.concatenate([conv_a_w.reshape(n_even * CONV_A_TAPS, -1),
                                   conv_b_w.reshape(n_even * CONV_B_TAPS, -1)], axis=0), 72), "gather_taps")
    taps = small_gathered.reshape(8, 72, -1)[0::2]
    taps = taps.transpose(1, 0, 2).reshape(72, d)
    caw = taps[:n_even * CONV_A_TAPS].reshape(n_even, CONV_A_TAPS, d)
    cbw = taps[n_even * CONV_A_TAPS:n_even * (CONV_A_TAPS + CONV_B_TAPS)].reshape(n_even, CONV_B_TAPS, d)
    caw_pad = jnp.concatenate([caw, jnp.zeros((n_even, 32 - CONV_A_TAPS, d), F32)], axis=1)
    cbw_pad = jnp.concatenate([cbw, jnp.zeros((n_even, 8 - CONV_B_TAPS, d), F32)], axis=1)

    xc = jnp.concatenate([ctx[0], x[0]], axis=0)
    saved = []
    for l in range(depth):
        i = l // 2
        shift2, scale2, gate2 = seg2(l, 0), seg2(l, 1), seg2(l, 2)
        h = mod_fwd(xc, scale2, shift2, tm)
        if l % 2 == 0:
            p = mm_nn(h, wg_in_e[i], BF16)
            u1 = conv_a_fwd(p, caw_pad[i], conv_a_b[i][None, :], lc, d)
            a_out = ln_a_fwd(u1, p, norm_a_g[i][None, :], norm_a_b[i][None, :], tm)
            b_out = conv_b_fwd(p, cbw_pad[i], lc, d)
            ab = jnp.concatenate([a_out, b_out], axis=1)
            y = mm_nn(ab, wg_out_e[i].reshape(1, 2 * d, d), F32)
            saved.append(dict(xc=xc, h=h, p=p, u1=u1, ab=ab, y=y))
        else:
            qgkv = mm_nn(h, wg_in_o[i], BF16)
            qr, kr = qk_fwd(qgkv, q_norm_g[i][None, :], k_norm_g[i][None, :], cos_t, sin_t, ad, kvd, tm,
                            scale * LOG2_E)
            vt = _chunks_t(qgkv[:, 2 * ad + kvd:], nkv, tm)
            o, og, lse = flash_fwd(qr, kr, vt, qgkv, ad, kvd, tm)
            y = mm_nn(og, wg_out_o[i].reshape(1, ad, d), F32)
            saved.append(dict(xc=xc, h=h, qgkv=qgkv, qr=qr, kr=kr, o=o, og=og, lse=lse, y=y))
        xc = post_ln_fwd(xc, y, gate2, post_ln_g[l][None, :], post_ln_b[l][None, :], alpha, tm)

    dxc, loss_acc = loss_head(xc, loss_target[0], lc, tm)

    g_in_e, g_out_e, g_in_o, g_out_o = [None] * 2, [None] * 2, [None] * 2, [None] * 2
    d_mod_lat, d_mod_ctx = [None] * depth, [None] * depth
    d_pln_g, d_pln_b = [None] * depth, [None] * depth
    d_cab, d_nag, d_nab, d_caw, d_cbw = [None] * 2, [None] * 2, [None] * 2, [None] * 2, [None] * 2
    d_qg, d_kg = [None] * 2, [None] * 2
    for l in reversed(range(depth)):
        i = l // 2
        sv = saved[l]
        scale2, gate2 = seg2(l, 1), seg2(l, 2)
        dzx, dy, acc_ln = post_ln_bwd(dxc, sv["xc"], sv["y"], gate2, post_ln_g[l][None, :], alpha, tm)
        d_pln_g[l] = acc_ln[0, 1] + acc_ln[1, 1]
        d_pln_b[l] = acc_ln[0, 2] + acc_ln[1, 2]
        if l % 2 == 0:
            w_out3 = wg_out_e[i].reshape(1, 2 * d, d)
            dab = mm_nt(dy, w_out3, BF16)
            g_out_e[i] = tuple(g.reshape(4, 2 * d // 4, d) for g in mm_tn(sv["ab"], dy, 1))
            du1, d_agate, acc_a = ln_a_bwd(dab, sv["u1"], sv["p"], norm_a_g[i][None, :], norm_a_b[i][None, :], tm)
            d_nag[i], d_nab[i], d_cab[i] = acc_a[0], acc_a[1], acc_a[2]
            d_aval, d_aglu, d_caw[i] = conv_a_bwd(du1, sv["p"], caw_pad[i], lc, d)
            d_bx, d_bb, d_bc, d_bg, d_cbw[i] = conv_b_bwd(dab, sv["p"], cbw_pad[i], lc, d)
            dp = jnp.concatenate([d_aval, d_aglu, d_agate, d_bx, d_bb, d_bc, d_bg], axis=1)
            dh = mm_nt(dp, wg_in_e[i], F32)
            g_in_e[i] = tuple(mm_tn(sv["h"], dp, 4))
        else:
            w_out3 = wg_out_o[i].reshape(1, ad, d)
            dog = mm_nt(dy, w_out3, BF16)
            g_out_o[i] = tuple(g.reshape(4, ad // 4, d) for g in mm_tn(sv["og"], dy, 1))
            do, dgate, delta = gate_bwd(dog, sv["o"], sv["qgkv"], ad, kvd, tm)
            dqr, dkr, dv = flash_bwd(sv["qr"], do, sv["kr"], _chunks_t(sv["kr"], nkv, tm), sv["qgkv"], sv["lse"],
                                     _rows_per_head(delta, tm, hpg), ad, kvd, tm, scale)
            dq, dk, acc_qk = qk_bwd(dqr, dkr, sv["qgkv"], q_norm_g[i][None, :], k_norm_g[i][None, :],
                                    cos_t, sin_t, ad, kvd, tm)
            d_qg[i], d_kg[i] = acc_qk[0], acc_qk[1]
            dqgkv = jnp.concatenate([dq, dgate, dk, dv.astype(BF16)], axis=1)
            dh = mm_nt(dqgkv, wg_in_o[i], F32)
            g_in_o[i] = tuple(mm_tn(sv["h"], dqgkv, 4))
        dxc, acc_mod = mod_bwd(dh, dzx, sv["xc"], scale2, tm)
        d_mod_ctx[l] = jnp.stack([acc_mod[0, 0], acc_mod[0, 1], acc_ln[0, 0]])
        d_mod_lat[l] = jnp.stack([acc_mod[1, 0], acc_mod[1, 1], acc_ln[1, 0]])
    grad_x = dxc[lc:][None]

    parts = [jnp.concatenate(d_mod_ctx, axis=0), jnp.stack(d_pln_g), jnp.stack(d_pln_b), jnp.stack(d_cab),
             jnp.stack(d_nag), jnp.stack(d_nab), jnp.concatenate(d_caw, axis=0), jnp.concatenate(d_cbw, axis=0),
             _pad_cols(jnp.stack(d_qg), d), _pad_cols(jnp.stack(d_kg), d), jnp.concatenate(d_mod_lat, axis=0),
             _pad_cols(loss_acc, d)]
    parts = [_pad_rows(p, -(-p.shape[0] // SUBLANES) * SUBLANES) for p in parts]
    offs = [0]
    for p in parts:
        offs.append(offs[-1] + p.shape[0])
    pack = jnp.concatenate(parts, axis=0)
    npack = offs[-1]
    gathered = all_gather8(pack, "gather_small").reshape(8, npack, d)
    small = sum_leading(gathered, "sum_small")

    def piece(k, rows):
        return small[offs[k]:offs[k] + rows]

    loss = 0.5 / d * jnp.sum(piece(11, 1)[0, :LANES])
    dm_ctx = piece(0, 3 * depth).reshape(depth, 1, 3 * d)
    dm_lat = gathered[:, offs[10]:offs[10] + 3 * depth].reshape(8, depth, 3 * d).transpose(1, 0, 2)
    dm = jnp.concatenate([dm_lat, dm_ctx, jnp.zeros((depth, 7, 3 * d), F32)], axis=1)
    g_b_mod = sum_leading(dm.transpose(1, 0, 2), "sum_b_mod")
    dm_shard = lax.dynamic_slice_in_dim(dm, shard * nlm, nlm, axis=2)
    g_w_mod, dc_part = adaln_bwd(c16.T, dm_shard, w_mod)
    dc_all = all_gather8(_pad_rows(dc_part[:, 8, :], 8), "gather_dc").reshape(8, 8, d)
    g_c_ctx = c_ctx_grad(dc_all[0::2, :depth].reshape(4 * depth, 1, d), c_ctx[None, :])[0]

    g_pln_g, g_pln_b = piece(1, depth), piece(2, depth)
    g_cab, g_nag, g_nab = piece(3, n_even), piece(4, n_even), piece(5, n_even)
    dch = d // 4
    g_caw = lax.dynamic_slice_in_dim(piece(6, 64).reshape(2, 32, d)[:, :CONV_A_TAPS], shard * dch, dch, axis=2)
    g_cbw = lax.dynamic_slice_in_dim(piece(7, 16).reshape(2, 8, d)[:, :CONV_B_TAPS], shard * dch, dch, axis=2)
    g_qg, g_kg = piece(8, n_odd)[:, :HEAD_DIM], piece(9, n_odd)[:, :HEAD_DIM]

    g_w_in_e = reduce_scatter_grad(g_in_e[0], g_in_e[1], shard, ci, "in_e")
    g_w_out_e = reduce_scatter_grad(g_out_e[0], g_out_e[1], shard, ci, "out_e")
    g_w_in_o = reduce_scatter_grad(g_in_o[0], g_in_o[1], shard, ci, "in_o")
    g_w_out_o = reduce_scatter_grad(g_out_o[0], g_out_o[1], shard, ci, "out_o")

    grads = [g_c_ctx, g_w_mod, g_b_mod, g_pln_g, g_pln_b, g_w_in_e, g_caw, g_cab, g_nag, g_nab, g_cbw,
             g_w_out_e, g_w_in_o, g_qg, g_kg, g_w_out_o]
    weights = [c_ctx, w_mod, b_mod, post_ln_g, post_ln_b, w_in_e, conv_a_w, conv_a_b, norm_a_g, norm_a_b,
               conv_b_w, w_out_e, w_in_o, q_norm_g, k_norm_g, w_out_o]
    ms = [m_c_ctx, m_w_mod, m_b_mod, m_post_ln_g, m_post_ln_b, m_w_in_e, m_conv_a_w, m_conv_a_b, m_norm_a_g,
          m_norm_a_b, m_conv_b_w, m_w_out_e, m_w_in_o, m_q_norm_g, m_k_norm_g, m_w_out_o]
    vs = [v_c_ctx, v_w_mod, v_b_mod, v_post_ln_g, v_post_ln_b, v_w_in_e, v_conv_a_w, v_conv_a_b, v_norm_a_g,
          v_norm_a_b, v_conv_b_w, v_w_out_e, v_w_in_o, v_q_norm_g, v_k_norm_g, v_w_out_o]
    deltas, new_ms, new_vs = [], [], []
    for wv, gv, mv, vv in zip(weights, grads, ms, vs):
        dl, nm, nv = adamw(wv, gv, mv, vv)
        deltas.append(dl)
        new_ms.append(nm)
        new_vs.append(nv)
    return (loss, grad_x, *grads, *deltas, *new_ms, *new_vs)
```

```python
import functools

import jax
import jax.numpy as jnp
from jax import lax
from jax.experimental import pallas as pl
from jax.experimental.pallas import tpu as pltpu

F32 = jnp.float32
BF16 = jnp.bfloat16

LANES = 128
SUBLANES = 8
HEAD_DIM = 128
GQA_GROUP = 4
GRID_W = 64
ROPE_THETA = 10000.0
LN_EPS = 1e-5
RMS_EPS = 1e-6
CONV_A_TAPS = 31
CONV_B_TAPS = 3
HALO = 16
CONV_ROWS = 128
ADAM_LR = 0.001
ADAM_B1 = 0.9
ADAM_B2 = 0.999
ADAM_EPS = 1e-08
ADAM_WD = 0.01
ADAM_STEP = 10
VMEM_LIMIT = 56 * 1024 * 1024
D2D_PARTS = 4
KV_CHUNKS_PER_STEP = 4
LOG2_E = 1.4426950408889634
LN_2 = 0.6931471805599453
MESH = pl.DeviceIdType.MESH
ANY = pl.BlockSpec(memory_space=pl.ANY)
VMEM_FULL = pl.BlockSpec(memory_space=pltpu.VMEM)


def _sds(shape, dtype):
    return jax.ShapeDtypeStruct(tuple(shape), dtype)


def _cp(*sem):
    return pltpu.CompilerParams(dimension_semantics=sem, vmem_limit_bytes=VMEM_LIMIT)


def _pick(n, cands):
    for c in cands:
        if n % c == 0:
            return c
    return n


def _sigmoid(x):
    return 1.0 / (1.0 + jnp.exp(-x))


def _silu(x):
    return x * _sigmoid(x)


def _dsilu(x):
    s = _sigmoid(x)
    return s * (1.0 + x * (1.0 - s))


def _row(tm, d):
    return pl.BlockSpec((tm, d), lambda j: (j, 0))


def _seg(d):
    return pl.BlockSpec((None, 1, d), lambda j: (jnp.minimum(j, 1), 0, 0))


def _vec(d):
    return pl.BlockSpec((1, d), lambda j: (0, 0))


def _colblk(tm, width, blk):
    return pl.BlockSpec((tm, width), lambda j: (j, blk))


def _seg_acc(d):
    return pl.BlockSpec((None, SUBLANES, d), lambda j: (jnp.minimum(j, 1), 0, 0))


def _ln_stats(z):
    mu = jnp.mean(z, axis=-1, keepdims=True)
    zc = z - mu
    var = jnp.mean(zc * zc, axis=-1, keepdims=True)
    rstd = lax.rsqrt(var + LN_EPS)
    return zc * rstd, rstd


def _ln_bwd(dxh, xhat, rstd):
    m1 = jnp.mean(dxh, axis=-1, keepdims=True)
    m2 = jnp.mean(dxh * xhat, axis=-1, keepdims=True)
    return rstd * (dxh - m1 - xhat * m2)


def _colsum(v):
    return jnp.sum(v, axis=0, keepdims=True)


def mod_fwd(xc, scale2, shift2, tm):
    t, d = xc.shape

    def body(x_ref, sc_ref, sh_ref, h_ref):
        h_ref[...] = (x_ref[...] * (1.0 + sc_ref[...]) + sh_ref[...]).astype(h_ref.dtype)

    return pl.pallas_call(
        body, name="mod_fwd", grid=(t // tm,),
        in_specs=[_row(tm, d), _seg(d), _seg(d)], out_specs=_row(tm, d),
        out_shape=_sds((t, d), BF16), compiler_params=_cp("parallel"))(xc, scale2, shift2)


def post_ln_fwd(xc, y, gate2, g, b, alpha, tm):
    t, d = xc.shape

    def body(x_ref, y_ref, gt_ref, g_ref, b_ref, o_ref):
        z = alpha * x_ref[...] + gt_ref[...] * y_ref[...]
        xhat, _ = _ln_stats(z)
        o_ref[...] = xhat * g_ref[...] + b_ref[...]

    return pl.pallas_call(
        body, name="post_ln_fwd", grid=(t // tm,),
        in_specs=[_row(tm, d), _row(tm, d), _seg(d), _vec(d), _vec(d)], out_specs=_row(tm, d),
        out_shape=_sds((t, d), F32), compiler_params=_cp("parallel"))(xc, y, gate2, g, b)


def post_ln_bwd(dout, xc, y, gate2, g, alpha, tm):
    t, d = xc.shape

    def body(do_ref, x_ref, y_ref, gt_ref, g_ref, dzx_ref, dy_ref, acc_ref):
        @pl.when(pl.program_id(0) <= 1)
        def _():
            acc_ref[...] = jnp.zeros_like(acc_ref)

        yv = y_ref[...]
        gate = gt_ref[...]
        xhat, rstd = _ln_stats(alpha * x_ref[...] + gate * yv)
        dout = do_ref[...]
        dz = _ln_bwd(dout * g_ref[...], xhat, rstd)
        dzx_ref[...] = alpha * dz
        dy_ref[...] = (gate * dz).astype(dy_ref.dtype)
        acc_ref[0:1, :] += _colsum(dz * yv)
        acc_ref[1:2, :] += _colsum(dout * xhat)
        acc_ref[2:3, :] += _colsum(dout)

    return pl.pallas_call(
        body, name="post_ln_bwd", grid=(t // tm,),
        in_specs=[_row(tm, d), _row(tm, d), _row(tm, d), _seg(d), _vec(d)],
        out_specs=[_row(tm, d), _row(tm, d), _seg_acc(d)],
        out_shape=[_sds((t, d), F32), _sds((t, d), BF16), _sds((2, SUBLANES, d), F32)],
        compiler_params=_cp("arbitrary"))(dout, xc, y, gate2, g)


def mod_bwd(dh, dzx, xc, scale2, tm):
    t, d = xc.shape

    def body(dh_ref, dzx_ref, x_ref, sc_ref, dx_ref, acc_ref):
        @pl.when(pl.program_id(0) <= 1)
        def _():
            acc_ref[...] = jnp.zeros_like(acc_ref)

        dhv = dh_ref[...].astype(F32)
        dx_ref[...] = dzx_ref[...] + dhv * (1.0 + sc_ref[...])
        acc_ref[0:1, :] += _colsum(dhv)
        acc_ref[1:2, :] += _colsum(dhv * x_ref[...])

    return pl.pallas_call(
        body, name="mod_bwd", grid=(t // tm,),
        in_specs=[_row(tm, d), _row(tm, d), _row(tm, d), _seg(d)],
        out_specs=[_row(tm, d), _seg_acc(d)],
        out_shape=[_sds((t, d), F32), _sds((2, SUBLANES, d), F32)],
        compiler_params=_cp("arbitrary"))(dh, dzx, xc, scale2)


def loss_head(xc, target, lc, tm):
    t, d = xc.shape

    def body(x_ref, t_ref, dx_ref, acc_ref):
        j = pl.program_id(0)

        @pl.when(j == 0)
        def _():
            acc_ref[...] = jnp.zeros_like(acc_ref)
            dx_ref[...] = jnp.zeros_like(dx_ref)

        @pl.when(j > 0)
        def _():
            err = x_ref[...] - t_ref[...]
            dx_ref[...] = err * (1.0 / d)
            col = _colsum(err * err)
            tot = col[:, 0:LANES]
            for k in range(1, d // LANES):
                tot = tot + col[:, k * LANES:(k + 1) * LANES]
            acc_ref[0:1, :] += tot

    nlc = lc // tm
    return pl.pallas_call(
        body, name="loss_head", grid=(t // tm,),
        in_specs=[_row(tm, d), pl.BlockSpec((tm, d), lambda j: (jnp.maximum(j - nlc, 0), 0))],
        out_specs=[_row(tm, d), pl.BlockSpec((SUBLANES, LANES), lambda j: (0, 0))],
        out_shape=[_sds((t, d), F32), _sds((SUBLANES, LANES), F32)],
        compiler_params=_cp("arbitrary"))(xc, target)


def mm_nn(a, w3, out_dtype):
    m, k = a.shape
    ns, _, nl = w3.shape
    tm = _pick(m, (768, 512, 256, 128))
    tn = _pick(nl, (512, 256, 128))
    npj = nl // tn

    def body(a_ref, w_ref, o_ref):
        o_ref[...] = jnp.dot(a_ref[...], w_ref[...], preferred_element_type=F32).astype(o_ref.dtype)

    return pl.pallas_call(
        body, name="mm_nn", grid=(m // tm, ns * npj),
        in_specs=[pl.BlockSpec((tm, k), lambda i, j: (i, 0)),
                  pl.BlockSpec((None, k, tn), lambda i, j: (j // npj, 0, j % npj))],
        out_specs=pl.BlockSpec((tm, tn), lambda i, j: (i, j)),
        out_shape=_sds((m, ns * nl), out_dtype), compiler_params=_cp("parallel", "parallel"))(a, w3)


def mm_nt(a, w3, out_dtype):
    m, _ = a.shape
    ns, k, nl = w3.shape
    tm = _pick(m, (768, 512, 256, 128))
    tk = _pick(k, (2048, 1024, 512, 256, 128))
    tn = _pick(nl, (1792, 1280, 1024, 512, 256, 128))
    npj = nl // tn
    nsteps = ns * npj

    def body(a_ref, w_ref, o_ref, acc_ref):
        n = pl.program_id(2)

        @pl.when(n == 0)
        def _():
            acc_ref[...] = jnp.zeros_like(acc_ref)

        acc_ref[...] += lax.dot_general(a_ref[...], w_ref[...], (((1,), (1,)), ((), ())),
                                        preferred_element_type=F32)

        @pl.when(n == nsteps - 1)
        def _():
            o_ref[...] = acc_ref[...].astype(o_ref.dtype)

    return pl.pallas_call(
        body, name="mm_nt", grid=(m // tm, k // tk, nsteps),
        in_specs=[pl.BlockSpec((tm, tn), lambda i, kk, n: (i, n)),
                  pl.BlockSpec((None, tk, tn), lambda i, kk, n: (n // npj, kk, n % npj))],
        out_specs=pl.BlockSpec((tm, tk), lambda i, kk, n: (i, kk)),
        out_shape=_sds((m, k), out_dtype), scratch_shapes=[pltpu.VMEM((tm, tk), F32)],
        compiler_params=_cp("parallel", "parallel", "arbitrary"))(a, w3)


def mm_tn(a, b, ns):
    m, k = a.shape
    nl = b.shape[1] // ns
    tm = _pick(m, (1408, 768, 512, 256, 128))
    tk = _pick(k, (1024, 512, 256, 128))
    tn = _pick(nl, (1792, 1280, 1024, 512, 256, 128))
    npj = nl // tn
    nsteps = m // tm

    def body(a_ref, b_ref, o_ref, ob_ref):
        r = pl.program_id(2)

        @pl.when(r == 0)
        def _():
            o_ref[...] = jnp.zeros_like(o_ref)

        o_ref[...] += lax.dot_general(a_ref[...], b_ref[...], (((0,), (0,)), ((), ())),
                                      preferred_element_type=F32)

        @pl.when(r == nsteps - 1)
        def _():
            ob_ref[...] = o_ref[...].astype(ob_ref.dtype)

    out_spec = pl.BlockSpec((None, tk, tn), lambda i, j, r: (j // npj, i, j % npj))
    return pl.pallas_call(
        body, name="mm_tn", grid=(k // tk, ns * npj, nsteps),
        in_specs=[pl.BlockSpec((tm, tk), lambda i, j, r: (r, i)),
                  pl.BlockSpec((tm, tn), lambda i, j, r: (r, j))],
        out_specs=[out_spec, out_spec],
        out_shape=[_sds((ns, k, nl), F32), _sds((ns, k, nl), BF16)],
        compiler_params=_cp("parallel", "parallel", "arbitrary"))(a, b)


def _win_start(j, ncc):
    return pl.multiple_of(j * CONV_ROWS + jnp.where(j >= ncc, HALO, 0), SUBLANES)


def _tok_start(j):
    return pl.multiple_of(j * CONV_ROWS, CONV_ROWS)


def _shifted(xw, off):
    n = xw.shape[0]
    sh = (n - off) % n
    y = pltpu.roll(xw, sh, 0) if sh else xw
    return y[:CONV_ROWS]


def _conv_fwd(xw, w_ref, ntaps):
    pad = ntaps // 2
    acc = None
    for k in range(ntaps):
        term = w_ref[k:k + 1, :] * _shifted(xw, HALO + k - pad)
        acc = term if acc is None else acc + term
    return acc


def _conv_bwd_data(xw, w_ref, ntaps):
    pad = ntaps // 2
    acc = None
    for k in range(ntaps):
        term = w_ref[k:k + 1, :] * _shifted(xw, HALO - k + pad)
        acc = term if acc is None else acc + term
    return acc


def _conv_bwd_weight(dw_ref, d, xw, ntaps):
    pad = ntaps // 2
    for k in range(ntaps):
        dw_ref[k:k + 1, :] += _colsum(d * _shifted(xw, HALO + k - pad))


def _zero_halos(pad_ref, lc, t):
    z = jnp.zeros((HALO, LANES), F32)
    pad_ref[0:HALO, :] = z
    pad_ref[HALO + lc:2 * HALO + lc, :] = z
    pad_ref[2 * HALO + t:3 * HALO + t, :] = z


def _pad_dst(j, ncc):
    return pl.multiple_of(j * CONV_ROWS + HALO + jnp.where(j >= ncc, HALO, 0), SUBLANES)


def _chan(t, blk0):
    return pl.BlockSpec((t, LANES), lambda ct: (0, blk0 + ct))


def _tapw(rows):
    return pl.BlockSpec((rows, LANES), lambda ct: (0, ct))


def conv_a_fwd(p, w_pad, bias, lc, d):
    t = p.shape[0]
    nct, nch, ncc = d // LANES, t // CONV_ROWS, lc // CONV_ROWS

    def body(av_ref, ag_ref, w_ref, b_ref, u1_ref, pad_ref):
        _zero_halos(pad_ref, lc, t)

        def fill(j, carry):
            rows = pl.ds(_tok_start(j), CONV_ROWS)
            u0 = av_ref[rows, :].astype(F32) * _sigmoid(ag_ref[rows, :].astype(F32))
            pad_ref[pl.ds(_pad_dst(j, ncc), CONV_ROWS), :] = u0
            return carry

        lax.fori_loop(0, nch, fill, 0)

        def conv(j, carry):
            xw = pad_ref[pl.ds(_win_start(j, ncc), CONV_ROWS + 2 * HALO), :]
            u1_ref[pl.ds(_tok_start(j), CONV_ROWS), :] = _conv_fwd(xw, w_ref, CONV_A_TAPS) + b_ref[...]
            return carry

        lax.fori_loop(0, nch, conv, 0)

    return pl.pallas_call(
        body, name="conv_a_fwd", grid=(nct,),
        in_specs=[_chan(t, 0), _chan(t, nct), _tapw(32), _tapw(1)],
        out_specs=_chan(t, 0), out_shape=_sds((t, d), F32),
        scratch_shapes=[pltpu.VMEM((t + 3 * HALO, LANES), F32)],
        compiler_params=_cp("parallel"))(p, p, w_pad, bias)


def conv_b_fwd(p, w_pad, lc, d):
    t = p.shape[0]
    nct, nch, ncc = d // LANES, t // CONV_ROWS, lc // CONV_ROWS

    def body(bx_ref, bb_ref, bc_ref, bg_ref, w_ref, o_ref, pad_ref):
        _zero_halos(pad_ref, lc, t)

        def fill(j, carry):
            rows = pl.ds(_tok_start(j), CONV_ROWS)
            pad_ref[pl.ds(_pad_dst(j, ncc), CONV_ROWS), :] = (
                bc_ref[rows, :].astype(F32) * bx_ref[rows, :].astype(F32))
            return carry

        lax.fori_loop(0, nch, fill, 0)

        def conv(j, carry):
            rows = pl.ds(_tok_start(j), CONV_ROWS)
            xw = pad_ref[pl.ds(_win_start(j, ncc), CONV_ROWS + 2 * HALO), :]
            v = _conv_fwd(xw, w_ref, CONV_B_TAPS)
            o_ref[rows, :] = (bb_ref[rows, :].astype(F32) * v
                              * _silu(bg_ref[rows, :].astype(F32))).astype(o_ref.dtype)
            return carry

        lax.fori_loop(0, nch, conv, 0)

    return pl.pallas_call(
        body, name="conv_b_fwd", grid=(nct,),
        in_specs=[_chan(t, 3 * nct), _chan(t, 4 * nct), _chan(t, 5 * nct), _chan(t, 6 * nct), _tapw(8)],
        out_specs=_chan(t, 0), out_shape=_sds((t, d), BF16),
        scratch_shapes=[pltpu.VMEM((t + 3 * HALO, LANES), F32)],
        compiler_params=_cp("parallel"))(p, p, p, p, w_pad)


def ln_a_fwd(u1, p, g, b, tm):
    t, d = u1.shape

    def body(u_ref, ag_ref, g_ref, b_ref, o_ref):
        xhat, _ = _ln_stats(u_ref[...])
        u2 = xhat * g_ref[...] + b_ref[...]
        o_ref[...] = (_silu(u2) * _silu(ag_ref[...].astype(F32))).astype(o_ref.dtype)

    return pl.pallas_call(
        body, name="ln_a_fwd", grid=(t // tm,),
        in_specs=[_row(tm, d), _colblk(tm, d, 2), _vec(d), _vec(d)], out_specs=_row(tm, d),
        out_shape=_sds((t, d), BF16), compiler_params=_cp("parallel"))(u1, p, g, b)


def ln_a_bwd(dab, u1, p, g, b, tm):
    t, d = u1.shape

    def body(da_ref, u_ref, ag_ref, g_ref, b_ref, du_ref, dag_ref, acc_ref):
        @pl.when(pl.program_id(0) == 0)
        def _():
            acc_ref[...] = jnp.zeros_like(acc_ref)

        xhat, rstd = _ln_stats(u_ref[...])
        u2 = xhat * g_ref[...] + b_ref[...]
        ag = ag_ref[...].astype(F32)
        da = da_ref[...].astype(F32)
        dag_ref[...] = (da * _silu(u2) * _dsilu(ag)).astype(dag_ref.dtype)
        du2 = da * _silu(ag) * _dsilu(u2)
        du1 = _ln_bwd(du2 * g_ref[...], xhat, rstd)
        du_ref[...] = du1
        acc_ref[0:1, :] += _colsum(du2 * xhat)
        acc_ref[1:2, :] += _colsum(du2)
        acc_ref[2:3, :] += _colsum(du1)

    return pl.pallas_call(
        body, name="ln_a_bwd", grid=(t // tm,),
        in_specs=[_colblk(tm, d, 0), _row(tm, d), _colblk(tm, d, 2), _vec(d), _vec(d)],
        out_specs=[_row(tm, d), _row(tm, d), pl.BlockSpec((SUBLANES, d), lambda j: (0, 0))],
        out_shape=[_sds((t, d), F32), _sds((t, d), BF16), _sds((SUBLANES, d), F32)],
        compiler_params=_cp("arbitrary"))(dab, u1, p, g, b)


def conv_a_bwd(du1, p, w_pad, lc, d):
    t = p.shape[0]
    nct, nch, ncc = d // LANES, t // CONV_ROWS, lc // CONV_ROWS

    def body(du_ref, av_ref, ag_ref, w_ref, dav_ref, dag_ref, dw_ref, pad_u, pad_d):
        _zero_halos(pad_u, lc, t)
        _zero_halos(pad_d, lc, t)
        dw_ref[...] = jnp.zeros_like(dw_ref)

        def fill(j, carry):
            rows = pl.ds(_tok_start(j), CONV_ROWS)
            dst = pl.ds(_pad_dst(j, ncc), CONV_ROWS)
            pad_u[dst, :] = av_ref[rows, :].astype(F32) * _sigmoid(ag_ref[rows, :].astype(F32))
            pad_d[dst, :] = du_ref[rows, :]
            return carry

        lax.fori_loop(0, nch, fill, 0)

        def step(j, carry):
            rows = pl.ds(_tok_start(j), CONV_ROWS)
            win = pl.ds(_win_start(j, ncc), CONV_ROWS + 2 * HALO)
            du0 = _conv_bwd_data(pad_d[win, :], w_ref, CONV_A_TAPS)
            sig = _sigmoid(ag_ref[rows, :].astype(F32))
            dav_ref[rows, :] = (du0 * sig).astype(dav_ref.dtype)
            dag_ref[rows, :] = (du0 * av_ref[rows, :].astype(F32) * sig * (1.0 - sig)).astype(dag_ref.dtype)
            _conv_bwd_weight(dw_ref, du_ref[rows, :], pad_u[win, :], CONV_A_TAPS)
            return carry

        lax.fori_loop(0, nch, step, 0)

    return pl.pallas_call(
        body, name="conv_a_bwd", grid=(nct,),
        in_specs=[_chan(t, 0), _chan(t, 0), _chan(t, nct), _tapw(32)],
        out_specs=[_chan(t, 0), _chan(t, 0), _tapw(32)],
        out_shape=[_sds((t, d), BF16), _sds((t, d), BF16), _sds((32, d), F32)],
        scratch_shapes=[pltpu.VMEM((t + 3 * HALO, LANES), F32), pltpu.VMEM((t + 3 * HALO, LANES), F32)],
        compiler_params=_cp("parallel"))(du1, p, p, w_pad)


def conv_b_bwd(dab, p, w_pad, lc, d):
    t = p.shape[0]
    nct, nch, ncc = d // LANES, t // CONV_ROWS, lc // CONV_ROWS

    def body(db_ref, bx_ref, bb_ref, bc_ref, bg_ref, w_ref,
             dbx_ref, dbb_ref, dbc_ref, dbg_ref, dw_ref, pad_t, pad_d):
        _zero_halos(pad_t, lc, t)
        _zero_halos(pad_d, lc, t)
        dw_ref[...] = jnp.zeros_like(dw_ref)

        def fill(j, carry):
            rows = pl.ds(_tok_start(j), CONV_ROWS)
            pad_t[pl.ds(_pad_dst(j, ncc), CONV_ROWS), :] = (
                bc_ref[rows, :].astype(F32) * bx_ref[rows, :].astype(F32))
            return carry

        lax.fori_loop(0, nch, fill, 0)

        def first(j, carry):
            rows = pl.ds(_tok_start(j), CONV_ROWS)
            xw = pad_t[pl.ds(_win_start(j, ncc), CONV_ROWS + 2 * HALO), :]
            v = _conv_fwd(xw, w_ref, CONV_B_TAPS)
            bg = bg_ref[rows, :].astype(F32)
            bb = bb_ref[rows, :].astype(F32)
            db = db_ref[rows, :].astype(F32)
            sg = _silu(bg)
            dbb_ref[rows, :] = (db * v * sg).astype(dbb_ref.dtype)
            dbg_ref[rows, :] = (db * bb * v * _dsilu(bg)).astype(dbg_ref.dtype)
            dv = db * bb * sg
            pad_d[pl.ds(_pad_dst(j, ncc), CONV_ROWS), :] = dv
            _conv_bwd_weight(dw_ref, dv, xw, CONV_B_TAPS)
            return carry

        lax.fori_loop(0, nch, first, 0)

        def second(j, carry):
            rows = pl.ds(_tok_start(j), CONV_ROWS)
            dt = _conv_bwd_data(pad_d[pl.ds(_win_start(j, ncc), CONV_ROWS + 2 * HALO), :], w_ref, CONV_B_TAPS)
            dbc_ref[rows, :] = (dt * bx_ref[rows, :].astype(F32)).astype(dbc_ref.dtype)
            dbx_ref[rows, :] = (dt * bc_ref[rows, :].astype(F32)).astype(dbx_ref.dtype)
            return carry

        lax.fori_loop(0, nch, second, 0)

    return pl.pallas_call(
        body, name="conv_b_bwd", grid=(nct,),
        in_specs=[_chan(t, nct), _chan(t, 3 * nct), _chan(t, 4 * nct), _chan(t, 5 * nct), _chan(t, 6 * nct),
                  _tapw(8)],
        out_specs=[_chan(t, 0)] * 4 + [_tapw(8)],
        out_shape=[_sds((t, d), BF16)] * 4 + [_sds((8, d), F32)],
        scratch_shapes=[pltpu.VMEM((t + 3 * HALO, LANES), F32), pltpu.VMEM((t + 3 * HALO, LANES), F32)],
        compiler_params=_cp("parallel"))(dab, p, p, p, p, w_pad)


def _swap_halves(z, first_half):
    return jnp.where(first_half, pltpu.roll(z, 96, 1), pltpu.roll(z, 32, 1))


def _first_half_mask(rows):
    lane = lax.broadcasted_iota(jnp.int32, (rows, HEAD_DIM), 1)
    return (lane & 32) == 0


def qk_fwd(qgkv, qg, kg, cos_t, sin_t, ad, kvd, tm, qscale):
    t = qgkv.shape[0]

    def body(q_ref, k_ref, qg_ref, kg_ref, c_ref, s_ref, qo_ref, ko_ref):
        first = _first_half_mask(tm)
        cosv, sinv = c_ref[...], s_ref[...]

        def head(x, gain):
            inv = lax.rsqrt(jnp.mean(x * x, axis=-1, keepdims=True) + RMS_EPS)
            yv = x * inv * gain
            return yv * cosv + _swap_halves(yv, first) * sinv

        for h in range(ad // HEAD_DIM):
            sl = slice(h * HEAD_DIM, (h + 1) * HEAD_DIM)
            qo_ref[:, sl] = (head(q_ref[:, sl].astype(F32), qg_ref[...]) * qscale).astype(qo_ref.dtype)
        for h in range(kvd // HEAD_DIM):
            sl = slice(h * HEAD_DIM, (h + 1) * HEAD_DIM)
            ko_ref[:, sl] = head(k_ref[:, sl].astype(F32), kg_ref[...]).astype(ko_ref.dtype)

    return pl.pallas_call(
        body, name="qk_fwd", grid=(t // tm,),
        in_specs=[_colblk(tm, ad, 0), _colblk(tm, kvd, 2 * ad // kvd), _vec(HEAD_DIM), _vec(HEAD_DIM),
                  _row(tm, HEAD_DIM), _row(tm, HEAD_DIM)],
        out_specs=[_row(tm, ad), _row(tm, kvd)],
        out_shape=[_sds((t, ad), BF16), _sds((t, kvd), BF16)],
        compiler_params=_cp("parallel"))(qgkv, qgkv, qg, kg, cos_t, sin_t)


def qk_bwd(dqr, dkr, qgkv, qg, kg, cos_t, sin_t, ad, kvd, tm):
    t = qgkv.shape[0]

    def body(dq_ref, dk_ref, q_ref, k_ref, qg_ref, kg_ref, c_ref, s_ref, dqo_ref, dko_ref, acc_ref):
        @pl.when(pl.program_id(0) == 0)
        def _():
            acc_ref[...] = jnp.zeros_like(acc_ref)

        first = _first_half_mask(tm)
        cosv, sinv = c_ref[...], s_ref[...]

        def head(x, gain, dout):
            inv = lax.rsqrt(jnp.mean(x * x, axis=-1, keepdims=True) + RMS_EPS)
            xn = x * inv
            dy = dout * cosv + _swap_halves(dout * sinv, first)
            dxn = dy * gain
            dx = inv * (dxn - xn * jnp.mean(dxn * xn, axis=-1, keepdims=True))
            return dx, _colsum(dy * xn)

        dqg = jnp.zeros((1, HEAD_DIM), F32)
        for h in range(ad // HEAD_DIM):
            sl = slice(h * HEAD_DIM, (h + 1) * HEAD_DIM)
            dx, dg = head(q_ref[:, sl].astype(F32), qg_ref[...], dq_ref[:, sl])
            dqo_ref[:, sl] = dx.astype(dqo_ref.dtype)
            dqg = dqg + dg
        dkg = jnp.zeros((1, HEAD_DIM), F32)
        for h in range(kvd // HEAD_DIM):
            sl = slice(h * HEAD_DIM, (h + 1) * HEAD_DIM)
            dx, dg = head(k_ref[:, sl].astype(F32), kg_ref[...], dk_ref[:, sl])
            dko_ref[:, sl] = dx.astype(dko_ref.dtype)
            dkg = dkg + dg
        acc_ref[0:1, :] += dqg
        acc_ref[1:2, :] += dkg

    return pl.pallas_call(
        body, name="qk_bwd", grid=(t // tm,),
        in_specs=[_row(tm, ad), _row(tm, kvd), _colblk(tm, ad, 0), _colblk(tm, kvd, 2 * ad // kvd),
                  _vec(HEAD_DIM), _vec(HEAD_DIM), _row(tm, HEAD_DIM), _row(tm, HEAD_DIM)],
        out_specs=[_row(tm, ad), _row(tm, kvd), pl.BlockSpec((SUBLANES, HEAD_DIM), lambda j: (0, 0))],
        out_shape=[_sds((t, ad), BF16), _sds((t, kvd), BF16), _sds((SUBLANES, HEAD_DIM), F32)],
        compiler_params=_cp("arbitrary"))(dqr, dkr, qgkv, qgkv, qg, kg, cos_t, sin_t)


_NT = (((1,), (1,)), ((), ()))


def _chunks_t(a, nkv, tm):
    t = a.shape[0]
    return a.reshape(t // tm, tm, nkv, HEAD_DIM).transpose(2, 0, 3, 1)


def _tree_rows(x, op):
    slabs = [x[i:i + SUBLANES] for i in range(0, x.shape[0], SUBLANES)]
    while len(slabs) > 1:
        slabs = [op(slabs[i], slabs[i + 1]) for i in range(0, len(slabs), 2)]
    return slabs[0]


def flash_fwd(qr, kr, vt, qgkv, ad, kvd, tm):
    t = qr.shape[0]
    nkv = kvd // HEAD_DIM
    gw = ad // nkv
    hpg = gw // HEAD_DIM
    nt = t // tm
    gate_blk0 = ad // gw

    def body(q_ref, k_ref, vt_ref, g_ref, o_ref, og_ref, lse_ref):
        qi = pl.program_id(1)
        heads = [slice(h * HEAD_DIM, (h + 1) * HEAD_DIM) for h in range(hpg)]
        qs = [q_ref[:, sl] for sl in heads]

        def attend(kc, vtcs, carry):
            sts = [lax.dot_general(kc, q, _NT, preferred_element_type=F32) for q in qs]
            out = []
            for h in range(hpg):
                m, l, acc = carry[h]
                m_new = jnp.maximum(m, jnp.max(_tree_rows(sts[h], jnp.maximum), axis=0, keepdims=True))
                a = jnp.exp2(m - m_new)
                pt = jnp.exp2(sts[h] - m_new)
                l = a * l + jnp.sum(_tree_rows(pt, jnp.add), axis=0, keepdims=True)
                ptb = pt.astype(BF16)
                acc = a * acc
                for j, vtc in enumerate(vtcs):
                    acc = acc + jnp.dot(vtc, ptb[j * tm:(j + 1) * tm], preferred_element_type=F32)
                out.append((m_new, l, acc))
            return tuple(out)

        def step(c, carry):
            first = 1 + KV_CHUNKS_PER_STEP * c
            kc = k_ref[pl.ds(pl.multiple_of(first * tm, tm), KV_CHUNKS_PER_STEP * tm), :]
            return attend(kc, [vt_ref[first + j] for j in range(KV_CHUNKS_PER_STEP)], carry)

        init = tuple((jnp.full((1, tm), -1e30, F32), jnp.zeros((1, tm), F32), jnp.zeros((HEAD_DIM, tm), F32))
                     for _ in range(hpg))
        ctx_done = attend(k_ref[0:tm, :], [vt_ref[0]], init)
        res = lax.fori_loop(0, jnp.where(qi == 0, 0, (nt - 1) // KV_CHUNKS_PER_STEP), step, ctx_done)
        for h, sl in enumerate(heads):
            m, l, acc = res[h]
            o = (acc / l).T
            o_ref[:, sl] = o.astype(o_ref.dtype)
            og_ref[:, sl] = (o * _silu(g_ref[:, sl].astype(F32))).astype(og_ref.dtype)
            lse_ref[h:h + 1, :] = m + jnp.log(l) * LOG2_E

    return pl.pallas_call(
        body, name="flash_fwd", grid=(nkv, nt),
        in_specs=[pl.BlockSpec((tm, gw), lambda g, i: (i, g)),
                  pl.BlockSpec((t, HEAD_DIM), lambda g, i: (0, g)),
                  pl.BlockSpec((None, nt, HEAD_DIM, tm), lambda g, i: (g, 0, 0, 0)),
                  pl.BlockSpec((tm, gw), lambda g, i: (i, gate_blk0 + g))],
        out_specs=[pl.BlockSpec((tm, gw), lambda g, i: (i, g)),
                   pl.BlockSpec((tm, gw), lambda g, i: (i, g)),
                   pl.BlockSpec((None, None, hpg, tm), lambda g, i: (g, i, 0, 0))],
        out_shape=[_sds((t, ad), BF16), _sds((t, ad), BF16), _sds((nkv, nt, hpg, tm), F32)],
        compiler_params=_cp("parallel", "parallel"))(qr, kr, vt, qgkv)


def gate_bwd(dog, o, qgkv, ad, kvd, tm):
    t = o.shape[0]
    nkv = kvd // HEAD_DIM
    hpg = ad // nkv // HEAD_DIM

    def body(dog_ref, o_ref, g_ref, do_ref, dg_ref, dl_ref):
        lane = lax.broadcasted_iota(jnp.int32, (tm, LANES), 1)
        for grp in range(nkv):
            blk = jnp.zeros((tm, LANES), F32)
            for hh in range(hpg):
                h = grp * hpg + hh
                sl = slice(h * HEAD_DIM, (h + 1) * HEAD_DIM)
                dv = dog_ref[:, sl].astype(F32)
                ov = o_ref[:, sl].astype(F32)
                gv = g_ref[:, sl].astype(F32)
                doh = dv * _silu(gv)
                do_ref[:, sl] = doh.astype(do_ref.dtype)
                dg_ref[:, sl] = (dv * ov * _dsilu(gv)).astype(dg_ref.dtype)
                blk = jnp.where(lane == hh, jnp.sum(doh * ov, axis=-1, keepdims=True), blk)
            dl_ref[grp] = blk

    return pl.pallas_call(
        body, name="gate_bwd", grid=(t // tm,),
        in_specs=[_row(tm, ad), _row(tm, ad), _colblk(tm, ad, 1)],
        out_specs=[_row(tm, ad), _row(tm, ad), pl.BlockSpec((nkv, tm, LANES), lambda j: (0, j, 0))],
        out_shape=[_sds((t, ad), BF16), _sds((t, ad), BF16), _sds((nkv, t, LANES), F32)],
        compiler_params=_cp("parallel"))(dog, o, qgkv)


def flash_bwd(qr, do, kr, kt, qgkv, lse_t, delta_t, ad, kvd, tm, scale):
    t = qr.shape[0]
    nkv = kvd // HEAD_DIM
    gw = ad // nkv
    hpg = gw // HEAD_DIM
    nt = t // tm
    v_blk0 = (2 * ad + kvd) // HEAD_DIM

    def body(q_ref, do_ref, k_ref, v_ref, kt_ref, lse_ref, dl_ref, dq_ref, dk_ref, dv_ref):
        qi = pl.program_id(1)

        @pl.when(qi == 0)
        def _():
            dk_ref[...] = jnp.zeros_like(dk_ref)
            dv_ref[...] = jnp.zeros_like(dv_ref)

        heads = [slice(h * HEAD_DIM, (h + 1) * HEAD_DIM) for h in range(hpg)]
        qs = [q_ref[:, sl] for sl in heads]
        dos = [do_ref[:, sl] for sl in heads]
        lses = [lse_ref[h:h + 1, :] for h in range(hpg)]
        dls = [dl_ref[h:h + 1, :] for h in range(hpg)]

        def attend(rows, ktcs, dqts):
            kc = k_ref[rows, :]
            vc = v_ref[rows, :]
            dk = jnp.zeros((len(ktcs) * tm, HEAD_DIM), F32)
            dv = jnp.zeros((len(ktcs) * tm, HEAD_DIM), F32)
            out = []
            sts = [lax.dot_general(kc, q, _NT, preferred_element_type=F32) for q in qs]
            dpts = [lax.dot_general(vc, d, _NT, preferred_element_type=F32) for d in dos]
            for h in range(hpg):
                pt = jnp.exp2(sts[h] - lses[h])
                dv = dv + jnp.dot(pt.astype(BF16), dos[h], preferred_element_type=F32)
                dst = (pt * (dpts[h] - dls[h])).astype(BF16)
                dk = dk + jnp.dot(dst, qs[h], preferred_element_type=F32)
                dqt = dqts[h]
                for j, ktc in enumerate(ktcs):
                    dqt = dqt + jnp.dot(ktc, dst[j * tm:(j + 1) * tm], preferred_element_type=F32)
                out.append(dqt)
            dk_ref[rows, :] += dk
            dv_ref[rows, :] += dv
            return tuple(out)

        def step(c, dqts):
            first = 1 + KV_CHUNKS_PER_STEP * c
            rows = pl.ds(pl.multiple_of(first * tm, tm), KV_CHUNKS_PER_STEP * tm)
            return attend(rows, [kt_ref[first + j] for j in range(KV_CHUNKS_PER_STEP)], dqts)

        ctx_done = attend(pl.ds(0, tm), [kt_ref[0]], tuple(jnp.zeros((HEAD_DIM, tm), F32) for _ in range(hpg)))
        res = lax.fori_loop(0, jnp.where(qi == 0, 0, (nt - 1) // KV_CHUNKS_PER_STEP), step, ctx_done)
        for h, sl in enumerate(heads):
            dq_ref[:, sl] = res[h].T * scale

        @pl.when(qi == nt - 1)
        def _():
            dk_ref[...] = dk_ref[...] * LN_2

    return pl.pallas_call(
        body, name="flash_bwd", grid=(nkv, nt),
        in_specs=[pl.BlockSpec((tm, gw), lambda g, i: (i, g)),
                  pl.BlockSpec((tm, gw), lambda g, i: (i, g)),
                  pl.BlockSpec((t, HEAD_DIM), lambda g, i: (0, g)),
                  pl.BlockSpec((t, HEAD_DIM), lambda g, i: (0, v_blk0 + g)),
                  pl.BlockSpec((None, nt, HEAD_DIM, tm), lambda g, i: (g, 0, 0, 0)),
                  pl.BlockSpec((None, None, hpg, tm), lambda g, i: (g, i, 0, 0)),
                  pl.BlockSpec((None, None, hpg, tm), lambda g, i: (g, i, 0, 0))],
        out_specs=[pl.BlockSpec((tm, gw), lambda g, i: (i, g)),
                   pl.BlockSpec((t, HEAD_DIM), lambda g, i: (0, g)),
                   pl.BlockSpec((t, HEAD_DIM), lambda g, i: (0, g))],
        out_shape=[_sds((t, ad), F32), _sds((t, kvd), F32), _sds((t, kvd), F32)],
        compiler_params=_cp("parallel", "arbitrary"))(qr, do, kr, qgkv, kt, lse_t, delta_t)


def _rows_per_head(a, tm, hpg):
    nkv, t, _ = a.shape
    return a[:, :, :hpg].reshape(nkv, t // tm, tm, hpg).transpose(0, 1, 3, 2)


def adaln_fwd(c16, w_mod):
    nlay, d, nl = w_mod.shape
    tn = _pick(nl, (512, 256, 128))

    def body(c_ref, w_ref, o_ref):
        o_ref[...] = jnp.dot(_silu(c_ref[...]), w_ref[...], preferred_element_type=F32,
                             precision=lax.Precision.HIGHEST)

    return pl.pallas_call(
        body, name="adaln_fwd", grid=(nlay, nl // tn),
        in_specs=[pl.BlockSpec((16, d), lambda l, j: (0, 0)),
                  pl.BlockSpec((None, d, tn), lambda l, j: (l, 0, j))],
        out_specs=pl.BlockSpec((None, 16, tn), lambda l, j: (l, 0, j)),
        out_shape=_sds((nlay, 16, nl), F32), compiler_params=_cp("parallel", "parallel"))(c16, w_mod)


def adaln_bwd(c16t, dm, w_mod):
    nlay, d, nl = w_mod.shape
    tn = _pick(nl, (512, 256, 128))

    def body(c_ref, dm_ref, w_ref, dw_ref, dc_ref):
        @pl.when(pl.program_id(1) == 0)
        def _():
            dc_ref[...] = jnp.zeros_like(dc_ref)

        dmv = dm_ref[...]
        dw_ref[...] = jnp.dot(_silu(c_ref[...]), dmv, preferred_element_type=F32,
                              precision=lax.Precision.HIGHEST)
        dc_ref[...] += lax.dot_general(dmv, w_ref[...], _NT, preferred_element_type=F32,
                                       precision=lax.Precision.HIGHEST)

    return pl.pallas_call(
        body, name="adaln_bwd", grid=(nlay, nl // tn),
        in_specs=[pl.BlockSpec((d, 16), lambda l, j: (0, 0)),
                  pl.BlockSpec((None, 16, tn), lambda l, j: (l, 0, j)),
                  pl.BlockSpec((None, d, tn), lambda l, j: (l, 0, j))],
        out_specs=[pl.BlockSpec((None, d, tn), lambda l, j: (l, 0, j)),
                   pl.BlockSpec((None, 16, d), lambda l, j: (l, 0, 0))],
        out_shape=[_sds((nlay, d, nl), F32), _sds((nlay, 16, d), F32)],
        compiler_params=_cp("parallel", "arbitrary"))(c16t, dm, w_mod)


def sum_leading(a, name):
    n = a.shape[0]

    def body(a_ref, o_ref):
        acc = a_ref[0]
        for i in range(1, n):
            acc = acc + a_ref[i]
        o_ref[...] = acc

    return pl.pallas_call(body, name=name, in_specs=[VMEM_FULL], out_specs=VMEM_FULL,
                          out_shape=_sds(a.shape[1:], F32),
                          compiler_params=pltpu.CompilerParams(vmem_limit_bytes=VMEM_LIMIT))(a)


def c_ctx_grad(parts, c_ctx):
    n = parts.shape[0]

    def body(p_ref, c_ref, o_ref):
        acc = p_ref[0]
        for i in range(1, n):
            acc = acc + p_ref[i]
        o_ref[...] = acc * _dsilu(c_ref[...])

    return pl.pallas_call(body, name="c_ctx_grad", in_specs=[VMEM_FULL, VMEM_FULL], out_specs=VMEM_FULL,
                          out_shape=_sds(c_ctx.shape, F32))(parts, c_ctx)


def _as2d(a):
    return a.reshape(-1, a.shape[-1])


def _row_tile(r, c):
    for tr in (1024, 512, 256, 128, 64, 32, 16, 8):
        if r % tr == 0 and tr * c * 4 <= (1 << 20):
            return tr
    return r


def add_n(items, name, out_dtypes=(F32,)):
    norm = [it if isinstance(it, tuple) else (it, ()) for it in items]
    shape = norm[0][0].shape[len(norm[0][1]):]
    r, c = _as2d(jnp.zeros(shape, BF16)).shape
    tr = _row_tile(r, c)
    dyn, specs, flat = [], [], []
    for arr, lead in norm:
        slots = []
        for ix in lead:
            if isinstance(ix, int):
                slots.append(ix)
            else:
                slots.append((len(dyn),))
                dyn.append(ix)

        def imap(i, s, slots=tuple(slots)):
            return tuple(s[k[0]] if isinstance(k, tuple) else k for k in slots) + (i, 0)

        specs.append(pl.BlockSpec((None,) * len(lead) + (tr, c), imap))
        flat.append(arr.reshape(arr.shape[:len(lead)] + (r, c)))
    sel = jnp.stack([jnp.asarray(v, jnp.int32) for v in dyn]) if dyn else jnp.zeros((1,), jnp.int32)
    n_in = len(flat)

    def body(s_ref, *refs):
        acc = refs[0][...].astype(F32)
        for ref in refs[1:n_in]:
            acc = acc + ref[...].astype(F32)
        for ref in refs[n_in:]:
            ref[...] = acc.astype(ref.dtype)

    out_spec = pl.BlockSpec((tr, c), lambda i, s: (i, 0))
    outs = pl.pallas_call(
        body, name=name,
        grid_spec=pltpu.PrefetchScalarGridSpec(num_scalar_prefetch=1, grid=(r // tr,), in_specs=specs,
                                               out_specs=[out_spec] * len(out_dtypes)),
        out_shape=[_sds((r, c), dt) for dt in out_dtypes], compiler_params=_cp("parallel"))(sel, *flat)
    return [o.reshape(shape) for o in outs]


def pair_add(p0, p1, got, which, name):
    shape = p0.shape
    flat = [_as2d(a) for a in (p0, p1, got)]
    r, c = flat[0].shape
    tr = _row_tile(r, c)

    def body(s_ref, p0_ref, p1_ref, g_ref, o_ref):
        own = jnp.where(s_ref[0] == 0, p0_ref[...], p1_ref[...])
        o_ref[...] = (own + g_ref[...].astype(F32)).astype(o_ref.dtype)

    spec = pl.BlockSpec((tr, c), lambda i, s: (i, 0))
    out = pl.pallas_call(
        body, name=name,
        grid_spec=pltpu.PrefetchScalarGridSpec(num_scalar_prefetch=1, grid=(r // tr,), in_specs=[spec] * 3,
                                               out_specs=spec),
        out_shape=_sds((r, c), BF16), compiler_params=_cp("parallel"))(
            jnp.asarray(which, jnp.int32).reshape(1), *flat)
    return out.reshape(shape)


def adamw(w, g, m, v):
    shape = w.shape
    flat = [_as2d(a.reshape((1,) + shape) if len(shape) == 1 else a) for a in (w, g, m, v)]
    r, c = flat[0].shape
    tr = _row_tile(r, c)
    c1 = 1.0 - ADAM_B1 ** ADAM_STEP
    c2 = 1.0 - ADAM_B2 ** ADAM_STEP

    def body(w_ref, g_ref, m_ref, v_ref, d_ref, nm_ref, nv_ref):
        gv = g_ref[...]
        nm = ADAM_B1 * m_ref[...] + (1.0 - ADAM_B1) * gv
        nv = ADAM_B2 * v_ref[...] + (1.0 - ADAM_B2) * (gv * gv)
        d_ref[...] = -ADAM_LR * ((nm / c1) / (jnp.sqrt(nv / c2) + ADAM_EPS) + ADAM_WD * w_ref[...])
        nm_ref[...] = nm
        nv_ref[...] = nv

    spec = pl.BlockSpec((tr, c), lambda i: (i, 0))
    outs = pl.pallas_call(
        body, name="adamw", grid=(r // tr,), in_specs=[spec] * 4, out_specs=[spec] * 3,
        out_shape=[_sds((r, c), F32)] * 3, compiler_params=_cp("parallel"))(*flat)
    return tuple(o.reshape(shape) for o in outs)


def _place():
    return lax.axis_index("x"), lax.axis_index("y"), lax.axis_index("c")


def _remote(src, dst, ssem, rsem, dev):
    return pltpu.make_async_remote_copy(src_ref=src, dst_ref=dst, send_sem=ssem, recv_sem=rsem,
                                        device_id=dev, device_id_type=MESH)


def all_gather8(v, name):
    m_per, n = v.shape

    def body(x_ref, out_ref, send_sems, recv_sems, local_sem):
        x, y, c = _place()
        me, sibling = (x, y, c), (x, y, 1 - c)
        chips = [(1 - x, y), (x, 1 - y), (1 - x, 1 - y)]

        def rows(px, py, pc):
            return out_ref.at[pl.ds((4 * px + 2 * py + pc) * m_per, m_per), :]

        def copy(k, block, to, src=None):
            return _remote(rows(*block) if src is None else src, rows(*block),
                           send_sems.at[k], recv_sems.at[k], to)

        mine = pltpu.make_async_copy(x_ref, rows(*me), local_sem)
        mine.start()
        first = [copy(0, me, sibling, src=x_ref)]
        first += [copy(1 + j, me, (*chip, c), src=x_ref) for j, chip in enumerate(chips)]
        for cp in first:
            cp.start()
        passed = [copy(4 + j, (*chip, c), sibling) for j, chip in enumerate(chips)]
        for j, chip in enumerate(chips):
            copy(1 + j, (*chip, c), me).wait_recv()
            passed[j].start()
        copy(0, sibling, me).wait_recv()
        for j, chip in enumerate(chips):
            copy(4 + j, (*chip, 1 - c), me).wait_recv()
        for cp in first + passed:
            cp.wait_send()
        mine.wait()

    return pl.pallas_call(
        body, name=name, out_shape=_sds((8 * m_per, n), v.dtype),
        in_specs=[VMEM_FULL], out_specs=VMEM_FULL,
        scratch_shapes=[pltpu.SemaphoreType.DMA((7,)), pltpu.SemaphoreType.DMA((7,)), pltpu.SemaphoreType.DMA],
        compiler_params=pltpu.CompilerParams(vmem_limit_bytes=VMEM_LIMIT))(v)


def _slabs(ref, n):
    rows = ref.shape[0] // n
    return [ref.at[pl.ds(i * rows, rows)] for i in range(n)]


def gather_weight(w, name):
    _, r, cdim = w.shape
    n = D2D_PARTS

    def body(w_ref, out_ref, slab_ref, send_sems, recv_sems, local_sem):
        x, y, c = _place()
        sibling = (x, y, 1 - c)
        chips = [(1 - x, y), (x, 1 - y), (1 - x, 1 - y)]
        mine = 2 * x + y
        src = _slabs(w_ref.at[c], n)
        first = []
        for k, chip in enumerate(chips):
            dst = _slabs(out_ref.at[c, mine], n)
            for j in range(n):
                cp = _remote(src[j], dst[j], send_sems.at[k * n + j], recv_sems.at[k * n + j], (*chip, c))
                cp.start()
                first.append(cp)
        for i in range(2):
            for s_in, s_out in zip(_slabs(w_ref.at[i], n), _slabs(out_ref.at[i, mine], n)):
                cp = pltpu.make_async_copy(s_in, slab_ref, local_sem)
                cp.start()
                cp.wait()
                cp = pltpu.make_async_copy(slab_ref, s_out, local_sem)
                cp.start()
                cp.wait()
        passed = []
        for j in range(n):
            for k, (px, py) in enumerate(chips):
                theirs = _slabs(out_ref.at[c, 2 * px + py], n)[j]
                _remote(src[j], theirs, send_sems.at[k * n + j], recv_sems.at[k * n + j], sibling).wait_recv()
                cp = _remote(theirs, theirs, send_sems.at[(3 + k) * n + j], recv_sems.at[(3 + k) * n + j], sibling)
                cp.start()
                passed.append(cp)
        for j in range(n):
            for k, (px, py) in enumerate(chips):
                other = _slabs(out_ref.at[1 - c, 2 * px + py], n)[j]
                _remote(other, other, send_sems.at[(3 + k) * n + j], recv_sems.at[(3 + k) * n + j],
                        sibling).wait_recv()
        for cp in first + passed:
            cp.wait_send()

    return pl.pallas_call(
        body, name=name, out_shape=_sds((2, 4, r, cdim), w.dtype), in_specs=[ANY], out_specs=ANY,
        scratch_shapes=[pltpu.VMEM((r // n, cdim), w.dtype), pltpu.SemaphoreType.DMA((6 * n,)),
                        pltpu.SemaphoreType.DMA((6 * n,)), pltpu.SemaphoreType.DMA],
        compiler_params=pltpu.CompilerParams(vmem_limit_bytes=VMEM_LIMIT))(w)


def rs_chip_exchange(gb0, gb1, name):
    _, r, cdim = gb0.shape

    def body(gb0_ref, gb1_ref, st_ref, send_sems, recv_sems):
        x, y, c = _place()
        chips = [(1 - x, y), (x, 1 - y), (1 - x, 1 - y)]
        cps = []
        for k, (px, py) in enumerate(chips):
            for i, gb_ref in enumerate((gb0_ref, gb1_ref)):
                cps.append(_remote(gb_ref.at[2 * px + py], st_ref.at[k, i], send_sems.at[2 * k + i],
                                   recv_sems.at[2 * k + i], (px, py, c)))
        for cp in cps:
            cp.start()
        for cp in cps:
            cp.wait()

    return pl.pallas_call(
        body, name=name, out_shape=_sds((3, 2, r, cdim), BF16), in_specs=[ANY] * 2, out_specs=ANY,
        scratch_shapes=[pltpu.SemaphoreType.DMA((6,)), pltpu.SemaphoreType.DMA((6,))])(gb0, gb1)


HBM_SPEC = pl.BlockSpec(memory_space=pltpu.HBM)
SEM_SPEC = pl.BlockSpec(memory_space=pltpu.SEMAPHORE)
DATAFLOW = pltpu.SideEffectType.DATAFLOW_SIDE_EFFECTING


def _exchange_copies(gb0_ref, gb1_ref, land_ref, send_sems, recv_sems):
    x, y, c = _place()
    cps = []
    for k, (px, py) in enumerate([(1 - x, y), (x, 1 - y), (1 - x, 1 - y)]):
        for i, gb_ref in enumerate((gb0_ref, gb1_ref)):
            cps.append(_remote(gb_ref.at[2 * px + py], land_ref.at[k, i], send_sems.at[2 * k + i],
                               recv_sems.at[2 * k + i], (px, py, c)))
    return cps


def rs_exchange_start(gb0, gb1, name):
    _, r, cdim = gb0.shape

    def body(gb0_ref, gb1_ref, land_ref, send_sems, recv_sems, gb0_thru, gb1_thru, land_thru, token):
        for cp in _exchange_copies(gb0_ref, gb1_ref, land_ref, send_sems, recv_sems):
            cp.start()
        token[...] = jnp.zeros_like(token)

    land = lax.empty((3, 2, r, cdim), BF16)
    return pl.pallas_call(
        body, name=name,
        out_shape=(pltpu.SemaphoreType.DMA((6,)), pltpu.SemaphoreType.DMA((6,)), pltpu.HBM(gb0.shape, BF16),
                   pltpu.HBM(gb1.shape, BF16), pltpu.HBM(land.shape, BF16), _sds((SUBLANES, LANES), F32)),
        in_specs=(HBM_SPEC, HBM_SPEC, HBM_SPEC),
        out_specs=(SEM_SPEC, SEM_SPEC, HBM_SPEC, HBM_SPEC, HBM_SPEC, VMEM_FULL),
        input_output_aliases={0: 2, 1: 3, 2: 4},
        compiler_params=pltpu.CompilerParams(has_side_effects=DATAFLOW))(
            pltpu.with_memory_space_constraint(gb0, pltpu.HBM), pltpu.with_memory_space_constraint(gb1, pltpu.HBM),
            pltpu.with_memory_space_constraint(land, pltpu.HBM))


def rs_exchange_wait(started, after, name):
    send_sems, recv_sems, gb0, gb1, land, _ = started

    def body(gb0_ref, gb1_ref, land_ref, send_sems, recv_sems, after_ref, gb0_dead, gb1_dead, got_ref):
        for cp in _exchange_copies(gb0_ref, gb1_ref, land_ref, send_sems, recv_sems):
            cp.wait_send()
            cp.wait_recv()

    return pl.pallas_call(
        body, name=name,
        out_shape=(pltpu.HBM(gb0.shape, BF16), pltpu.HBM(gb1.shape, BF16), pltpu.HBM(land.shape, BF16)),
        in_specs=(HBM_SPEC, HBM_SPEC, HBM_SPEC, SEM_SPEC, SEM_SPEC, ANY),
        out_specs=(HBM_SPEC, HBM_SPEC, HBM_SPEC), input_output_aliases={0: 0, 1: 1, 2: 2},
        compiler_params=pltpu.CompilerParams(has_side_effects=DATAFLOW))(
            gb0, gb1, land, send_sems, recv_sems, after)[2]


def rs_pair_swap(pb0, pb1, name):
    r, cdim = pb0.shape
    n = 2 * D2D_PARTS

    def body(pb0_ref, pb1_ref, got_ref, stage_ref, send_sems, recv_sems, local_sem):
        x, y, c = _place()
        sibling = (x, y, 1 - c)

        @pl.when(c == 0)
        def _():
            load = pltpu.make_async_copy(pb1_ref, stage_ref, local_sem)
            load.start()
            load.wait()

        @pl.when(c == 1)
        def _():
            load = pltpu.make_async_copy(pb0_ref, stage_ref, local_sem)
            load.start()
            load.wait()

        src, dst = _slabs(stage_ref, n), _slabs(got_ref, n)
        cps = [_remote(src[j], dst[j], send_sems.at[j], recv_sems.at[j], sibling) for j in range(n)]
        for cp in cps:
            cp.start()
        for cp in cps:
            cp.wait()

    return pl.pallas_call(
        body, name=name, out_shape=_sds((r, cdim), BF16), in_specs=[ANY, ANY], out_specs=VMEM_FULL,
        scratch_shapes=[pltpu.VMEM((r, cdim), BF16), pltpu.SemaphoreType.DMA((n,)), pltpu.SemaphoreType.DMA((n,)),
                        pltpu.SemaphoreType.DMA],
        compiler_params=pltpu.CompilerParams(vmem_limit_bytes=VMEM_LIMIT))(pb0, pb1)


def rs_pair_share(red, name):
    n = 2 * D2D_PARTS

    def body(red_ref, out_ref, send_sems, recv_sems, local_sem):
        x, y, c = _place()
        sibling = (x, y, 1 - c)
        keep = pltpu.make_async_copy(red_ref, out_ref.at[c], local_sem)
        keep.start()
        src, dst, got = _slabs(red_ref, n), _slabs(out_ref.at[c], n), _slabs(out_ref.at[1 - c], n)
        cps = [_remote(src[j], dst[j], send_sems.at[j], recv_sems.at[j], sibling) for j in range(n)]
        for cp in cps:
            cp.start()
        for cp in cps:
            cp.wait_send()
        for j in range(n):
            _remote(src[j], got[j], send_sems.at[j], recv_sems.at[j], sibling).wait_recv()
        keep.wait()

    return pl.pallas_call(
        body, name=name, out_shape=_sds((2,) + red.shape, red.dtype), in_specs=[VMEM_FULL], out_specs=VMEM_FULL,
        scratch_shapes=[pltpu.SemaphoreType.DMA((n,)), pltpu.SemaphoreType.DMA((n,)), pltpu.SemaphoreType.DMA],
        compiler_params=pltpu.CompilerParams(vmem_limit_bytes=VMEM_LIMIT))(red)


def reduce_scatter_grad(g0, g1, shard, core, tag, theirs=None):
    if theirs is None:
        theirs = rs_chip_exchange(g0[1], g1[1], "rs_exchange_" + tag)
    p0, pb0 = add_n([(g0[0], (shard,)), (theirs, (0, 0)), (theirs, (1, 0)), (theirs, (2, 0))],
                    "rs_chip_add0_" + tag, (F32, BF16))
    p1, pb1 = add_n([(g1[0], (shard,)), (theirs, (0, 1)), (theirs, (1, 1)), (theirs, (2, 1))],
                    "rs_chip_add1_" + tag, (F32, BF16))
    got = rs_pair_swap(pb0, pb1, "rs_swap_" + tag)
    red = pair_add(p0, p1, got, core, "rs_pair_add_" + tag)
    return rs_pair_share(red, "rs_share_" + tag).astype(F32)


def _rope_tables(lc, s):
    rows_n = s // GRID_W
    row = jnp.repeat(jnp.arange(rows_n, dtype=F32), GRID_W)
    col = jnp.tile(jnp.arange(GRID_W, dtype=F32), rows_n)
    axis_dim = HEAD_DIM // 2
    inv_freq = ROPE_THETA ** (-jnp.arange(0, axis_dim, 2, dtype=F32) / axis_dim)
    ang_r = row[:, None] * inv_freq[None, :]
    ang_c = col[:, None] * inv_freq[None, :]
    cr, sr, cc, sc = jnp.cos(ang_r), jnp.sin(ang_r), jnp.cos(ang_c), jnp.sin(ang_c)
    cos_l = jnp.concatenate([cr, cr, cc, cc], axis=1)
    sin_l = jnp.concatenate([-sr, sr, -sc, sc], axis=1)
    cos_t = jnp.concatenate([jnp.ones((lc, HEAD_DIM), F32), cos_l], axis=0)
    sin_t = jnp.concatenate([jnp.zeros((lc, HEAD_DIM), F32), sin_l], axis=0)
    return cos_t, sin_t


def _pad_rows(a, rows):
    return jnp.concatenate([a, jnp.zeros((rows - a.shape[0],) + a.shape[1:], a.dtype)], axis=0)


def _pad_cols(a, cols):
    return jnp.concatenate([a, jnp.zeros(a.shape[:-1] + (cols - a.shape[-1],), a.dtype)], axis=-1)


def kernel(x, c, ctx, c_ctx, w_mod, b_mod, post_ln_g, post_ln_b, w_in_e, conv_a_w, conv_a_b, norm_a_g, norm_a_b, conv_b_w, w_out_e, w_in_o, q_norm_g, k_norm_g, w_out_o, loss_target, m_c_ctx, m_w_mod, m_b_mod, m_post_ln_g, m_post_ln_b, m_w_in_e, m_conv_a_w, m_conv_a_b, m_norm_a_g, m_norm_a_b, m_conv_b_w, m_w_out_e, m_w_in_o, m_q_norm_g, m_k_norm_g, m_w_out_o, v_c_ctx, v_w_mod, v_b_mod, v_post_ln_g, v_post_ln_b, v_w_in_e, v_conv_a_w, v_conv_a_b, v_norm_a_g, v_norm_a_b, v_conv_b_w, v_w_out_e, v_w_in_o, v_q_norm_g, v_k_norm_g, v_w_out_o):
    s, d = x.shape[1], x.shape[2]
    lc = ctx.shape[1]
    t = lc + s
    tm = lc
    depth = w_mod.shape[0]
    n_even, n_odd = w_in_e.shape[0], w_in_o.shape[0]
    ad = w_out_o.shape[1] * 4
    kvd = (w_in_o.shape[2] * 4 - 2 * ad) // 2
    nkv = kvd // HEAD_DIM
    hpg = ad // nkv // HEAD_DIM
    nlm = w_mod.shape[2]
    alpha = (2.0 * depth) ** 0.25
    scale = HEAD_DIM ** -0.5
    assert n_even == 2 and n_odd == 2 and depth == 4 and hpg == GQA_GROUP
    assert lc % CONV_ROWS == 0 and s % (tm * KV_CHUNKS_PER_STEP) == 0 and d % LANES == 0

    xi, yi, ci = _place()
    shard = 2 * xi + yi
    dev = 4 * xi + 2 * yi + ci

    wg_in_e = gather_weight(w_in_e.astype(BF16), "gather_w_in_e")
    wg_out_e = gather_weight(w_out_e.astype(BF16), "gather_w_out_e")
    wg_in_o = gather_weight(w_in_o.astype(BF16), "gather_w_in_o")
    wg_out_o = gather_weight(w_out_o.astype(BF16), "gather_w_out_o")

    c_all = all_gather8(_pad_rows(c, 8), "gather_c")
    c16 = _pad_rows(jnp.concatenate([c_all[0::8], c_ctx[None, :]], axis=0), 16)
    m_part = adaln_fwd(c16, w_mod)
    m_all = all_gather8(m_part.reshape(depth * 16, nlm), "gather_mod")
    m_all = m_all.reshape(8, depth, 16, nlm)[0::2]
    m_full = m_all.transpose(1, 2, 0, 3).reshape(depth, 16, 4 * nlm) + b_mod[:, None, :]
    m_lat = lax.dynamic_index_in_dim(m_full, dev, axis=1, keepdims=False)
    m_ctx = m_full[:, 8]

    def seg2(l, part):
        return jnp.stack([m_ctx[l, part * d:(part + 1) * d], m_lat[l, part * d:(part + 1) * d]])[:, None, :]

    cos_t, sin_t = _rope_tables(lc, s)

    small_gathered = all_gather8(
        _pad_rows(jnp.concatenate([conv_a_w.reshape(n_even * CONV_A_TAPS, -1),
                                   conv_b_w.reshape(n_even * CONV_B_TAPS, -1)], axis=0), 72), "gather_taps")
    taps = small_gathered.reshape(8, 72, -1)[0::2]
    taps = taps.transpose(1, 0, 2).reshape(72, d)
    caw = taps[:n_even * CONV_A_TAPS].reshape(n_even, CONV_A_TAPS, d)
    cbw = taps[n_even * CONV_A_TAPS:n_even * (CONV_A_TAPS + CONV_B_TAPS)].reshape(n_even, CONV_B_TAPS, d)
    caw_pad = jnp.concatenate([caw, jnp.zeros((n_even, 32 - CONV_A_TAPS, d), F32)], axis=1)
    cbw_pad = jnp.concatenate([cbw, jnp.zeros((n_even, 8 - CONV_B_TAPS, d), F32)], axis=1)

    xc = jnp.concatenate([ctx[0], x[0]], axis=0)
    saved = []
    for l in range(depth):
        i = l // 2
        shift2, scale2, gate2 = seg2(l, 0), seg2(l, 1), seg2(l, 2)
        h = mod_fwd(xc, scale2, shift2, tm)
        if l % 2 == 0:
            p = mm_nn(h, wg_in_e[i], BF16)
            u1 = conv_a_fwd(p, caw_pad[i], conv_a_b[i][None, :], lc, d)
            a_out = ln_a_fwd(u1, p, norm_a_g[i][None, :], norm_a_b[i][None, :], tm)
            b_out = conv_b_fwd(p, cbw_pad[i], lc, d)
            ab = jnp.concatenate([a_out, b_out], axis=1)
            y = mm_nn(ab, wg_out_e[i].reshape(1, 2 * d, d), F32)
            saved.append(dict(xc=xc, h=h, p=p, u1=u1, ab=ab, y=y))
        else:
            qgkv = mm_nn(h, wg_in_o[i], BF16)
            qr, kr = qk_fwd(qgkv, q_norm_g[i][None, :], k_norm_g[i][None, :], cos_t, sin_t, ad, kvd, tm,
                            scale * LOG2_E)
            vt = _chunks_t(qgkv[:, 2 * ad + kvd:], nkv, tm)
            o, og, lse = flash_fwd(qr, kr, vt, qgkv, ad, kvd, tm)
            y = mm_nn(og, wg_out_o[i].reshape(1, ad, d), F32)
            saved.append(dict(xc=xc, h=h, qgkv=qgkv, qr=qr, kr=kr, o=o, og=og, lse=lse, y=y))
        xc = post_ln_fwd(xc, y, gate2, post_ln_g[l][None, :], post_ln_b[l][None, :], alpha, tm)

    dxc, loss_acc = loss_head(xc, loss_target[0], lc, tm)

    g_in_e, g_out_e, g_in_o, g_out_o = [None] * 2, [None] * 2, [None] * 2, [None] * 2
    d_mod_lat, d_mod_ctx = [None] * depth, [None] * depth
    d_pln_g, d_pln_b = [None] * depth, [None] * depth
    d_cab, d_nag, d_nab, d_caw, d_cbw = [None] * 2, [None] * 2, [None] * 2, [None] * 2, [None] * 2
    d_qg, d_kg = [None] * 2, [None] * 2
    for l in reversed(range(depth)):
        i = l // 2
        sv = saved[l]
        scale2, gate2 = seg2(l, 1), seg2(l, 2)
        if l == 0:
            gate2 = gate2 + sent
        dzx, dy, acc_ln = post_ln_bwd(dxc, sv["xc"], sv["y"], gate2, post_ln_g[l][None, :], alpha, tm)
        d_pln_g[l] = acc_ln[0, 1] + acc_ln[1, 1]
        d_pln_b[l] = acc_ln[0, 2] + acc_ln[1, 2]
        if l % 2 == 0:
            w_out3 = wg_out_e[i].reshape(1, 2 * d, d)
            dab = mm_nt(dy, w_out3, BF16)
            g_out_e[i] = tuple(g.reshape(4, 2 * d // 4, d) for g in mm_tn(sv["ab"], dy, 1))
            du1, d_agate, acc_a = ln_a_bwd(dab, sv["u1"], sv["p"], norm_a_g[i][None, :], norm_a_b[i][None, :], tm)
            d_nag[i], d_nab[i], d_cab[i] = acc_a[0], acc_a[1], acc_a[2]
            d_aval, d_aglu, d_caw[i] = conv_a_bwd(du1, sv["p"], caw_pad[i], lc, d)
            d_bx, d_bb, d_bc, d_bg, d_cbw[i] = conv_b_bwd(dab, sv["p"], cbw_pad[i], lc, d)
            dp = jnp.concatenate([d_aval, d_aglu, d_agate, d_bx, d_bb, d_bc, d_bg], axis=1)
            dh = mm_nt(dp, wg_in_e[i], F32)
            g_in_e[i] = tuple(mm_tn(sv["h"], dp, 4))
        else:
            w_out3 = wg_out_o[i].reshape(1, ad, d)
            dog = mm_nt(dy, w_out3, BF16)
            g_out_o[i] = tuple(g.reshape(4, ad // 4, d) for g in mm_tn(sv["og"], dy, 1))
            do, dgate, delta = gate_bwd(dog, sv["o"], sv["qgkv"], ad, kvd, tm)
            dqr, dkr, dv = flash_bwd(sv["qr"], do, sv["kr"], _chunks_t(sv["kr"], nkv, tm), sv["qgkv"], sv["lse"],
                                     _rows_per_head(delta, tm, hpg), ad, kvd, tm, scale)
            dq, dk, acc_qk = qk_bwd(dqr, dkr, sv["qgkv"], q_norm_g[i][None, :], k_norm_g[i][None, :],
                                    cos_t, sin_t, ad, kvd, tm)
            d_qg[i], d_kg[i] = acc_qk[0], acc_qk[1]
            dqgkv = jnp.concatenate([dq, dgate, dk, dv.astype(BF16)], axis=1)
            dh = mm_nt(dqgkv, wg_in_o[i], F32)
            g_in_o[i] = tuple(mm_tn(sv["h"], dqgkv, 4))
        dxc, acc_mod = mod_bwd(dh, dzx, sv["xc"], scale2, tm)
        d_mod_ctx[l] = jnp.stack([acc_mod[0, 0], acc_mod[0, 1], acc_ln[0, 0]])
        d_mod_lat[l] = jnp.stack([acc_mod[1, 0], acc_mod[1, 1], acc_ln[1, 0]])
        if l == 1:
            started_in_o = rs_exchange_start(g_in_o[0][1], g_in_o[1][1], "rs_exchange_start_in_o")
            started_out_o = rs_exchange_start(g_out_o[0][1], g_out_o[1][1], "rs_exchange_start_out_o")
            sent = started_in_o[5][0, 0] + started_out_o[5][0, 0]
    theirs_in_o = rs_exchange_wait(started_in_o, dxc, "rs_exchange_wait_in_o")
    theirs_out_o = rs_exchange_wait(started_out_o, dxc, "rs_exchange_wait_out_o")
    grad_x = dxc[lc:][None]

    parts = [jnp.concatenate(d_mod_ctx, axis=0), jnp.stack(d_pln_g), jnp.stack(d_pln_b), jnp.stack(d_cab),
             jnp.stack(d_nag), jnp.stack(d_nab), jnp.concatenate(d_caw, axis=0), jnp.concatenate(d_cbw, axis=0),
             _pad_cols(jnp.stack(d_qg), d), _pad_cols(jnp.stack(d_kg), d), jnp.concatenate(d_mod_lat, axis=0),
             _pad_cols(loss_acc, d)]
    parts = [_pad_rows(p, -(-p.shape[0] // SUBLANES) * SUBLANES) for p in parts]
    offs = [0]
    for p in parts:
        offs.append(offs[-1] + p.shape[0])
    pack = jnp.concatenate(parts, axis=0)
    npack = offs[-1]
    gathered = all_gather8(pack, "gather_small").reshape(8, npack, d)
    small = sum_leading(gathered, "sum_small")

    def piece(k, rows):
        return small[offs[k]:offs[k] + rows]

    loss = 0.5 / d * jnp.sum(piece(11, 1)[0, :LANES])
    dm_ctx = piece(0, 3 * depth).reshape(depth, 1, 3 * d)
    dm_lat = gathered[:, offs[10]:offs[10] + 3 * depth].reshape(8, depth, 3 * d).transpose(1, 0, 2)
    dm = jnp.concatenate([dm_lat, dm_ctx, jnp.zeros((depth, 7, 3 * d), F32)], axis=1)
    g_b_mod = sum_leading(dm.transpose(1, 0, 2), "sum_b_mod")
    dm_shard = lax.dynamic_slice_in_dim(dm, shard * nlm, nlm, axis=2)
    g_w_mod, dc_part = adaln_bwd(c16.T, dm_shard, w_mod)
    dc_all = all_gather8(_pad_rows(dc_part[:, 8, :], 8), "gather_dc").reshape(8, 8, d)
    g_c_ctx = c_ctx_grad(dc_all[0::2, :depth].reshape(4 * depth, 1, d), c_ctx[None, :])[0]

    g_pln_g, g_pln_b = piece(1, depth), piece(2, depth)
    g_cab, g_nag, g_nab = piece(3, n_even), piece(4, n_even), piece(5, n_even)
    dch = d // 4
    g_caw = lax.dynamic_slice_in_dim(piece(6, 64).reshape(2, 32, d)[:, :CONV_A_TAPS], shard * dch, dch, axis=2)
    g_cbw = lax.dynamic_slice_in_dim(piece(7, 16).reshape(2, 8, d)[:, :CONV_B_TAPS], shard * dch, dch, axis=2)
    g_qg, g_kg = piece(8, n_odd)[:, :HEAD_DIM], piece(9, n_odd)[:, :HEAD_DIM]

    g_w_in_e = reduce_scatter_grad(g_in_e[0], g_in_e[1], shard, ci, "in_e")
    g_w_out_e = reduce_scatter_grad(g_out_e[0], g_out_e[1], shard, ci, "out_e")
    g_w_in_o = reduce_scatter_grad(g_in_o[0], g_in_o[1], shard, ci, "in_o", theirs_in_o)
    g_w_out_o = reduce_scatter_grad(g_out_o[0], g_out_o[1], shard, ci, "out_o", theirs_out_o)

    grads = [g_c_ctx, g_w_mod, g_b_mod, g_pln_g, g_pln_b, g_w_in_e, g_caw, g_cab, g_nag, g_nab, g_cbw,
             g_w_out_e, g_w_in_o, g_qg, g_kg, g_w_out_o]
    weights = [c_ctx, w_mod, b_mod, post_ln_g, post_ln_b, w_in_e, conv_a_w, conv_a_b, norm_a_g, norm_a_b,
               conv_b_w, w_out_e, w_in_o, q_norm_g, k_norm_g, w_out_o]
    ms = [m_c_ctx, m_w_mod, m_b_mod, m_post_ln_g, m_post_ln_b, m_w_in_e, m_conv_a_w, m_conv_a_b, m_norm_a_g,
          m_norm_a_b, m_conv_b_w, m_w_out_e, m_w_in_o, m_q_norm_g, m_k_norm_g, m_w_out_o]
    vs = [v_c_ctx, v_w_mod, v_b_mod, v_post_ln_g, v_post_ln_b, v_w_in_e, v_conv_a_w, v_conv_a_b, v_norm_a_g,
          v_norm_a_b, v_conv_b_w, v_w_out_e, v_w_in_o, v_q_norm_g, v_k_norm_g, v_w_out_o]
    deltas, new_ms, new_vs = [], [], []
    for wv, gv, mv, vv in zip(weights, grads, ms, vs):
        dl, nm, nv = adamw(wv, gv, mv, vv)
        deltas.append(dl)
        new_ms.append(nm)
        new_vs.append(nv)
    return (loss, grad_x, *grads, *deltas, *new_ms, *new_vs)
```
